```python
import jax
import jax.numpy as jnp
from jax import lax
import numpy as np

D_MODEL = 1024
BATCH = 32
SEQ = 256
DEPTH = 1
DEC_BATCH = 2
DEC_SEQ = 1024
PAST_LEN = 256

GRID_W = 64
D_MIX = D_MODEL
RWKV_WIDTH = D_MIX // 2
RWKV_HEAD_DIM = 64
RWKV_HEADS = RWKV_WIDTH // RWKV_HEAD_DIM
DECAY_LORA = 64
AAA_LORA = 64
GATE_LORA = 128
CONV_WIDTH = D_MIX - RWKV_WIDTH
CONV_K = 31
RWKV_IN = 3 * RWKV_WIDTH + DECAY_LORA + AAA_LORA + GATE_LORA
IN_WIDTH = RWKV_IN + 2 * CONV_WIDTH
N_EXPERTS = 256
TOP_K = 8
N_GROUPS = 8
TOPK_GROUPS = 4
EXPERT_DIM = 256
SHARED_DIM = 256
ROUTED_SCALE = 2.5
MOE_BLOCK = 128
DEEPNORM_ALPHA = (2.0 * DEPTH) ** 0.25
DEEPNORM_BETA = (8.0 * DEPTH) ** -0.25
LN_EPS = 1e-5
GN_EPS = 64e-5

kernel_name = 'hybrid_rwkv7_conformer_moe_diffusion_step'


def layer_norm(x, g, b, eps=LN_EPS):
    xf = x.astype(jnp.float32)
    mu = jnp.mean(xf, axis=-1, keepdims=True)
    var = jnp.mean(jnp.square(xf - mu), axis=-1, keepdims=True)
    return ((xf - mu) * lax.rsqrt(var + eps)).astype(x.dtype) * g + b


def to_heads(z):
    return z.reshape(z.shape[:-1] + (RWKV_HEADS, RWKV_HEAD_DIM))


def bidir_shift(p):
    zero = jnp.zeros_like(p[:, :1])
    prev = jnp.concatenate([zero, p[:, :-1]], axis=1)
    nxt = jnp.concatenate([p[:, 1:], zero], axis=1)
    return 0.5 * (prev + nxt)


def stack_dirs_shared(z):
    return jnp.moveaxis(jnp.stack([z, z[:, ::-1]], axis=2), 1, 0)


def stack_dirs(z):
    return jnp.moveaxis(jnp.stack([z[:, :, 0], z[:, ::-1, 1]], axis=2), 1, 0)


def rwkv7_scan(r, w, k, v, kk, a, s0):
    def step(s, inp):
        r_t, w_t, k_t, v_t, kk_t, a_t = inp
        s_kk = jnp.einsum('bdhij,bdhj->bdhi', s, kk_t)
        s = s * w_t[..., None, :] - s_kk[..., :, None] * (kk_t * a_t)[..., None, :] + v_t[..., :, None] * k_t[..., None, :]
        return s, jnp.einsum('bdhij,bdhj->bdhi', s, r_t)
    s_final, ys = lax.scan(step, s0, (r, w, k, v, kk, a))
    return ys, s_final


def rwkv7_mixer(p, s0, mu_shift, w_decay0, w_decay2, w_a0, w_a2, w_g2, k_k, k_a, r_k, lnx_g, lnx_b):
    f32 = jnp.float32
    bsz, seq, _ = p.shape
    p = p + mu_shift * (bidir_shift(p) - p)
    splits = [RWKV_WIDTH, 2 * RWKV_WIDTH, 3 * RWKV_WIDTH, 3 * RWKV_WIDTH + DECAY_LORA, 3 * RWKV_WIDTH + DECAY_LORA + AAA_LORA]
    r, k, v, wd, ad, gd = jnp.split(p, splits, axis=-1)
    w_logit = (w_decay0 + jnp.einsum('btl,dlc->btdc', jnp.tanh(wd), w_decay2)).astype(f32)
    decay = jnp.exp(-jnp.exp(-jax.nn.softplus(-w_logit) - 0.5))
    a = jax.nn.sigmoid((w_a0 + jnp.einsum('btl,dlc->btdc', ad, w_a2)).astype(f32))
    g = jax.nn.sigmoid(gd) @ w_g2
    kk = to_heads((k * k_k).astype(f32))
    kk = kk * lax.rsqrt(jnp.sum(jnp.square(kk), axis=-1, keepdims=True) + 1e-12)
    k_dir = k.astype(f32)[:, :, None] * (1.0 + (a - 1.0) * k_a.astype(f32))
    rh = to_heads(r.astype(f32))
    vh = to_heads(v.astype(f32))
    kh = to_heads(k_dir)
    ys, s_final = rwkv7_scan(stack_dirs_shared(rh), to_heads(stack_dirs(decay)), stack_dirs(kh),
                             stack_dirs_shared(vh), stack_dirs_shared(kk), to_heads(stack_dirs(a)),
                             s0.astype(f32))
    y = jnp.moveaxis(ys[:, :, 0] + ys[::-1, :, 1], 0, 1)
    mu = jnp.mean(y, axis=-1, keepdims=True)
    var = jnp.mean(jnp.square(y - mu), axis=-1, keepdims=True)
    y = (y - mu) * lax.rsqrt(var + GN_EPS)
    bonus = jnp.sum(jnp.sum(rh[:, :, None] * kh * to_heads(r_k.astype(f32)), axis=-1, keepdims=True) * vh[:, :, None], axis=2)
    y = y.reshape(bsz, seq, RWKV_WIDTH) * lnx_g + lnx_b + bonus.reshape(bsz, seq, RWKV_WIDTH)
    return y.astype(p.dtype) * g, s_final


def depthwise_conv(u, w, b):
    ch = u.shape[-1]
    out = lax.conv_general_dilated(u, w[:, None, :].astype(u.dtype), window_strides=(1,),
                                   padding=[(CONV_K // 2, CONV_K // 2)],
                                   dimension_numbers=('NWC', 'WIO', 'NWC'), feature_group_count=ch)
    return out + b


def conv_mixer(u, grid, conv_w, conv_b, ln_g, ln_b):
    bsz, seq, _ = u.shape
    glu = u[..., :CONV_WIDTH] * jax.nn.sigmoid(u[..., CONV_WIDTH:])
    if grid:
        rows = seq // GRID_W
        half = CONV_WIDTH // 2
        z = glu.reshape(bsz, rows, GRID_W, CONV_WIDTH)
        zh = z[..., :half].reshape(bsz * rows, GRID_W, half)
        yh = depthwise_conv(zh, conv_w[:, :half], conv_b[:half]).reshape(bsz, rows, GRID_W, half)
        zv = jnp.swapaxes(z[..., half:], 1, 2).reshape(bsz * GRID_W, rows, half)
        yv = jnp.swapaxes(depthwise_conv(zv, conv_w[:, half:], conv_b[half:]).reshape(bsz, GRID_W, rows, half), 1, 2)
        y = jnp.concatenate([yh, yv], axis=-1).reshape(bsz, seq, CONV_WIDTH)
    else:
        y = depthwise_conv(glu, conv_w, conv_b)
    return jax.nn.silu(layer_norm(y, ln_g, ln_b))


def moe_ffn(h, router_w, router_bias, w_gu, w_down, sh_gu, sh_down):
    bsz, seq, dim = h.shape
    n_tok = bsz * seq
    xt = h.reshape(n_tok, dim)
    scores = jax.nn.sigmoid((xt @ router_w).astype(jnp.float32))
    sel = scores + router_bias.astype(jnp.float32)
    grp = lax.top_k(sel.reshape(n_tok, N_GROUPS, N_EXPERTS // N_GROUPS), 2)[0].sum(-1)
    _, gidx = lax.top_k(grp, TOPK_GROUPS)
    gmask = jax.nn.one_hot(gidx, N_GROUPS, dtype=jnp.bool_).any(axis=1)
    emask = jnp.repeat(gmask, N_EXPERTS // N_GROUPS, axis=1)
    _, eidx = lax.top_k(jnp.where(emask, sel, -jnp.inf), TOP_K)
    wts = jnp.take_along_axis(scores, eidx, axis=-1)
    wts = wts / jnp.sum(wts, axis=-1, keepdims=True) * ROUTED_SCALE
    n_assign = n_tok * TOP_K
    n_blocks = (n_assign + N_EXPERTS * (MOE_BLOCK - 1) + MOE_BLOCK - 1) // MOE_BLOCK
    flat_e = eidx.reshape(-1)
    flat_tok = jnp.repeat(jnp.arange(n_tok, dtype=jnp.int32), TOP_K)
    flat_w = wts.reshape(-1).astype(h.dtype)
    order = jnp.argsort(flat_e)
    s_e, s_tok, s_w = flat_e[order], flat_tok[order], flat_w[order]
    counts = jnp.bincount(flat_e, length=N_EXPERTS)
    padded = (counts + MOE_BLOCK - 1) // MOE_BLOCK * MOE_BLOCK
    start = jnp.cumsum(counts) - counts
    pend = jnp.cumsum(padded)
    pstart = pend - padded
    dest = pstart[s_e] + jnp.arange(n_assign, dtype=jnp.int32) - start[s_e]
    row_tok = jnp.zeros((n_blocks * MOE_BLOCK,), jnp.int32).at[dest].set(s_tok)
    row_w = jnp.zeros((n_blocks * MOE_BLOCK,), h.dtype).at[dest].set(s_w)
    blk_e = jnp.minimum(jnp.searchsorted(pend, jnp.arange(n_blocks, dtype=jnp.int32) * MOE_BLOCK, side='right'), N_EXPERTS - 1)

    def expert_block(args):
        tok, wrow, e = args
        xb = xt[tok]
        gate, up = jnp.split(xb @ w_gu[e], 2, axis=-1)
        return ((jax.nn.silu(gate) * up) @ w_down[e]) * wrow[:, None]

    ys = lax.map(expert_block, (row_tok.reshape(n_blocks, MOE_BLOCK), row_w.reshape(n_blocks, MOE_BLOCK), blk_e))
    routed = jnp.zeros_like(xt).at[row_tok].add(ys.reshape(n_blocks * MOE_BLOCK, dim))
    sg, su = jnp.split(xt @ sh_gu, 2, axis=-1)
    shared = (jax.nn.silu(sg) * su) @ sh_down
    return (shared + routed).reshape(bsz, seq, dim)


def trunk_layer(x, cond, s0, grid, lp):
    mod = jax.nn.silu(cond) @ lp['w_ada'] + lp['b_ada']
    if mod.ndim == 2:
        mod = mod[:, None, :]
    sh1, sc1, g1, sh2, sc2, g2 = jnp.split(mod, 6, axis=-1)
    h = x * (1.0 + sc1) + sh1
    proj = h @ lp['w_in']
    y_a, s_final = rwkv7_mixer(proj[..., :RWKV_IN], s0, lp['mu_shift'], lp['w_decay0'], lp['w_decay2'],
                               lp['w_a0'], lp['w_a2'], lp['w_g2'], lp['k_k'], lp['k_a'], lp['r_k'],
                               lp['lnx_g'], lp['lnx_b'])
    y_b = conv_mixer(proj[..., RWKV_IN:], grid, lp['conv_w'], lp['conv_b'], lp['conv_ln_g'], lp['conv_ln_b'])
    mix = jnp.concatenate([y_a, y_b], axis=-1) @ lp['w_out']
    x = layer_norm(DEEPNORM_ALPHA * x + g1 * mix, lp['ln1_g'], lp['ln1_b'])
    h = x * (1.0 + sc2) + sh2
    ffn = moe_ffn(h, lp['router_w'], lp['router_bias'], lp['expert_w_gu'], lp['expert_w_down'],
                  lp['shared_w_gu'], lp['shared_w_down'])
    x = layer_norm(DEEPNORM_ALPHA * x + g2 * ffn, lp['ln2_g'], lp['ln2_b'])
    return x, s_final


def setup_inputs(seed: int = 0) -> dict:
    key = jax.random.key(seed)
    ks = jax.random.split(key, 40)
    f32 = jnp.float32

    def nrm(k, shape, scale):
        return jax.random.normal(k, shape, f32) * scale

    L = DEPTH
    return {
        'x_prompt': nrm(ks[0], (BATCH, SEQ, D_MODEL), 1.0),
        'x_sample': nrm(ks[1], (DEC_BATCH, DEC_SEQ, D_MODEL), 1.0),
        'state_rwkv': nrm(ks[2], (DEC_BATCH, DEPTH, 2, RWKV_HEADS, RWKV_HEAD_DIM, RWKV_HEAD_DIM), 0.5),
        'c': nrm(ks[3], (DEC_BATCH, D_MODEL), 1.0),
        'c_ctx': nrm(ks[4], (D_MODEL,), 1.0),
        'w_ada': nrm(ks[5], (L, D_MODEL, 6 * D_MODEL), 0.5 * D_MODEL ** -0.5),
        'b_ada': nrm(ks[6], (L, 6 * D_MODEL), 0.02),
        'w_in': nrm(ks[7], (L, D_MODEL, IN_WIDTH), D_MODEL ** -0.5),
        'mu_shift': jax.random.uniform(ks[8], (L, RWKV_IN), f32),
        'w_decay0': 0.5 + nrm(ks[9], (L, 2, RWKV_WIDTH), 0.5),
        'w_decay2': nrm(ks[10], (L, 2, DECAY_LORA, RWKV_WIDTH), 0.1),
        'w_a0': nrm(ks[11], (L, 2, RWKV_WIDTH), 0.5),
        'w_a2': nrm(ks[12], (L, 2, AAA_LORA, RWKV_WIDTH), 0.1),
        'w_g2': nrm(ks[13], (L, GATE_LORA, RWKV_WIDTH), GATE_LORA ** -0.5),
        'k_k': 0.85 + nrm(ks[14], (L, RWKV_WIDTH), 0.05),
        'k_a': 1.0 + nrm(ks[15], (L, RWKV_WIDTH), 0.05),
        'r_k': nrm(ks[16], (L, RWKV_WIDTH), 0.1),
        'lnx_g': 1.0 + nrm(ks[17], (L, RWKV_WIDTH), 0.02),
        'lnx_b': nrm(ks[18], (L, RWKV_WIDTH), 0.02),
        'conv_w': nrm(ks[19], (L, CONV_K, CONV_WIDTH), CONV_K ** -0.5),
        'conv_b': nrm(ks[20], (L, CONV_WIDTH), 0.02),
        'conv_ln_g': 1.0 + nrm(ks[21], (L, CONV_WIDTH), 0.02),
        'conv_ln_b': nrm(ks[22], (L, CONV_WIDTH), 0.02),
        'w_out': nrm(ks[23], (L, D_MIX, D_MODEL), DEEPNORM_BETA * D_MIX ** -0.5),
        'ln1_g': 1.0 + nrm(ks[24], (L, D_MODEL), 0.02),
        'ln1_b': nrm(ks[25], (L, D_MODEL), 0.02),
        'router_w': nrm(ks[26], (L, D_MODEL, N_EXPERTS), D_MODEL ** -0.5),
        'router_bias': nrm(ks[27], (L, N_EXPERTS), 0.01),
        'expert_w_gu': nrm(ks[28], (L, N_EXPERTS, D_MODEL, 2 * EXPERT_DIM), D_MODEL ** -0.5),
        'expert_w_down': nrm(ks[29], (L, N_EXPERTS, EXPERT_DIM, D_MODEL), DEEPNORM_BETA * EXPERT_DIM ** -0.5),
        'shared_w_gu': nrm(ks[30], (L, D_MODEL, 2 * SHARED_DIM), D_MODEL ** -0.5),
        'shared_w_down': nrm(ks[31], (L, SHARED_DIM, D_MODEL), DEEPNORM_BETA * SHARED_DIM ** -0.5),
        'ln2_g': 1.0 + nrm(ks[32], (L, D_MODEL), 0.02),
        'ln2_b': nrm(ks[33], (L, D_MODEL), 0.02),
    }


def reference(x_prompt, x_sample, state_rwkv, c, c_ctx, w_ada, b_ada, w_in, mu_shift, w_decay0, w_decay2,
              w_a0, w_a2, w_g2, k_k, k_a, r_k, lnx_g, lnx_b, conv_w, conv_b, conv_ln_g, conv_ln_b, w_out,
              ln1_g, ln1_b, router_w, router_bias, expert_w_gu, expert_w_down, shared_w_gu, shared_w_down,
              ln2_g, ln2_b):
    def layer_params(l):
        return dict(w_ada=w_ada[l], b_ada=b_ada[l], w_in=w_in[l], mu_shift=mu_shift[l], w_decay0=w_decay0[l],
                    w_decay2=w_decay2[l], w_a0=w_a0[l], w_a2=w_a2[l], w_g2=w_g2[l], k_k=k_k[l], k_a=k_a[l],
                    r_k=r_k[l], lnx_g=lnx_g[l], lnx_b=lnx_b[l], conv_w=conv_w[l], conv_b=conv_b[l],
                    conv_ln_g=conv_ln_g[l], conv_ln_b=conv_ln_b[l], w_out=w_out[l], ln1_g=ln1_g[l],
                    ln1_b=ln1_b[l], router_w=router_w[l], router_bias=router_bias[l],
                    expert_w_gu=expert_w_gu[l], expert_w_down=expert_w_down[l], shared_w_gu=shared_w_gu[l],
                    shared_w_down=shared_w_down[l], ln2_g=ln2_g[l], ln2_b=ln2_b[l])

    s_zero = jnp.zeros((x_prompt.shape[0], 2, RWKV_HEADS, RWKV_HEAD_DIM, RWKV_HEAD_DIM), jnp.float32)
    y_prompt = x_prompt
    ctx_states = []
    for l in range(DEPTH):
        y_prompt, s_ctx = trunk_layer(y_prompt, c_ctx, s_zero, False, layer_params(l))
        ctx_states.append(s_ctx)
    new_state_rwkv = jnp.stack(ctx_states, axis=1).astype(x_prompt.dtype)

    y_sample = x_sample
    for l in range(DEPTH):
        y_sample, _ = trunk_layer(y_sample, c, state_rwkv[:, l], True, layer_params(l))
    return (y_prompt, y_sample, new_state_rwkv)
```

```python
import functools
import math

import numpy as np
import jax
import jax.numpy as jnp
from jax import lax
from jax.experimental import pallas as pl
from jax.experimental.pallas import tpu as pltpu

F32 = jnp.float32
BF16 = jnp.bfloat16

D_MODEL = 1024
RW = 512
HD = 64
CW = 512
CONV_K = 31
RWKV_IN = 3 * RW + 64 + 64 + 128
N_EXPERTS = 256
TOP_K = 8
N_GROUPS = 8
TOPK_GROUPS = 4
EXPERT_DIM = 256
ROUTED_SCALE = 2.5
MOE_BLOCK = 128
GRID_W = 64
LN_EPS = 1e-5
GN_EPS = 64e-5
CHUNK = 64
GW = 256
DECAY_SCALE = math.exp(-0.5)
VMEM_LIMIT = 52 * 1024 * 1024


def _cparams(sem):
    return pltpu.CompilerParams(dimension_semantics=sem, vmem_limit_bytes=VMEM_LIMIT)


def _split_bf16(x):
    hi = x.astype(BF16)
    lo = (x - hi.astype(F32)).astype(BF16)
    return hi, lo


def _dot(a, b):
    return jnp.dot(a, b, preferred_element_type=F32)


def _dot_nt(a, b):
    return lax.dot_general(a, b, (((1,), (1,)), ((), ())), preferred_element_type=F32)


def _dot_tn(a, b):
    return lax.dot_general(a, b, (((0,), (0,)), ((), ())), preferred_element_type=F32)


def _sigmoid(x):
    return 1.0 / (1.0 + jnp.exp(-x))


def _layer_norm(x, g, b):
    mu = jnp.mean(x, axis=-1, keepdims=True)
    d = x - mu
    var = jnp.mean(d * d, axis=-1, keepdims=True)
    return d * lax.rsqrt(var + LN_EPS) * g + b


def _ada_kernel(c_ref, w_ref, b_ref, o_ref):
    c = c_ref[...]
    s = c * _sigmoid(c)
    s_hi, s_lo = _split_bf16(s)
    w_hi, w_lo = _split_bf16(w_ref[...])
    o_ref[...] = _dot(s_hi, w_hi) + _dot(s_lo, w_hi) + _dot(s_hi, w_lo) + b_ref[...]


def _ada(cond8, w_ada, b_ada):
    n = w_ada.shape[1]
    tn = 1536
    return pl.pallas_call(
        _ada_kernel,
        grid=(n // tn,),
        in_specs=[pl.BlockSpec((8, D_MODEL), lambda j: (0, 0)),
                  pl.BlockSpec((D_MODEL, tn), lambda j: (0, j)),
                  pl.BlockSpec((1, tn), lambda j: (0, j))],
        out_specs=pl.BlockSpec((8, tn), lambda j: (0, j)),
        out_shape=jax.ShapeDtypeStruct((8, n), F32),
        compiler_params=_cparams(("arbitrary",)),
        name="ada",
    )(cond8, w_ada, b_ada.reshape(1, n))


def _inproj_kernel(x_ref, sc_ref, sh_ref, wa_ref, wb_ref, oa_ref, ob_ref):
    h = (x_ref[...] * (1.0 + sc_ref[0]) + sh_ref[0]).astype(BF16)
    oa_ref[...] = _dot(h, wa_ref[...])
    ob_ref[...] = _dot(h, wb_ref[...])


def _in_proj(x_all, sc1, sh1, w_a, w_b, tm, row_of_tile):
    t_all = x_all.shape[0]
    return pl.pallas_call(
        _inproj_kernel,
        grid=(t_all // tm,),
        in_specs=[pl.BlockSpec((tm, D_MODEL), lambda i: (i, 0)),
                  pl.BlockSpec((1, 1, D_MODEL), lambda i: (row_of_tile(i), 0, 0)),
                  pl.BlockSpec((1, 1, D_MODEL), lambda i: (row_of_tile(i), 0, 0)),
                  pl.BlockSpec(w_a.shape, lambda i: (0, 0)),
                  pl.BlockSpec(w_b.shape, lambda i: (0, 0))],
        out_specs=[pl.BlockSpec((tm, RWKV_IN), lambda i: (i, 0)),
                   pl.BlockSpec((tm, 2 * CW), lambda i: (i, 0))],
        out_shape=[jax.ShapeDtypeStruct((t_all, RWKV_IN), F32),
                   jax.ShapeDtypeStruct((t_all, 2 * CW), F32)],
        compiler_params=_cparams(("arbitrary",)),
        name="in_proj",
    )(x_all, sc1, sh1, w_a, w_b)


def _rwkv_consts():
    c = CHUNK
    t = np.arange(c)[:, None]
    s = np.arange(GW)[None, :] % c
    sl = (s < t).astype(np.float32)
    il = (s <= t).astype(np.float32)
    su = (s > t).astype(np.float32)
    iu = (s >= t).astype(np.float32)
    eye_cat = (s == t).astype(np.float32)
    tri = np.stack([sl, il, su, iu, eye_cat])
    r = np.arange(GW)
    mask_bd = (r[:, None] // HD == r[None, :] // HD).astype(np.float32)
    eye = np.eye(GW, dtype=np.float32)
    cum = np.stack([np.tril(np.ones((c, c), np.float32)), np.triu(np.ones((c, c), np.float32))])
    q = np.arange(RW)
    seg = (q[:, None] // HD == q[None, :] // HD).astype(np.float32)
    return (jnp.asarray(tri), jnp.asarray(mask_bd, dtype=BF16), jnp.asarray(mask_bd), jnp.asarray(eye),
            jnp.asarray(cum, dtype=BF16), jnp.asarray(seg, dtype=BF16))


def _rwkv_kernel(p_ref, s0_ref, mu_ref, wlora_ref, wg2_ref, wd0_ref, wa0_ref, kk_ref, ka_ref, rk_ref,
                 lng_ref, lnb_ref, tri_ref, mbd16_ref, mbd_ref, eye_ref, cum_ref, seg_ref,
                 ya_ref, sfin_ref,
                 r_s, v_s, kk_s, g_s, bon_s, a_s, lw_s, kd_s, y_s, st_s, *, seq_len):
    nc = seq_len // CHUNK
    seg = seg_ref[...]

    def seg_sum(x):
        hi, lo = _split_bf16(x)
        return _dot(hi, seg) + _dot(lo, seg)

    def phase_a(c, carry):
        t0 = pl.multiple_of(c * CHUNK, CHUNK)
        cur = p_ref[pl.ds(t0, CHUNK), :]
        prev_row = p_ref[pl.ds(jnp.maximum(t0 - 1, 0), 1), :] * (t0 > 0).astype(F32)
        next_row = p_ref[pl.ds(jnp.minimum(t0 + CHUNK, seq_len - 1), 1), :] * (t0 + CHUNK < seq_len).astype(F32)
        row = lax.broadcasted_iota(jnp.int32, cur.shape, 0)
        prev = jnp.where(row == 0, prev_row, pltpu.roll(cur, 1, 0))
        nxt = jnp.where(row == CHUNK - 1, next_row, pltpu.roll(cur, CHUNK - 1, 0))
        p = cur + mu_ref[...] * (0.5 * (prev + nxt) - cur)
        r = p[:, 0:RW]
        k = p[:, RW:2 * RW]
        v = p[:, 2 * RW:3 * RW]
        z = p[:, 3 * RW:3 * RW + 128]
        gd = p[:, 3 * RW + 128:3 * RW + 256]
        lane = lax.broadcasted_iota(jnp.int32, z.shape, 1)
        z = jnp.where(lane < 64, jnp.tanh(z), z)
        lora = _dot(z.astype(BF16), wlora_ref[...])
        g = _dot(_sigmoid(gd).astype(BF16), wg2_ref[...])
        kks = k * kk_ref[...]
        kk = kks * lax.rsqrt(seg_sum(kks * kks) + 1e-12)
        rows = pl.ds(t0, CHUNK)
        r_s[rows, :] = r
        v_s[rows, :] = v
        kk_s[rows, :] = kk
        g_s[rows, :] = g
        kd_sum = jnp.zeros_like(k)
        for d in range(2):
            w_logit = wd0_ref[:, d * RW:(d + 1) * RW] + lora[:, d * RW:(d + 1) * RW]
            a = _sigmoid(wa0_ref[:, d * RW:(d + 1) * RW] + lora[:, (2 + d) * RW:(3 + d) * RW])
            kd = k * (1.0 + (a - 1.0) * ka_ref[...])
            lw_s[d, rows, :] = -DECAY_SCALE * _sigmoid(w_logit)
            a_s[d, rows, :] = a
            kd_s[d, rows, :] = kd
            kd_sum = kd_sum + kd
        bon_s[rows, :] = seg_sum(r * kd_sum * rk_ref[...]) * v
        return carry

    lax.fori_loop(0, nc, phase_a, 0)

    mbd16 = mbd16_ref[...]
    mbd = mbd_ref[...]
    eye = eye_ref[...]

    def bd(x16):
        return jnp.concatenate([x16, x16, x16, x16], axis=0) * mbd16

    for u in range(4):
        s0 = s0_ref[0, u]
        st_s[u] = jnp.concatenate([s0, s0, s0, s0], axis=0) * mbd

    def unit(d, grp, t0):
        u = d * 2 + grp
        rows = pl.ds(t0, CHUNK)
        lanes = slice(grp * GW, (grp + 1) * GW)
        strict = tri_ref[2 * d]
        incl = tri_ref[2 * d + 1]
        lw = lw_s[d, rows, lanes]
        hi, lo = _split_bf16(lw)
        cl = _dot(cum_ref[d], hi) + _dot(cum_ref[d], lo)
        e_cl = jnp.exp(cl)
        e_ce = jnp.exp(cl - lw)
        e_ncl = jnp.exp(-cl)
        tot = cl[CHUNK - 1:CHUNK, :] if d == 0 else cl[0:1, :]
        wc = jnp.exp(tot)
        kk = kk_s[rows, lanes]
        vv = v_s[rows, lanes]
        akk = a_s[d, rows, lanes] * kk
        kq = kk * e_ce
        rt = r_s[rows, lanes] * e_cl
        bt = akk * e_ncl
        kt = kd_s[d, rows, lanes] * e_ncl
        btw16 = (bt * wc).astype(BF16)
        ktw16 = (kt * wc).astype(BF16)
        kq16 = kq.astype(BF16)
        v16 = vv.astype(BF16)
        g16 = jnp.concatenate([kq16, rt.astype(BF16)], axis=0)
        out_b = _dot_nt(g16, bd(bt.astype(BF16)))
        out_k = _dot_nt(g16, bd(kt.astype(BF16)))
        a_bk = out_b[:CHUNK] * strict
        p_rb = (out_b[CHUNK:] * incl).astype(BF16)
        a_kk = (out_k[:CHUNK] * strict).astype(BF16)
        p_rk = (out_k[CHUNK:] * incl).astype(BF16)
        tinv = tri_ref[4] - a_bk
        pw = a_bk.astype(BF16)
        for _ in range(5):
            pw2 = _dot(pw, bd(pw))
            pw = pw2.astype(BF16)
            tinv = tinv + _dot(pw, bd(tinv.astype(BF16)))
        bdv = bd(v16)
        akkv = _dot(a_kk, bdv)
        x = _dot(tinv.astype(BF16), jnp.concatenate([bd(kq16), bd(akkv.astype(BF16))], axis=1))
        qh16 = x[:, :GW].astype(BF16)
        u016 = x[:, GW:].astype(BF16)
        tn1 = _dot_tn(btw16, jnp.concatenate([qh16, u016], axis=1))
        tn2 = _dot_tn(ktw16, v16)
        mc = (eye * wc - tn1[:, :GW]) * mbd
        ncm = (tn2 - tn1[:, GW:]) * mbd
        r1 = _dot(p_rb, jnp.concatenate([bd(qh16), bd(u016)], axis=1))
        rh = rt - r1[:, :GW]
        y0 = _dot(p_rk, bdv) - r1[:, GW:]
        st16 = st_s[u].astype(BF16)
        big = _dot(jnp.concatenate([rh.astype(BF16), mc.astype(BF16)], axis=0), st16)
        y_s[d, rows, lanes] = big[:CHUNK] + y0
        st_s[u] = big[CHUNK:] + ncm

    def phase_b(i, carry):
        tf = pl.multiple_of(i * CHUNK, CHUNK)
        tb = pl.multiple_of((nc - 1 - i) * CHUNK, CHUNK)
        for grp in range(2):
            unit(0, grp, tf)
            unit(1, grp, tb)
        return carry

    lax.fori_loop(0, nc, phase_b, 0)

    for u in range(4):
        st = st_s[u]
        sfin_ref[0, u] = st[0:HD] + st[HD:2 * HD] + st[2 * HD:3 * HD] + st[3 * HD:4 * HD]

    def phase_c(c, carry):
        rows = pl.ds(pl.multiple_of(c * CHUNK, CHUNK), CHUNK)
        y = y_s[0, rows, :] + y_s[1, rows, :]
        mean = seg_sum(y) * (1.0 / HD)
        dlt = y - mean
        var = seg_sum(dlt * dlt) * (1.0 / HD)
        yn = dlt * lax.rsqrt(var + GN_EPS)
        ya_ref[rows, :] = (yn * lng_ref[...] + lnb_ref[...] + bon_s[rows, :]) * g_s[rows, :]
        return carry

    lax.fori_loop(0, nc, phase_c, 0)


def _rwkv(proj_a, s0_cat, wts, consts, seq_len, n_seq, first_block):
    full = lambda a: pl.BlockSpec(a.shape, lambda s, _n=a.ndim: (0,) * _n)
    in_specs = [pl.BlockSpec((seq_len, RWKV_IN), lambda s: (first_block + s, 0)),
                pl.BlockSpec((1, 4, HD, GW), lambda s: (s, 0, 0, 0))]
    in_specs += [full(a) for a in wts] + [full(a) for a in consts]
    args = [proj_a, s0_cat] + list(wts) + list(consts)
    seq = (seq_len, RW)
    return pl.pallas_call(
        functools.partial(_rwkv_kernel, seq_len=seq_len),
        grid=(n_seq,),
        in_specs=in_specs,
        out_specs=[pl.BlockSpec((seq_len, RW), lambda s: (s, 0)),
                   pl.BlockSpec((1, 4, HD, GW), lambda s: (s, 0, 0, 0))],
        out_shape=[jax.ShapeDtypeStruct((n_seq * seq_len, RW), F32),
                   jax.ShapeDtypeStruct((n_seq, 4, HD, GW), F32)],
        scratch_shapes=[pltpu.VMEM(seq, F32), pltpu.VMEM(seq, F32), pltpu.VMEM(seq, F32), pltpu.VMEM(seq, F32),
                        pltpu.VMEM(seq, F32),
                        pltpu.VMEM((2,) + seq, F32), pltpu.VMEM((2,) + seq, F32), pltpu.VMEM((2,) + seq, F32),
                        pltpu.VMEM((2,) + seq, F32), pltpu.VMEM((4, GW, GW), F32)],
        compiler_params=_cparams(("arbitrary",)),
        name="rwkv_%d" % seq_len,
    )(*args)


CONV_PAD = 16
VPAD = (CONV_K // 2) * GRID_W


def _conv_kernel(u_ref, w_ref, b_ref, g_ref, beta_ref, o_ref, pad_s, vpad_s, y_s, *, seq_len, grid):
    rb = 256
    glu = u_ref[:, :CW] * _sigmoid(u_ref[:, CW:])
    zeros = jnp.zeros((CONV_PAD, CW), F32)
    pad_s[0:CONV_PAD, :] = zeros
    pad_s[CONV_PAD + seq_len:2 * CONV_PAD + seq_len, :] = zeros
    pad_s[CONV_PAD:CONV_PAD + seq_len, :] = glu
    half = CW // 2
    if grid:
        vz = jnp.zeros((VPAD, half), F32)
        vpad_s[0:VPAD, :] = vz
        vpad_s[VPAD + seq_len:2 * VPAD + seq_len, :] = vz
        vpad_s[VPAD:VPAD + seq_len, :] = glu[:, half:]
    n_h_lanes = (half if grid else CW) // 128
    for r0 in range(0, seq_len, rb):
        for lb in range(n_h_lanes):
            lanes = slice(lb * 128, (lb + 1) * 128)
            acc = jnp.zeros((rb, 128), F32)
            if grid:
                col = lax.broadcasted_iota(jnp.int32, (rb, 128), 0) % GRID_W
            for k in range(CONV_K):
                win = pad_s[r0 + k + 1:r0 + k + 1 + rb, lanes]
                term = win * w_ref[k:k + 1, lanes]
                if grid:
                    src = col + (k - CONV_K // 2)
                    term = jnp.where((src >= 0) & (src < GRID_W), term, 0.0)
                acc = acc + term
            y_s[r0:r0 + rb, lanes] = acc
        if grid:
            for lb in range(half // 128):
                lanes = slice(lb * 128, (lb + 1) * 128)
                wl = slice(half + lb * 128, half + (lb + 1) * 128)
                acc = jnp.zeros((rb, 128), F32)
                for k in range(CONV_K):
                    start = VPAD + r0 + (k - CONV_K // 2) * GRID_W
                    acc = acc + vpad_s[start:start + rb, lanes] * w_ref[k:k + 1, wl]
                y_s[r0:r0 + rb, wl] = acc
    for r0 in range(0, seq_len, rb):
        y = y_s[r0:r0 + rb, :] + b_ref[...]
        yl = _layer_norm(y, g_ref[...], beta_ref[...])
        o_ref[r0:r0 + rb, :] = yl * _sigmoid(yl)


def _conv(proj_b, wts, seq_len, n_seq, first_block, grid):
    full = lambda a: pl.BlockSpec(a.shape, lambda s, _n=a.ndim: (0,) * _n)
    in_specs = [pl.BlockSpec((seq_len, 2 * CW), lambda s: (first_block + s, 0))] + [full(a) for a in wts]
    args = [proj_b] + list(wts)
    return pl.pallas_call(
        functools.partial(_conv_kernel, seq_len=seq_len, grid=grid),
        grid=(n_seq,),
        in_specs=in_specs,
        out_specs=pl.BlockSpec((seq_len, CW), lambda s: (s, 0)),
        out_shape=jax.ShapeDtypeStruct((n_seq * seq_len, CW), F32),
        scratch_shapes=[pltpu.VMEM((seq_len + 2 * CONV_PAD, CW), F32),
                        pltpu.VMEM((seq_len + 2 * VPAD if grid else 8, CW // 2), F32),
                        pltpu.VMEM((seq_len, CW), F32)],
        compiler_params=_cparams(("arbitrary",)),
        name="conv_%d" % seq_len,
    )(*args)


def _outproj_kernel(yac_ref, yal_ref, ybc_ref, ybl_ref, x_ref, g1_ref, sc2_ref, sh2_ref, wa_ref, wb_ref,
                    lg_ref, lb_ref, x1_ref, h2_ref, *, alpha, ctx_tiles):
    is_ctx = pl.program_id(0) < ctx_tiles
    ya = jnp.where(is_ctx, yac_ref[...], yal_ref[...])
    yb = jnp.where(is_ctx, ybc_ref[...], ybl_ref[...])
    mix = _dot(ya.astype(BF16), wa_ref[...]) + _dot(yb.astype(BF16), wb_ref[...])
    x1 = _layer_norm(alpha * x_ref[...] + g1_ref[0] * mix, lg_ref[...], lb_ref[...])
    x1_ref[...] = x1
    h2_ref[...] = x1 * (1.0 + sc2_ref[0]) + sh2_ref[0]


def _out_proj(ya_c, ya_l, yb_c, yb_l, x_all, g1, sc2, sh2, w_oa, w_ob, ln_g, ln_b, tm, row_of_tile, alpha):
    t_all = x_all.shape[0]
    ctx_tiles = ya_c.shape[0] // tm
    tile = lambda w: pl.BlockSpec((tm, w), lambda i: (i, 0))
    ctx_tile = lambda w: pl.BlockSpec((tm, w), lambda i: (jnp.minimum(i, ctx_tiles - 1), 0))
    lat_tile = lambda w: pl.BlockSpec((tm, w), lambda i: (jnp.maximum(i - ctx_tiles, 0), 0))
    modrow = pl.BlockSpec((1, 1, D_MODEL), lambda i: (row_of_tile(i), 0, 0))
    full = lambda a: pl.BlockSpec(a.shape, lambda i, _n=a.ndim: (0,) * _n)
    return pl.pallas_call(
        functools.partial(_outproj_kernel, alpha=alpha, ctx_tiles=ctx_tiles),
        grid=(t_all // tm,),
        in_specs=[ctx_tile(RW), lat_tile(RW), ctx_tile(CW), lat_tile(CW), tile(D_MODEL), modrow, modrow, modrow,
                  full(w_oa), full(w_ob), full(ln_g), full(ln_b)],
        out_specs=[tile(D_MODEL), tile(D_MODEL)],
        out_shape=[jax.ShapeDtypeStruct((t_all, D_MODEL), F32), jax.ShapeDtypeStruct((t_all, D_MODEL), F32)],
        compiler_params=_cparams(("arbitrary",)),
        name="out_proj",
    )(ya_c, ya_l, yb_c, yb_l, x_all, g1, sc2, sh2, w_oa, w_ob, ln_g, ln_b)


def _route_kernel(h_ref, wt_hi_ref, wt_lo_ref, bias_ref, ustrict_ref,
                  eidx_ref, wts_ref, pos_ref, cnt_ref, carry_s, *, tm):
    i = pl.program_id(0)

    @pl.when(i == 0)
    def _():
        carry_s[...] = jnp.zeros_like(carry_s)

    h_hi, h_lo = _split_bf16(h_ref[...])
    logits = _dot_nt(wt_hi_ref[...], h_hi) + _dot_nt(wt_hi_ref[...], h_lo) + _dot_nt(wt_lo_ref[...], h_hi)
    scores = _sigmoid(logits)
    sel = scores + bias_ref[...]
    neg = jnp.float32(-jnp.inf)
    gsz = N_EXPERTS // N_GROUPS
    gs = []
    for g in range(N_GROUPS):
        blk = sel[g * gsz:(g + 1) * gsz]
        m1 = jnp.max(blk, axis=0, keepdims=True)
        eq = blk == m1
        cnt = jnp.sum(eq.astype(F32), axis=0, keepdims=True)
        m2 = jnp.max(jnp.where(eq, neg, blk), axis=0, keepdims=True)
        gs.append(m1 + jnp.where(cnt >= 2.0, m1, m2))
    masked = []
    for g in range(N_GROUPS):
        rank = jnp.zeros_like(gs[g])
        for o in range(N_GROUPS):
            if o == g:
                continue
            beats = (gs[o] > gs[g]) | ((gs[o] == gs[g]) & (o < g))
            rank = rank + beats.astype(F32)
        keep = rank < float(TOPK_GROUPS)
        masked.append(jnp.where(keep, sel[g * gsz:(g + 1) * gsz], neg))
    cur = jnp.concatenate(masked, axis=0)
    iota_e = lax.broadcasted_iota(jnp.int32, cur.shape, 0).astype(F32)
    idxs, ws = [], []
    selmask = jnp.zeros(cur.shape, F32)
    for _ in range(TOP_K):
        m = jnp.max(cur, axis=0, keepdims=True)
        idx = jnp.min(jnp.where(cur == m, iota_e, float(N_EXPERTS)), axis=0, keepdims=True)
        onehot = iota_e == idx
        ws.append(jnp.sum(jnp.where(onehot, scores, 0.0), axis=0, keepdims=True))
        idxs.append(idx)
        cur = jnp.where(onehot, neg, cur)
        selmask = jnp.where(onehot, 1.0, selmask)
    pos = _dot(selmask.astype(BF16), ustrict_ref[...]) + carry_s[...]
    carry_s[...] = carry_s[...] + jnp.sum(selmask, axis=1, keepdims=True)
    cnt_ref[...] = carry_s[...]
    wsum = ws[0]
    for k in range(1, TOP_K):
        wsum = wsum + ws[k]
    pks = [jnp.sum(jnp.where(iota_e == idxs[k], pos, 0.0), axis=0, keepdims=True) for k in range(TOP_K)]
    eidx_ref[...] = jnp.concatenate(idxs, axis=0).astype(jnp.int32)
    wts_ref[...] = jnp.concatenate([w / wsum * ROUTED_SCALE for w in ws], axis=0)
    pos_ref[...] = jnp.concatenate(pks, axis=0).astype(jnp.int32)


def _route(h2, wt_hi, wt_lo, bias_col, tm):
    t_all = h2.shape[0]
    ustrict = jnp.asarray(np.triu(np.ones((tm, tm), np.float32), 1), dtype=BF16)
    full = lambda a: pl.BlockSpec(a.shape, lambda i, _n=a.ndim: (0,) * _n)
    tok = pl.BlockSpec((TOP_K, tm), lambda i: (0, i))
    return pl.pallas_call(
        functools.partial(_route_kernel, tm=tm),
        grid=(t_all // tm,),
        in_specs=[pl.BlockSpec((tm, D_MODEL), lambda i: (i, 0)), full(wt_hi), full(wt_lo), full(bias_col),
                  full(ustrict)],
        out_specs=[tok, tok, tok, pl.BlockSpec((N_EXPERTS, 1), lambda i: (0, 0))],
        out_shape=[jax.ShapeDtypeStruct((TOP_K, t_all), jnp.int32), jax.ShapeDtypeStruct((TOP_K, t_all), F32),
                   jax.ShapeDtypeStruct((TOP_K, t_all), jnp.int32), jax.ShapeDtypeStruct((N_EXPERTS, 1), F32)],
        scratch_shapes=[pltpu.VMEM((N_EXPERTS, 1), F32)],
        compiler_params=_cparams(("arbitrary",)),
        name="route",
    )(h2, wt_hi, wt_lo, bias_col, ustrict)


def _expert_kernel(blk_e_ref, nused_ref, tok_ref, tok_next_ref, h_hbm, wgu_ref, wdn_ref, o_ref,
                   xbuf, sems, *, n_blocks):
    i = pl.program_id(0)
    nused = nused_ref[0]

    def gather(tref, slot):
        for j in range(MOE_BLOCK):
            pltpu.make_async_copy(h_hbm.at[pl.ds(tref[0, 0, j], 1), :], xbuf.at[slot, pl.ds(j, 1), :],
                                  sems.at[slot]).start()

    @pl.when(jnp.logical_and(i == 0, nused > 0))
    def _():
        gather(tok_ref, 0)

    @pl.when(i + 1 < nused)
    def _():
        gather(tok_next_ref, (i + 1) % 2)

    @pl.when(i < nused)
    def _():
        slot = i % 2
        pltpu.make_async_copy(h_hbm.at[pl.ds(0, MOE_BLOCK), :], xbuf.at[slot], sems.at[slot]).wait()
        x = xbuf[slot].astype(BF16)
        gu = _dot(x, wgu_ref[0].astype(BF16))
        gate = gu[:, :EXPERT_DIM]
        up = gu[:, EXPERT_DIM:]
        act = (gate * _sigmoid(gate) * up).astype(BF16)
        o_ref[...] = _dot(act, wdn_ref[0].astype(BF16))

    @pl.when(i >= nused)
    def _():
        o_ref[...] = jnp.zeros_like(o_ref)


def _experts(h2, blk_e, nused, row_tok3, w_gu, w_down):
    n_blocks = blk_e.shape[0]
    grid_spec = pltpu.PrefetchScalarGridSpec(
        num_scalar_prefetch=2,
        grid=(n_blocks,),
        in_specs=[pl.BlockSpec((1, 1, MOE_BLOCK), lambda i, be, nu: (i, 0, 0), memory_space=pltpu.SMEM),
                  pl.BlockSpec((1, 1, MOE_BLOCK), lambda i, be, nu: (jnp.minimum(i + 1, n_blocks - 1), 0, 0),
                               memory_space=pltpu.SMEM),
                  pl.BlockSpec(memory_space=pl.ANY),
                  pl.BlockSpec((1, D_MODEL, 2 * EXPERT_DIM), lambda i, be, nu: (be[i], 0, 0)),
                  pl.BlockSpec((1, EXPERT_DIM, D_MODEL), lambda i, be, nu: (be[i], 0, 0))],
        out_specs=pl.BlockSpec((MOE_BLOCK, D_MODEL), lambda i, be, nu: (i, 0)),
        scratch_shapes=[pltpu.VMEM((2, MOE_BLOCK, D_MODEL), F32), pltpu.SemaphoreType.DMA((2,))],
    )
    return pl.pallas_call(
        functools.partial(_expert_kernel, n_blocks=n_blocks),
        grid_spec=grid_spec,
        out_shape=jax.ShapeDtypeStruct((n_blocks * MOE_BLOCK, D_MODEL), F32),
        compiler_params=_cparams(("arbitrary",)),
        name="experts",
    )(blk_e, nused, row_tok3, row_tok3, h2, w_gu, w_down)


TC_TOK = 128


def _combine_kernel(slot_ref, slot_next_ref, wt_ref, x1_ref, h2_ref, g2_ref, sgu_ref, sdn_ref, lg_ref, lb_ref,
                    y_hbm, o_ref, gbuf, sems, *, n_tiles, alpha):
    i = pl.program_id(0)

    def gather(sref, b):
        def body(j, carry):
            for k in range(TOP_K):
                pltpu.make_async_copy(y_hbm.at[pl.ds(sref[0, 0, j * TOP_K + k], 1), :],
                                      gbuf.at[b, k, pl.ds(j, 1), :], sems.at[b]).start()
            return carry
        lax.fori_loop(0, TC_TOK, body, 0)

    @pl.when(i == 0)
    def _():
        gather(slot_ref, 0)

    @pl.when(i + 1 < n_tiles)
    def _():
        gather(slot_next_ref, (i + 1) % 2)

    b = i % 2
    h = h2_ref[...].astype(BF16)
    su = _dot(h, sgu_ref[...])
    sg = su[:, :EXPERT_DIM]
    act = (sg * _sigmoid(sg) * su[:, EXPERT_DIM:]).astype(BF16)
    ffn = _dot(act, sdn_ref[...])
    for k in range(TOP_K):
        pltpu.make_async_copy(y_hbm.at[pl.ds(0, TC_TOK), :], gbuf.at[b, k], sems.at[b]).wait()
    for k in range(TOP_K):
        ffn = ffn + gbuf[b, k] * wt_ref[:, k:k + 1]
    o_ref[...] = _layer_norm(alpha * x1_ref[...] + g2_ref[0] * ffn, lg_ref[...], lb_ref[...])


def _combine(slot3, wts_t, x1, h2, g2, sh_gu, sh_dn, ln_g, ln_b, y_sorted, row_of_tile, alpha):
    t_all = x1.shape[0]
    n_tiles = t_all // TC_TOK
    tile = lambda w: pl.BlockSpec((TC_TOK, w), lambda i: (i, 0))
    full = lambda a: pl.BlockSpec(a.shape, lambda i, _n=a.ndim: (0,) * _n)
    return pl.pallas_call(
        functools.partial(_combine_kernel, n_tiles=n_tiles, alpha=alpha),
        grid=(n_tiles,),
        in_specs=[pl.BlockSpec((1, 1, TC_TOK * TOP_K), lambda i: (i, 0, 0), memory_space=pltpu.SMEM),
                  pl.BlockSpec((1, 1, TC_TOK * TOP_K), lambda i: (jnp.minimum(i + 1, n_tiles - 1), 0, 0),
                               memory_space=pltpu.SMEM),
                  tile(TOP_K), tile(D_MODEL), tile(D_MODEL),
                  pl.BlockSpec((1, 1, D_MODEL), lambda i: (row_of_tile(i), 0, 0)),
                  full(sh_gu), full(sh_dn), full(ln_g), full(ln_b),
                  pl.BlockSpec(memory_space=pl.ANY)],
        out_specs=tile(D_MODEL),
        out_shape=jax.ShapeDtypeStruct((t_all, D_MODEL), F32),
        scratch_shapes=[pltpu.VMEM((2, TOP_K, TC_TOK, D_MODEL), F32), pltpu.SemaphoreType.DMA((2,))],
        compiler_params=_cparams(("arbitrary",)),
        name="combine",
    )(slot3, slot3, wts_t, x1, h2, g2, sh_gu, sh_dn, ln_g, ln_b, y_sorted)


def kernel(x_prompt, x_sample, state_rwkv, c, c_ctx, w_ada, b_ada, w_in, mu_shift, w_decay0, w_decay2, w_a0, w_a2, w_g2, k_k, k_a, r_k, lnx_g, lnx_b, conv_w, conv_b, conv_ln_g, conv_ln_b, w_out, ln1_g, ln1_b, router_w, router_bias, expert_w_gu, expert_w_down, shared_w_gu, shared_w_down, ln2_g, ln2_b):
    depth = w_ada.shape[0]
    assert depth == 1
    alpha = (2.0 * depth) ** 0.25
    n_ctx, t_ctx, _ = x_prompt.shape
    n_lat, t_lat, _ = x_sample.shape
    tok_ctx = n_ctx * t_ctx
    tok_lat = n_lat * t_lat
    t_all = tok_ctx + tok_lat
    tm = 1024
    assert tok_ctx % tm == 0 and t_lat % tm == 0 and t_lat // GRID_W * GRID_W == t_lat
    l = 0

    x_all = jnp.concatenate([x_prompt.reshape(tok_ctx, D_MODEL), x_sample.reshape(tok_lat, D_MODEL)], axis=0)

    cond8 = jnp.zeros((8, D_MODEL), F32).at[0].set(c_ctx).at[1:1 + n_lat].set(c)
    mod = _ada(cond8, w_ada[l], b_ada[l])
    sh1, sc1, g1, sh2, sc2, g2 = [m.reshape(8, 1, D_MODEL) for m in jnp.split(mod, 6, axis=-1)]

    def row_of_tile_for(tile_rows):
        ctx_tiles = tok_ctx // tile_rows
        per_seq = t_lat // tile_rows
        return lambda i: jnp.where(i < ctx_tiles, 0, 1 + (i - ctx_tiles) // per_seq)

    w_in_a = w_in[l][:, :RWKV_IN].astype(BF16)
    w_in_b = w_in[l][:, RWKV_IN:].astype(BF16)
    proj_a, proj_b = _in_proj(x_all, sc1, sh1, w_in_a, w_in_b, tm, row_of_tile_for(tm))

    zeros64 = jnp.zeros((64, 2 * RW), F32)
    wd2 = jnp.concatenate([w_decay2[l, 0], w_decay2[l, 1]], axis=1)
    wa2 = jnp.concatenate([w_a2[l, 0], w_a2[l, 1]], axis=1)
    w_lora = jnp.concatenate([jnp.concatenate([wd2, zeros64], axis=1),
                              jnp.concatenate([zeros64, wa2], axis=1)], axis=0).astype(BF16)
    row = lambda v: v.reshape(1, -1)
    rwkv_w = [row(mu_shift[l]), w_lora, w_g2[l].astype(BF16), row(w_decay0[l]), row(w_a0[l]),
              row(k_k[l]), row(k_a[l]), row(r_k[l]), row(lnx_g[l]), row(lnx_b[l])]
    consts = _rwkv_consts()

    def to_cat(s):
        b = s.shape[0]
        s = s.reshape(b, 2, 2, 4, HD, HD)
        return jnp.transpose(s, (0, 1, 2, 5, 3, 4)).reshape(b, 4, HD, GW)

    def from_cat(s):
        b = s.shape[0]
        s = s.reshape(b, 2, 2, HD, 4, HD)
        return jnp.transpose(s, (0, 1, 2, 4, 5, 3)).reshape(b, 2, 8, HD, HD)

    s0_ctx = jnp.zeros((n_ctx, 4, HD, GW), F32)
    s0_lat = to_cat(state_rwkv[:, l].astype(F32))
    ya_c, sfin_ctx = _rwkv(proj_a, s0_ctx, rwkv_w, consts, t_ctx, n_ctx, 0)
    ya_l, _ = _rwkv(proj_a, s0_lat, rwkv_w, consts, t_lat, n_lat, tok_ctx // t_lat)
    new_state = from_cat(sfin_ctx)[:, None].astype(x_prompt.dtype)

    conv_wts = [conv_w[l], row(conv_b[l]), row(conv_ln_g[l]), row(conv_ln_b[l])]
    yb_c = _conv(proj_b, conv_wts, t_ctx, n_ctx, 0, False)
    yb_l = _conv(proj_b, conv_wts, t_lat, n_lat, tok_ctx // t_lat, True)

    w_oa = w_out[l][:RW].astype(BF16)
    w_ob = w_out[l][RW:].astype(BF16)
    x1, h2 = _out_proj(ya_c, ya_l, yb_c, yb_l, x_all, g1, sc2, sh2, w_oa, w_ob, row(ln1_g[l]), row(ln1_b[l]), tm,
                       row_of_tile_for(tm), alpha)

    rwt = router_w[l].T
    rwt_hi = rwt.astype(BF16)
    rwt_lo = (rwt - rwt_hi.astype(F32)).astype(BF16)
    eidx, wts, pos, counts = _route(h2, rwt_hi, rwt_lo, router_bias[l].reshape(N_EXPERTS, 1), 512)

    n_assign = t_all * TOP_K
    n_blocks = (n_assign + N_EXPERTS * (MOE_BLOCK - 1) + MOE_BLOCK - 1) // MOE_BLOCK
    cnt = counts.reshape(N_EXPERTS).astype(jnp.int32)
    padded = (cnt + MOE_BLOCK - 1) // MOE_BLOCK * MOE_BLOCK
    pend = jnp.cumsum(padded)
    pstart = pend - padded
    slot = pstart[eidx] + pos
    tok_ids = jnp.broadcast_to(jnp.arange(t_all, dtype=jnp.int32)[None, :], slot.shape)
    row_tok = jnp.zeros((n_blocks * MOE_BLOCK,), jnp.int32).at[slot.reshape(-1)].set(tok_ids.reshape(-1))
    blk_e = jnp.minimum(jnp.searchsorted(pend, jnp.arange(n_blocks, dtype=jnp.int32) * MOE_BLOCK, side='right'),
                        N_EXPERTS - 1).astype(jnp.int32)
    nused = (pend[-1] // MOE_BLOCK).astype(jnp.int32).reshape(1)

    y_sorted = _experts(h2, blk_e, nused, row_tok.reshape(n_blocks, 1, MOE_BLOCK), expert_w_gu[l], expert_w_down[l])

    slot3 = slot.T.reshape(t_all // TC_TOK, 1, TC_TOK * TOP_K)
    y_all = _combine(slot3, wts.T, x1, h2, g2, shared_w_gu[l].astype(BF16), shared_w_down[l].astype(BF16),
                     row(ln2_g[l]), row(ln2_b[l]), y_sorted, row_of_tile_for(TC_TOK), alpha)

    y_prompt = y_all[:tok_ctx].reshape(n_ctx, t_ctx, D_MODEL)
    y_sample = y_all[tok_ctx:].reshape(n_lat, t_lat, D_MODEL)
    return (y_prompt, y_sample, new_state)
```

```python
import functools
import math

import numpy as np
import jax
import jax.numpy as jnp
from jax import lax
from jax.experimental import pallas as pl
from jax.experimental.pallas import tpu as pltpu

F32 = jnp.float32
BF16 = jnp.bfloat16

D_MODEL = 1024
RW = 512
HD = 64
CW = 512
CONV_K = 31
RWKV_IN = 3 * RW + 64 + 64 + 128
N_EXPERTS = 256
TOP_K = 8
N_GROUPS = 8
TOPK_GROUPS = 4
EXPERT_DIM = 256
ROUTED_SCALE = 2.5
MOE_BLOCK = 128
GRID_W = 64
LN_EPS = 1e-5
GN_EPS = 64e-5
CHUNK = 64
GW = 256
DECAY_SCALE = math.exp(-0.5)
VMEM_LIMIT = 52 * 1024 * 1024


def _cparams(sem):
    return pltpu.CompilerParams(dimension_semantics=sem, vmem_limit_bytes=VMEM_LIMIT)


def _split_bf16(x):
    hi = x.astype(BF16)
    lo = (x - hi.astype(F32)).astype(BF16)
    return hi, lo


def _dot(a, b):
    return jnp.dot(a, b, preferred_element_type=F32)


def _dot_nt(a, b):
    return lax.dot_general(a, b, (((1,), (1,)), ((), ())), preferred_element_type=F32)


def _dot_tn(a, b):
    return lax.dot_general(a, b, (((0,), (0,)), ((), ())), preferred_element_type=F32)


def _sigmoid(x):
    return 1.0 / (1.0 + jnp.exp(-x))


def _layer_norm(x, g, b):
    mu = jnp.mean(x, axis=-1, keepdims=True)
    d = x - mu
    var = jnp.mean(d * d, axis=-1, keepdims=True)
    return d * lax.rsqrt(var + LN_EPS) * g + b


def _ada_kernel(c_ref, w_ref, b_ref, o_ref):
    c = c_ref[...]
    s = c * _sigmoid(c)
    s_hi, s_lo = _split_bf16(s)
    w_hi, w_lo = _split_bf16(w_ref[...])
    o_ref[...] = _dot(s_hi, w_hi) + _dot(s_lo, w_hi) + _dot(s_hi, w_lo) + b_ref[...]


def _ada(cond8, w_ada, b_ada):
    n = w_ada.shape[1]
    tn = 1536
    return pl.pallas_call(
        _ada_kernel,
        grid=(n // tn,),
        in_specs=[pl.BlockSpec((8, D_MODEL), lambda j: (0, 0)),
                  pl.BlockSpec((D_MODEL, tn), lambda j: (0, j)),
                  pl.BlockSpec((1, tn), lambda j: (0, j))],
        out_specs=pl.BlockSpec((8, tn), lambda j: (0, j)),
        out_shape=jax.ShapeDtypeStruct((8, n), F32),
        compiler_params=_cparams(("arbitrary",)),
        name="ada",
    )(cond8, w_ada, b_ada.reshape(1, n))


def _inproj_kernel(x_ref, sc_ref, sh_ref, wa_ref, wb_ref, oa_ref, ob_ref):
    h = (x_ref[...] * (1.0 + sc_ref[0]) + sh_ref[0]).astype(BF16)
    oa_ref[...] = _dot(h, wa_ref[...])
    ob_ref[...] = _dot(h, wb_ref[...])


def _in_proj(x_all, sc1, sh1, w_a, w_b, tm, row_of_tile):
    t_all = x_all.shape[0]
    return pl.pallas_call(
        _inproj_kernel,
        grid=(t_all // tm,),
        in_specs=[pl.BlockSpec((tm, D_MODEL), lambda i: (i, 0)),
                  pl.BlockSpec((1, 1, D_MODEL), lambda i: (row_of_tile(i), 0, 0)),
                  pl.BlockSpec((1, 1, D_MODEL), lambda i: (row_of_tile(i), 0, 0)),
                  pl.BlockSpec(w_a.shape, lambda i: (0, 0)),
                  pl.BlockSpec(w_b.shape, lambda i: (0, 0))],
        out_specs=[pl.BlockSpec((tm, RWKV_IN), lambda i: (i, 0)),
                   pl.BlockSpec((tm, 2 * CW), lambda i: (i, 0))],
        out_shape=[jax.ShapeDtypeStruct((t_all, RWKV_IN), F32),
                   jax.ShapeDtypeStruct((t_all, 2 * CW), F32)],
        compiler_params=_cparams(("arbitrary",)),
        name="in_proj",
    )(x_all, sc1, sh1, w_a, w_b)


def _rwkv_consts():
    c = CHUNK
    t = np.arange(c)[:, None]
    s = np.arange(GW)[None, :] % c
    sl = (s < t).astype(np.float32)
    il = (s <= t).astype(np.float32)
    su = (s > t).astype(np.float32)
    iu = (s >= t).astype(np.float32)
    eye_cat = (s == t).astype(np.float32)
    tri = np.stack([sl, il, su, iu, eye_cat])
    r = np.arange(GW)
    mask_bd = (r[:, None] // HD == r[None, :] // HD).astype(np.float32)
    eye = np.eye(GW, dtype=np.float32)
    cum = np.stack([np.tril(np.ones((c, c), np.float32)), np.triu(np.ones((c, c), np.float32))])
    q = np.arange(RW)
    seg = (q[:, None] // HD == q[None, :] // HD).astype(np.float32)
    return (jnp.asarray(tri), jnp.asarray(mask_bd, dtype=BF16), jnp.asarray(mask_bd), jnp.asarray(eye),
            jnp.asarray(cum, dtype=BF16), jnp.asarray(seg, dtype=BF16))


def _rwkv_kernel(p_ref, s0_ref, mu_ref, wlora_ref, wg2_ref, wd0_ref, wa0_ref, kk_ref, ka_ref, rk_ref,
                 lng_ref, lnb_ref, tri_ref, mbd16_ref, mbd_ref, eye_ref, cum_ref, seg_ref,
                 ya_ref, sfin_ref,
                 r_s, v_s, kk_s, g_s, bon_s, a_s, lw_s, kd_s, y_s, st_s, *, seq_len):
    nc = seq_len // CHUNK
    seg = seg_ref[...]

    def seg_sum(x):
        hi, lo = _split_bf16(x)
        return _dot(hi, seg) + _dot(lo, seg)

    def phase_a(c, carry):
        t0 = pl.multiple_of(c * CHUNK, CHUNK)
        cur = p_ref[pl.ds(t0, CHUNK), :]
        prev_row = p_ref[pl.ds(jnp.maximum(t0 - 1, 0), 1), :] * (t0 > 0).astype(F32)
        next_row = p_ref[pl.ds(jnp.minimum(t0 + CHUNK, seq_len - 1), 1), :] * (t0 + CHUNK < seq_len).astype(F32)
        row = lax.broadcasted_iota(jnp.int32, cur.shape, 0)
        prev = jnp.where(row == 0, prev_row, pltpu.roll(cur, 1, 0))
        nxt = jnp.where(row == CHUNK - 1, next_row, pltpu.roll(cur, CHUNK - 1, 0))
        p = cur + mu_ref[...] * (0.5 * (prev + nxt) - cur)
        r = p[:, 0:RW]
        k = p[:, RW:2 * RW]
        v = p[:, 2 * RW:3 * RW]
        z = p[:, 3 * RW:3 * RW + 128]
        gd = p[:, 3 * RW + 128:3 * RW + 256]
        lane = lax.broadcasted_iota(jnp.int32, z.shape, 1)
        z = jnp.where(lane < 64, jnp.tanh(z), z)
        lora = _dot(z.astype(BF16), wlora_ref[...])
        g = _dot(_sigmoid(gd).astype(BF16), wg2_ref[...])
        kks = k * kk_ref[...]
        kk = kks * lax.rsqrt(seg_sum(kks * kks) + 1e-12)
        rows = pl.ds(t0, CHUNK)
        r_s[rows, :] = r
        v_s[rows, :] = v
        kk_s[rows, :] = kk
        g_s[rows, :] = g
        kd_sum = jnp.zeros_like(k)
        for d in range(2):
            w_logit = wd0_ref[:, d * RW:(d + 1) * RW] + lora[:, d * RW:(d + 1) * RW]
            a = _sigmoid(wa0_ref[:, d * RW:(d + 1) * RW] + lora[:, (2 + d) * RW:(3 + d) * RW])
            kd = k * (1.0 + (a - 1.0) * ka_ref[...])
            lw_s[d, rows, :] = -DECAY_SCALE * _sigmoid(w_logit)
            a_s[d, rows, :] = a
            kd_s[d, rows, :] = kd
            kd_sum = kd_sum + kd
        bon_s[rows, :] = seg_sum(r * kd_sum * rk_ref[...]) * v
        return carry

    lax.fori_loop(0, nc, phase_a, 0)

    mbd16 = mbd16_ref[...]
    mbd = mbd_ref[...]
    eye = eye_ref[...]

    def bd(x16):
        return jnp.concatenate([x16, x16, x16, x16], axis=0) * mbd16

    for u in range(4):
        s0 = s0_ref[0, u]
        st_s[u] = jnp.concatenate([s0, s0, s0, s0], axis=0) * mbd

    def unit(d, grp, t0):
        u = d * 2 + grp
        rows = pl.ds(t0, CHUNK)
        lanes = slice(grp * GW, (grp + 1) * GW)
        strict = tri_ref[2 * d]
        incl = tri_ref[2 * d + 1]
        lw = lw_s[d, rows, lanes]
        hi, lo = _split_bf16(lw)
        cl = _dot(cum_ref[d], hi) + _dot(cum_ref[d], lo)
        e_cl = jnp.exp(cl)
        e_ce = jnp.exp(cl - lw)
        e_ncl = jnp.exp(-cl)
        tot = cl[CHUNK - 1:CHUNK, :] if d == 0 else cl[0:1, :]
        wc = jnp.exp(tot)
        kk = kk_s[rows, lanes]
        vv = v_s[rows, lanes]
        akk = a_s[d, rows, lanes] * kk
        kq = kk * e_ce
        rt = r_s[rows, lanes] * e_cl
        bt = akk * e_ncl
        kt = kd_s[d, rows, lanes] * e_ncl
        btw16 = (bt * wc).astype(BF16)
        ktw16 = (kt * wc).astype(BF16)
        kq16 = kq.astype(BF16)
        v16 = vv.astype(BF16)
        g16 = jnp.concatenate([kq16, rt.astype(BF16)], axis=0)
        out_b = _dot_nt(g16, bd(bt.astype(BF16)))
        out_k = _dot_nt(g16, bd(kt.astype(BF16)))
        a_bk = out_b[:CHUNK] * strict
        p_rb = (out_b[CHUNK:] * incl).astype(BF16)
        a_kk = (out_k[:CHUNK] * strict).astype(BF16)
        p_rk = (out_k[CHUNK:] * incl).astype(BF16)
        tinv = tri_ref[4] - a_bk
        pw = a_bk.astype(BF16)
        for _ in range(5):
            pw2 = _dot(pw, bd(pw))
            pw = pw2.astype(BF16)
            tinv = tinv + _dot(pw, bd(tinv.astype(BF16)))
        bdv = bd(v16)
        akkv = _dot(a_kk, bdv)
        x = _dot(tinv.astype(BF16), jnp.concatenate([bd(kq16), bd(akkv.astype(BF16))], axis=1))
        qh16 = x[:, :GW].astype(BF16)
        u016 = x[:, GW:].astype(BF16)
        tn1 = _dot_tn(btw16, jnp.concatenate([qh16, u016], axis=1))
        tn2 = _dot_tn(ktw16, v16)
        mc = (eye * wc - tn1[:, :GW]) * mbd
        ncm = (tn2 - tn1[:, GW:]) * mbd
        r1 = _dot(p_rb, jnp.concatenate([bd(qh16), bd(u016)], axis=1))
        rh = rt - r1[:, :GW]
        y0 = _dot(p_rk, bdv) - r1[:, GW:]
        st16 = st_s[u].astype(BF16)
        big = _dot(jnp.concatenate([rh.astype(BF16), mc.astype(BF16)], axis=0), st16)
        y_s[d, rows, lanes] = big[:CHUNK] + y0
        st_s[u] = big[CHUNK:] + ncm

    def phase_b(i, carry):
        tf = pl.multiple_of(i * CHUNK, CHUNK)
        tb = pl.multiple_of((nc - 1 - i) * CHUNK, CHUNK)
        for grp in range(2):
            unit(0, grp, tf)
            unit(1, grp, tb)
        return carry

    lax.fori_loop(0, nc, phase_b, 0)

    for u in range(4):
        st = st_s[u]
        sfin_ref[0, u] = st[0:HD] + st[HD:2 * HD] + st[2 * HD:3 * HD] + st[3 * HD:4 * HD]

    def phase_c(c, carry):
        rows = pl.ds(pl.multiple_of(c * CHUNK, CHUNK), CHUNK)
        y = y_s[0, rows, :] + y_s[1, rows, :]
        mean = seg_sum(y) * (1.0 / HD)
        dlt = y - mean
        var = seg_sum(dlt * dlt) * (1.0 / HD)
        yn = dlt * lax.rsqrt(var + GN_EPS)
        ya_ref[rows, :] = (yn * lng_ref[...] + lnb_ref[...] + bon_s[rows, :]) * g_s[rows, :]
        return carry

    lax.fori_loop(0, nc, phase_c, 0)


def _rwkv(proj_a, s0_cat, wts, consts, seq_len, n_seq, first_block):
    full = lambda a: pl.BlockSpec(a.shape, lambda s, _n=a.ndim: (0,) * _n)
    in_specs = [pl.BlockSpec((seq_len, RWKV_IN), lambda s: (first_block + s, 0)),
                pl.BlockSpec((1, 4, HD, GW), lambda s: (s, 0, 0, 0))]
    in_specs += [full(a) for a in wts] + [full(a) for a in consts]
    args = [proj_a, s0_cat] + list(wts) + list(consts)
    seq = (seq_len, RW)
    return pl.pallas_call(
        functools.partial(_rwkv_kernel, seq_len=seq_len),
        grid=(n_seq,),
        in_specs=in_specs,
        out_specs=[pl.BlockSpec((seq_len, RW), lambda s: (s, 0)),
                   pl.BlockSpec((1, 4, HD, GW), lambda s: (s, 0, 0, 0))],
        out_shape=[jax.ShapeDtypeStruct((n_seq * seq_len, RW), F32),
                   jax.ShapeDtypeStruct((n_seq, 4, HD, GW), F32)],
        scratch_shapes=[pltpu.VMEM(seq, F32), pltpu.VMEM(seq, F32), pltpu.VMEM(seq, F32), pltpu.VMEM(seq, F32),
                        pltpu.VMEM(seq, F32),
                        pltpu.VMEM((2,) + seq, F32), pltpu.VMEM((2,) + seq, F32), pltpu.VMEM((2,) + seq, F32),
                        pltpu.VMEM((2,) + seq, F32), pltpu.VMEM((4, GW, GW), F32)],
        compiler_params=_cparams(("arbitrary",)),
        name="rwkv_%d" % seq_len,
    )(*args)


CONV_PAD = 16
VPAD = (CONV_K // 2) * GRID_W


def _conv_kernel(u_ref, w_ref, b_ref, g_ref, beta_ref, o_ref, pad_s, vpad_s, y_s, *, seq_len, grid):
    rb = 256
    glu = u_ref[:, :CW] * _sigmoid(u_ref[:, CW:])
    zeros = jnp.zeros((CONV_PAD, CW), F32)
    pad_s[0:CONV_PAD, :] = zeros
    pad_s[CONV_PAD + seq_len:2 * CONV_PAD + seq_len, :] = zeros
    pad_s[CONV_PAD:CONV_PAD + seq_len, :] = glu
    half = CW // 2
    if grid:
        vz = jnp.zeros((VPAD, half), F32)
        vpad_s[0:VPAD, :] = vz
        vpad_s[VPAD + seq_len:2 * VPAD + seq_len, :] = vz
        vpad_s[VPAD:VPAD + seq_len, :] = glu[:, half:]
    n_h_lanes = (half if grid else CW) // 128
    for r0 in range(0, seq_len, rb):
        for lb in range(n_h_lanes):
            lanes = slice(lb * 128, (lb + 1) * 128)
            acc = jnp.zeros((rb, 128), F32)
            if grid:
                col = lax.broadcasted_iota(jnp.int32, (rb, 128), 0) % GRID_W
            for k in range(CONV_K):
                win = pad_s[r0 + k + 1:r0 + k + 1 + rb, lanes]
                term = win * w_ref[k:k + 1, lanes]
                if grid:
                    src = col + (k - CONV_K // 2)
                    term = jnp.where((src >= 0) & (src < GRID_W), term, 0.0)
                acc = acc + term
            y_s[r0:r0 + rb, lanes] = acc
        if grid:
            for lb in range(half // 128):
                lanes = slice(lb * 128, (lb + 1) * 128)
                wl = slice(half + lb * 128, half + (lb + 1) * 128)
                acc = jnp.zeros((rb, 128), F32)
                for k in range(CONV_K):
                    start = VPAD + r0 + (k - CONV_K // 2) * GRID_W
                    acc = acc + vpad_s[start:start + rb, lanes] * w_ref[k:k + 1, wl]
                y_s[r0:r0 + rb, wl] = acc
    for r0 in range(0, seq_len, rb):
        y = y_s[r0:r0 + rb, :] + b_ref[...]
        yl = _layer_norm(y, g_ref[...], beta_ref[...])
        o_ref[r0:r0 + rb, :] = yl * _sigmoid(yl)


def _conv(proj_b, wts, seq_len, n_seq, first_block, grid):
    full = lambda a: pl.BlockSpec(a.shape, lambda s, _n=a.ndim: (0,) * _n)
    in_specs = [pl.BlockSpec((seq_len, 2 * CW), lambda s: (first_block + s, 0))] + [full(a) for a in wts]
    args = [proj_b] + list(wts)
    return pl.pallas_call(
        functools.partial(_conv_kernel, seq_len=seq_len, grid=grid),
        grid=(n_seq,),
        in_specs=in_specs,
        out_specs=pl.BlockSpec((seq_len, CW), lambda s: (s, 0)),
        out_shape=jax.ShapeDtypeStruct((n_seq * seq_len, CW), F32),
        scratch_shapes=[pltpu.VMEM((seq_len + 2 * CONV_PAD, CW), F32),
                        pltpu.VMEM((seq_len + 2 * VPAD if grid else 8, CW // 2), F32),
                        pltpu.VMEM((seq_len, CW), F32)],
        compiler_params=_cparams(("arbitrary",)),
        name="conv_%d" % seq_len,
    )(*args)


def _outproj_kernel(yac_ref, yal_ref, ybc_ref, ybl_ref, x_ref, g1_ref, sc2_ref, sh2_ref, wa_ref, wb_ref,
                    lg_ref, lb_ref, x1_ref, h2_ref, *, alpha, ctx_tiles):
    is_ctx = pl.program_id(0) < ctx_tiles
    ya = jnp.where(is_ctx, yac_ref[...], yal_ref[...])
    yb = jnp.where(is_ctx, ybc_ref[...], ybl_ref[...])
    mix = _dot(ya.astype(BF16), wa_ref[...]) + _dot(yb.astype(BF16), wb_ref[...])
    x1 = _layer_norm(alpha * x_ref[...] + g1_ref[0] * mix, lg_ref[...], lb_ref[...])
    x1_ref[...] = x1
    h2_ref[...] = x1 * (1.0 + sc2_ref[0]) + sh2_ref[0]


def _out_proj(ya_c, ya_l, yb_c, yb_l, x_all, g1, sc2, sh2, w_oa, w_ob, ln_g, ln_b, tm, row_of_tile, alpha):
    t_all = x_all.shape[0]
    ctx_tiles = ya_c.shape[0] // tm
    tile = lambda w: pl.BlockSpec((tm, w), lambda i: (i, 0))
    ctx_tile = lambda w: pl.BlockSpec((tm, w), lambda i: (jnp.minimum(i, ctx_tiles - 1), 0))
    lat_tile = lambda w: pl.BlockSpec((tm, w), lambda i: (jnp.maximum(i - ctx_tiles, 0), 0))
    modrow = pl.BlockSpec((1, 1, D_MODEL), lambda i: (row_of_tile(i), 0, 0))
    full = lambda a: pl.BlockSpec(a.shape, lambda i, _n=a.ndim: (0,) * _n)
    return pl.pallas_call(
        functools.partial(_outproj_kernel, alpha=alpha, ctx_tiles=ctx_tiles),
        grid=(t_all // tm,),
        in_specs=[ctx_tile(RW), lat_tile(RW), ctx_tile(CW), lat_tile(CW), tile(D_MODEL), modrow, modrow, modrow,
                  full(w_oa), full(w_ob), full(ln_g), full(ln_b)],
        out_specs=[tile(D_MODEL), tile(D_MODEL)],
        out_shape=[jax.ShapeDtypeStruct((t_all, D_MODEL), F32), jax.ShapeDtypeStruct((t_all, D_MODEL), F32)],
        compiler_params=_cparams(("arbitrary",)),
        name="out_proj",
    )(ya_c, ya_l, yb_c, yb_l, x_all, g1, sc2, sh2, w_oa, w_ob, ln_g, ln_b)


def _route_kernel(h_ref, wt_hi_ref, wt_lo_ref, bias_ref, ustrict_ref,
                  eidx_ref, wts_ref, pos_ref, cnt_ref, carry_s, *, tm):
    i = pl.program_id(0)

    @pl.when(i == 0)
    def _():
        carry_s[...] = jnp.zeros_like(carry_s)

    h_hi, h_lo = _split_bf16(h_ref[...])
    logits = _dot_nt(wt_hi_ref[...], h_hi) + _dot_nt(wt_hi_ref[...], h_lo) + _dot_nt(wt_lo_ref[...], h_hi)
    scores = _sigmoid(logits)
    sel = scores + bias_ref[...]
    neg = jnp.float32(-jnp.inf)
    gsz = N_EXPERTS // N_GROUPS
    gs = []
    for g in range(N_GROUPS):
        blk = sel[g * gsz:(g + 1) * gsz]
        m1 = jnp.max(blk, axis=0, keepdims=True)
        eq = blk == m1
        cnt = jnp.sum(eq.astype(F32), axis=0, keepdims=True)
        m2 = jnp.max(jnp.where(eq, neg, blk), axis=0, keepdims=True)
        gs.append(m1 + jnp.where(cnt >= 2.0, m1, m2))
    masked = []
    for g in range(N_GROUPS):
        rank = jnp.zeros_like(gs[g])
        for o in range(N_GROUPS):
            if o == g:
                continue
            beats = (gs[o] > gs[g]) | ((gs[o] == gs[g]) & (o < g))
            rank = rank + beats.astype(F32)
        keep = rank < float(TOPK_GROUPS)
        masked.append(jnp.where(keep, sel[g * gsz:(g + 1) * gsz], neg))
    cur = jnp.concatenate(masked, axis=0)
    iota_e = lax.broadcasted_iota(jnp.int32, cur.shape, 0).astype(F32)
    idxs, ws = [], []
    selmask = jnp.zeros(cur.shape, F32)
    for _ in range(TOP_K):
        m = jnp.max(cur, axis=0, keepdims=True)
        idx = jnp.min(jnp.where(cur == m, iota_e, float(N_EXPERTS)), axis=0, keepdims=True)
        onehot = iota_e == idx
        ws.append(jnp.sum(jnp.where(onehot, scores, 0.0), axis=0, keepdims=True))
        idxs.append(idx)
        cur = jnp.where(onehot, neg, cur)
        selmask = jnp.where(onehot, 1.0, selmask)
    pos = _dot(selmask.astype(BF16), ustrict_ref[...]) + carry_s[...]
    carry_s[...] = carry_s[...] + jnp.sum(selmask, axis=1, keepdims=True)
    cnt_ref[...] = carry_s[...]
    wsum = ws[0]
    for k in range(1, TOP_K):
        wsum = wsum + ws[k]
    pks = [jnp.sum(jnp.where(iota_e == idxs[k], pos, 0.0), axis=0, keepdims=True) for k in range(TOP_K)]
    eidx_ref[...] = jnp.concatenate(idxs, axis=0).astype(jnp.int32)
    wts_ref[...] = jnp.concatenate([w / wsum * ROUTED_SCALE for w in ws], axis=0)
    pos_ref[...] = jnp.concatenate(pks, axis=0).astype(jnp.int32)


def _route(h2, wt_hi, wt_lo, bias_col, tm):
    t_all = h2.shape[0]
    ustrict = jnp.asarray(np.triu(np.ones((tm, tm), np.float32), 1), dtype=BF16)
    full = lambda a: pl.BlockSpec(a.shape, lambda i, _n=a.ndim: (0,) * _n)
    tok = pl.BlockSpec((TOP_K, tm), lambda i: (0, i))
    return pl.pallas_call(
        functools.partial(_route_kernel, tm=tm),
        grid=(t_all // tm,),
        in_specs=[pl.BlockSpec((tm, D_MODEL), lambda i: (i, 0)), full(wt_hi), full(wt_lo), full(bias_col),
                  full(ustrict)],
        out_specs=[tok, tok, tok, pl.BlockSpec((N_EXPERTS, 1), lambda i: (0, 0))],
        out_shape=[jax.ShapeDtypeStruct((TOP_K, t_all), jnp.int32), jax.ShapeDtypeStruct((TOP_K, t_all), F32),
                   jax.ShapeDtypeStruct((TOP_K, t_all), jnp.int32), jax.ShapeDtypeStruct((N_EXPERTS, 1), F32)],
        scratch_shapes=[pltpu.VMEM((N_EXPERTS, 1), F32)],
        compiler_params=_cparams(("arbitrary",)),
        name="route",
    )(h2, wt_hi, wt_lo, bias_col, ustrict)


def _slots_kernel(eidx_ref, pos_ref, pstart_ref, slotx_ref, sloty_ref):
    tm = eidx_ref.shape[1]
    iota_e = lax.broadcasted_iota(jnp.int32, (N_EXPERTS, tm), 0)
    for c, out in enumerate((slotx_ref, sloty_ref)):
        rows = []
        for k in range(TOP_K):
            onehot = iota_e == eidx_ref[k:k + 1, :]
            rows.append(jnp.sum(jnp.where(onehot, pstart_ref[:, c:c + 1], 0.0), axis=0, keepdims=True))
        out[...] = jnp.concatenate(rows, axis=0).astype(jnp.int32) + pos_ref[...]


def _slots(eidx, pos, pstart_cols, tm):
    t_all = eidx.shape[1]
    tok = pl.BlockSpec((TOP_K, tm), lambda i: (0, i))
    return pl.pallas_call(
        _slots_kernel,
        grid=(t_all // tm,),
        in_specs=[tok, tok, pl.BlockSpec((N_EXPERTS, 2), lambda i: (0, 0))],
        out_specs=[tok, tok],
        out_shape=[jax.ShapeDtypeStruct((TOP_K, t_all), jnp.int32), jax.ShapeDtypeStruct((TOP_K, t_all), jnp.int32)],
        compiler_params=_cparams(("arbitrary",)),
        name="slots",
    )(eidx, pos, pstart_cols)


TD_TOK = 128
XS_ALIGN = 8


def _dispatch_kernel(fstart_ref, flen_ref, total_ref, slot_ref, h_ref, xs_out, zero_s, sem, zsem, *, xs_rows):
    i = pl.program_id(0)

    def body(j, carry):
        for k in range(TOP_K):
            pltpu.make_async_copy(h_ref.at[pl.ds(j, 1), :], xs_out.at[pl.ds(slot_ref[0, 0, j * TOP_K + k], 1), :],
                                  sem).start(priority=k % 2)
        return carry

    lax.fori_loop(0, TD_TOK, body, 0)
    for k in range(TOP_K):
        pltpu.make_async_copy(h_ref, xs_out.at[pl.ds(0, TD_TOK), :], sem).wait()

    @pl.when(i == pl.num_programs(0) - 1)
    def _():
        zero_s[...] = jnp.zeros_like(zero_s)
        total = total_ref[0]
        n_tail = (xs_rows - total) // XS_ALIGN

        def gap_copy(e, r):
            return pltpu.make_async_copy(zero_s.at[pl.ds(0, 1), :], xs_out.at[pl.ds(fstart_ref[e] + r, 1), :], zsem)

        def tail_copy(q):
            start = pl.multiple_of(total + q * XS_ALIGN, XS_ALIGN)
            return pltpu.make_async_copy(zero_s, xs_out.at[pl.ds(start, XS_ALIGN), :], zsem)

        def for_gaps(fn):
            def per_expert(e, carry):
                for r in range(XS_ALIGN - 1):
                    @pl.when(r < flen_ref[e])
                    def _():
                        fn(gap_copy(e, r))
                return carry
            lax.fori_loop(0, N_EXPERTS, per_expert, 0)

        def for_tail(fn):
            def per_q(q, carry):
                fn(tail_copy(q))
                return carry
            lax.fori_loop(0, n_tail, per_q, 0)

        for_gaps(lambda cp: cp.start())
        for_tail(lambda cp: cp.start())
        for_gaps(lambda cp: cp.wait())
        for_tail(lambda cp: cp.wait())


def _dispatch(slot3, h2, fstart, flen, total, xs_rows):
    t_all = h2.shape[0]
    grid_spec = pltpu.PrefetchScalarGridSpec(
        num_scalar_prefetch=3,
        grid=(t_all // TD_TOK,),
        in_specs=[pl.BlockSpec((1, 1, TD_TOK * TOP_K), lambda i, a, b, c: (i, 0, 0), memory_space=pltpu.SMEM),
                  pl.BlockSpec((TD_TOK, D_MODEL), lambda i, a, b, c: (i, 0))],
        out_specs=pl.BlockSpec(memory_space=pl.ANY),
        scratch_shapes=[pltpu.VMEM((XS_ALIGN, D_MODEL), F32), pltpu.SemaphoreType.DMA(()),
                        pltpu.SemaphoreType.DMA(())],
    )
    return pl.pallas_call(
        functools.partial(_dispatch_kernel, xs_rows=xs_rows),
        grid_spec=grid_spec,
        out_shape=jax.ShapeDtypeStruct((xs_rows, D_MODEL), F32),
        compiler_params=_cparams(("arbitrary",)),
        name="dispatch",
    )(fstart, flen, total, slot3, h2)


def _expert_kernel(bstart_ref, nblk_ref, nused_ref, xrow_ref, xs_hbm, wgu_ref, wdn_ref, ys_hbm,
                   xbuf, ybuf, wgu16, wdn16, sem_in, sem_out, *, n_blocks):
    e = pl.program_id(0)
    nused = nused_ref[0]
    n = nblk_ref[e]
    g0 = bstart_ref[e]

    def in_copy(g, s):
        start = pl.multiple_of(xrow_ref[g], XS_ALIGN)
        return pltpu.make_async_copy(xs_hbm.at[pl.ds(start, MOE_BLOCK), :], xbuf.at[s], sem_in.at[s])

    def out_copy(g, s):
        return pltpu.make_async_copy(ybuf.at[s], ys_hbm.at[pl.ds(g * MOE_BLOCK, MOE_BLOCK), :], sem_out.at[s])

    @pl.when(jnp.logical_and(e == 0, nused > 0))
    def _():
        in_copy(0, 0).start()

    @pl.when(n > 0)
    def _():
        wgu16[...] = wgu_ref[0].astype(BF16)
        wdn16[...] = wdn_ref[0].astype(BF16)

    def block(b, carry):
        g = g0 + b
        s = g % 2

        @pl.when(g + 1 < nused)
        def _():
            in_copy(g + 1, 1 - s).start()

        in_copy(g, s).wait()
        gu = _dot(xbuf[s].astype(BF16), wgu16[...])
        gate = gu[:, :EXPERT_DIM]
        act = (gate * _sigmoid(gate) * gu[:, EXPERT_DIM:]).astype(BF16)
        y = _dot(act, wdn16[...])

        @pl.when(g >= 2)
        def _():
            out_copy(g - 2, s).wait()

        ybuf[s] = y
        out_copy(g, s).start()
        return carry

    lax.fori_loop(0, n, block, 0)

    @pl.when(e == pl.num_programs(0) - 1)
    def _():
        @pl.when(nused >= 2)
        def _():
            out_copy(nused - 2, nused % 2).wait()

        @pl.when(nused >= 1)
        def _():
            out_copy(nused - 1, (nused - 1) % 2).wait()

        ybuf[0] = jnp.zeros((MOE_BLOCK, D_MODEL), F32)

        def tail(g, carry):
            out_copy(g, 0).start()
            out_copy(g, 0).wait()
            return carry

        lax.fori_loop(nused, n_blocks, tail, 0)


def _experts(xs, bstart, nblk, nused, xrow, w_gu, w_down):
    n_blocks = xrow.shape[0]
    grid_spec = pltpu.PrefetchScalarGridSpec(
        num_scalar_prefetch=4,
        grid=(N_EXPERTS,),
        in_specs=[pl.BlockSpec(memory_space=pl.ANY),
                  pl.BlockSpec((1, D_MODEL, 2 * EXPERT_DIM), lambda e, bs, nb, nu, xr: (e, 0, 0)),
                  pl.BlockSpec((1, EXPERT_DIM, D_MODEL), lambda e, bs, nb, nu, xr: (e, 0, 0))],
        out_specs=pl.BlockSpec(memory_space=pl.ANY),
        scratch_shapes=[pltpu.VMEM((2, MOE_BLOCK, D_MODEL), F32),
                        pltpu.VMEM((2, MOE_BLOCK, D_MODEL), F32),
                        pltpu.VMEM((D_MODEL, 2 * EXPERT_DIM), BF16),
                        pltpu.VMEM((EXPERT_DIM, D_MODEL), BF16),
                        pltpu.SemaphoreType.DMA((2,)), pltpu.SemaphoreType.DMA((2,))],
    )
    return pl.pallas_call(
        functools.partial(_expert_kernel, n_blocks=n_blocks),
        grid_spec=grid_spec,
        out_shape=jax.ShapeDtypeStruct((n_blocks * MOE_BLOCK, D_MODEL), F32),
        compiler_params=_cparams(("arbitrary",)),
        name="experts",
    )(bstart, nblk, nused, xrow, xs, w_gu, w_down)


TC_TOK = 128


def _combine_kernel(slot_ref, slot_next_ref, wt_ref, x1_ref, h2_ref, g2_ref, sgu_ref, sdn_ref, lg_ref, lb_ref,
                    y_hbm, o_ref, gbuf, sems, *, n_tiles, alpha):
    i = pl.program_id(0)

    def gather(sref, b):
        def body(j, carry):
            for k in range(TOP_K):
                pltpu.make_async_copy(y_hbm.at[pl.ds(sref[0, 0, j * TOP_K + k], 1), :],
                                      gbuf.at[b, k, pl.ds(j, 1), :], sems.at[b]).start(priority=k % 2)
            return carry
        lax.fori_loop(0, TC_TOK, body, 0)

    @pl.when(i == 0)
    def _():
        gather(slot_ref, 0)

    @pl.when(i + 1 < n_tiles)
    def _():
        gather(slot_next_ref, (i + 1) % 2)

    b = i % 2
    h = h2_ref[...].astype(BF16)
    su = _dot(h, sgu_ref[...])
    sg = su[:, :EXPERT_DIM]
    act = (sg * _sigmoid(sg) * su[:, EXPERT_DIM:]).astype(BF16)
    ffn = _dot(act, sdn_ref[...])
    for k in range(TOP_K):
        pltpu.make_async_copy(y_hbm.at[pl.ds(0, TC_TOK), :], gbuf.at[b, k], sems.at[b]).wait()
    for k in range(TOP_K):
        ffn = ffn + gbuf[b, k] * wt_ref[:, k:k + 1]
    o_ref[...] = _layer_norm(alpha * x1_ref[...] + g2_ref[0] * ffn, lg_ref[...], lb_ref[...])


def _combine(slot3, wts_t, x1, h2, g2, sh_gu, sh_dn, ln_g, ln_b, y_sorted, row_of_tile, alpha):
    t_all = x1.shape[0]
    n_tiles = t_all // TC_TOK
    tile = lambda w: pl.BlockSpec((TC_TOK, w), lambda i: (i, 0))
    full = lambda a: pl.BlockSpec(a.shape, lambda i, _n=a.ndim: (0,) * _n)
    return pl.pallas_call(
        functools.partial(_combine_kernel, n_tiles=n_tiles, alpha=alpha),
        grid=(n_tiles,),
        in_specs=[pl.BlockSpec((1, 1, TC_TOK * TOP_K), lambda i: (i, 0, 0), memory_space=pltpu.SMEM),
                  pl.BlockSpec((1, 1, TC_TOK * TOP_K), lambda i: (jnp.minimum(i + 1, n_tiles - 1), 0, 0),
                               memory_space=pltpu.SMEM),
                  tile(TOP_K), tile(D_MODEL), tile(D_MODEL),
                  pl.BlockSpec((1, 1, D_MODEL), lambda i: (row_of_tile(i), 0, 0)),
                  full(sh_gu), full(sh_dn), full(ln_g), full(ln_b),
                  pl.BlockSpec(memory_space=pl.ANY)],
        out_specs=tile(D_MODEL),
        out_shape=jax.ShapeDtypeStruct((t_all, D_MODEL), F32),
        scratch_shapes=[pltpu.VMEM((2, TOP_K, TC_TOK, D_MODEL), F32), pltpu.SemaphoreType.DMA((2,))],
        compiler_params=_cparams(("arbitrary",)),
        name="combine",
    )(slot3, slot3, wts_t, x1, h2, g2, sh_gu, sh_dn, ln_g, ln_b, y_sorted)


def kernel(x_prompt, x_sample, state_rwkv, c, c_ctx, w_ada, b_ada, w_in, mu_shift, w_decay0, w_decay2, w_a0, w_a2, w_g2, k_k, k_a, r_k, lnx_g, lnx_b, conv_w, conv_b, conv_ln_g, conv_ln_b, w_out, ln1_g, ln1_b, router_w, router_bias, expert_w_gu, expert_w_down, shared_w_gu, shared_w_down, ln2_g, ln2_b):
    depth = w_ada.shape[0]
    assert depth == 1
    alpha = (2.0 * depth) ** 0.25
    n_ctx, t_ctx, _ = x_prompt.shape
    n_lat, t_lat, _ = x_sample.shape
    tok_ctx = n_ctx * t_ctx
    tok_lat = n_lat * t_lat
    t_all = tok_ctx + tok_lat
    tm = 1024
    assert tok_ctx % tm == 0 and t_lat % tm == 0 and t_lat // GRID_W * GRID_W == t_lat
    l = 0

    x_all = jnp.concatenate([x_prompt.reshape(tok_ctx, D_MODEL), x_sample.reshape(tok_lat, D_MODEL)], axis=0)

    cond8 = jnp.zeros((8, D_MODEL), F32).at[0].set(c_ctx).at[1:1 + n_lat].set(c)
    mod = _ada(cond8, w_ada[l], b_ada[l])
    sh1, sc1, g1, sh2, sc2, g2 = [m.reshape(8, 1, D_MODEL) for m in jnp.split(mod, 6, axis=-1)]

    def row_of_tile_for(tile_rows):
        ctx_tiles = tok_ctx // tile_rows
        per_seq = t_lat // tile_rows
        return lambda i: jnp.where(i < ctx_tiles, 0, 1 + (i - ctx_tiles) // per_seq)

    w_in_a = w_in[l][:, :RWKV_IN].astype(BF16)
    w_in_b = w_in[l][:, RWKV_IN:].astype(BF16)
    proj_a, proj_b = _in_proj(x_all, sc1, sh1, w_in_a, w_in_b, tm, row_of_tile_for(tm))

    zeros64 = jnp.zeros((64, 2 * RW), F32)
    wd2 = jnp.concatenate([w_decay2[l, 0], w_decay2[l, 1]], axis=1)
    wa2 = jnp.concatenate([w_a2[l, 0], w_a2[l, 1]], axis=1)
    w_lora = jnp.concatenate([jnp.concatenate([wd2, zeros64], axis=1),
                              jnp.concatenate([zeros64, wa2], axis=1)], axis=0).astype(BF16)
    row = lambda v: v.reshape(1, -1)
    rwkv_w = [row(mu_shift[l]), w_lora, w_g2[l].astype(BF16), row(w_decay0[l]), row(w_a0[l]),
              row(k_k[l]), row(k_a[l]), row(r_k[l]), row(lnx_g[l]), row(lnx_b[l])]
    consts = _rwkv_consts()

    def to_cat(s):
        b = s.shape[0]
        s = s.reshape(b, 2, 2, 4, HD, HD)
        return jnp.transpose(s, (0, 1, 2, 5, 3, 4)).reshape(b, 4, HD, GW)

    def from_cat(s):
        b = s.shape[0]
        s = s.reshape(b, 2, 2, HD, 4, HD)
        return jnp.transpose(s, (0, 1, 2, 4, 5, 3)).reshape(b, 2, 8, HD, HD)

    s0_ctx = jnp.zeros((n_ctx, 4, HD, GW), F32)
    s0_lat = to_cat(state_rwkv[:, l].astype(F32))
    ya_c, sfin_ctx = _rwkv(proj_a, s0_ctx, rwkv_w, consts, t_ctx, n_ctx, 0)
    ya_l, _ = _rwkv(proj_a, s0_lat, rwkv_w, consts, t_lat, n_lat, tok_ctx // t_lat)
    new_state = from_cat(sfin_ctx)[:, None].astype(x_prompt.dtype)

    conv_wts = [conv_w[l], row(conv_b[l]), row(conv_ln_g[l]), row(conv_ln_b[l])]
    yb_c = _conv(proj_b, conv_wts, t_ctx, n_ctx, 0, False)
    yb_l = _conv(proj_b, conv_wts, t_lat, n_lat, tok_ctx // t_lat, True)

    w_oa = w_out[l][:RW].astype(BF16)
    w_ob = w_out[l][RW:].astype(BF16)
    x1, h2 = _out_proj(ya_c, ya_l, yb_c, yb_l, x_all, g1, sc2, sh2, w_oa, w_ob, row(ln1_g[l]), row(ln1_b[l]), tm,
                       row_of_tile_for(tm), alpha)

    rwt = router_w[l].T
    rwt_hi = rwt.astype(BF16)
    rwt_lo = (rwt - rwt_hi.astype(F32)).astype(BF16)
    eidx, wts, pos, counts = _route(h2, rwt_hi, rwt_lo, router_bias[l].reshape(N_EXPERTS, 1), 512)

    n_assign = t_all * TOP_K
    n_blocks = (n_assign + N_EXPERTS * (MOE_BLOCK - 1) + MOE_BLOCK - 1) // MOE_BLOCK
    cnt = counts.reshape(N_EXPERTS).astype(jnp.int32)
    padded = (cnt + MOE_BLOCK - 1) // MOE_BLOCK * MOE_BLOCK
    pend = jnp.cumsum(padded)
    pstart = pend - padded
    padded_x = (cnt + XS_ALIGN - 1) // XS_ALIGN * XS_ALIGN
    pend_x = jnp.cumsum(padded_x)
    pstart_x = pend_x - padded_x
    xs_rows = n_assign + N_EXPERTS * (XS_ALIGN - 1) // XS_ALIGN * XS_ALIGN + MOE_BLOCK
    slot_x, slot_y = _slots(eidx, pos, jnp.stack([pstart_x, pstart], axis=1).astype(F32), 512)
    per_tile = lambda s: s.T.reshape(t_all // TC_TOK, 1, TC_TOK * TOP_K)
    xs = _dispatch(per_tile(slot_x), h2, (pstart_x + cnt).astype(jnp.int32), (padded_x - cnt).astype(jnp.int32),
                   pend_x[-1].astype(jnp.int32).reshape(1), xs_rows)
    bstart = (pstart // MOE_BLOCK).astype(jnp.int32)
    nblk = (padded // MOE_BLOCK).astype(jnp.int32)
    nused = (pend[-1] // MOE_BLOCK).astype(jnp.int32).reshape(1)
    blk_ids = jnp.arange(n_blocks, dtype=jnp.int32)
    blk_e = jnp.minimum(jnp.sum((pend[None, :] <= (blk_ids * MOE_BLOCK)[:, None]).astype(jnp.int32), axis=1),
                        N_EXPERTS - 1)
    shift = jnp.sum(jnp.where(blk_e[:, None] == jnp.arange(N_EXPERTS, dtype=jnp.int32)[None, :],
                              (pstart_x - pstart)[None, :], 0), axis=1)
    xrow = jnp.clip(blk_ids * MOE_BLOCK + shift, 0, xs_rows - MOE_BLOCK).astype(jnp.int32)
    y_sorted = _experts(xs, bstart, nblk, nused, xrow, expert_w_gu[l], expert_w_down[l])
    y_all = _combine(per_tile(slot_y), wts.T, x1, h2, g2, shared_w_gu[l].astype(BF16), shared_w_down[l].astype(BF16),
                     row(ln2_g[l]), row(ln2_b[l]), y_sorted, row_of_tile_for(TC_TOK), alpha)

    y_prompt = y_all[:tok_ctx].reshape(n_ctx, t_ctx, D_MODEL)
    y_sample = y_all[tok_ctx:].reshape(n_lat, t_lat, D_MODEL)
    return (y_prompt, y_sample, new_state)
```

```python
import functools
import math

import numpy as np
import jax
import jax.numpy as jnp
from jax import lax
from jax.experimental import pallas as pl
from jax.experimental.pallas import tpu as pltpu

F32 = jnp.float32
BF16 = jnp.bfloat16

D_MODEL = 1024
RW = 512
HD = 64
CW = 512
CONV_K = 31
RWKV_IN = 3 * RW + 64 + 64 + 128
N_EXPERTS = 256
TOP_K = 8
N_GROUPS = 8
TOPK_GROUPS = 4
EXPERT_DIM = 256
ROUTED_SCALE = 2.5
MOE_BLOCK = 128
GRID_W = 64
LN_EPS = 1e-5
GN_EPS = 64e-5
CHUNK = 64
GW = 256
DECAY_SCALE = math.exp(-0.5)
VMEM_LIMIT = 52 * 1024 * 1024


def _cparams(sem):
    return pltpu.CompilerParams(dimension_semantics=sem, vmem_limit_bytes=VMEM_LIMIT)


def _split_bf16(x):
    hi = x.astype(BF16)
    lo = (x - hi.astype(F32)).astype(BF16)
    return hi, lo


def _dot(a, b):
    return jnp.dot(a, b, preferred_element_type=F32)


def _dot_nt(a, b):
    return lax.dot_general(a, b, (((1,), (1,)), ((), ())), preferred_element_type=F32)


def _dot_tn(a, b):
    return lax.dot_general(a, b, (((0,), (0,)), ((), ())), preferred_element_type=F32)


def _sigmoid(x):
    return 1.0 / (1.0 + jnp.exp(-x))


def _layer_norm(x, g, b):
    mu = jnp.mean(x, axis=-1, keepdims=True)
    d = x - mu
    var = jnp.mean(d * d, axis=-1, keepdims=True)
    return d * lax.rsqrt(var + LN_EPS) * g + b


def _ada_kernel(c_ref, w_ref, b_ref, o_ref):
    c = c_ref[...]
    s = c * _sigmoid(c)
    s_hi, s_lo = _split_bf16(s)
    w_hi, w_lo = _split_bf16(w_ref[...])
    o_ref[...] = _dot(s_hi, w_hi) + _dot(s_lo, w_hi) + _dot(s_hi, w_lo) + b_ref[...]


def _ada(cond8, w_ada, b_ada):
    n = w_ada.shape[1]
    tn = 1536
    return pl.pallas_call(
        _ada_kernel,
        grid=(n // tn,),
        in_specs=[pl.BlockSpec((8, D_MODEL), lambda j: (0, 0)),
                  pl.BlockSpec((D_MODEL, tn), lambda j: (0, j)),
                  pl.BlockSpec((1, tn), lambda j: (0, j))],
        out_specs=pl.BlockSpec((8, tn), lambda j: (0, j)),
        out_shape=jax.ShapeDtypeStruct((8, n), F32),
        compiler_params=_cparams(("arbitrary",)),
        name="ada",
    )(cond8, w_ada, b_ada.reshape(1, n))


def _inproj_kernel(x_ref, sc_ref, sh_ref, wa_ref, wb_ref, oa_ref, ob_ref):
    h = (x_ref[...] * (1.0 + sc_ref[0]) + sh_ref[0]).astype(BF16)
    oa_ref[...] = _dot(h, wa_ref[...])
    ob_ref[...] = _dot(h, wb_ref[...])


def _in_proj(x_all, sc1, sh1, w_a, w_b, tm, row_of_tile):
    t_all = x_all.shape[0]
    return pl.pallas_call(
        _inproj_kernel,
        grid=(t_all // tm,),
        in_specs=[pl.BlockSpec((tm, D_MODEL), lambda i: (i, 0)),
                  pl.BlockSpec((1, 1, D_MODEL), lambda i: (row_of_tile(i), 0, 0)),
                  pl.BlockSpec((1, 1, D_MODEL), lambda i: (row_of_tile(i), 0, 0)),
                  pl.BlockSpec(w_a.shape, lambda i: (0, 0)),
                  pl.BlockSpec(w_b.shape, lambda i: (0, 0))],
        out_specs=[pl.BlockSpec((tm, RWKV_IN), lambda i: (i, 0)),
                   pl.BlockSpec((tm, 2 * CW), lambda i: (i, 0))],
        out_shape=[jax.ShapeDtypeStruct((t_all, RWKV_IN), F32),
                   jax.ShapeDtypeStruct((t_all, 2 * CW), F32)],
        compiler_params=_cparams(("arbitrary",)),
        name="in_proj",
    )(x_all, sc1, sh1, w_a, w_b)


def _rwkv_consts():
    c = CHUNK
    t = np.arange(c)[:, None]
    s = np.arange(GW)[None, :] % c
    sl = (s < t).astype(np.float32)
    il = (s <= t).astype(np.float32)
    su = (s > t).astype(np.float32)
    iu = (s >= t).astype(np.float32)
    eye_cat = (s == t).astype(np.float32)
    tri = np.stack([sl, il, su, iu, eye_cat])
    r = np.arange(GW)
    mask_bd = (r[:, None] // HD == r[None, :] // HD).astype(np.float32)
    eye = np.eye(GW, dtype=np.float32)
    cum = np.stack([np.tril(np.ones((c, c), np.float32)), np.triu(np.ones((c, c), np.float32))])
    q = np.arange(RW)
    seg = (q[:, None] // HD == q[None, :] // HD).astype(np.float32)
    return (jnp.asarray(tri), jnp.asarray(mask_bd, dtype=BF16), jnp.asarray(mask_bd), jnp.asarray(eye),
            jnp.asarray(cum, dtype=BF16), jnp.asarray(seg, dtype=BF16))


def _rwkv_kernel(p_ref, s0_ref, mu_ref, wlora_ref, wg2_ref, wd0_ref, wa0_ref, kk_ref, ka_ref, rk_ref,
                 lng_ref, lnb_ref, tri_ref, mbd16_ref, mbd_ref, eye_ref, cum_ref, seg_ref,
                 ya_ref, sfin_ref,
                 r_s, v_s, kk_s, g_s, bon_s, a_s, lw_s, kd_s, y_s, st_s, *, seq_len):
    nc = seq_len // CHUNK
    seg = seg_ref[...]

    def seg_sum(x):
        hi, lo = _split_bf16(x)
        return _dot(hi, seg) + _dot(lo, seg)

    def phase_a(c, carry):
        t0 = pl.multiple_of(c * CHUNK, CHUNK)
        cur = p_ref[pl.ds(t0, CHUNK), :]
        prev_row = p_ref[pl.ds(jnp.maximum(t0 - 1, 0), 1), :] * (t0 > 0).astype(F32)
        next_row = p_ref[pl.ds(jnp.minimum(t0 + CHUNK, seq_len - 1), 1), :] * (t0 + CHUNK < seq_len).astype(F32)
        row = lax.broadcasted_iota(jnp.int32, cur.shape, 0)
        prev = jnp.where(row == 0, prev_row, pltpu.roll(cur, 1, 0))
        nxt = jnp.where(row == CHUNK - 1, next_row, pltpu.roll(cur, CHUNK - 1, 0))
        p = cur + mu_ref[...] * (0.5 * (prev + nxt) - cur)
        r = p[:, 0:RW]
        k = p[:, RW:2 * RW]
        v = p[:, 2 * RW:3 * RW]
        z = p[:, 3 * RW:3 * RW + 128]
        gd = p[:, 3 * RW + 128:3 * RW + 256]
        lane = lax.broadcasted_iota(jnp.int32, z.shape, 1)
        z = jnp.where(lane < 64, jnp.tanh(z), z)
        lora = _dot(z.astype(BF16), wlora_ref[...])
        g = _dot(_sigmoid(gd).astype(BF16), wg2_ref[...])
        kks = k * kk_ref[...]
        kk = kks * lax.rsqrt(seg_sum(kks * kks) + 1e-12)
        rows = pl.ds(t0, CHUNK)
        r_s[rows, :] = r
        v_s[rows, :] = v
        kk_s[rows, :] = kk
        g_s[rows, :] = g
        kd_sum = jnp.zeros_like(k)
        for d in range(2):
            w_logit = wd0_ref[:, d * RW:(d + 1) * RW] + lora[:, d * RW:(d + 1) * RW]
            a = _sigmoid(wa0_ref[:, d * RW:(d + 1) * RW] + lora[:, (2 + d) * RW:(3 + d) * RW])
            kd = k * (1.0 + (a - 1.0) * ka_ref[...])
            lw_s[d, rows, :] = -DECAY_SCALE * _sigmoid(w_logit)
            a_s[d, rows, :] = a
            kd_s[d, rows, :] = kd
            kd_sum = kd_sum + kd
        bon_s[rows, :] = seg_sum(r * kd_sum * rk_ref[...]) * v
        return carry

    lax.fori_loop(0, nc, phase_a, 0)

    mbd16 = mbd16_ref[...]
    mbd = mbd_ref[...]
    eye = eye_ref[...]

    def bd(x16):
        return jnp.concatenate([x16, x16, x16, x16], axis=0) * mbd16

    for u in range(4):
        s0 = s0_ref[0, u]
        st_s[u] = jnp.concatenate([s0, s0, s0, s0], axis=0) * mbd

    def prep(d, grp, t0):
        rows = pl.ds(t0, CHUNK)
        lanes = slice(grp * GW, (grp + 1) * GW)
        lw = lw_s[d, rows, lanes]
        hi, lo = _split_bf16(lw)
        cl = _dot(cum_ref[d], hi) + _dot(cum_ref[d], lo)
        e_cl = jnp.exp(cl)
        e_ce = jnp.exp(cl - lw)
        e_ncl = jnp.exp(-cl)
        tot = cl[CHUNK - 1:CHUNK, :] if d == 0 else cl[0:1, :]
        wc = jnp.exp(tot)
        kk = kk_s[rows, lanes]
        rt = r_s[rows, lanes] * e_cl
        bt = a_s[d, rows, lanes] * kk * e_ncl
        kt = kd_s[d, rows, lanes] * e_ncl
        kq16 = (kk * e_ce).astype(BF16)
        return dict(d=d, u=d * 2 + grp, rows=rows, lanes=lanes, wc=wc, rt=rt, kq16=kq16,
                    v16=v_s[rows, lanes].astype(BF16), bt16=bt.astype(BF16), kt16=kt.astype(BF16),
                    btw16=(bt * wc).astype(BF16), ktw16=(kt * wc).astype(BF16),
                    g16=jnp.concatenate([kq16, rt.astype(BF16)], axis=0))

    def phase_b(i, carry):
        tf = pl.multiple_of(i * CHUNK, CHUNK)
        tb = pl.multiple_of((nc - 1 - i) * CHUNK, CHUNK)
        us = [prep(0, 0, tf), prep(1, 0, tb), prep(0, 1, tf), prep(1, 1, tb)]
        each = lambda fn: [fn(q) for q in us]
        strict = lambda q: tri_ref[2 * q['d']]
        incl = lambda q: tri_ref[2 * q['d'] + 1]
        out_b = each(lambda q: _dot_nt(q['g16'], bd(q['bt16'])))
        out_k = each(lambda q: _dot_nt(q['g16'], bd(q['kt16'])))
        a_bk = [o[:CHUNK] * strict(q) for o, q in zip(out_b, us)]
        p_rb = [(o[CHUNK:] * incl(q)).astype(BF16) for o, q in zip(out_b, us)]
        a_kk = [(o[:CHUNK] * strict(q)).astype(BF16) for o, q in zip(out_k, us)]
        p_rk = [(o[CHUNK:] * incl(q)).astype(BF16) for o, q in zip(out_k, us)]
        tinv = [tri_ref[4] - a for a in a_bk]
        pw = [a.astype(BF16) for a in a_bk]
        for _ in range(5):
            pw = [_dot(p, bd(p)).astype(BF16) for p in pw]
            tinv = [t + _dot(p, bd(t.astype(BF16))) for t, p in zip(tinv, pw)]
        bdv = each(lambda q: bd(q['v16']))
        akkv = [_dot(a, b) for a, b in zip(a_kk, bdv)]
        x = [_dot(t.astype(BF16), jnp.concatenate([bd(q['kq16']), bd(k.astype(BF16))], axis=1))
             for t, q, k in zip(tinv, us, akkv)]
        qu16 = [z.astype(BF16) for z in x]
        tn1 = [_dot_tn(q['btw16'], z) for q, z in zip(us, qu16)]
        tn2 = each(lambda q: _dot_tn(q['ktw16'], q['v16']))
        r1 = [_dot(p, jnp.concatenate([bd(z[:, :GW]), bd(z[:, GW:])], axis=1)) for p, z in zip(p_rb, qu16)]
        y0 = [_dot(p, b) - r[:, GW:] for p, b, r in zip(p_rk, bdv, r1)]
        for q, t1, t2, r, y in zip(us, tn1, tn2, r1, y0):
            mc = (eye * q['wc'] - t1[:, :GW]) * mbd
            ncm = (t2 - t1[:, GW:]) * mbd
            rh = q['rt'] - r[:, :GW]
            st16 = st_s[q['u']].astype(BF16)
            big = _dot(jnp.concatenate([rh.astype(BF16), mc.astype(BF16)], axis=0), st16)
            y_s[q['d'], q['rows'], q['lanes']] = big[:CHUNK] + y
            st_s[q['u']] = big[CHUNK:] + ncm
        return carry

    lax.fori_loop(0, nc, phase_b, 0)

    for u in range(4):
        st = st_s[u]
        sfin_ref[0, u] = st[0:HD] + st[HD:2 * HD] + st[2 * HD:3 * HD] + st[3 * HD:4 * HD]

    def phase_c(c, carry):
        rows = pl.ds(pl.multiple_of(c * CHUNK, CHUNK), CHUNK)
        y = y_s[0, rows, :] + y_s[1, rows, :]
        mean = seg_sum(y) * (1.0 / HD)
        dlt = y - mean
        var = seg_sum(dlt * dlt) * (1.0 / HD)
        yn = dlt * lax.rsqrt(var + GN_EPS)
        ya_ref[rows, :] = (yn * lng_ref[...] + lnb_ref[...] + bon_s[rows, :]) * g_s[rows, :]
        return carry

    lax.fori_loop(0, nc, phase_c, 0)


def _rwkv(proj_a, s0_cat, wts, consts, seq_len, n_seq, first_block):
    full = lambda a: pl.BlockSpec(a.shape, lambda s, _n=a.ndim: (0,) * _n)
    in_specs = [pl.BlockSpec((seq_len, RWKV_IN), lambda s: (first_block + s, 0)),
                pl.BlockSpec((1, 4, HD, GW), lambda s: (s, 0, 0, 0))]
    in_specs += [full(a) for a in wts] + [full(a) for a in consts]
    args = [proj_a, s0_cat] + list(wts) + list(consts)
    seq = (seq_len, RW)
    return pl.pallas_call(
        functools.partial(_rwkv_kernel, seq_len=seq_len),
        grid=(n_seq,),
        in_specs=in_specs,
        out_specs=[pl.BlockSpec((seq_len, RW), lambda s: (s, 0)),
                   pl.BlockSpec((1, 4, HD, GW), lambda s: (s, 0, 0, 0))],
        out_shape=[jax.ShapeDtypeStruct((n_seq * seq_len, RW), F32),
                   jax.ShapeDtypeStruct((n_seq, 4, HD, GW), F32)],
        scratch_shapes=[pltpu.VMEM(seq, F32), pltpu.VMEM(seq, F32), pltpu.VMEM(seq, F32), pltpu.VMEM(seq, F32),
                        pltpu.VMEM(seq, F32),
                        pltpu.VMEM((2,) + seq, F32), pltpu.VMEM((2,) + seq, F32), pltpu.VMEM((2,) + seq, F32),
                        pltpu.VMEM((2,) + seq, F32), pltpu.VMEM((4, GW, GW), F32)],
        compiler_params=_cparams(("arbitrary",)),
        name="rwkv_%d" % seq_len,
    )(*args)


CONV_PAD = 16
VPAD = (CONV_K // 2) * GRID_W


def _conv_kernel(u_ref, w_ref, b_ref, g_ref, beta_ref, o_ref, pad_s, vpad_s, y_s, *, seq_len, grid):
    rb = 256
    glu = u_ref[:, :CW] * _sigmoid(u_ref[:, CW:])
    zeros = jnp.zeros((CONV_PAD, CW), F32)
    pad_s[0:CONV_PAD, :] = zeros
    pad_s[CONV_PAD + seq_len:2 * CONV_PAD + seq_len, :] = zeros
    pad_s[CONV_PAD:CONV_PAD + seq_len, :] = glu
    half = CW // 2
    if grid:
        vz = jnp.zeros((VPAD, half), F32)
        vpad_s[0:VPAD, :] = vz
        vpad_s[VPAD + seq_len:2 * VPAD + seq_len, :] = vz
        vpad_s[VPAD:VPAD + seq_len, :] = glu[:, half:]
    n_h_lanes = (half if grid else CW) // 128
    for r0 in range(0, seq_len, rb):
        for lb in range(n_h_lanes):
            lanes = slice(lb * 128, (lb + 1) * 128)
            acc = jnp.zeros((rb, 128), F32)
            if grid:
                col = lax.broadcasted_iota(jnp.int32, (rb, 128), 0) % GRID_W
            for k in range(CONV_K):
                win = pad_s[r0 + k + 1:r0 + k + 1 + rb, lanes]
                term = win * w_ref[k:k + 1, lanes]
                if grid:
                    src = col + (k - CONV_K // 2)
                    term = jnp.where((src >= 0) & (src < GRID_W), term, 0.0)
                acc = acc + term
            y_s[r0:r0 + rb, lanes] = acc
        if grid:
            for lb in range(half // 128):
                lanes = slice(lb * 128, (lb + 1) * 128)
                wl = slice(half + lb * 128, half + (lb + 1) * 128)
                acc = jnp.zeros((rb, 128), F32)
                for k in range(CONV_K):
                    start = VPAD + r0 + (k - CONV_K // 2) * GRID_W
                    acc = acc + vpad_s[start:start + rb, lanes] * w_ref[k:k + 1, wl]
                y_s[r0:r0 + rb, wl] = acc
    for r0 in range(0, seq_len, rb):
        y = y_s[r0:r0 + rb, :] + b_ref[...]
        yl = _layer_norm(y, g_ref[...], beta_ref[...])
        o_ref[r0:r0 + rb, :] = yl * _sigmoid(yl)


def _conv(proj_b, wts, seq_len, n_seq, first_block, grid):
    full = lambda a: pl.BlockSpec(a.shape, lambda s, _n=a.ndim: (0,) * _n)
    in_specs = [pl.BlockSpec((seq_len, 2 * CW), lambda s: (first_block + s, 0))] + [full(a) for a in wts]
    args = [proj_b] + list(wts)
    return pl.pallas_call(
        functools.partial(_conv_kernel, seq_len=seq_len, grid=grid),
        grid=(n_seq,),
        in_specs=in_specs,
        out_specs=pl.BlockSpec((seq_len, CW), lambda s: (s, 0)),
        out_shape=jax.ShapeDtypeStruct((n_seq * seq_len, CW), F32),
        scratch_shapes=[pltpu.VMEM((seq_len + 2 * CONV_PAD, CW), F32),
                        pltpu.VMEM((seq_len + 2 * VPAD if grid else 8, CW // 2), F32),
                        pltpu.VMEM((seq_len, CW), F32)],
        compiler_params=_cparams(("arbitrary",)),
        name="conv_%d" % seq_len,
    )(*args)


def _outproj_kernel(yac_ref, yal_ref, ybc_ref, ybl_ref, x_ref, g1_ref, sc2_ref, sh2_ref, wa_ref, wb_ref,
                    lg_ref, lb_ref, x1_ref, h2_ref, *, alpha, ctx_tiles):
    is_ctx = pl.program_id(0) < ctx_tiles
    ya = jnp.where(is_ctx, yac_ref[...], yal_ref[...])
    yb = jnp.where(is_ctx, ybc_ref[...], ybl_ref[...])
    mix = _dot(ya.astype(BF16), wa_ref[...]) + _dot(yb.astype(BF16), wb_ref[...])
    x1 = _layer_norm(alpha * x_ref[...] + g1_ref[0] * mix, lg_ref[...], lb_ref[...])
    x1_ref[...] = x1
    h2_ref[...] = x1 * (1.0 + sc2_ref[0]) + sh2_ref[0]


def _out_proj(ya_c, ya_l, yb_c, yb_l, x_all, g1, sc2, sh2, w_oa, w_ob, ln_g, ln_b, tm, row_of_tile, alpha):
    t_all = x_all.shape[0]
    ctx_tiles = ya_c.shape[0] // tm
    tile = lambda w: pl.BlockSpec((tm, w), lambda i: (i, 0))
    ctx_tile = lambda w: pl.BlockSpec((tm, w), lambda i: (jnp.minimum(i, ctx_tiles - 1), 0))
    lat_tile = lambda w: pl.BlockSpec((tm, w), lambda i: (jnp.maximum(i - ctx_tiles, 0), 0))
    modrow = pl.BlockSpec((1, 1, D_MODEL), lambda i: (row_of_tile(i), 0, 0))
    full = lambda a: pl.BlockSpec(a.shape, lambda i, _n=a.ndim: (0,) * _n)
    return pl.pallas_call(
        functools.partial(_outproj_kernel, alpha=alpha, ctx_tiles=ctx_tiles),
        grid=(t_all // tm,),
        in_specs=[ctx_tile(RW), lat_tile(RW), ctx_tile(CW), lat_tile(CW), tile(D_MODEL), modrow, modrow, modrow,
                  full(w_oa), full(w_ob), full(ln_g), full(ln_b)],
        out_specs=[tile(D_MODEL), tile(D_MODEL)],
        out_shape=[jax.ShapeDtypeStruct((t_all, D_MODEL), F32), jax.ShapeDtypeStruct((t_all, D_MODEL), F32)],
        compiler_params=_cparams(("arbitrary",)),
        name="out_proj",
    )(ya_c, ya_l, yb_c, yb_l, x_all, g1, sc2, sh2, w_oa, w_ob, ln_g, ln_b)


def _route_kernel(h_ref, wt_hi_ref, wt_lo_ref, bias_ref, ustrict_ref,
                  eidx_ref, wts_ref, pos_ref, cnt_ref, carry_s, *, tm):
    i = pl.program_id(0)

    @pl.when(i == 0)
    def _():
        carry_s[...] = jnp.zeros_like(carry_s)

    h_hi, h_lo = _split_bf16(h_ref[...])
    logits = _dot_nt(wt_hi_ref[...], h_hi) + _dot_nt(wt_hi_ref[...], h_lo) + _dot_nt(wt_lo_ref[...], h_hi)
    scores = _sigmoid(logits)
    sel = scores + bias_ref[...]
    neg = jnp.float32(-jnp.inf)
    gsz = N_EXPERTS // N_GROUPS
    gs = []
    for g in range(N_GROUPS):
        blk = sel[g * gsz:(g + 1) * gsz]
        m1 = jnp.max(blk, axis=0, keepdims=True)
        eq = blk == m1
        cnt = jnp.sum(eq.astype(F32), axis=0, keepdims=True)
        m2 = jnp.max(jnp.where(eq, neg, blk), axis=0, keepdims=True)
        gs.append(m1 + jnp.where(cnt >= 2.0, m1, m2))
    masked = []
    for g in range(N_GROUPS):
        rank = jnp.zeros_like(gs[g])
        for o in range(N_GROUPS):
            if o == g:
                continue
            beats = (gs[o] > gs[g]) | ((gs[o] == gs[g]) & (o < g))
            rank = rank + beats.astype(F32)
        keep = rank < float(TOPK_GROUPS)
        masked.append(jnp.where(keep, sel[g * gsz:(g + 1) * gsz], neg))
    cur = jnp.concatenate(masked, axis=0)
    iota_e = lax.broadcasted_iota(jnp.int32, cur.shape, 0).astype(F32)
    idxs, ws = [], []
    selmask = jnp.zeros(cur.shape, F32)
    for _ in range(TOP_K):
        m = jnp.max(cur, axis=0, keepdims=True)
        idx = jnp.min(jnp.where(cur == m, iota_e, float(N_EXPERTS)), axis=0, keepdims=True)
        onehot = iota_e == idx
        ws.append(jnp.sum(jnp.where(onehot, scores, 0.0), axis=0, keepdims=True))
        idxs.append(idx)
        cur = jnp.where(onehot, neg, cur)
        selmask = jnp.where(onehot, 1.0, selmask)
    pos = _dot(selmask.astype(BF16), ustrict_ref[...]) + carry_s[...]
    carry_s[...] = carry_s[...] + jnp.sum(selmask, axis=1, keepdims=True)
    cnt_ref[...] = carry_s[...]
    wsum = ws[0]
    for k in range(1, TOP_K):
        wsum = wsum + ws[k]
    pks = [jnp.sum(jnp.where(iota_e == idxs[k], pos, 0.0), axis=0, keepdims=True) for k in range(TOP_K)]
    eidx_ref[...] = jnp.concatenate(idxs, axis=0).astype(jnp.int32)
    wts_ref[...] = jnp.concatenate([w / wsum * ROUTED_SCALE for w in ws], axis=0)
    pos_ref[...] = jnp.concatenate(pks, axis=0).astype(jnp.int32)


def _route(h2, wt_hi, wt_lo, bias_col, tm):
    t_all = h2.shape[0]
    ustrict = jnp.asarray(np.triu(np.ones((tm, tm), np.float32), 1), dtype=BF16)
    full = lambda a: pl.BlockSpec(a.shape, lambda i, _n=a.ndim: (0,) * _n)
    tok = pl.BlockSpec((TOP_K, tm), lambda i: (0, i))
    return pl.pallas_call(
        functools.partial(_route_kernel, tm=tm),
        grid=(t_all // tm,),
        in_specs=[pl.BlockSpec((tm, D_MODEL), lambda i: (i, 0)), full(wt_hi), full(wt_lo), full(bias_col),
                  full(ustrict)],
        out_specs=[tok, tok, tok, pl.BlockSpec((N_EXPERTS, 1), lambda i: (0, 0))],
        out_shape=[jax.ShapeDtypeStruct((TOP_K, t_all), jnp.int32), jax.ShapeDtypeStruct((TOP_K, t_all), F32),
                   jax.ShapeDtypeStruct((TOP_K, t_all), jnp.int32), jax.ShapeDtypeStruct((N_EXPERTS, 1), F32)],
        scratch_shapes=[pltpu.VMEM((N_EXPERTS, 1), F32)],
        compiler_params=_cparams(("arbitrary",)),
        name="route",
    )(h2, wt_hi, wt_lo, bias_col, ustrict)


def _slots_kernel(eidx_ref, pos_ref, pstart_ref, slotx_ref, sloty_ref):
    tm = eidx_ref.shape[1]
    iota_e = lax.broadcasted_iota(jnp.int32, (N_EXPERTS, tm), 0)
    for c, out in enumerate((slotx_ref, sloty_ref)):
        rows = []
        for k in range(TOP_K):
            onehot = iota_e == eidx_ref[k:k + 1, :]
            rows.append(jnp.sum(jnp.where(onehot, pstart_ref[:, c:c + 1], 0.0), axis=0, keepdims=True))
        out[...] = jnp.concatenate(rows, axis=0).astype(jnp.int32) + pos_ref[...]


def _slots(eidx, pos, pstart_cols, tm):
    t_all = eidx.shape[1]
    tok = pl.BlockSpec((TOP_K, tm), lambda i: (0, i))
    return pl.pallas_call(
        _slots_kernel,
        grid=(t_all // tm,),
        in_specs=[tok, tok, pl.BlockSpec((N_EXPERTS, 2), lambda i: (0, 0))],
        out_specs=[tok, tok],
        out_shape=[jax.ShapeDtypeStruct((TOP_K, t_all), jnp.int32), jax.ShapeDtypeStruct((TOP_K, t_all), jnp.int32)],
        compiler_params=_cparams(("arbitrary",)),
        name="slots",
    )(eidx, pos, pstart_cols)


TD_TOK = 128
XS_ALIGN = 8


def _dispatch_kernel(fstart_ref, flen_ref, total_ref, slot_ref, h_ref, xs_out, zero_s, sem, zsem, *, xs_rows):
    i = pl.program_id(0)

    def body(j, carry):
        for k in range(TOP_K):
            pltpu.make_async_copy(h_ref.at[pl.ds(j, 1), :], xs_out.at[pl.ds(slot_ref[0, 0, j * TOP_K + k], 1), :],
                                  sem).start(priority=k % 2)
        return carry

    lax.fori_loop(0, TD_TOK, body, 0)
    for k in range(TOP_K):
        pltpu.make_async_copy(h_ref, xs_out.at[pl.ds(0, TD_TOK), :], sem).wait()

    @pl.when(i == pl.num_programs(0) - 1)
    def _():
        zero_s[...] = jnp.zeros_like(zero_s)
        total = total_ref[0]
        n_tail = (xs_rows - total) // XS_ALIGN

        def gap_copy(e, r):
            return pltpu.make_async_copy(zero_s.at[pl.ds(0, 1), :], xs_out.at[pl.ds(fstart_ref[e] + r, 1), :], zsem)

        def tail_copy(q):
            start = pl.multiple_of(total + q * XS_ALIGN, XS_ALIGN)
            return pltpu.make_async_copy(zero_s, xs_out.at[pl.ds(start, XS_ALIGN), :], zsem)

        def for_gaps(fn):
            def per_expert(e, carry):
                for r in range(XS_ALIGN - 1):
                    @pl.when(r < flen_ref[e])
                    def _():
                        fn(gap_copy(e, r))
                return carry
            lax.fori_loop(0, N_EXPERTS, per_expert, 0)

        def for_tail(fn):
            def per_q(q, carry):
                fn(tail_copy(q))
                return carry
            lax.fori_loop(0, n_tail, per_q, 0)

        for_gaps(lambda cp: cp.start())
        for_tail(lambda cp: cp.start())
        for_gaps(lambda cp: cp.wait())
        for_tail(lambda cp: cp.wait())


def _dispatch(slot3, h2, fstart, flen, total, xs_rows):
    t_all = h2.shape[0]
    grid_spec = pltpu.PrefetchScalarGridSpec(
        num_scalar_prefetch=3,
        grid=(t_all // TD_TOK,),
        in_specs=[pl.BlockSpec((1, 1, TD_TOK * TOP_K), lambda i, a, b, c: (i, 0, 0), memory_space=pltpu.SMEM),
                  pl.BlockSpec((TD_TOK, D_MODEL), lambda i, a, b, c: (i, 0))],
        out_specs=pl.BlockSpec(memory_space=pl.ANY),
        scratch_shapes=[pltpu.VMEM((XS_ALIGN, D_MODEL), F32), pltpu.SemaphoreType.DMA(()),
                        pltpu.SemaphoreType.DMA(())],
    )
    return pl.pallas_call(
        functools.partial(_dispatch_kernel, xs_rows=xs_rows),
        grid_spec=grid_spec,
        out_shape=jax.ShapeDtypeStruct((xs_rows, D_MODEL), F32),
        compiler_params=_cparams(("arbitrary",)),
        name="dispatch",
    )(fstart, flen, total, slot3, h2)


X_RING = 4


def _expert_kernel(bstart_ref, nblk_ref, nused_ref, xrow_ref, xs_hbm, wgu_ref, wdn_ref, ys_hbm,
                   xbuf, ybuf, wgu16, wdn16, sem_in, sem_out, *, n_blocks):
    e = pl.program_id(0)
    nused = nused_ref[0]
    n = nblk_ref[e]
    g0 = bstart_ref[e]

    def in_copy(g, s):
        start = pl.multiple_of(xrow_ref[g], XS_ALIGN)
        return pltpu.make_async_copy(xs_hbm.at[pl.ds(start, MOE_BLOCK), :], xbuf.at[s], sem_in.at[s])

    def out_copy(g, s):
        return pltpu.make_async_copy(ybuf.at[s], ys_hbm.at[pl.ds(g * MOE_BLOCK, MOE_BLOCK), :], sem_out.at[s])

    @pl.when(e == 0)
    def _():
        for q in range(X_RING - 1):
            @pl.when(q < nused)
            def _():
                in_copy(q, q).start()

    @pl.when(n > 0)
    def _():
        wgu16[...] = wgu_ref[0].astype(BF16)
        wdn16[...] = wdn_ref[0].astype(BF16)

    def block(b, carry):
        g = g0 + b
        s = g % 2
        ahead = g + (X_RING - 1)

        @pl.when(ahead < nused)
        def _():
            in_copy(ahead, ahead % X_RING).start()

        in_copy(g, g % X_RING).wait()
        gu = _dot(xbuf[g % X_RING].astype(BF16), wgu16[...])
        gate = gu[:, :EXPERT_DIM]
        act = (gate * _sigmoid(gate) * gu[:, EXPERT_DIM:]).astype(BF16)
        y = _dot(act, wdn16[...])

        @pl.when(g >= 2)
        def _():
            out_copy(g - 2, s).wait()

        ybuf[s] = y
        out_copy(g, s).start()
        return carry

    lax.fori_loop(0, n, block, 0)

    @pl.when(e == pl.num_programs(0) - 1)
    def _():
        @pl.when(nused >= 2)
        def _():
            out_copy(nused - 2, nused % 2).wait()

        @pl.when(nused >= 1)
        def _():
            out_copy(nused - 1, (nused - 1) % 2).wait()

        ybuf[0] = jnp.zeros((MOE_BLOCK, D_MODEL), F32)

        def tail(g, carry):
            out_copy(g, 0).start()
            out_copy(g, 0).wait()
            return carry

        lax.fori_loop(nused, n_blocks, tail, 0)


def _experts(xs, bstart, nblk, nused, xrow, w_gu, w_down):
    n_blocks = xrow.shape[0]
    grid_spec = pltpu.PrefetchScalarGridSpec(
        num_scalar_prefetch=4,
        grid=(N_EXPERTS,),
        in_specs=[pl.BlockSpec(memory_space=pl.ANY),
                  pl.BlockSpec((1, D_MODEL, 2 * EXPERT_DIM), lambda e, bs, nb, nu, xr: (e, 0, 0)),
                  pl.BlockSpec((1, EXPERT_DIM, D_MODEL), lambda e, bs, nb, nu, xr: (e, 0, 0))],
        out_specs=pl.BlockSpec(memory_space=pl.ANY),
        scratch_shapes=[pltpu.VMEM((X_RING, MOE_BLOCK, D_MODEL), F32),
                        pltpu.VMEM((2, MOE_BLOCK, D_MODEL), F32),
                        pltpu.VMEM((D_MODEL, 2 * EXPERT_DIM), BF16),
                        pltpu.VMEM((EXPERT_DIM, D_MODEL), BF16),
                        pltpu.SemaphoreType.DMA((X_RING,)), pltpu.SemaphoreType.DMA((2,))],
    )
    return pl.pallas_call(
        functools.partial(_expert_kernel, n_blocks=n_blocks),
        grid_spec=grid_spec,
        out_shape=jax.ShapeDtypeStruct((n_blocks * MOE_BLOCK, D_MODEL), F32),
        compiler_params=_cparams(("arbitrary",)),
        name="experts",
    )(bstart, nblk, nused, xrow, xs, w_gu, w_down)


TC_TOK = 128


def _combine_kernel(slot_ref, slot_next_ref, wt_ref, x1_ref, h2_ref, g2_ref, sgu_ref, sdn_ref, lg_ref, lb_ref,
                    y_hbm, o_ref, gbuf, sems, *, n_tiles, alpha):
    i = pl.program_id(0)

    def row_copy(sref, b, j, k):
        return pltpu.make_async_copy(y_hbm.at[pl.ds(sref[0, 0, j * TOP_K + k], 1), :],
                                     gbuf.at[b, k, pl.ds(j, 1), :], sems.at[b])

    def wait_tile(b):
        for k in range(TOP_K):
            pltpu.make_async_copy(y_hbm.at[pl.ds(0, TC_TOK), :], gbuf.at[b, k], sems.at[b]).wait()

    @pl.when(i == 0)
    def _():
        def body(j, carry):
            for k in range(TOP_K):
                row_copy(slot_ref, 0, j, k).start(priority=k % 2)
            return carry
        lax.fori_loop(0, TC_TOK, body, 0)

    b = i % 2
    nb = 1 - b

    def issue_next(j0, j1):
        for j in range(j0, j1):
            for k in range(TOP_K):
                row_copy(slot_next_ref, nb, j, k).start(priority=k % 2)

    issue_next(0, TC_TOK // 2)
    h = h2_ref[...].astype(BF16)
    su = _dot(h, sgu_ref[...])
    sg = su[:, :EXPERT_DIM]
    act = (sg * _sigmoid(sg) * su[:, EXPERT_DIM:]).astype(BF16)
    ffn = _dot(act, sdn_ref[...])
    wait_tile(b)
    issue_next(TC_TOK // 2, TC_TOK)
    for k in range(TOP_K):
        ffn = ffn + gbuf[b, k] * wt_ref[:, k:k + 1]
    o_ref[...] = _layer_norm(alpha * x1_ref[...] + g2_ref[0] * ffn, lg_ref[...], lb_ref[...])

    @pl.when(i == n_tiles - 1)
    def _():
        wait_tile(nb)


def _combine(slot3, wts_t, x1, h2, g2, sh_gu, sh_dn, ln_g, ln_b, y_sorted, row_of_tile, alpha):
    t_all = x1.shape[0]
    n_tiles = t_all // TC_TOK
    tile = lambda w: pl.BlockSpec((TC_TOK, w), lambda i: (i, 0))
    full = lambda a: pl.BlockSpec(a.shape, lambda i, _n=a.ndim: (0,) * _n)
    return pl.pallas_call(
        functools.partial(_combine_kernel, n_tiles=n_tiles, alpha=alpha),
        grid=(n_tiles,),
        in_specs=[pl.BlockSpec((1, 1, TC_TOK * TOP_K), lambda i: (i, 0, 0), memory_space=pltpu.SMEM),
                  pl.BlockSpec((1, 1, TC_TOK * TOP_K), lambda i: (jnp.minimum(i + 1, n_tiles - 1), 0, 0),
                               memory_space=pltpu.SMEM),
                  tile(TOP_K), tile(D_MODEL), tile(D_MODEL),
                  pl.BlockSpec((1, 1, D_MODEL), lambda i: (row_of_tile(i), 0, 0)),
                  full(sh_gu), full(sh_dn), full(ln_g), full(ln_b),
                  pl.BlockSpec(memory_space=pl.ANY)],
        out_specs=tile(D_MODEL),
        out_shape=jax.ShapeDtypeStruct((t_all, D_MODEL), F32),
        scratch_shapes=[pltpu.VMEM((2, TOP_K, TC_TOK, D_MODEL), F32), pltpu.SemaphoreType.DMA((2,))],
        compiler_params=_cparams(("arbitrary",)),
        name="combine",
    )(slot3, slot3, wts_t, x1, h2, g2, sh_gu, sh_dn, ln_g, ln_b, y_sorted)


def kernel(x_prompt, x_sample, state_rwkv, c, c_ctx, w_ada, b_ada, w_in, mu_shift, w_decay0, w_decay2, w_a0, w_a2, w_g2, k_k, k_a, r_k, lnx_g, lnx_b, conv_w, conv_b, conv_ln_g, conv_ln_b, w_out, ln1_g, ln1_b, router_w, router_bias, expert_w_gu, expert_w_down, shared_w_gu, shared_w_down, ln2_g, ln2_b):
    depth = w_ada.shape[0]
    assert depth == 1
    alpha = (2.0 * depth) ** 0.25
    n_ctx, t_ctx, _ = x_prompt.shape
    n_lat, t_lat, _ = x_sample.shape
    tok_ctx = n_ctx * t_ctx
    tok_lat = n_lat * t_lat
    t_all = tok_ctx + tok_lat
    tm = 1024
    assert tok_ctx % tm == 0 and t_lat % tm == 0 and t_lat // GRID_W * GRID_W == t_lat
    l = 0

    x_all = jnp.concatenate([x_prompt.reshape(tok_ctx, D_MODEL), x_sample.reshape(tok_lat, D_MODEL)], axis=0)

    cond8 = jnp.zeros((8, D_MODEL), F32).at[0].set(c_ctx).at[1:1 + n_lat].set(c)
    mod = _ada(cond8, w_ada[l], b_ada[l])
    sh1, sc1, g1, sh2, sc2, g2 = [m.reshape(8, 1, D_MODEL) for m in jnp.split(mod, 6, axis=-1)]

    def row_of_tile_for(tile_rows):
        ctx_tiles = tok_ctx // tile_rows
        per_seq = t_lat // tile_rows
        return lambda i: jnp.where(i < ctx_tiles, 0, 1 + (i - ctx_tiles) // per_seq)

    w_in_a = w_in[l][:, :RWKV_IN].astype(BF16)
    w_in_b = w_in[l][:, RWKV_IN:].astype(BF16)
    proj_a, proj_b = _in_proj(x_all, sc1, sh1, w_in_a, w_in_b, tm, row_of_tile_for(tm))

    zeros64 = jnp.zeros((64, 2 * RW), F32)
    wd2 = jnp.concatenate([w_decay2[l, 0], w_decay2[l, 1]], axis=1)
    wa2 = jnp.concatenate([w_a2[l, 0], w_a2[l, 1]], axis=1)
    w_lora = jnp.concatenate([jnp.concatenate([wd2, zeros64], axis=1),
                              jnp.concatenate([zeros64, wa2], axis=1)], axis=0).astype(BF16)
    row = lambda v: v.reshape(1, -1)
    rwkv_w = [row(mu_shift[l]), w_lora, w_g2[l].astype(BF16), row(w_decay0[l]), row(w_a0[l]),
              row(k_k[l]), row(k_a[l]), row(r_k[l]), row(lnx_g[l]), row(lnx_b[l])]
    consts = _rwkv_consts()

    def to_cat(s):
        b = s.shape[0]
        s = s.reshape(b, 2, 2, 4, HD, HD)
        return jnp.transpose(s, (0, 1, 2, 5, 3, 4)).reshape(b, 4, HD, GW)

    def from_cat(s):
        b = s.shape[0]
        s = s.reshape(b, 2, 2, HD, 4, HD)
        return jnp.transpose(s, (0, 1, 2, 4, 5, 3)).reshape(b, 2, 8, HD, HD)

    s0_ctx = jnp.zeros((n_ctx, 4, HD, GW), F32)
    s0_lat = to_cat(state_rwkv[:, l].astype(F32))
    ya_c, sfin_ctx = _rwkv(proj_a, s0_ctx, rwkv_w, consts, t_ctx, n_ctx, 0)
    ya_l, _ = _rwkv(proj_a, s0_lat, rwkv_w, consts, t_lat, n_lat, tok_ctx // t_lat)
    new_state = from_cat(sfin_ctx)[:, None].astype(x_prompt.dtype)

    conv_wts = [conv_w[l], row(conv_b[l]), row(conv_ln_g[l]), row(conv_ln_b[l])]
    yb_c = _conv(proj_b, conv_wts, t_ctx, n_ctx, 0, False)
    yb_l = _conv(proj_b, conv_wts, t_lat, n_lat, tok_ctx // t_lat, True)

    w_oa = w_out[l][:RW].astype(BF16)
    w_ob = w_out[l][RW:].astype(BF16)
    x1, h2 = _out_proj(ya_c, ya_l, yb_c, yb_l, x_all, g1, sc2, sh2, w_oa, w_ob, row(ln1_g[l]), row(ln1_b[l]), tm,
                       row_of_tile_for(tm), alpha)

    rwt = router_w[l].T
    rwt_hi = rwt.astype(BF16)
    rwt_lo = (rwt - rwt_hi.astype(F32)).astype(BF16)
    eidx, wts, pos, counts = _route(h2, rwt_hi, rwt_lo, router_bias[l].reshape(N_EXPERTS, 1), 512)

    n_assign = t_all * TOP_K
    n_blocks = (n_assign + N_EXPERTS * (MOE_BLOCK - 1) + MOE_BLOCK - 1) // MOE_BLOCK
    cnt = counts.reshape(N_EXPERTS).astype(jnp.int32)
    padded = (cnt + MOE_BLOCK - 1) // MOE_BLOCK * MOE_BLOCK
    pend = jnp.cumsum(padded)
    pstart = pend - padded
    padded_x = (cnt + XS_ALIGN - 1) // XS_ALIGN * XS_ALIGN
    pend_x = jnp.cumsum(padded_x)
    pstart_x = pend_x - padded_x
    xs_rows = n_assign + N_EXPERTS * (XS_ALIGN - 1) // XS_ALIGN * XS_ALIGN + MOE_BLOCK
    slot_x, slot_y = _slots(eidx, pos, jnp.stack([pstart_x, pstart], axis=1).astype(F32), 512)
    per_tile = lambda s: s.T.reshape(t_all // TC_TOK, 1, TC_TOK * TOP_K)
    xs = _dispatch(per_tile(slot_x), h2, (pstart_x + cnt).astype(jnp.int32), (padded_x - cnt).astype(jnp.int32),
                   pend_x[-1].astype(jnp.int32).reshape(1), xs_rows)
    bstart = (pstart // MOE_BLOCK).astype(jnp.int32)
    nblk = (padded // MOE_BLOCK).astype(jnp.int32)
    nused = (pend[-1] // MOE_BLOCK).astype(jnp.int32).reshape(1)
    blk_ids = jnp.arange(n_blocks, dtype=jnp.int32)
    blk_e = jnp.minimum(jnp.sum((pend[None, :] <= (blk_ids * MOE_BLOCK)[:, None]).astype(jnp.int32), axis=1),
                        N_EXPERTS - 1)
    shift = jnp.sum(jnp.where(blk_e[:, None] == jnp.arange(N_EXPERTS, dtype=jnp.int32)[None, :],
                              (pstart_x - pstart)[None, :], 0), axis=1)
    xrow = jnp.clip(blk_ids * MOE_BLOCK + shift, 0, xs_rows - MOE_BLOCK).astype(jnp.int32)
    y_sorted = _experts(xs, bstart, nblk, nused, xrow, expert_w_gu[l], expert_w_down[l])
    y_all = _combine(per_tile(slot_y), wts.T, x1, h2, g2, shared_w_gu[l].astype(BF16), shared_w_down[l].astype(BF16),
                     row(ln2_g[l]), row(ln2_b[l]), y_sorted, row_of_tile_for(TC_TOK), alpha)

    y_prompt = y_all[:tok_ctx].reshape(n_ctx, t_ctx, D_MODEL)
    y_sample = y_all[tok_ctx:].reshape(n_lat, t_lat, D_MODEL)
    return (y_prompt, y_sample, new_state)
```

```python
import functools
import math

import numpy as np
import jax
import jax.numpy as jnp
from jax import lax
from jax.experimental import pallas as pl
from jax.experimental.pallas import tpu as pltpu

F32 = jnp.float32
BF16 = jnp.bfloat16

D_MODEL = 1024
RW = 512
HD = 64
CW = 512
CONV_K = 31
RWKV_IN = 3 * RW + 64 + 64 + 128
N_EXPERTS = 256
TOP_K = 8
N_GROUPS = 8
TOPK_GROUPS = 4
EXPERT_DIM = 256
ROUTED_SCALE = 2.5
MOE_BLOCK = 128
GRID_W = 64
LN_EPS = 1e-5
GN_EPS = 64e-5
CHUNK = 64
GW = 256
DECAY_SCALE = math.exp(-0.5)
VMEM_LIMIT = 56 * 1024 * 1024


def _cparams(sem):
    return pltpu.CompilerParams(dimension_semantics=sem, vmem_limit_bytes=VMEM_LIMIT)


def _split_bf16(x):
    hi = x.astype(BF16)
    lo = (x - hi.astype(F32)).astype(BF16)
    return hi, lo


def _dot(a, b):
    return jnp.dot(a, b, preferred_element_type=F32)


def _dot_nt(a, b):
    return lax.dot_general(a, b, (((1,), (1,)), ((), ())), preferred_element_type=F32)


def _dot_tn(a, b):
    return lax.dot_general(a, b, (((0,), (0,)), ((), ())), preferred_element_type=F32)


def _sigmoid(x):
    return 1.0 / (1.0 + jnp.exp(-x))


def _layer_norm(x, g, b):
    mu = jnp.mean(x, axis=-1, keepdims=True)
    d = x - mu
    var = jnp.mean(d * d, axis=-1, keepdims=True)
    return d * lax.rsqrt(var + LN_EPS) * g + b


def _ada_kernel(c_ref, w_ref, b_ref, o_ref):
    c = c_ref[...]
    s = c * _sigmoid(c)
    s_hi, s_lo = _split_bf16(s)
    w_hi, w_lo = _split_bf16(w_ref[...])
    o_ref[...] = _dot(s_hi, w_hi) + _dot(s_lo, w_hi) + _dot(s_hi, w_lo) + b_ref[...]


def _ada(cond8, w_ada, b_ada):
    n = w_ada.shape[1]
    tn = 1536
    return pl.pallas_call(
        _ada_kernel,
        grid=(n // tn,),
        in_specs=[pl.BlockSpec((8, D_MODEL), lambda j: (0, 0)),
                  pl.BlockSpec((D_MODEL, tn), lambda j: (0, j)),
                  pl.BlockSpec((1, tn), lambda j: (0, j))],
        out_specs=pl.BlockSpec((8, tn), lambda j: (0, j)),
        out_shape=jax.ShapeDtypeStruct((8, n), F32),
        compiler_params=_cparams(("arbitrary",)),
        name="ada",
    )(cond8, w_ada, b_ada.reshape(1, n))


def _two_source_specs(tm, width, ctx_tiles):
    return [pl.BlockSpec((tm, width), lambda i: (jnp.minimum(i, ctx_tiles - 1), 0)),
            pl.BlockSpec((tm, width), lambda i: (jnp.maximum(i - ctx_tiles, 0), 0))]


def _inproj_kernel(xc_ref, xl_ref, sc_ref, sh_ref, wa_ref, wb_ref, oa_ref, ob_ref, *, ctx_tiles):
    x = jnp.where(pl.program_id(0) < ctx_tiles, xc_ref[...], xl_ref[...])
    h = (x * (1.0 + sc_ref[0]) + sh_ref[0]).astype(BF16)
    oa_ref[...] = _dot(h, wa_ref[...])
    ob_ref[...] = _dot(h, wb_ref[...])


def _in_proj(x_ctx, x_lat, sc1, sh1, w_a, w_b, tm, row_of_tile):
    t_all = x_ctx.shape[0] + x_lat.shape[0]
    ctx_tiles = x_ctx.shape[0] // tm
    return pl.pallas_call(
        functools.partial(_inproj_kernel, ctx_tiles=ctx_tiles),
        grid=(t_all // tm,),
        in_specs=_two_source_specs(tm, D_MODEL, ctx_tiles) + [
                  pl.BlockSpec((1, 1, D_MODEL), lambda i: (row_of_tile(i), 0, 0)),
                  pl.BlockSpec((1, 1, D_MODEL), lambda i: (row_of_tile(i), 0, 0)),
                  pl.BlockSpec(w_a.shape, lambda i: (0, 0)),
                  pl.BlockSpec(w_b.shape, lambda i: (0, 0))],
        out_specs=[pl.BlockSpec((tm, RWKV_IN), lambda i: (i, 0)),
                   pl.BlockSpec((tm, 2 * CW), lambda i: (i, 0))],
        out_shape=[jax.ShapeDtypeStruct((t_all, RWKV_IN), F32),
                   jax.ShapeDtypeStruct((t_all, 2 * CW), F32)],
        compiler_params=_cparams(("arbitrary",)),
        name="in_proj",
    )(x_ctx, x_lat, sc1, sh1, w_a, w_b)


def _rwkv_consts():
    c = CHUNK
    t = np.arange(c)[:, None]
    s = np.arange(GW)[None, :] % c
    sl = (s < t).astype(np.float32)
    il = (s <= t).astype(np.float32)
    su = (s > t).astype(np.float32)
    iu = (s >= t).astype(np.float32)
    eye_cat = (s == t).astype(np.float32)
    tri = np.stack([sl, il, su, iu, eye_cat])
    r = np.arange(GW)
    mask_bd = (r[:, None] // HD == r[None, :] // HD).astype(np.float32)
    eye = np.eye(GW, dtype=np.float32)
    cum = np.stack([np.tril(np.ones((c, c), np.float32)), np.triu(np.ones((c, c), np.float32))])
    q = np.arange(RW)
    seg = (q[:, None] // HD == q[None, :] // HD).astype(np.float32)
    return (jnp.asarray(tri), jnp.asarray(mask_bd, dtype=BF16), jnp.asarray(mask_bd), jnp.asarray(eye),
            jnp.asarray(cum, dtype=BF16), jnp.asarray(seg, dtype=BF16))


def _rwkv_kernel(p_ref, s0_ref, mu_ref, wlora_ref, wg2_ref, wd0_ref, wa0_ref, kk_ref, ka_ref, rk_ref,
                 lng_ref, lnb_ref, tri_ref, mbd16_ref, mbd_ref, eye_ref, cum_ref, seg_ref,
                 ya_ref, sfin_ref,
                 r_s, v_s, kk_s, g_s, bon_s, a_s, lw_s, kd_s, y_s, st_s, *, seq_len):
    nc = seq_len // CHUNK
    seg = seg_ref[...]

    def seg_sum(x):
        hi, lo = _split_bf16(x)
        return _dot(hi, seg) + _dot(lo, seg)

    def phase_a(c, carry):
        t0 = pl.multiple_of(c * CHUNK, CHUNK)
        cur = p_ref[pl.ds(t0, CHUNK), :]
        prev_row = p_ref[pl.ds(jnp.maximum(t0 - 1, 0), 1), :] * (t0 > 0).astype(F32)
        next_row = p_ref[pl.ds(jnp.minimum(t0 + CHUNK, seq_len - 1), 1), :] * (t0 + CHUNK < seq_len).astype(F32)
        row = lax.broadcasted_iota(jnp.int32, cur.shape, 0)
        prev = jnp.where(row == 0, prev_row, pltpu.roll(cur, 1, 0))
        nxt = jnp.where(row == CHUNK - 1, next_row, pltpu.roll(cur, CHUNK - 1, 0))
        p = cur + mu_ref[...] * (0.5 * (prev + nxt) - cur)
        r = p[:, 0:RW]
        k = p[:, RW:2 * RW]
        v = p[:, 2 * RW:3 * RW]
        z = p[:, 3 * RW:3 * RW + 128]
        gd = p[:, 3 * RW + 128:3 * RW + 256]
        lane = lax.broadcasted_iota(jnp.int32, z.shape, 1)
        z = jnp.where(lane < 64, jnp.tanh(z), z)
        lora = _dot(z.astype(BF16), wlora_ref[...])
        g = _dot(_sigmoid(gd).astype(BF16), wg2_ref[...])
        kks = k * kk_ref[...]
        kk = kks * lax.rsqrt(seg_sum(kks * kks) + 1e-12)
        rows = pl.ds(t0, CHUNK)
        r_s[rows, :] = r
        v_s[rows, :] = v.astype(BF16)
        kk_s[rows, :] = kk
        g_s[rows, :] = g
        kd_sum = jnp.zeros_like(k)
        for d in range(2):
            w_logit = wd0_ref[:, d * RW:(d + 1) * RW] + lora[:, d * RW:(d + 1) * RW]
            a = _sigmoid(wa0_ref[:, d * RW:(d + 1) * RW] + lora[:, (2 + d) * RW:(3 + d) * RW])
            kd = k * (1.0 + (a - 1.0) * ka_ref[...])
            lw_s[d, rows, :] = -DECAY_SCALE * _sigmoid(w_logit)
            a_s[d, rows, :] = a
            kd_s[d, rows, :] = kd
            kd_sum = kd_sum + kd
        bon_s[rows, :] = seg_sum(r * kd_sum * rk_ref[...]) * v
        return carry

    lax.fori_loop(0, nc, phase_a, 0)

    mbd16 = mbd16_ref[...]
    mbd = mbd_ref[...]
    eye = eye_ref[...]

    def bd(x16):
        return jnp.concatenate([x16, x16, x16, x16], axis=0) * mbd16

    for u in range(4):
        s0 = s0_ref[0, u]
        st_s[u] = jnp.concatenate([s0, s0, s0, s0], axis=0) * mbd

    def prep(d, grp, t0):
        rows = pl.ds(t0, CHUNK)
        lanes = slice(grp * GW, (grp + 1) * GW)
        lw = lw_s[d, rows, lanes]
        hi, lo = _split_bf16(lw)
        cl = _dot(cum_ref[d], hi) + _dot(cum_ref[d], lo)
        e_cl = jnp.exp(cl)
        e_ce = jnp.exp(cl - lw)
        e_ncl = jnp.exp(-cl)
        tot = cl[CHUNK - 1:CHUNK, :] if d == 0 else cl[0:1, :]
        wc = jnp.exp(tot)
        kk = kk_s[rows, lanes]
        rt = r_s[rows, lanes] * e_cl
        bt = a_s[d, rows, lanes] * kk * e_ncl
        kt = kd_s[d, rows, lanes] * e_ncl
        kq16 = (kk * e_ce).astype(BF16)
        return dict(d=d, u=d * 2 + grp, rows=rows, lanes=lanes, wc=wc, rt=rt, kq16=kq16,
                    v16=v_s[rows, lanes], bt16=bt.astype(BF16), kt16=kt.astype(BF16),
                    btw16=(bt * wc).astype(BF16), ktw16=(kt * wc).astype(BF16),
                    g16=jnp.concatenate([kq16, rt.astype(BF16)], axis=0))

    def phase_b(i, carry):
        tf = pl.multiple_of(i * CHUNK, CHUNK)
        tb = pl.multiple_of((nc - 1 - i) * CHUNK, CHUNK)
        us = [prep(0, 0, tf), prep(1, 0, tb), prep(0, 1, tf), prep(1, 1, tb)]
        each = lambda fn: [fn(q) for q in us]
        strict = lambda q: tri_ref[2 * q['d']]
        incl = lambda q: tri_ref[2 * q['d'] + 1]
        out_b = each(lambda q: _dot_nt(q['g16'], bd(q['bt16'])))
        out_k = each(lambda q: _dot_nt(q['g16'], bd(q['kt16'])))
        a_bk = [o[:CHUNK] * strict(q) for o, q in zip(out_b, us)]
        p_rb = [(o[CHUNK:] * incl(q)).astype(BF16) for o, q in zip(out_b, us)]
        a_kk = [(o[:CHUNK] * strict(q)).astype(BF16) for o, q in zip(out_k, us)]
        p_rk = [(o[CHUNK:] * incl(q)).astype(BF16) for o, q in zip(out_k, us)]
        tinv = [tri_ref[4] - a for a in a_bk]
        pw = [a.astype(BF16) for a in a_bk]
        for _ in range(5):
            pw = [_dot(p, bd(p)).astype(BF16) for p in pw]
            tinv = [t + _dot(p, bd(t.astype(BF16))) for t, p in zip(tinv, pw)]
        bdv = each(lambda q: bd(q['v16']))
        akkv = [_dot(a, b) for a, b in zip(a_kk, bdv)]
        x = [_dot(t.astype(BF16), jnp.concatenate([bd(q['kq16']), bd(k.astype(BF16))], axis=1))
             for t, q, k in zip(tinv, us, akkv)]
        qu16 = [z.astype(BF16) for z in x]
        tn1 = [_dot_tn(q['btw16'], z) for q, z in zip(us, qu16)]
        tn2 = each(lambda q: _dot_tn(q['ktw16'], q['v16']))
        r1 = [_dot(p, jnp.concatenate([bd(z[:, :GW]), bd(z[:, GW:])], axis=1)) for p, z in zip(p_rb, qu16)]
        y0 = [_dot(p, b) - r[:, GW:] for p, b, r in zip(p_rk, bdv, r1)]
        for q, t1, t2, r, y in zip(us, tn1, tn2, r1, y0):
            mc = (eye * q['wc'] - t1[:, :GW]) * mbd
            ncm = (t2 - t1[:, GW:]) * mbd
            rh = q['rt'] - r[:, :GW]
            st16 = st_s[q['u']].astype(BF16)
            big = _dot(jnp.concatenate([rh.astype(BF16), mc.astype(BF16)], axis=0), st16)
            y_s[q['d'], q['rows'], q['lanes']] = big[:CHUNK] + y
            st_s[q['u']] = big[CHUNK:] + ncm
        return carry

    lax.fori_loop(0, nc, phase_b, 0)

    for u in range(4):
        st = st_s[u]
        st_t = st.T
        sfin_ref[0, u] = st_t[:, 0:HD] + st_t[:, HD:2 * HD] + st_t[:, 2 * HD:3 * HD] + st_t[:, 3 * HD:4 * HD]

    def phase_c(c, carry):
        rows = pl.ds(pl.multiple_of(c * CHUNK, CHUNK), CHUNK)
        y = y_s[0, rows, :] + y_s[1, rows, :]
        mean = seg_sum(y) * (1.0 / HD)
        dlt = y - mean
        var = seg_sum(dlt * dlt) * (1.0 / HD)
        yn = dlt * lax.rsqrt(var + GN_EPS)
        ya_ref[rows, :] = (yn * lng_ref[...] + lnb_ref[...] + bon_s[rows, :]) * g_s[rows, :]
        return carry

    lax.fori_loop(0, nc, phase_c, 0)


def _rwkv(proj_a, s0_cat, wts, consts, seq_len, n_seq, first_block):
    full = lambda a: pl.BlockSpec(a.shape, lambda s, _n=a.ndim: (0,) * _n)
    in_specs = [pl.BlockSpec((seq_len, RWKV_IN), lambda s: (first_block + s, 0)),
                pl.BlockSpec((1, 4, HD, GW), lambda s: (s, 0, 0, 0))]
    in_specs += [full(a) for a in wts] + [full(a) for a in consts]
    args = [proj_a, s0_cat] + list(wts) + list(consts)
    seq = (seq_len, RW)
    return pl.pallas_call(
        functools.partial(_rwkv_kernel, seq_len=seq_len),
        grid=(n_seq,),
        in_specs=in_specs,
        out_specs=[pl.BlockSpec((seq_len, RW), lambda s: (s, 0)),
                   pl.BlockSpec((1, 4, GW, HD), lambda s: (s, 0, 0, 0))],
        out_shape=[jax.ShapeDtypeStruct((n_seq * seq_len, RW), F32),
                   jax.ShapeDtypeStruct((n_seq, 4, GW, HD), F32)],
        scratch_shapes=[pltpu.VMEM(seq, F32), pltpu.VMEM(seq, BF16), pltpu.VMEM(seq, F32), pltpu.VMEM(seq, F32),
                        pltpu.VMEM(seq, F32),
                        pltpu.VMEM((2,) + seq, F32), pltpu.VMEM((2,) + seq, F32), pltpu.VMEM((2,) + seq, F32),
                        pltpu.VMEM((2,) + seq, F32), pltpu.VMEM((4, GW, GW), F32)],
        compiler_params=_cparams(("arbitrary",)),
        name="rwkv_%d" % seq_len,
    )(*args)


CONV_PAD = 16
VPAD = (CONV_K // 2) * GRID_W


def _conv_kernel(u_ref, w_ref, b_ref, g_ref, beta_ref, o_ref, pad_s, vpad_s, y_s, *, seq_len, grid):
    rb = 256
    glu = u_ref[:, :CW] * _sigmoid(u_ref[:, CW:])
    zeros = jnp.zeros((CONV_PAD, CW), F32)
    pad_s[0:CONV_PAD, :] = zeros
    pad_s[CONV_PAD + seq_len:2 * CONV_PAD + seq_len, :] = zeros
    pad_s[CONV_PAD:CONV_PAD + seq_len, :] = glu
    half = CW // 2
    if grid:
        vz = jnp.zeros((VPAD, half), F32)
        vpad_s[0:VPAD, :] = vz
        vpad_s[VPAD + seq_len:2 * VPAD + seq_len, :] = vz
        vpad_s[VPAD:VPAD + seq_len, :] = glu[:, half:]
    n_h_lanes = (half if grid else CW) // 128
    for r0 in range(0, seq_len, rb):
        for lb in range(n_h_lanes):
            lanes = slice(lb * 128, (lb + 1) * 128)
            acc = jnp.zeros((rb, 128), F32)
            if grid:
                col = lax.broadcasted_iota(jnp.int32, (rb, 128), 0) % GRID_W
            for k in range(CONV_K):
                win = pad_s[r0 + k + 1:r0 + k + 1 + rb, lanes]
                term = win * w_ref[k:k + 1, lanes]
                if grid:
                    src = col + (k - CONV_K // 2)
                    term = jnp.where((src >= 0) & (src < GRID_W), term, 0.0)
                acc = acc + term
            y_s[r0:r0 + rb, lanes] = acc
        if grid:
            for lb in range(half // 128):
                lanes = slice(lb * 128, (lb + 1) * 128)
                wl = slice(half + lb * 128, half + (lb + 1) * 128)
                acc = jnp.zeros((rb, 128), F32)
                for k in range(CONV_K):
                    start = VPAD + r0 + (k - CONV_K // 2) * GRID_W
                    acc = acc + vpad_s[start:start + rb, lanes] * w_ref[k:k + 1, wl]
                y_s[r0:r0 + rb, wl] = acc
    for r0 in range(0, seq_len, rb):
        y = y_s[r0:r0 + rb, :] + b_ref[...]
        yl = _layer_norm(y, g_ref[...], beta_ref[...])
        o_ref[r0:r0 + rb, :] = yl * _sigmoid(yl)


def _conv(proj_b, wts, seq_len, n_seq, first_block, grid):
    full = lambda a: pl.BlockSpec(a.shape, lambda s, _n=a.ndim: (0,) * _n)
    in_specs = [pl.BlockSpec((seq_len, 2 * CW), lambda s: (first_block + s, 0))] + [full(a) for a in wts]
    args = [proj_b] + list(wts)
    return pl.pallas_call(
        functools.partial(_conv_kernel, seq_len=seq_len, grid=grid),
        grid=(n_seq,),
        in_specs=in_specs,
        out_specs=pl.BlockSpec((seq_len, CW), lambda s: (s, 0)),
        out_shape=jax.ShapeDtypeStruct((n_seq * seq_len, CW), F32),
        scratch_shapes=[pltpu.VMEM((seq_len + 2 * CONV_PAD, CW), F32),
                        pltpu.VMEM((seq_len + 2 * VPAD if grid else 8, CW // 2), F32),
                        pltpu.VMEM((seq_len, CW), F32)],
        compiler_params=_cparams(("arbitrary",)),
        name="conv_%d" % seq_len,
    )(*args)


def _outproj_kernel(yac_ref, yal_ref, ybc_ref, ybl_ref, xc_ref, xl_ref, g1_ref, sc2_ref, sh2_ref, wa_ref, wb_ref,
                    lg_ref, lb_ref, x1_ref, h2_ref, *, alpha, ctx_tiles):
    is_ctx = pl.program_id(0) < ctx_tiles
    ya = jnp.where(is_ctx, yac_ref[...], yal_ref[...])
    yb = jnp.where(is_ctx, ybc_ref[...], ybl_ref[...])
    x = jnp.where(is_ctx, xc_ref[...], xl_ref[...])
    mix = _dot(ya.astype(BF16), wa_ref[...]) + _dot(yb.astype(BF16), wb_ref[...])
    x1 = _layer_norm(alpha * x + g1_ref[0] * mix, lg_ref[...], lb_ref[...])
    x1_ref[...] = x1
    h2_ref[...] = x1 * (1.0 + sc2_ref[0]) + sh2_ref[0]


def _out_proj(ya_c, ya_l, yb_c, yb_l, x_ctx, x_lat, g1, sc2, sh2, w_oa, w_ob, ln_g, ln_b, tm, row_of_tile, alpha):
    t_all = x_ctx.shape[0] + x_lat.shape[0]
    ctx_tiles = x_ctx.shape[0] // tm
    tile = lambda w: pl.BlockSpec((tm, w), lambda i: (i, 0))
    modrow = pl.BlockSpec((1, 1, D_MODEL), lambda i: (row_of_tile(i), 0, 0))
    full = lambda a: pl.BlockSpec(a.shape, lambda i, _n=a.ndim: (0,) * _n)
    return pl.pallas_call(
        functools.partial(_outproj_kernel, alpha=alpha, ctx_tiles=ctx_tiles),
        grid=(t_all // tm,),
        in_specs=_two_source_specs(tm, RW, ctx_tiles) + _two_source_specs(tm, CW, ctx_tiles)
        + _two_source_specs(tm, D_MODEL, ctx_tiles)
        + [modrow, modrow, modrow, full(w_oa), full(w_ob), full(ln_g), full(ln_b)],
        out_specs=[tile(D_MODEL), tile(D_MODEL)],
        out_shape=[jax.ShapeDtypeStruct((t_all, D_MODEL), F32), jax.ShapeDtypeStruct((t_all, D_MODEL), F32)],
        compiler_params=_cparams(("arbitrary",)),
        name="out_proj",
    )(ya_c, ya_l, yb_c, yb_l, x_ctx, x_lat, g1, sc2, sh2, w_oa, w_ob, ln_g, ln_b)


def _route_kernel(h_ref, wt_hi_ref, wt_lo_ref, bias_ref, ustrict_ref,
                  eidx_ref, wts_ref, pos_ref, cnt_ref, carry_s, *, tm):
    i = pl.program_id(0)

    @pl.when(i == 0)
    def _():
        carry_s[...] = jnp.zeros_like(carry_s)

    h_hi, h_lo = _split_bf16(h_ref[...])
    logits = _dot_nt(wt_hi_ref[...], h_hi) + _dot_nt(wt_hi_ref[...], h_lo) + _dot_nt(wt_lo_ref[...], h_hi)
    scores = _sigmoid(logits)
    sel = scores + bias_ref[...]
    neg = jnp.float32(-jnp.inf)
    gsz = N_EXPERTS // N_GROUPS
    gs = []
    for g in range(N_GROUPS):
        blk = sel[g * gsz:(g + 1) * gsz]
        m1 = jnp.max(blk, axis=0, keepdims=True)
        eq = blk == m1
        cnt = jnp.sum(eq.astype(F32), axis=0, keepdims=True)
        m2 = jnp.max(jnp.where(eq, neg, blk), axis=0, keepdims=True)
        gs.append(m1 + jnp.where(cnt >= 2.0, m1, m2))
    masked = []
    for g in range(N_GROUPS):
        rank = jnp.zeros_like(gs[g])
        for o in range(N_GROUPS):
            if o == g:
                continue
            beats = (gs[o] > gs[g]) | ((gs[o] == gs[g]) & (o < g))
            rank = rank + beats.astype(F32)
        keep = rank < float(TOPK_GROUPS)
        masked.append(jnp.where(keep, sel[g * gsz:(g + 1) * gsz], neg))
    cur = jnp.concatenate(masked, axis=0)
    iota_e = lax.broadcasted_iota(jnp.int32, cur.shape, 0).astype(F32)
    idxs, ws = [], []
    selmask = jnp.zeros(cur.shape, F32)
    for _ in range(TOP_K):
        m = jnp.max(cur, axis=0, keepdims=True)
        idx = jnp.min(jnp.where(cur == m, iota_e, float(N_EXPERTS)), axis=0, keepdims=True)
        onehot = iota_e == idx
        ws.append(jnp.sum(jnp.where(onehot, scores, 0.0), axis=0, keepdims=True))
        idxs.append(idx)
        cur = jnp.where(onehot, neg, cur)
        selmask = jnp.where(onehot, 1.0, selmask)
    pos = _dot(selmask.astype(BF16), ustrict_ref[...]) + carry_s[...]
    carry_s[...] = carry_s[...] + jnp.sum(selmask, axis=1, keepdims=True)
    cnt_ref[...] = carry_s[...]
    wsum = ws[0]
    for k in range(1, TOP_K):
        wsum = wsum + ws[k]
    pks = [jnp.sum(jnp.where(iota_e == idxs[k], pos, 0.0), axis=0, keepdims=True) for k in range(TOP_K)]
    eidx_ref[...] = jnp.concatenate(idxs, axis=0).astype(jnp.int32)
    wts_ref[...] = jnp.concatenate([w / wsum * ROUTED_SCALE for w in ws], axis=0)
    pos_ref[...] = jnp.concatenate(pks, axis=0).astype(jnp.int32)


def _route(h2, wt_hi, wt_lo, bias_col, tm):
    t_all = h2.shape[0]
    ustrict = jnp.asarray(np.triu(np.ones((tm, tm), np.float32), 1), dtype=BF16)
    full = lambda a: pl.BlockSpec(a.shape, lambda i, _n=a.ndim: (0,) * _n)
    tok = pl.BlockSpec((TOP_K, tm), lambda i: (0, i))
    return pl.pallas_call(
        functools.partial(_route_kernel, tm=tm),
        grid=(t_all // tm,),
        in_specs=[pl.BlockSpec((tm, D_MODEL), lambda i: (i, 0)), full(wt_hi), full(wt_lo), full(bias_col),
                  full(ustrict)],
        out_specs=[tok, tok, tok, pl.BlockSpec((N_EXPERTS, 1), lambda i: (0, 0))],
        out_shape=[jax.ShapeDtypeStruct((TOP_K, t_all), jnp.int32), jax.ShapeDtypeStruct((TOP_K, t_all), F32),
                   jax.ShapeDtypeStruct((TOP_K, t_all), jnp.int32), jax.ShapeDtypeStruct((N_EXPERTS, 1), F32)],
        scratch_shapes=[pltpu.VMEM((N_EXPERTS, 1), F32)],
        compiler_params=_cparams(("arbitrary",)),
        name="route",
    )(h2, wt_hi, wt_lo, bias_col, ustrict)


def _slots_kernel(eidx_ref, pos_ref, pstart_ref, slotx_ref, sloty_ref):
    tm = eidx_ref.shape[1]
    iota_e = lax.broadcasted_iota(jnp.int32, (N_EXPERTS, tm), 0)
    for c, out in enumerate((slotx_ref, sloty_ref)):
        rows = []
        for k in range(TOP_K):
            onehot = iota_e == eidx_ref[k:k + 1, :]
            rows.append(jnp.sum(jnp.where(onehot, pstart_ref[:, c:c + 1], 0.0), axis=0, keepdims=True))
        out[...] = jnp.concatenate(rows, axis=0).astype(jnp.int32) + pos_ref[...]


def _slots(eidx, pos, pstart_cols, tm):
    t_all = eidx.shape[1]
    tok = pl.BlockSpec((TOP_K, tm), lambda i: (0, i))
    return pl.pallas_call(
        _slots_kernel,
        grid=(t_all // tm,),
        in_specs=[tok, tok, pl.BlockSpec((N_EXPERTS, 2), lambda i: (0, 0))],
        out_specs=[tok, tok],
        out_shape=[jax.ShapeDtypeStruct((TOP_K, t_all), jnp.int32), jax.ShapeDtypeStruct((TOP_K, t_all), jnp.int32)],
        compiler_params=_cparams(("arbitrary",)),
        name="slots",
    )(eidx, pos, pstart_cols)


TD_TOK = 512
XS_ALIGN = 8


def _dispatch_kernel(fstart_ref, flen_ref, total_ref, slot_ref, h_ref, xs_out, zero_s, sem, zsem, *, xs_rows):
    i = pl.program_id(0)

    def body(j, carry):
        for k in range(TOP_K):
            pltpu.make_async_copy(h_ref.at[pl.ds(j, 1), :], xs_out.at[pl.ds(slot_ref[0, 0, j * TOP_K + k], 1), :],
                                  sem).start(priority=k % 2)
        return carry

    lax.fori_loop(0, TD_TOK, body, 0)
    for k in range(TOP_K):
        pltpu.make_async_copy(h_ref, xs_out.at[pl.ds(0, TD_TOK), :], sem).wait()

    @pl.when(i == pl.num_programs(0) - 1)
    def _():
        zero_s[...] = jnp.zeros_like(zero_s)
        total = total_ref[0]
        n_tail = (xs_rows - total) // XS_ALIGN

        def gap_copy(e, r):
            return pltpu.make_async_copy(zero_s.at[pl.ds(0, 1), :], xs_out.at[pl.ds(fstart_ref[e] + r, 1), :], zsem)

        def tail_copy(q):
            start = pl.multiple_of(total + q * XS_ALIGN, XS_ALIGN)
            return pltpu.make_async_copy(zero_s, xs_out.at[pl.ds(start, XS_ALIGN), :], zsem)

        def for_gaps(fn):
            def per_expert(e, carry):
                for r in range(XS_ALIGN - 1):
                    @pl.when(r < flen_ref[e])
                    def _():
                        fn(gap_copy(e, r))
                return carry
            lax.fori_loop(0, N_EXPERTS, per_expert, 0)

        def for_tail(fn):
            def per_q(q, carry):
                fn(tail_copy(q))
                return carry
            lax.fori_loop(0, n_tail, per_q, 0)

        for_gaps(lambda cp: cp.start())
        for_tail(lambda cp: cp.start())
        for_gaps(lambda cp: cp.wait())
        for_tail(lambda cp: cp.wait())


def _dispatch(slot3, h2, fstart, flen, total, xs_rows):
    t_all = h2.shape[0]
    grid_spec = pltpu.PrefetchScalarGridSpec(
        num_scalar_prefetch=3,
        grid=(t_all // TD_TOK,),
        in_specs=[pl.BlockSpec((1, 1, TD_TOK * TOP_K), lambda i, a, b, c: (i, 0, 0), memory_space=pltpu.SMEM),
                  pl.BlockSpec((TD_TOK, D_MODEL), lambda i, a, b, c: (i, 0))],
        out_specs=pl.BlockSpec(memory_space=pl.ANY),
        scratch_shapes=[pltpu.VMEM((XS_ALIGN, D_MODEL), F32), pltpu.SemaphoreType.DMA(()),
                        pltpu.SemaphoreType.DMA(())],
    )
    return pl.pallas_call(
        functools.partial(_dispatch_kernel, xs_rows=xs_rows),
        grid_spec=grid_spec,
        out_shape=jax.ShapeDtypeStruct((xs_rows, D_MODEL), F32),
        compiler_params=_cparams(("arbitrary",)),
        name="dispatch",
    )(fstart, flen, total, slot3, h2)


X_RING = 4


def _expert_kernel(bstart_ref, nblk_ref, nused_ref, xrow_ref, xs_hbm, wgu_ref, wdn_ref, ys_hbm,
                   xbuf, ybuf, wgu16, wdn16, sem_in, sem_out, *, n_blocks):
    e = pl.program_id(0)
    nused = nused_ref[0]
    n = nblk_ref[e]
    g0 = bstart_ref[e]

    def in_copy(g, s):
        start = pl.multiple_of(xrow_ref[g], XS_ALIGN)
        return pltpu.make_async_copy(xs_hbm.at[pl.ds(start, MOE_BLOCK), :], xbuf.at[s], sem_in.at[s])

    def out_copy(g, s):
        return pltpu.make_async_copy(ybuf.at[s], ys_hbm.at[pl.ds(g * MOE_BLOCK, MOE_BLOCK), :], sem_out.at[s])

    @pl.when(e == 0)
    def _():
        for q in range(X_RING - 1):
            @pl.when(q < nused)
            def _():
                in_copy(q, q).start()

    @pl.when(n > 0)
    def _():
        wgu16[...] = wgu_ref[0].astype(BF16)
        wdn16[...] = wdn_ref[0].astype(BF16)

    def block(b, carry):
        g = g0 + b
        s = g % 2
        ahead = g + (X_RING - 1)

        @pl.when(ahead < nused)
        def _():
            in_copy(ahead, ahead % X_RING).start()

        in_copy(g, g % X_RING).wait()
        gu = _dot(xbuf[g % X_RING].astype(BF16), wgu16[...])
        gate = gu[:, :EXPERT_DIM]
        act = (gate * _sigmoid(gate) * gu[:, EXPERT_DIM:]).astype(BF16)
        y = _dot(act, wdn16[...])

        @pl.when(g >= 2)
        def _():
            out_copy(g - 2, s).wait()

        ybuf[s] = y
        out_copy(g, s).start()
        return carry

    lax.fori_loop(0, n, block, 0)

    @pl.when(e == pl.num_programs(0) - 1)
    def _():
        @pl.when(nused >= 2)
        def _():
            out_copy(nused - 2, nused % 2).wait()

        @pl.when(nused >= 1)
        def _():
            out_copy(nused - 1, (nused - 1) % 2).wait()

        ybuf[0] = jnp.zeros((MOE_BLOCK, D_MODEL), F32)

        def tail(g, carry):
            out_copy(g, 0).start()
            out_copy(g, 0).wait()
            return carry

        lax.fori_loop(nused, n_blocks, tail, 0)


def _experts(xs, bstart, nblk, nused, xrow, w_gu, w_down):
    n_blocks = xrow.shape[0]
    grid_spec = pltpu.PrefetchScalarGridSpec(
        num_scalar_prefetch=4,
        grid=(N_EXPERTS,),
        in_specs=[pl.BlockSpec(memory_space=pl.ANY),
                  pl.BlockSpec((1, D_MODEL, 2 * EXPERT_DIM), lambda e, bs, nb, nu, xr: (e, 0, 0)),
                  pl.BlockSpec((1, EXPERT_DIM, D_MODEL), lambda e, bs, nb, nu, xr: (e, 0, 0))],
        out_specs=pl.BlockSpec(memory_space=pl.ANY),
        scratch_shapes=[pltpu.VMEM((X_RING, MOE_BLOCK, D_MODEL), F32),
                        pltpu.VMEM((2, MOE_BLOCK, D_MODEL), F32),
                        pltpu.VMEM((D_MODEL, 2 * EXPERT_DIM), BF16),
                        pltpu.VMEM((EXPERT_DIM, D_MODEL), BF16),
                        pltpu.SemaphoreType.DMA((X_RING,)), pltpu.SemaphoreType.DMA((2,))],
    )
    return pl.pallas_call(
        functools.partial(_expert_kernel, n_blocks=n_blocks),
        grid_spec=grid_spec,
        out_shape=jax.ShapeDtypeStruct((n_blocks * MOE_BLOCK, D_MODEL), F32),
        compiler_params=_cparams(("arbitrary",)),
        name="experts",
    )(bstart, nblk, nused, xrow, xs, w_gu, w_down)


TC_TOK = 128


def _combine_kernel(slot_ref, slot_next_ref, wt_ref, x1_ref, h2_ref, g2_ref, sgu_ref, sdn_ref, lg_ref, lb_ref,
                    y_hbm, oc_ref, ol_ref, gbuf, sems, *, n_tiles, ctx_tiles, alpha):
    i = pl.program_id(0)

    def row_copy(sref, b, j, k):
        return pltpu.make_async_copy(y_hbm.at[pl.ds(sref[0, 0, j * TOP_K + k], 1), :],
                                     gbuf.at[b, k, pl.ds(j, 1), :], sems.at[b])

    def wait_tile(b):
        for k in range(TOP_K):
            pltpu.make_async_copy(y_hbm.at[pl.ds(0, TC_TOK), :], gbuf.at[b, k], sems.at[b]).wait()

    @pl.when(i == 0)
    def _():
        def body(j, carry):
            for k in range(TOP_K):
                row_copy(slot_ref, 0, j, k).start(priority=k % 2)
            return carry
        lax.fori_loop(0, TC_TOK, body, 0)

    b = i % 2
    nb = 1 - b

    def issue_next(j0, j1):
        for j in range(j0, j1):
            for k in range(TOP_K):
                row_copy(slot_next_ref, nb, j, k).start(priority=k % 2)

    issue_next(0, TC_TOK // 2)
    h = h2_ref[...].astype(BF16)
    su = _dot(h, sgu_ref[...])
    sg = su[:, :EXPERT_DIM]
    act = (sg * _sigmoid(sg) * su[:, EXPERT_DIM:]).astype(BF16)
    ffn = _dot(act, sdn_ref[...])
    wait_tile(b)
    issue_next(TC_TOK // 2, TC_TOK)
    for k in range(TOP_K):
        ffn = ffn + gbuf[b, k] * wt_ref[:, k:k + 1]
    out = _layer_norm(alpha * x1_ref[...] + g2_ref[0] * ffn, lg_ref[...], lb_ref[...])

    @pl.when(i < ctx_tiles)
    def _():
        oc_ref[...] = out

    @pl.when(i >= ctx_tiles)
    def _():
        ol_ref[...] = out

    @pl.when(i == n_tiles - 1)
    def _():
        wait_tile(nb)


def _combine(slot3, wts_t, x1, h2, g2, sh_gu, sh_dn, ln_g, ln_b, y_sorted, row_of_tile, alpha, tok_ctx):
    t_all = x1.shape[0]
    n_tiles = t_all // TC_TOK
    ctx_tiles = tok_ctx // TC_TOK
    tile = lambda w: pl.BlockSpec((TC_TOK, w), lambda i: (i, 0))
    full = lambda a: pl.BlockSpec(a.shape, lambda i, _n=a.ndim: (0,) * _n)
    return pl.pallas_call(
        functools.partial(_combine_kernel, n_tiles=n_tiles, ctx_tiles=ctx_tiles, alpha=alpha),
        grid=(n_tiles,),
        in_specs=[pl.BlockSpec((1, 1, TC_TOK * TOP_K), lambda i: (i, 0, 0), memory_space=pltpu.SMEM),
                  pl.BlockSpec((1, 1, TC_TOK * TOP_K), lambda i: (jnp.minimum(i + 1, n_tiles - 1), 0, 0),
                               memory_space=pltpu.SMEM),
                  tile(TOP_K), tile(D_MODEL), tile(D_MODEL),
                  pl.BlockSpec((1, 1, D_MODEL), lambda i: (row_of_tile(i), 0, 0)),
                  full(sh_gu), full(sh_dn), full(ln_g), full(ln_b),
                  pl.BlockSpec(memory_space=pl.ANY)],
        out_specs=_two_source_specs(TC_TOK, D_MODEL, ctx_tiles),
        out_shape=[jax.ShapeDtypeStruct((tok_ctx, D_MODEL), F32),
                   jax.ShapeDtypeStruct((t_all - tok_ctx, D_MODEL), F32)],
        scratch_shapes=[pltpu.VMEM((2, TOP_K, TC_TOK, D_MODEL), F32), pltpu.SemaphoreType.DMA((2,))],
        compiler_params=_cparams(("arbitrary",)),
        name="combine",
    )(slot3, slot3, wts_t, x1, h2, g2, sh_gu, sh_dn, ln_g, ln_b, y_sorted)


def kernel(x_prompt, x_sample, state_rwkv, c, c_ctx, w_ada, b_ada, w_in, mu_shift, w_decay0, w_decay2, w_a0, w_a2, w_g2, k_k, k_a, r_k, lnx_g, lnx_b, conv_w, conv_b, conv_ln_g, conv_ln_b, w_out, ln1_g, ln1_b, router_w, router_bias, expert_w_gu, expert_w_down, shared_w_gu, shared_w_down, ln2_g, ln2_b):
    depth = w_ada.shape[0]
    assert depth == 1
    alpha = (2.0 * depth) ** 0.25
    n_ctx, t_ctx, _ = x_prompt.shape
    n_lat, t_lat, _ = x_sample.shape
    tok_ctx = n_ctx * t_ctx
    tok_lat = n_lat * t_lat
    t_all = tok_ctx + tok_lat
    tm = 1024
    assert tok_ctx % tm == 0 and t_lat % tm == 0 and t_lat // GRID_W * GRID_W == t_lat
    l = 0

    x_ctx = x_prompt.reshape(tok_ctx, D_MODEL)
    x_lat = x_sample.reshape(tok_lat, D_MODEL)

    cond8 = jnp.zeros((8, D_MODEL), F32).at[0].set(c_ctx).at[1:1 + n_lat].set(c)
    mod = _ada(cond8, w_ada[l], b_ada[l])
    sh1, sc1, g1, sh2, sc2, g2 = [m.reshape(8, 1, D_MODEL) for m in jnp.split(mod, 6, axis=-1)]

    def row_of_tile_for(tile_rows):
        ctx_tiles = tok_ctx // tile_rows
        per_seq = t_lat // tile_rows
        return lambda i: jnp.where(i < ctx_tiles, 0, 1 + (i - ctx_tiles) // per_seq)

    w_in_a = w_in[l][:, :RWKV_IN].astype(BF16)
    w_in_b = w_in[l][:, RWKV_IN:].astype(BF16)
    proj_a, proj_b = _in_proj(x_ctx, x_lat, sc1, sh1, w_in_a, w_in_b, tm, row_of_tile_for(tm))

    zeros64 = jnp.zeros((64, 2 * RW), F32)
    wd2 = jnp.concatenate([w_decay2[l, 0], w_decay2[l, 1]], axis=1)
    wa2 = jnp.concatenate([w_a2[l, 0], w_a2[l, 1]], axis=1)
    w_lora = jnp.concatenate([jnp.concatenate([wd2, zeros64], axis=1),
                              jnp.concatenate([zeros64, wa2], axis=1)], axis=0).astype(BF16)
    row = lambda v: v.reshape(1, -1)
    rwkv_w = [row(mu_shift[l]), w_lora, w_g2[l].astype(BF16), row(w_decay0[l]), row(w_a0[l]),
              row(k_k[l]), row(k_a[l]), row(r_k[l]), row(lnx_g[l]), row(lnx_b[l])]
    consts = _rwkv_consts()

    def to_cat(s):
        b = s.shape[0]
        s = s.reshape(b, 2, 2, 4, HD, HD)
        return jnp.transpose(s, (0, 1, 2, 5, 3, 4)).reshape(b, 4, HD, GW)

    s0_ctx = jnp.zeros((n_ctx, 4, HD, GW), F32)
    s0_lat = to_cat(state_rwkv[:, l].astype(F32))
    ya_c, sfin_ctx = _rwkv(proj_a, s0_ctx, rwkv_w, consts, t_ctx, n_ctx, 0)
    ya_l, _ = _rwkv(proj_a, s0_lat, rwkv_w, consts, t_lat, n_lat, tok_ctx // t_lat)
    new_state = sfin_ctx.reshape(n_ctx, 1, 2, 2 * 4, HD, HD).astype(x_prompt.dtype)

    conv_wts = [conv_w[l], row(conv_b[l]), row(conv_ln_g[l]), row(conv_ln_b[l])]
    yb_c = _conv(proj_b, conv_wts, t_ctx, n_ctx, 0, False)
    yb_l = _conv(proj_b, conv_wts, t_lat, n_lat, tok_ctx // t_lat, True)

    w_oa = w_out[l][:RW].astype(BF16)
    w_ob = w_out[l][RW:].astype(BF16)
    x1, h2 = _out_proj(ya_c, ya_l, yb_c, yb_l, x_ctx, x_lat, g1, sc2, sh2, w_oa, w_ob, row(ln1_g[l]), row(ln1_b[l]), tm,
                       row_of_tile_for(tm), alpha)

    rwt = router_w[l].T
    rwt_hi = rwt.astype(BF16)
    rwt_lo = (rwt - rwt_hi.astype(F32)).astype(BF16)
    eidx, wts, pos, counts = _route(h2, rwt_hi, rwt_lo, router_bias[l].reshape(N_EXPERTS, 1), 512)

    n_assign = t_all * TOP_K
    n_blocks = (n_assign + N_EXPERTS * (MOE_BLOCK - 1) + MOE_BLOCK - 1) // MOE_BLOCK
    cnt = counts.reshape(N_EXPERTS).astype(jnp.int32)
    padded = (cnt + MOE_BLOCK - 1) // MOE_BLOCK * MOE_BLOCK
    pend = jnp.cumsum(padded)
    pstart = pend - padded
    padded_x = (cnt + XS_ALIGN - 1) // XS_ALIGN * XS_ALIGN
    pend_x = jnp.cumsum(padded_x)
    pstart_x = pend_x - padded_x
    xs_rows = n_assign + N_EXPERTS * (XS_ALIGN - 1) // XS_ALIGN * XS_ALIGN + MOE_BLOCK
    slot_x, slot_y = _slots(eidx, pos, jnp.stack([pstart_x, pstart], axis=1).astype(F32), 512)
    per_tile = lambda s, tok: s.T.reshape(t_all // tok, 1, tok * TOP_K)
    xs = _dispatch(per_tile(slot_x, TD_TOK), h2, (pstart_x + cnt).astype(jnp.int32), (padded_x - cnt).astype(jnp.int32),
                   pend_x[-1].astype(jnp.int32).reshape(1), xs_rows)
    bstart = (pstart // MOE_BLOCK).astype(jnp.int32)
    nblk = (padded // MOE_BLOCK).astype(jnp.int32)
    nused = (pend[-1] // MOE_BLOCK).astype(jnp.int32).reshape(1)
    blk_ids = jnp.arange(n_blocks, dtype=jnp.int32)
    blk_e = jnp.minimum(jnp.sum((pend[None, :] <= (blk_ids * MOE_BLOCK)[:, None]).astype(jnp.int32), axis=1),
                        N_EXPERTS - 1)
    shift = jnp.sum(jnp.where(blk_e[:, None] == jnp.arange(N_EXPERTS, dtype=jnp.int32)[None, :],
                              (pstart_x - pstart)[None, :], 0), axis=1)
    xrow = jnp.clip(blk_ids * MOE_BLOCK + shift, 0, xs_rows - MOE_BLOCK).astype(jnp.int32)
    y_sorted = _experts(xs, bstart, nblk, nused, xrow, expert_w_gu[l], expert_w_down[l])
    y_ctx, y_lat = _combine(per_tile(slot_y, TC_TOK), wts.T, x1, h2, g2, shared_w_gu[l].astype(BF16),
                            shared_w_down[l].astype(BF16), row(ln2_g[l]), row(ln2_b[l]), y_sorted,
                            row_of_tile_for(TC_TOK), alpha, tok_ctx)
    return (y_ctx.reshape(n_ctx, t_ctx, D_MODEL), y_lat.reshape(n_lat, t_lat, D_MODEL), new_state)
```

```python
import functools
import math

import numpy as np
import jax
import jax.numpy as jnp
from jax import lax
from jax.experimental import pallas as pl
from jax.experimental.pallas import tpu as pltpu

F32 = jnp.float32
BF16 = jnp.bfloat16

D_MODEL = 1024
RW = 512
HD = 64
CW = 512
CONV_K = 31
RWKV_IN = 3 * RW + 64 + 64 + 128
N_EXPERTS = 256
TOP_K = 8
N_GROUPS = 8
TOPK_GROUPS = 4
EXPERT_DIM = 256
ROUTED_SCALE = 2.5
MOE_BLOCK = 128
GRID_W = 64
LN_EPS = 1e-5
GN_EPS = 64e-5
CHUNK = 64
GW = 256
SCAN_UNROLL = 2
DECAY_SCALE = math.exp(-0.5)
VMEM_LIMIT = 56 * 1024 * 1024


def _cparams(sem):
    return pltpu.CompilerParams(dimension_semantics=sem, vmem_limit_bytes=VMEM_LIMIT)


def _split_bf16(x):
    hi = x.astype(BF16)
    lo = (x - hi.astype(F32)).astype(BF16)
    return hi, lo


def _dot(a, b):
    return jnp.dot(a, b, preferred_element_type=F32)


def _dot_nt(a, b):
    return lax.dot_general(a, b, (((1,), (1,)), ((), ())), preferred_element_type=F32)


def _dot_tn(a, b):
    return lax.dot_general(a, b, (((0,), (0,)), ((), ())), preferred_element_type=F32)


def _sigmoid(x):
    return 1.0 / (1.0 + jnp.exp(-x))


def _layer_norm(x, g, b):
    mu = jnp.mean(x, axis=-1, keepdims=True)
    d = x - mu
    var = jnp.mean(d * d, axis=-1, keepdims=True)
    return d * lax.rsqrt(var + LN_EPS) * g + b


def _ada_kernel(c_ref, w_ref, b_ref, o_ref):
    c = c_ref[...]
    s = c * _sigmoid(c)
    s_hi, s_lo = _split_bf16(s)
    w_hi, w_lo = _split_bf16(w_ref[...])
    o_ref[...] = _dot(s_hi, w_hi) + _dot(s_lo, w_hi) + _dot(s_hi, w_lo) + b_ref[...]


def _ada(cond8, w_ada, b_ada):
    n = w_ada.shape[1]
    tn = 1536
    return pl.pallas_call(
        _ada_kernel,
        grid=(n // tn,),
        in_specs=[pl.BlockSpec((8, D_MODEL), lambda j: (0, 0)),
                  pl.BlockSpec((D_MODEL, tn), lambda j: (0, j)),
                  pl.BlockSpec((1, tn), lambda j: (0, j))],
        out_specs=pl.BlockSpec((8, tn), lambda j: (0, j)),
        out_shape=jax.ShapeDtypeStruct((8, n), F32),
        compiler_params=_cparams(("arbitrary",)),
        name="ada",
    )(cond8, w_ada, b_ada.reshape(1, n))


def _two_source_specs(tm, width, ctx_tiles):
    return [pl.BlockSpec((tm, width), lambda i: (jnp.minimum(i, ctx_tiles - 1), 0)),
            pl.BlockSpec((tm, width), lambda i: (jnp.maximum(i - ctx_tiles, 0), 0))]


def _inproj_kernel(xc_ref, xl_ref, sc_ref, sh_ref, wa_ref, wb_ref, oa_ref, ob_ref, *, ctx_tiles):
    x = jnp.where(pl.program_id(0) < ctx_tiles, xc_ref[...], xl_ref[...])
    h = (x * (1.0 + sc_ref[0]) + sh_ref[0]).astype(BF16)
    oa_ref[...] = _dot(h, wa_ref[...])
    ob_ref[...] = _dot(h, wb_ref[...])


def _in_proj(x_ctx, x_lat, sc1, sh1, w_a, w_b, tm, row_of_tile):
    t_all = x_ctx.shape[0] + x_lat.shape[0]
    ctx_tiles = x_ctx.shape[0] // tm
    return pl.pallas_call(
        functools.partial(_inproj_kernel, ctx_tiles=ctx_tiles),
        grid=(t_all // tm,),
        in_specs=_two_source_specs(tm, D_MODEL, ctx_tiles) + [
                  pl.BlockSpec((1, 1, D_MODEL), lambda i: (row_of_tile(i), 0, 0)),
                  pl.BlockSpec((1, 1, D_MODEL), lambda i: (row_of_tile(i), 0, 0)),
                  pl.BlockSpec(w_a.shape, lambda i: (0, 0)),
                  pl.BlockSpec(w_b.shape, lambda i: (0, 0))],
        out_specs=[pl.BlockSpec((tm, RWKV_IN), lambda i: (i, 0)),
                   pl.BlockSpec((tm, 2 * CW), lambda i: (i, 0))],
        out_shape=[jax.ShapeDtypeStruct((t_all, RWKV_IN), F32),
                   jax.ShapeDtypeStruct((t_all, 2 * CW), F32)],
        compiler_params=_cparams(("arbitrary",)),
        name="in_proj",
    )(x_ctx, x_lat, sc1, sh1, w_a, w_b)


def _rwkv_consts():
    c = CHUNK
    t = np.arange(c)[:, None]
    s = np.arange(GW)[None, :] % c
    sl = (s < t).astype(np.float32)
    il = (s <= t).astype(np.float32)
    su = (s > t).astype(np.float32)
    iu = (s >= t).astype(np.float32)
    eye_cat = (s == t).astype(np.float32)
    tri = np.stack([sl, il, su, iu, eye_cat])
    r = np.arange(GW)
    mask_bd = (r[:, None] // HD == r[None, :] // HD).astype(np.float32)
    eye = np.eye(GW, dtype=np.float32)
    cum = np.stack([np.tril(np.ones((c, c), np.float32)), np.triu(np.ones((c, c), np.float32))])
    q = np.arange(RW)
    seg = (q[:, None] // HD == q[None, :] // HD).astype(np.float32)
    return (jnp.asarray(tri), jnp.asarray(mask_bd, dtype=BF16), jnp.asarray(mask_bd), jnp.asarray(eye),
            jnp.asarray(cum, dtype=BF16), jnp.asarray(seg, dtype=BF16))


def _rwkv_kernel(p_ref, s0_ref, mu_ref, wlora_ref, wg2_ref, wd0_ref, wa0_ref, kk_ref, ka_ref, rk_ref,
                 lng_ref, lnb_ref, tri_ref, mbd16_ref, mbd_ref, eye_ref, cum_ref, seg_ref,
                 ya_ref, sfin_ref,
                 r_s, v_s, kk_s, g_s, bon_s, a_s, lw_s, kd_s, y_s, st_s, *, seq_len):
    nc = seq_len // CHUNK
    seg = seg_ref[...]

    def seg_sum(x):
        hi, lo = _split_bf16(x)
        return _dot(hi, seg) + _dot(lo, seg)

    def phase_a(c, carry):
        t0 = pl.multiple_of(c * CHUNK, CHUNK)
        cur = p_ref[pl.ds(t0, CHUNK), :]
        prev_row = p_ref[pl.ds(jnp.maximum(t0 - 1, 0), 1), :] * (t0 > 0).astype(F32)
        next_row = p_ref[pl.ds(jnp.minimum(t0 + CHUNK, seq_len - 1), 1), :] * (t0 + CHUNK < seq_len).astype(F32)
        row = lax.broadcasted_iota(jnp.int32, cur.shape, 0)
        prev = jnp.where(row == 0, prev_row, pltpu.roll(cur, 1, 0))
        nxt = jnp.where(row == CHUNK - 1, next_row, pltpu.roll(cur, CHUNK - 1, 0))
        p = cur + mu_ref[...] * (0.5 * (prev + nxt) - cur)
        r = p[:, 0:RW]
        k = p[:, RW:2 * RW]
        v = p[:, 2 * RW:3 * RW]
        z = p[:, 3 * RW:3 * RW + 128]
        gd = p[:, 3 * RW + 128:3 * RW + 256]
        lane = lax.broadcasted_iota(jnp.int32, z.shape, 1)
        z = jnp.where(lane < 64, jnp.tanh(z), z)
        lora = _dot(z.astype(BF16), wlora_ref[...])
        g = _dot(_sigmoid(gd).astype(BF16), wg2_ref[...])
        kks = k * kk_ref[...]
        kk = kks * lax.rsqrt(seg_sum(kks * kks) + 1e-12)
        rows = pl.ds(t0, CHUNK)
        r_s[rows, :] = r
        v_s[rows, :] = v.astype(BF16)
        kk_s[rows, :] = kk
        g_s[rows, :] = g
        kd_sum = jnp.zeros_like(k)
        for d in range(2):
            w_logit = wd0_ref[:, d * RW:(d + 1) * RW] + lora[:, d * RW:(d + 1) * RW]
            a = _sigmoid(wa0_ref[:, d * RW:(d + 1) * RW] + lora[:, (2 + d) * RW:(3 + d) * RW])
            kd = k * (1.0 + (a - 1.0) * ka_ref[...])
            lw_s[d, rows, :] = -DECAY_SCALE * _sigmoid(w_logit)
            a_s[d, rows, :] = a
            kd_s[d, rows, :] = kd
            kd_sum = kd_sum + kd
        bon_s[rows, :] = seg_sum(r * kd_sum * rk_ref[...]) * v
        return carry

    lax.fori_loop(0, nc, phase_a, 0, unroll=2)

    mbd16 = mbd16_ref[...]
    mbd = mbd_ref[...]
    eye = eye_ref[...]

    def bd(x16):
        return jnp.concatenate([x16, x16, x16, x16], axis=0) * mbd16

    for u in range(4):
        s0 = s0_ref[0, u]
        st_s[u] = jnp.concatenate([s0, s0, s0, s0], axis=0) * mbd

    def prep(d, grp, t0):
        rows = pl.ds(t0, CHUNK)
        lanes = slice(grp * GW, (grp + 1) * GW)
        lw = lw_s[d, rows, lanes]
        hi, lo = _split_bf16(lw)
        cl = _dot(cum_ref[d], hi) + _dot(cum_ref[d], lo)
        e_cl = jnp.exp(cl)
        e_ce = jnp.exp(cl - lw)
        e_ncl = jnp.exp(-cl)
        tot = cl[CHUNK - 1:CHUNK, :] if d == 0 else cl[0:1, :]
        wc = jnp.exp(tot)
        kk = kk_s[rows, lanes]
        rt = r_s[rows, lanes] * e_cl
        bt = a_s[d, rows, lanes] * kk * e_ncl
        kt = kd_s[d, rows, lanes] * e_ncl
        kq16 = (kk * e_ce).astype(BF16)
        return dict(d=d, u=d * 2 + grp, rows=rows, lanes=lanes, wc=wc, rt=rt, kq16=kq16,
                    v16=v_s[rows, lanes], bt16=bt.astype(BF16), kt16=kt.astype(BF16),
                    btw16=(bt * wc).astype(BF16), ktw16=(kt * wc).astype(BF16),
                    g16=jnp.concatenate([kq16, rt.astype(BF16)], axis=0))

    def phase_b(i2, carry):
        us = []
        for j in range(SCAN_UNROLL):
            i = i2 * SCAN_UNROLL + j
            tf = pl.multiple_of(i * CHUNK, CHUNK)
            tb = pl.multiple_of((nc - 1 - i) * CHUNK, CHUNK)
            us += [prep(0, 0, tf), prep(1, 0, tb), prep(0, 1, tf), prep(1, 1, tb)]
        each = lambda fn: [fn(q) for q in us]
        strict = lambda q: tri_ref[2 * q['d']]
        incl = lambda q: tri_ref[2 * q['d'] + 1]
        out_b = each(lambda q: _dot_nt(q['g16'], bd(q['bt16'])))
        out_k = each(lambda q: _dot_nt(q['g16'], bd(q['kt16'])))
        a_bk = [o[:CHUNK] * strict(q) for o, q in zip(out_b, us)]
        p_rb = [(o[CHUNK:] * incl(q)).astype(BF16) for o, q in zip(out_b, us)]
        a_kk = [(o[:CHUNK] * strict(q)).astype(BF16) for o, q in zip(out_k, us)]
        p_rk = [(o[CHUNK:] * incl(q)).astype(BF16) for o, q in zip(out_k, us)]
        tinv = [tri_ref[4] - a for a in a_bk]
        pw = [a.astype(BF16) for a in a_bk]
        for _ in range(5):
            pw = [_dot(p, bd(p)).astype(BF16) for p in pw]
            tinv = [t + _dot(p, bd(t.astype(BF16))) for t, p in zip(tinv, pw)]
        bdv = each(lambda q: bd(q['v16']))
        akkv = [_dot(a, b) for a, b in zip(a_kk, bdv)]
        x = [_dot(t.astype(BF16), jnp.concatenate([bd(q['kq16']), bd(k.astype(BF16))], axis=1))
             for t, q, k in zip(tinv, us, akkv)]
        qu16 = [z.astype(BF16) for z in x]
        tn1 = [_dot_tn(q['btw16'], z) for q, z in zip(us, qu16)]
        tn2 = each(lambda q: _dot_tn(q['ktw16'], q['v16']))
        r1 = [_dot(p, jnp.concatenate([bd(z[:, :GW]), bd(z[:, GW:])], axis=1)) for p, z in zip(p_rb, qu16)]
        y0 = [_dot(p, b) - r[:, GW:] for p, b, r in zip(p_rk, bdv, r1)]
        for q, t1, t2, r, y in zip(us, tn1, tn2, r1, y0):
            mc = (eye * q['wc'] - t1[:, :GW]) * mbd
            ncm = (t2 - t1[:, GW:]) * mbd
            rh = q['rt'] - r[:, :GW]
            st16 = st_s[q['u']].astype(BF16)
            big = _dot(jnp.concatenate([rh.astype(BF16), mc.astype(BF16)], axis=0), st16)
            y_s[q['d'], q['rows'], q['lanes']] = big[:CHUNK] + y
            st_s[q['u']] = big[CHUNK:] + ncm
        return carry

    lax.fori_loop(0, nc // SCAN_UNROLL, phase_b, 0)

    for u in range(4):
        st = st_s[u]
        sfin_ref[0, u] = st[0:HD] + st[HD:2 * HD] + st[2 * HD:3 * HD] + st[3 * HD:4 * HD]

    def phase_c(c, carry):
        rows = pl.ds(pl.multiple_of(c * CHUNK, CHUNK), CHUNK)
        y = y_s[0, rows, :] + y_s[1, rows, :]
        mean = seg_sum(y) * (1.0 / HD)
        dlt = y - mean
        var = seg_sum(dlt * dlt) * (1.0 / HD)
        yn = dlt * lax.rsqrt(var + GN_EPS)
        ya_ref[rows, :] = (yn * lng_ref[...] + lnb_ref[...] + bon_s[rows, :]) * g_s[rows, :]
        return carry

    lax.fori_loop(0, nc, phase_c, 0, unroll=2)


def _rwkv(proj_a, s0_cat, wts, consts, seq_len, n_seq, first_block):
    full = lambda a: pl.BlockSpec(a.shape, lambda s, _n=a.ndim: (0,) * _n)
    p_mode = dict(pipeline_mode=pl.Buffered(1)) if n_seq <= 2 else {}
    in_specs = [pl.BlockSpec((seq_len, RWKV_IN), lambda s: (first_block + s, 0), **p_mode),
                pl.BlockSpec((1, 4, HD, GW), lambda s: (s, 0, 0, 0))]
    in_specs += [full(a) for a in wts] + [full(a) for a in consts]
    args = [proj_a, s0_cat] + list(wts) + list(consts)
    seq = (seq_len, RW)
    return pl.pallas_call(
        functools.partial(_rwkv_kernel, seq_len=seq_len),
        grid=(n_seq,),
        in_specs=in_specs,
        out_specs=[pl.BlockSpec((seq_len, RW), lambda s: (s, 0)),
                   pl.BlockSpec((1, 4, HD, GW), lambda s: (s, 0, 0, 0))],
        out_shape=[jax.ShapeDtypeStruct((n_seq * seq_len, RW), F32),
                   jax.ShapeDtypeStruct((n_seq, 4, HD, GW), F32)],
        scratch_shapes=[pltpu.VMEM(seq, F32), pltpu.VMEM(seq, BF16), pltpu.VMEM(seq, F32), pltpu.VMEM(seq, F32),
                        pltpu.VMEM(seq, F32),
                        pltpu.VMEM((2,) + seq, F32), pltpu.VMEM((2,) + seq, F32), pltpu.VMEM((2,) + seq, F32),
                        pltpu.VMEM((2,) + seq, F32), pltpu.VMEM((4, GW, GW), F32)],
        compiler_params=_cparams(("arbitrary",)),
        name="rwkv_%d" % seq_len,
    )(*args)


CONV_PAD = 16
VPAD = (CONV_K // 2) * GRID_W


def _conv_kernel(u_ref, w_ref, b_ref, g_ref, beta_ref, o_ref, pad_s, vpad_s, y_s, *, seq_len, grid):
    rb = 256
    glu = u_ref[:, :CW] * _sigmoid(u_ref[:, CW:])
    zeros = jnp.zeros((CONV_PAD, CW), F32)
    pad_s[0:CONV_PAD, :] = zeros
    pad_s[CONV_PAD + seq_len:2 * CONV_PAD + seq_len, :] = zeros
    pad_s[CONV_PAD:CONV_PAD + seq_len, :] = glu
    half = CW // 2
    if grid:
        vz = jnp.zeros((VPAD, half), F32)
        vpad_s[0:VPAD, :] = vz
        vpad_s[VPAD + seq_len:2 * VPAD + seq_len, :] = vz
        vpad_s[VPAD:VPAD + seq_len, :] = glu[:, half:]
    n_h_lanes = (half if grid else CW) // 128
    for r0 in range(0, seq_len, rb):
        for lb in range(n_h_lanes):
            lanes = slice(lb * 128, (lb + 1) * 128)
            acc = jnp.zeros((rb, 128), F32)
            if grid:
                col = lax.broadcasted_iota(jnp.int32, (rb, 128), 0) % GRID_W
            span = rb + 2 * CONV_PAD
            xpad = pad_s[r0:r0 + span, lanes]
            for res in range(8):
                xrot = xpad if res == 0 else pltpu.roll(xpad, span - res, 0)
                for k in range(CONV_K):
                    if (k + 1) % 8 != res:
                        continue
                    off = (k + 1) - res
                    term = xrot[off:off + rb] * w_ref[k:k + 1, lanes]
                    if grid:
                        src = col + (k - CONV_K // 2)
                        term = jnp.where((src >= 0) & (src < GRID_W), term, 0.0)
                    acc = acc + term
            y_s[r0:r0 + rb, lanes] = acc
        if grid:
            for lb in range(half // 128):
                lanes = slice(lb * 128, (lb + 1) * 128)
                wl = slice(half + lb * 128, half + (lb + 1) * 128)
                acc = jnp.zeros((rb, 128), F32)
                for k in range(CONV_K):
                    start = VPAD + r0 + (k - CONV_K // 2) * GRID_W
                    acc = acc + vpad_s[start:start + rb, lanes] * w_ref[k:k + 1, wl]
                y_s[r0:r0 + rb, wl] = acc
    for r0 in range(0, seq_len, rb):
        y = y_s[r0:r0 + rb, :] + b_ref[...]
        yl = _layer_norm(y, g_ref[...], beta_ref[...])
        o_ref[r0:r0 + rb, :] = yl * _sigmoid(yl)


def _conv(proj_b, wts, seq_len, n_seq, first_block, grid):
    full = lambda a: pl.BlockSpec(a.shape, lambda s, _n=a.ndim: (0,) * _n)
    in_specs = [pl.BlockSpec((seq_len, 2 * CW), lambda s: (first_block + s, 0))] + [full(a) for a in wts]
    args = [proj_b] + list(wts)
    return pl.pallas_call(
        functools.partial(_conv_kernel, seq_len=seq_len, grid=grid),
        grid=(n_seq,),
        in_specs=in_specs,
        out_specs=pl.BlockSpec((seq_len, CW), lambda s: (s, 0)),
        out_shape=jax.ShapeDtypeStruct((n_seq * seq_len, CW), F32),
        scratch_shapes=[pltpu.VMEM((seq_len + 2 * CONV_PAD, CW), F32),
                        pltpu.VMEM((seq_len + 2 * VPAD if grid else 8, CW // 2), F32),
                        pltpu.VMEM((seq_len, CW), F32)],
        compiler_params=_cparams(("arbitrary",)),
        name="conv_%d" % seq_len,
    )(*args)


def _outproj_kernel(yac_ref, yal_ref, ybc_ref, ybl_ref, xc_ref, xl_ref, g1_ref, sc2_ref, sh2_ref, wa_ref, wb_ref,
                    lg_ref, lb_ref, x1_ref, h2_ref, *, alpha, ctx_tiles):
    is_ctx = pl.program_id(0) < ctx_tiles
    ya = jnp.where(is_ctx, yac_ref[...], yal_ref[...])
    yb = jnp.where(is_ctx, ybc_ref[...], ybl_ref[...])
    x = jnp.where(is_ctx, xc_ref[...], xl_ref[...])
    mix = _dot(ya.astype(BF16), wa_ref[...]) + _dot(yb.astype(BF16), wb_ref[...])
    x1 = _layer_norm(alpha * x + g1_ref[0] * mix, lg_ref[...], lb_ref[...])
    x1_ref[...] = x1
    h2_ref[...] = x1 * (1.0 + sc2_ref[0]) + sh2_ref[0]


def _out_proj(ya_c, ya_l, yb_c, yb_l, x_ctx, x_lat, g1, sc2, sh2, w_oa, w_ob, ln_g, ln_b, tm, row_of_tile, alpha):
    t_all = x_ctx.shape[0] + x_lat.shape[0]
    ctx_tiles = x_ctx.shape[0] // tm
    tile = lambda w: pl.BlockSpec((tm, w), lambda i: (i, 0))
    modrow = pl.BlockSpec((1, 1, D_MODEL), lambda i: (row_of_tile(i), 0, 0))
    full = lambda a: pl.BlockSpec(a.shape, lambda i, _n=a.ndim: (0,) * _n)
    return pl.pallas_call(
        functools.partial(_outproj_kernel, alpha=alpha, ctx_tiles=ctx_tiles),
        grid=(t_all // tm,),
        in_specs=_two_source_specs(tm, RW, ctx_tiles) + _two_source_specs(tm, CW, ctx_tiles)
        + _two_source_specs(tm, D_MODEL, ctx_tiles)
        + [modrow, modrow, modrow, full(w_oa), full(w_ob), full(ln_g), full(ln_b)],
        out_specs=[tile(D_MODEL), tile(D_MODEL)],
        out_shape=[jax.ShapeDtypeStruct((t_all, D_MODEL), F32), jax.ShapeDtypeStruct((t_all, D_MODEL), F32)],
        compiler_params=_cparams(("arbitrary",)),
        name="out_proj",
    )(ya_c, ya_l, yb_c, yb_l, x_ctx, x_lat, g1, sc2, sh2, w_oa, w_ob, ln_g, ln_b)


def _route_kernel(h_ref, wt_hi_ref, wt_lo_ref, bias_ref, ustrict_ref,
                  eidx_ref, wts_ref, pos_ref, cnt_ref, carry_s, *, tm):
    i = pl.program_id(0)

    @pl.when(i == 0)
    def _():
        carry_s[...] = jnp.zeros_like(carry_s)

    h_hi, h_lo = _split_bf16(h_ref[...])
    logits = _dot_nt(wt_hi_ref[...], h_hi) + _dot_nt(wt_hi_ref[...], h_lo) + _dot_nt(wt_lo_ref[...], h_hi)
    scores = _sigmoid(logits)
    sel = scores + bias_ref[...]
    neg = jnp.float32(-jnp.inf)
    gsz = N_EXPERTS // N_GROUPS
    gs = []
    for g in range(N_GROUPS):
        blk = sel[g * gsz:(g + 1) * gsz]
        m1 = jnp.max(blk, axis=0, keepdims=True)
        eq = blk == m1
        cnt = jnp.sum(eq.astype(F32), axis=0, keepdims=True)
        m2 = jnp.max(jnp.where(eq, neg, blk), axis=0, keepdims=True)
        gs.append(m1 + jnp.where(cnt >= 2.0, m1, m2))
    masked = []
    for g in range(N_GROUPS):
        rank = jnp.zeros_like(gs[g])
        for o in range(N_GROUPS):
            if o == g:
                continue
            beats = (gs[o] > gs[g]) | ((gs[o] == gs[g]) & (o < g))
            rank = rank + beats.astype(F32)
        keep = rank < float(TOPK_GROUPS)
        masked.append(jnp.where(keep, sel[g * gsz:(g + 1) * gsz], neg))
    cur = jnp.concatenate(masked, axis=0)
    iota_e = lax.broadcasted_iota(jnp.int32, cur.shape, 0).astype(F32)
    idxs, ws = [], []
    selmask = jnp.zeros(cur.shape, F32)
    for _ in range(TOP_K):
        m = jnp.max(cur, axis=0, keepdims=True)
        idx = jnp.min(jnp.where(cur == m, iota_e, float(N_EXPERTS)), axis=0, keepdims=True)
        onehot = iota_e == idx
        ws.append(jnp.sum(jnp.where(onehot, scores, 0.0), axis=0, keepdims=True))
        idxs.append(idx)
        cur = jnp.where(onehot, neg, cur)
        selmask = jnp.where(onehot, 1.0, selmask)
    pos = _dot(selmask.astype(BF16), ustrict_ref[...]) + carry_s[...]
    carry_s[...] = carry_s[...] + jnp.sum(selmask, axis=1, keepdims=True)
    cnt_ref[...] = carry_s[...]
    wsum = ws[0]
    for k in range(1, TOP_K):
        wsum = wsum + ws[k]
    pks = [jnp.sum(jnp.where(iota_e == idxs[k], pos, 0.0), axis=0, keepdims=True) for k in range(TOP_K)]
    eidx_ref[...] = jnp.concatenate(idxs, axis=0).astype(jnp.int32)
    wts_ref[...] = jnp.concatenate([w / wsum * ROUTED_SCALE for w in ws], axis=0)
    pos_ref[...] = jnp.concatenate(pks, axis=0).astype(jnp.int32)


def _route(h2, wt_hi, wt_lo, bias_col, tm):
    t_all = h2.shape[0]
    ustrict = jnp.asarray(np.triu(np.ones((tm, tm), np.float32), 1), dtype=BF16)
    full = lambda a: pl.BlockSpec(a.shape, lambda i, _n=a.ndim: (0,) * _n)
    tok = pl.BlockSpec((TOP_K, tm), lambda i: (0, i))
    return pl.pallas_call(
        functools.partial(_route_kernel, tm=tm),
        grid=(t_all // tm,),
        in_specs=[pl.BlockSpec((tm, D_MODEL), lambda i: (i, 0)), full(wt_hi), full(wt_lo), full(bias_col),
                  full(ustrict)],
        out_specs=[tok, tok, tok, pl.BlockSpec((N_EXPERTS, 1), lambda i: (0, 0))],
        out_shape=[jax.ShapeDtypeStruct((TOP_K, t_all), jnp.int32), jax.ShapeDtypeStruct((TOP_K, t_all), F32),
                   jax.ShapeDtypeStruct((TOP_K, t_all), jnp.int32), jax.ShapeDtypeStruct((N_EXPERTS, 1), F32)],
        scratch_shapes=[pltpu.VMEM((N_EXPERTS, 1), F32)],
        compiler_params=_cparams(("arbitrary",)),
        name="route",
    )(h2, wt_hi, wt_lo, bias_col, ustrict)


def _slots_kernel(eidx_ref, pos_ref, pstart_ref, slotx_ref, sloty_ref):
    tm = eidx_ref.shape[1]
    iota_e = lax.broadcasted_iota(jnp.int32, (N_EXPERTS, tm), 0)
    for c, out in enumerate((slotx_ref, sloty_ref)):
        rows = []
        for k in range(TOP_K):
            onehot = iota_e == eidx_ref[k:k + 1, :]
            rows.append(jnp.sum(jnp.where(onehot, pstart_ref[:, c:c + 1], 0.0), axis=0, keepdims=True))
        out[...] = jnp.concatenate(rows, axis=0).astype(jnp.int32) + pos_ref[...]


def _slots(eidx, pos, pstart_cols, tm):
    t_all = eidx.shape[1]
    tok = pl.BlockSpec((TOP_K, tm), lambda i: (0, i))
    return pl.pallas_call(
        _slots_kernel,
        grid=(t_all // tm,),
        in_specs=[tok, tok, pl.BlockSpec((N_EXPERTS, 2), lambda i: (0, 0))],
        out_specs=[tok, tok],
        out_shape=[jax.ShapeDtypeStruct((TOP_K, t_all), jnp.int32), jax.ShapeDtypeStruct((TOP_K, t_all), jnp.int32)],
        compiler_params=_cparams(("arbitrary",)),
        name="slots",
    )(eidx, pos, pstart_cols)


TD_TOK = 512
XS_ALIGN = 8


def _dispatch_kernel(fstart_ref, flen_ref, total_ref, slot_ref, h_ref, xs_out, zero_s, sem, zsem, *, xs_rows):
    i = pl.program_id(0)

    def body(j, carry):
        for k in range(TOP_K):
            pltpu.make_async_copy(h_ref.at[pl.ds(j, 1), :], xs_out.at[pl.ds(slot_ref[0, 0, j * TOP_K + k], 1), :],
                                  sem).start(priority=k % 2)
        return carry

    lax.fori_loop(0, TD_TOK, body, 0)
    for k in range(TOP_K):
        pltpu.make_async_copy(h_ref, xs_out.at[pl.ds(0, TD_TOK), :], sem).wait()

    @pl.when(i == pl.num_programs(0) - 1)
    def _():
        zero_s[...] = jnp.zeros_like(zero_s)
        total = total_ref[0]
        n_tail = (xs_rows - total) // XS_ALIGN

        def gap_copy(e, r):
            return pltpu.make_async_copy(zero_s.at[pl.ds(0, 1), :], xs_out.at[pl.ds(fstart_ref[e] + r, 1), :], zsem)

        def tail_copy(q):
            start = pl.multiple_of(total + q * XS_ALIGN, XS_ALIGN)
            return pltpu.make_async_copy(zero_s, xs_out.at[pl.ds(start, XS_ALIGN), :], zsem)

        def for_gaps(fn):
            def per_expert(e, carry):
                for r in range(XS_ALIGN - 1):
                    @pl.when(r < flen_ref[e])
                    def _():
                        fn(gap_copy(e, r))
                return carry
            lax.fori_loop(0, N_EXPERTS, per_expert, 0)

        def for_tail(fn):
            def per_q(q, carry):
                fn(tail_copy(q))
                return carry
            lax.fori_loop(0, n_tail, per_q, 0)

        for_gaps(lambda cp: cp.start())
        for_tail(lambda cp: cp.start())
        for_gaps(lambda cp: cp.wait())
        for_tail(lambda cp: cp.wait())


def _dispatch(slot3, h2, fstart, flen, total, xs_rows):
    t_all = h2.shape[0]
    grid_spec = pltpu.PrefetchScalarGridSpec(
        num_scalar_prefetch=3,
        grid=(t_all // TD_TOK,),
        in_specs=[pl.BlockSpec((1, 1, TD_TOK * TOP_K), lambda i, a, b, c: (i, 0, 0), memory_space=pltpu.SMEM),
                  pl.BlockSpec((TD_TOK, D_MODEL), lambda i, a, b, c: (i, 0))],
        out_specs=pl.BlockSpec(memory_space=pl.ANY),
        scratch_shapes=[pltpu.VMEM((XS_ALIGN, D_MODEL), F32), pltpu.SemaphoreType.DMA(()),
                        pltpu.SemaphoreType.DMA(())],
    )
    return pl.pallas_call(
        functools.partial(_dispatch_kernel, xs_rows=xs_rows),
        grid_spec=grid_spec,
        out_shape=jax.ShapeDtypeStruct((xs_rows, D_MODEL), F32),
        compiler_params=_cparams(("arbitrary",)),
        name="dispatch",
    )(fstart, flen, total, slot3, h2)


X_RING = 4


def _expert_kernel(bstart_ref, nblk_ref, nused_ref, xrow_ref, xs_hbm, wgu_ref, wdn_ref, ys_hbm,
                   xbuf, ybuf, wgu16, wdn16, sem_in, sem_out, *, n_blocks):
    e = pl.program_id(0)
    nused = nused_ref[0]
    n = nblk_ref[e]
    g0 = bstart_ref[e]

    def in_copy(g, s):
        start = pl.multiple_of(xrow_ref[g], XS_ALIGN)
        return pltpu.make_async_copy(xs_hbm.at[pl.ds(start, MOE_BLOCK), :], xbuf.at[s], sem_in.at[s])

    def out_copy(g, s):
        return pltpu.make_async_copy(ybuf.at[s], ys_hbm.at[pl.ds(g * MOE_BLOCK, MOE_BLOCK), :], sem_out.at[s])

    @pl.when(e == 0)
    def _():
        for q in range(X_RING - 1):
            @pl.when(q < nused)
            def _():
                in_copy(q, q).start()

    @pl.when(n > 0)
    def _():
        wgu16[...] = wgu_ref[0].astype(BF16)
        wdn16[...] = wdn_ref[0].astype(BF16)

    def block(b, carry):
        g = g0 + b
        s = g % 2
        ahead = g + (X_RING - 1)

        @pl.when(ahead < nused)
        def _():
            in_copy(ahead, ahead % X_RING).start()

        in_copy(g, g % X_RING).wait()
        gu = _dot(xbuf[g % X_RING].astype(BF16), wgu16[...])
        gate = gu[:, :EXPERT_DIM]
        act = (gate * _sigmoid(gate) * gu[:, EXPERT_DIM:]).astype(BF16)
        y = _dot(act, wdn16[...])

        @pl.when(g >= 2)
        def _():
            out_copy(g - 2, s).wait()

        ybuf[s] = y
        out_copy(g, s).start()
        return carry

    lax.fori_loop(0, n, block, 0)

    @pl.when(e == pl.num_programs(0) - 1)
    def _():
        @pl.when(nused >= 2)
        def _():
            out_copy(nused - 2, nused % 2).wait()

        @pl.when(nused >= 1)
        def _():
            out_copy(nused - 1, (nused - 1) % 2).wait()

        ybuf[0] = jnp.zeros((MOE_BLOCK, D_MODEL), F32)

        def tail(g, carry):
            out_copy(g, 0).start()
            out_copy(g, 0).wait()
            return carry

        lax.fori_loop(nused, n_blocks, tail, 0)


def _experts(xs, bstart, nblk, nused, xrow, w_gu, w_down):
    n_blocks = xrow.shape[0]
    grid_spec = pltpu.PrefetchScalarGridSpec(
        num_scalar_prefetch=4,
        grid=(N_EXPERTS,),
        in_specs=[pl.BlockSpec(memory_space=pl.ANY),
                  pl.BlockSpec((1, D_MODEL, 2 * EXPERT_DIM), lambda e, bs, nb, nu, xr: (e, 0, 0)),
                  pl.BlockSpec((1, EXPERT_DIM, D_MODEL), lambda e, bs, nb, nu, xr: (e, 0, 0))],
        out_specs=pl.BlockSpec(memory_space=pl.ANY),
        scratch_shapes=[pltpu.VMEM((X_RING, MOE_BLOCK, D_MODEL), F32),
                        pltpu.VMEM((2, MOE_BLOCK, D_MODEL), F32),
                        pltpu.VMEM((D_MODEL, 2 * EXPERT_DIM), BF16),
                        pltpu.VMEM((EXPERT_DIM, D_MODEL), BF16),
                        pltpu.SemaphoreType.DMA((X_RING,)), pltpu.SemaphoreType.DMA((2,))],
    )
    return pl.pallas_call(
        functools.partial(_expert_kernel, n_blocks=n_blocks),
        grid_spec=grid_spec,
        out_shape=jax.ShapeDtypeStruct((n_blocks * MOE_BLOCK, D_MODEL), F32),
        compiler_params=_cparams(("arbitrary",)),
        name="experts",
    )(bstart, nblk, nused, xrow, xs, w_gu, w_down)


TC_TOK = 128


def _combine_kernel(slot_ref, slot_next_ref, wt_ref, x1_ref, h2_ref, g2_ref, sgu_ref, sdn_ref, lg_ref, lb_ref,
                    y_hbm, oc_ref, ol_ref, gbuf, sems, *, n_tiles, ctx_tiles, alpha):
    i = pl.program_id(0)

    def row_copy(sref, b, j, k):
        return pltpu.make_async_copy(y_hbm.at[pl.ds(sref[0, 0, j * TOP_K + k], 1), :],
                                     gbuf.at[b, k, pl.ds(j, 1), :], sems.at[b])

    def wait_tile(b):
        for k in range(TOP_K):
            pltpu.make_async_copy(y_hbm.at[pl.ds(0, TC_TOK), :], gbuf.at[b, k], sems.at[b]).wait()

    @pl.when(i == 0)
    def _():
        def body(j, carry):
            for k in range(TOP_K):
                row_copy(slot_ref, 0, j, k).start(priority=k % 2)
            return carry
        lax.fori_loop(0, TC_TOK, body, 0)

    b = i % 2
    nb = 1 - b

    def issue_next(j0, j1):
        for j in range(j0, j1):
            for k in range(TOP_K):
                row_copy(slot_next_ref, nb, j, k).start(priority=k % 2)

    issue_next(0, TC_TOK // 2)
    h = h2_ref[...].astype(BF16)
    su = _dot(h, sgu_ref[...])
    sg = su[:, :EXPERT_DIM]
    act = (sg * _sigmoid(sg) * su[:, EXPERT_DIM:]).astype(BF16)
    ffn = _dot(act, sdn_ref[...])
    wait_tile(b)
    issue_next(TC_TOK // 2, TC_TOK)
    for k in range(TOP_K):
        ffn = ffn + gbuf[b, k] * wt_ref[:, k:k + 1]
    out = _layer_norm(alpha * x1_ref[...] + g2_ref[0] * ffn, lg_ref[...], lb_ref[...])

    @pl.when(i < ctx_tiles)
    def _():
        oc_ref[...] = out

    @pl.when(i >= ctx_tiles)
    def _():
        ol_ref[...] = out

    @pl.when(i == n_tiles - 1)
    def _():
        wait_tile(nb)


def _combine(slot3, wts_t, x1, h2, g2, sh_gu, sh_dn, ln_g, ln_b, y_sorted, row_of_tile, alpha, tok_ctx):
    t_all = x1.shape[0]
    n_tiles = t_all // TC_TOK
    ctx_tiles = tok_ctx // TC_TOK
    tile = lambda w: pl.BlockSpec((TC_TOK, w), lambda i: (i, 0))
    full = lambda a: pl.BlockSpec(a.shape, lambda i, _n=a.ndim: (0,) * _n)
    return pl.pallas_call(
        functools.partial(_combine_kernel, n_tiles=n_tiles, ctx_tiles=ctx_tiles, alpha=alpha),
        grid=(n_tiles,),
        in_specs=[pl.BlockSpec((1, 1, TC_TOK * TOP_K), lambda i: (i, 0, 0), memory_space=pltpu.SMEM),
                  pl.BlockSpec((1, 1, TC_TOK * TOP_K), lambda i: (jnp.minimum(i + 1, n_tiles - 1), 0, 0),
                               memory_space=pltpu.SMEM),
                  tile(TOP_K), tile(D_MODEL), tile(D_MODEL),
                  pl.BlockSpec((1, 1, D_MODEL), lambda i: (row_of_tile(i), 0, 0)),
                  full(sh_gu), full(sh_dn), full(ln_g), full(ln_b),
                  pl.BlockSpec(memory_space=pl.ANY)],
        out_specs=_two_source_specs(TC_TOK, D_MODEL, ctx_tiles),
        out_shape=[jax.ShapeDtypeStruct((tok_ctx, D_MODEL), F32),
                   jax.ShapeDtypeStruct((t_all - tok_ctx, D_MODEL), F32)],
        scratch_shapes=[pltpu.VMEM((2, TOP_K, TC_TOK, D_MODEL), F32), pltpu.SemaphoreType.DMA((2,))],
        compiler_params=_cparams(("arbitrary",)),
        name="combine",
    )(slot3, slot3, wts_t, x1, h2, g2, sh_gu, sh_dn, ln_g, ln_b, y_sorted)


def kernel(x_prompt, x_sample, state_rwkv, c, c_ctx, w_ada, b_ada, w_in, mu_shift, w_decay0, w_decay2, w_a0, w_a2, w_g2, k_k, k_a, r_k, lnx_g, lnx_b, conv_w, conv_b, conv_ln_g, conv_ln_b, w_out, ln1_g, ln1_b, router_w, router_bias, expert_w_gu, expert_w_down, shared_w_gu, shared_w_down, ln2_g, ln2_b):
    depth = w_ada.shape[0]
    assert depth == 1
    alpha = (2.0 * depth) ** 0.25
    n_ctx, t_ctx, _ = x_prompt.shape
    n_lat, t_lat, _ = x_sample.shape
    tok_ctx = n_ctx * t_ctx
    tok_lat = n_lat * t_lat
    t_all = tok_ctx + tok_lat
    tm = 1024
    assert tok_ctx % tm == 0 and t_lat % tm == 0 and t_lat // GRID_W * GRID_W == t_lat
    l = 0

    x_ctx = x_prompt.reshape(tok_ctx, D_MODEL)
    x_lat = x_sample.reshape(tok_lat, D_MODEL)

    cond8 = jnp.zeros((8, D_MODEL), F32).at[0].set(c_ctx).at[1:1 + n_lat].set(c)
    mod = _ada(cond8, w_ada[l], b_ada[l])
    sh1, sc1, g1, sh2, sc2, g2 = [m.reshape(8, 1, D_MODEL) for m in jnp.split(mod, 6, axis=-1)]

    def row_of_tile_for(tile_rows):
        ctx_tiles = tok_ctx // tile_rows
        per_seq = t_lat // tile_rows
        return lambda i: jnp.where(i < ctx_tiles, 0, 1 + (i - ctx_tiles) // per_seq)

    w_in_a = w_in[l][:, :RWKV_IN].astype(BF16)
    w_in_b = w_in[l][:, RWKV_IN:].astype(BF16)
    proj_a, proj_b = _in_proj(x_ctx, x_lat, sc1, sh1, w_in_a, w_in_b, tm, row_of_tile_for(tm))

    zeros64 = jnp.zeros((64, 2 * RW), F32)
    wd2 = jnp.concatenate([w_decay2[l, 0], w_decay2[l, 1]], axis=1)
    wa2 = jnp.concatenate([w_a2[l, 0], w_a2[l, 1]], axis=1)
    w_lora = jnp.concatenate([jnp.concatenate([wd2, zeros64], axis=1),
                              jnp.concatenate([zeros64, wa2], axis=1)], axis=0).astype(BF16)
    row = lambda v: v.reshape(1, -1)
    rwkv_w = [row(mu_shift[l]), w_lora, w_g2[l].astype(BF16), row(w_decay0[l]), row(w_a0[l]),
              row(k_k[l]), row(k_a[l]), row(r_k[l]), row(lnx_g[l]), row(lnx_b[l])]
    consts = _rwkv_consts()

    def to_cat(s):
        b = s.shape[0]
        s = s.reshape(b, 2, 2, 4, HD, HD)
        return jnp.transpose(s, (0, 1, 2, 5, 3, 4)).reshape(b, 4, HD, GW)

    s0_ctx = jnp.zeros((n_ctx, 4, HD, GW), F32)
    s0_lat = to_cat(state_rwkv[:, l].astype(F32))
    ya_c, sfin_ctx = _rwkv(proj_a, s0_ctx, rwkv_w, consts, t_ctx, n_ctx, 0)
    ya_l, _ = _rwkv(proj_a, s0_lat, rwkv_w, consts, t_lat, n_lat, tok_ctx // t_lat)
    sfin = jnp.transpose(sfin_ctx.reshape(n_ctx, 2, 2, HD, 4, HD), (0, 1, 2, 4, 5, 3))
    new_state = sfin.reshape(n_ctx, 1, 2, 2 * 4, HD, HD).astype(x_prompt.dtype)

    conv_wts = [conv_w[l], row(conv_b[l]), row(conv_ln_g[l]), row(conv_ln_b[l])]
    yb_c = _conv(proj_b, conv_wts, t_ctx, n_ctx, 0, False)
    yb_l = _conv(proj_b, conv_wts, t_lat, n_lat, tok_ctx // t_lat, True)

    w_oa = w_out[l][:RW].astype(BF16)
    w_ob = w_out[l][RW:].astype(BF16)
    x1, h2 = _out_proj(ya_c, ya_l, yb_c, yb_l, x_ctx, x_lat, g1, sc2, sh2, w_oa, w_ob, row(ln1_g[l]), row(ln1_b[l]), tm,
                       row_of_tile_for(tm), alpha)

    rwt = router_w[l].T
    rwt_hi = rwt.astype(BF16)
    rwt_lo = (rwt - rwt_hi.astype(F32)).astype(BF16)
    eidx, wts, pos, counts = _route(h2, rwt_hi, rwt_lo, router_bias[l].reshape(N_EXPERTS, 1), 512)

    n_assign = t_all * TOP_K
    n_blocks = (n_assign + N_EXPERTS * (MOE_BLOCK - 1) + MOE_BLOCK - 1) // MOE_BLOCK
    cnt = counts.reshape(N_EXPERTS).astype(jnp.int32)
    padded = (cnt + MOE_BLOCK - 1) // MOE_BLOCK * MOE_BLOCK
    pend = jnp.cumsum(padded)
    pstart = pend - padded
    padded_x = (cnt + XS_ALIGN - 1) // XS_ALIGN * XS_ALIGN
    pend_x = jnp.cumsum(padded_x)
    pstart_x = pend_x - padded_x
    xs_rows = n_assign + N_EXPERTS * (XS_ALIGN - 1) // XS_ALIGN * XS_ALIGN + MOE_BLOCK
    slot_x, slot_y = _slots(eidx, pos, jnp.stack([pstart_x, pstart], axis=1).astype(F32), 512)
    per_tile = lambda s, tok: s.T.reshape(t_all // tok, 1, tok * TOP_K)
    xs = _dispatch(per_tile(slot_x, TD_TOK), h2, (pstart_x + cnt).astype(jnp.int32), (padded_x - cnt).astype(jnp.int32),
                   pend_x[-1].astype(jnp.int32).reshape(1), xs_rows)
    bstart = (pstart // MOE_BLOCK).astype(jnp.int32)
    nblk = (padded // MOE_BLOCK).astype(jnp.int32)
    nused = (pend[-1] // MOE_BLOCK).astype(jnp.int32).reshape(1)
    blk_ids = jnp.arange(n_blocks, dtype=jnp.int32)
    blk_e = jnp.minimum(jnp.sum((pend[None, :] <= (blk_ids * MOE_BLOCK)[:, None]).astype(jnp.int32), axis=1),
                        N_EXPERTS - 1)
    shift = jnp.sum(jnp.where(blk_e[:, None] == jnp.arange(N_EXPERTS, dtype=jnp.int32)[None, :],
                              (pstart_x - pstart)[None, :], 0), axis=1)
    xrow = jnp.clip(blk_ids * MOE_BLOCK + shift, 0, xs_rows - MOE_BLOCK).astype(jnp.int32)
    y_sorted = _experts(xs, bstart, nblk, nused, xrow, expert_w_gu[l], expert_w_down[l])
    y_ctx, y_lat = _combine(per_tile(slot_y, TC_TOK), wts.T, x1, h2, g2, shared_w_gu[l].astype(BF16),
                            shared_w_down[l].astype(BF16), row(ln2_g[l]), row(ln2_b[l]), y_sorted,
                            row_of_tile_for(TC_TOK), alpha, tok_ctx)
    return (y_ctx.reshape(n_ctx, t_ctx, D_MODEL), y_lat.reshape(n_lat, t_lat, D_MODEL), new_state)
```

```python
import functools
import math

import numpy as np
import jax
import jax.numpy as jnp
from jax import lax
from jax.experimental import pallas as pl
from jax.experimental.pallas import tpu as pltpu

F32 = jnp.float32
BF16 = jnp.bfloat16

D_MODEL = 1024
RW = 512
HD = 64
CW = 512
CONV_K = 31
RWKV_IN = 3 * RW + 64 + 64 + 128
N_EXPERTS = 256
TOP_K = 8
N_GROUPS = 8
TOPK_GROUPS = 4
EXPERT_DIM = 256
ROUTED_SCALE = 2.5
MOE_BLOCK = 128
GRID_W = 64
LN_EPS = 1e-5
GN_EPS = 64e-5
CHUNK = 64
GW = 256
SCAN_UNROLL = 2
DECAY_SCALE = math.exp(-0.5)
VMEM_LIMIT = 56 * 1024 * 1024


def _cparams(sem):
    return pltpu.CompilerParams(dimension_semantics=sem, vmem_limit_bytes=VMEM_LIMIT)


def _split_bf16(x):
    hi = x.astype(BF16)
    lo = (x - hi.astype(F32)).astype(BF16)
    return hi, lo


def _dot(a, b):
    return jnp.dot(a, b, preferred_element_type=F32)


def _dot_nt(a, b):
    return lax.dot_general(a, b, (((1,), (1,)), ((), ())), preferred_element_type=F32)


def _dot_tn(a, b):
    return lax.dot_general(a, b, (((0,), (0,)), ((), ())), preferred_element_type=F32)


def _sigmoid(x):
    return 1.0 / (1.0 + jnp.exp(-x))


def _layer_norm(x, g, b):
    mu = jnp.mean(x, axis=-1, keepdims=True)
    d = x - mu
    var = jnp.mean(d * d, axis=-1, keepdims=True)
    return d * lax.rsqrt(var + LN_EPS) * g + b


def _ada_kernel(c_ref, w_ref, b_ref, o_ref):
    c = c_ref[...]
    s = c * _sigmoid(c)
    s_hi, s_lo = _split_bf16(s)
    w_hi, w_lo = _split_bf16(w_ref[...])
    o_ref[...] = _dot(s_hi, w_hi) + _dot(s_lo, w_hi) + _dot(s_hi, w_lo) + b_ref[...]


def _ada(cond8, w_ada, b_ada):
    n = w_ada.shape[1]
    tn = 1536
    return pl.pallas_call(
        _ada_kernel,
        grid=(n // tn,),
        in_specs=[pl.BlockSpec((8, D_MODEL), lambda j: (0, 0)),
                  pl.BlockSpec((D_MODEL, tn), lambda j: (0, j)),
                  pl.BlockSpec((1, tn), lambda j: (0, j))],
        out_specs=pl.BlockSpec((8, tn), lambda j: (0, j)),
        out_shape=jax.ShapeDtypeStruct((8, n), F32),
        compiler_params=_cparams(("arbitrary",)),
        name="ada",
    )(cond8, w_ada, b_ada.reshape(1, n))


def _two_source_specs(tm, width, ctx_tiles):
    return [pl.BlockSpec((tm, width), lambda i: (jnp.minimum(i, ctx_tiles - 1), 0)),
            pl.BlockSpec((tm, width), lambda i: (jnp.maximum(i - ctx_tiles, 0), 0))]


def _inproj_kernel(xc_ref, xl_ref, sc_ref, sh_ref, wa_ref, wb_ref, oa_ref, ob_ref, *, ctx_tiles):
    x = jnp.where(pl.program_id(0) < ctx_tiles, xc_ref[...], xl_ref[...])
    h = (x * (1.0 + sc_ref[0]) + sh_ref[0]).astype(BF16)
    oa_ref[...] = _dot(h, wa_ref[...])
    ob_ref[...] = _dot(h, wb_ref[...])


def _in_proj(x_ctx, x_lat, sc1, sh1, w_a, w_b, tm, row_of_tile):
    t_all = x_ctx.shape[0] + x_lat.shape[0]
    ctx_tiles = x_ctx.shape[0] // tm
    return pl.pallas_call(
        functools.partial(_inproj_kernel, ctx_tiles=ctx_tiles),
        grid=(t_all // tm,),
        in_specs=_two_source_specs(tm, D_MODEL, ctx_tiles) + [
                  pl.BlockSpec((1, 1, D_MODEL), lambda i: (row_of_tile(i), 0, 0)),
                  pl.BlockSpec((1, 1, D_MODEL), lambda i: (row_of_tile(i), 0, 0)),
                  pl.BlockSpec(w_a.shape, lambda i: (0, 0)),
                  pl.BlockSpec(w_b.shape, lambda i: (0, 0))],
        out_specs=[pl.BlockSpec((tm, RWKV_IN), lambda i: (i, 0)),
                   pl.BlockSpec((tm, 2 * CW), lambda i: (i, 0))],
        out_shape=[jax.ShapeDtypeStruct((t_all, RWKV_IN), F32),
                   jax.ShapeDtypeStruct((t_all, 2 * CW), F32)],
        compiler_params=_cparams(("arbitrary",)),
        name="in_proj",
    )(x_ctx, x_lat, sc1, sh1, w_a, w_b)


def _rwkv_consts():
    c = CHUNK
    t = np.arange(c)[:, None]
    s = np.arange(GW)[None, :] % c
    sl = (s < t).astype(np.float32)
    il = (s <= t).astype(np.float32)
    su = (s > t).astype(np.float32)
    iu = (s >= t).astype(np.float32)
    eye_cat = (s == t).astype(np.float32)
    tri = np.stack([sl, il, su, iu, eye_cat])
    r = np.arange(GW)
    mask_bd = (r[:, None] // HD == r[None, :] // HD).astype(np.float32)
    eye = np.eye(GW, dtype=np.float32)
    cum = np.stack([np.tril(np.ones((c, c), np.float32)), np.triu(np.ones((c, c), np.float32))])
    q = np.arange(RW)
    seg = (q[:, None] // HD == q[None, :] // HD).astype(np.float32)
    return (jnp.asarray(tri), jnp.asarray(mask_bd, dtype=BF16), jnp.asarray(mask_bd), jnp.asarray(eye),
            jnp.asarray(cum, dtype=BF16), jnp.asarray(seg, dtype=BF16))


def _rwkv_kernel(p_ref, s0_ref, mu_ref, wlora_ref, wg2_ref, wd0_ref, wa0_ref, kk_ref, ka_ref, rk_ref,
                 lng_ref, lnb_ref, tri_ref, mbd16_ref, mbd_ref, eye_ref, cum_ref, seg_ref,
                 ya_ref, sfin_ref,
                 r_s, v_s, kk_s, g_s, bon_s, a_s, lw_s, kd_s, y_s, st_s, *, seq_len):
    nc = seq_len // CHUNK
    scan_unroll = SCAN_UNROLL
    seg = seg_ref[...]

    def seg_sum(x):
        hi, lo = _split_bf16(x)
        return _dot(hi, seg) + _dot(lo, seg)

    def phase_a(c, carry):
        t0 = pl.multiple_of(c * CHUNK, CHUNK)
        cur = p_ref[pl.ds(t0, CHUNK), :]
        prev_row = p_ref[pl.ds(jnp.maximum(t0 - 1, 0), 1), :] * (t0 > 0).astype(F32)
        next_row = p_ref[pl.ds(jnp.minimum(t0 + CHUNK, seq_len - 1), 1), :] * (t0 + CHUNK < seq_len).astype(F32)
        row = lax.broadcasted_iota(jnp.int32, cur.shape, 0)
        prev = jnp.where(row == 0, prev_row, pltpu.roll(cur, 1, 0))
        nxt = jnp.where(row == CHUNK - 1, next_row, pltpu.roll(cur, CHUNK - 1, 0))
        p = cur + mu_ref[...] * (0.5 * (prev + nxt) - cur)
        r = p[:, 0:RW]
        k = p[:, RW:2 * RW]
        v = p[:, 2 * RW:3 * RW]
        z = p[:, 3 * RW:3 * RW + 128]
        gd = p[:, 3 * RW + 128:3 * RW + 256]
        lane = lax.broadcasted_iota(jnp.int32, z.shape, 1)
        z = jnp.where(lane < 64, jnp.tanh(z), z)
        lora = _dot(z.astype(BF16), wlora_ref[...])
        g = _dot(_sigmoid(gd).astype(BF16), wg2_ref[...])
        kks = k * kk_ref[...]
        kk = kks * lax.rsqrt(seg_sum(kks * kks) + 1e-12)
        rows = pl.ds(t0, CHUNK)
        r_s[rows, :] = r
        v_s[rows, :] = v.astype(BF16)
        kk_s[rows, :] = kk
        g_s[rows, :] = g
        kd_sum = jnp.zeros_like(k)
        for d in range(2):
            w_logit = wd0_ref[:, d * RW:(d + 1) * RW] + lora[:, d * RW:(d + 1) * RW]
            a = _sigmoid(wa0_ref[:, d * RW:(d + 1) * RW] + lora[:, (2 + d) * RW:(3 + d) * RW])
            kd = k * (1.0 + (a - 1.0) * ka_ref[...])
            lw_s[d, rows, :] = -DECAY_SCALE * _sigmoid(w_logit)
            a_s[d, rows, :] = a
            kd_s[d, rows, :] = kd
            kd_sum = kd_sum + kd
        bon_s[rows, :] = seg_sum(r * kd_sum * rk_ref[...]) * v
        return carry

    lax.fori_loop(0, nc, phase_a, 0, unroll=2)

    mbd16 = mbd16_ref[...]
    mbd = mbd_ref[...]
    eye = eye_ref[...]

    def bd(x16):
        return jnp.concatenate([x16, x16, x16, x16], axis=0) * mbd16

    for u in range(4):
        s0 = s0_ref[0, u]
        st_s[u] = jnp.concatenate([s0, s0, s0, s0], axis=0) * mbd

    def prep(d, grp, t0):
        rows = pl.ds(t0, CHUNK)
        lanes = slice(grp * GW, (grp + 1) * GW)
        lw = lw_s[d, rows, lanes]
        hi, lo = _split_bf16(lw)
        cl = _dot(cum_ref[d], hi) + _dot(cum_ref[d], lo)
        e_cl = jnp.exp(cl)
        e_ce = jnp.exp(cl - lw)
        e_ncl = jnp.exp(-cl)
        tot = cl[CHUNK - 1:CHUNK, :] if d == 0 else cl[0:1, :]
        wc = jnp.exp(tot)
        kk = kk_s[rows, lanes]
        rt = r_s[rows, lanes] * e_cl
        bt = a_s[d, rows, lanes] * kk * e_ncl
        kt = kd_s[d, rows, lanes] * e_ncl
        kq16 = (kk * e_ce).astype(BF16)
        return dict(d=d, u=d * 2 + grp, rows=rows, lanes=lanes, wc=wc, rt=rt, kq16=kq16,
                    v16=v_s[rows, lanes], bt16=bt.astype(BF16), kt16=kt.astype(BF16),
                    btw16=(bt * wc).astype(BF16), ktw16=(kt * wc).astype(BF16),
                    g16=jnp.concatenate([kq16, rt.astype(BF16)], axis=0))

    def phase_b(i2, carry):
        us = []
        for j in range(scan_unroll):
            i = i2 * scan_unroll + j
            tf = pl.multiple_of(i * CHUNK, CHUNK)
            tb = pl.multiple_of((nc - 1 - i) * CHUNK, CHUNK)
            us += [prep(0, 0, tf), prep(1, 0, tb), prep(0, 1, tf), prep(1, 1, tb)]
        each = lambda fn: [fn(q) for q in us]
        strict = lambda q: tri_ref[2 * q['d']]
        incl = lambda q: tri_ref[2 * q['d'] + 1]
        out_b = each(lambda q: _dot_nt(q['g16'], bd(q['bt16'])))
        out_k = each(lambda q: _dot_nt(q['g16'], bd(q['kt16'])))
        a_bk = [o[:CHUNK] * strict(q) for o, q in zip(out_b, us)]
        p_rb = [(o[CHUNK:] * incl(q)).astype(BF16) for o, q in zip(out_b, us)]
        a_kk = [(o[:CHUNK] * strict(q)).astype(BF16) for o, q in zip(out_k, us)]
        p_rk = [(o[CHUNK:] * incl(q)).astype(BF16) for o, q in zip(out_k, us)]
        tinv = [tri_ref[4] - a for a in a_bk]
        pw = [a.astype(BF16) for a in a_bk]
        for _ in range(5):
            pw = [_dot(p, bd(p)).astype(BF16) for p in pw]
            tinv = [t + _dot(p, bd(t.astype(BF16))) for t, p in zip(tinv, pw)]
        bdv = each(lambda q: bd(q['v16']))
        akkv = [_dot(a, b) for a, b in zip(a_kk, bdv)]
        x = [_dot(t.astype(BF16), jnp.concatenate([bd(q['kq16']), bd(k.astype(BF16))], axis=1))
             for t, q, k in zip(tinv, us, akkv)]
        qu16 = [z.astype(BF16) for z in x]
        tn1 = [_dot_tn(q['btw16'], z) for q, z in zip(us, qu16)]
        tn2 = each(lambda q: _dot_tn(q['ktw16'], q['v16']))
        r1 = [_dot(p, jnp.concatenate([bd(z[:, :GW]), bd(z[:, GW:])], axis=1)) for p, z in zip(p_rb, qu16)]
        y0 = [_dot(p, b) - r[:, GW:] for p, b, r in zip(p_rk, bdv, r1)]
        for q, t1, t2, r, y in zip(us, tn1, tn2, r1, y0):
            mc = (eye * q['wc'] - t1[:, :GW]) * mbd
            ncm = (t2 - t1[:, GW:]) * mbd
            rh = q['rt'] - r[:, :GW]
            st16 = st_s[q['u']].astype(BF16)
            big = _dot(jnp.concatenate([rh.astype(BF16), mc.astype(BF16)], axis=0), st16)
            y_s[q['d'], q['rows'], q['lanes']] = big[:CHUNK] + y
            st_s[q['u']] = big[CHUNK:] + ncm
        return carry

    lax.fori_loop(0, nc // scan_unroll, phase_b, 0)

    for u in range(4):
        st = st_s[u]
        sfin_ref[0, u] = st[0:HD] + st[HD:2 * HD] + st[2 * HD:3 * HD] + st[3 * HD:4 * HD]

    def phase_c(c, carry):
        rows = pl.ds(pl.multiple_of(c * CHUNK, CHUNK), CHUNK)
        y = y_s[0, rows, :] + y_s[1, rows, :]
        mean = seg_sum(y) * (1.0 / HD)
        dlt = y - mean
        var = seg_sum(dlt * dlt) * (1.0 / HD)
        yn = dlt * lax.rsqrt(var + GN_EPS)
        ya_ref[rows, :] = (yn * lng_ref[...] + lnb_ref[...] + bon_s[rows, :]) * g_s[rows, :]
        return carry

    lax.fori_loop(0, nc, phase_c, 0, unroll=2)


def _rwkv(proj_a, s0_cat, wts, consts, seq_len, n_seq, first_block):
    full = lambda a: pl.BlockSpec(a.shape, lambda s, _n=a.ndim: (0,) * _n)
    p_mode = dict(pipeline_mode=pl.Buffered(1)) if n_seq <= 2 else {}
    in_specs = [pl.BlockSpec((seq_len, RWKV_IN), lambda s: (first_block + s, 0), **p_mode),
                pl.BlockSpec((1, 4, HD, GW), lambda s: (s, 0, 0, 0))]
    in_specs += [full(a) for a in wts] + [full(a) for a in consts]
    args = [proj_a, s0_cat] + list(wts) + list(consts)
    seq = (seq_len, RW)
    return pl.pallas_call(
        functools.partial(_rwkv_kernel, seq_len=seq_len),
        grid=(n_seq,),
        in_specs=in_specs,
        out_specs=[pl.BlockSpec((seq_len, RW), lambda s: (s, 0)),
                   pl.BlockSpec((1, 4, HD, GW), lambda s: (s, 0, 0, 0))],
        out_shape=[jax.ShapeDtypeStruct((n_seq * seq_len, RW), F32),
                   jax.ShapeDtypeStruct((n_seq, 4, HD, GW), F32)],
        scratch_shapes=[pltpu.VMEM(seq, F32), pltpu.VMEM(seq, BF16), pltpu.VMEM(seq, F32), pltpu.VMEM(seq, F32),
                        pltpu.VMEM(seq, F32),
                        pltpu.VMEM((2,) + seq, F32), pltpu.VMEM((2,) + seq, F32), pltpu.VMEM((2,) + seq, F32),
                        pltpu.VMEM((2,) + seq, F32), pltpu.VMEM((4, GW, GW), F32)],
        compiler_params=_cparams(("arbitrary",)),
        name="rwkv_%d" % seq_len,
    )(*args)


CONV_PAD = 16
VPAD = (CONV_K // 2) * GRID_W


def _conv_kernel(u_ref, w_ref, b_ref, g_ref, beta_ref, o_ref, pad_s, vpad_s, y_s, *, seq_len, grid):
    rb = 256
    glu = u_ref[:, :CW] * _sigmoid(u_ref[:, CW:])
    zeros = jnp.zeros((CONV_PAD, CW), F32)
    pad_s[0:CONV_PAD, :] = zeros
    pad_s[CONV_PAD + seq_len:2 * CONV_PAD + seq_len, :] = zeros
    pad_s[CONV_PAD:CONV_PAD + seq_len, :] = glu
    half = CW // 2
    if grid:
        vz = jnp.zeros((VPAD, half), F32)
        vpad_s[0:VPAD, :] = vz
        vpad_s[VPAD + seq_len:2 * VPAD + seq_len, :] = vz
        vpad_s[VPAD:VPAD + seq_len, :] = glu[:, half:]
    n_h_lanes = (half if grid else CW) // 128
    for r0 in range(0, seq_len, rb):
        for lb in range(n_h_lanes):
            lanes = slice(lb * 128, (lb + 1) * 128)
            acc = jnp.zeros((rb, 128), F32)
            if grid:
                col = lax.broadcasted_iota(jnp.int32, (rb, 128), 0) % GRID_W
            span = rb + 2 * CONV_PAD
            xpad = pad_s[r0:r0 + span, lanes]
            for res in range(8):
                xrot = xpad if res == 0 else pltpu.roll(xpad, span - res, 0)
                for k in range(CONV_K):
                    if (k + 1) % 8 != res:
                        continue
                    off = (k + 1) - res
                    term = xrot[off:off + rb] * w_ref[k:k + 1, lanes]
                    if grid:
                        src = col + (k - CONV_K // 2)
                        term = jnp.where((src >= 0) & (src < GRID_W), term, 0.0)
                    acc = acc + term
            y_s[r0:r0 + rb, lanes] = acc
        if grid:
            for lb in range(half // 128):
                lanes = slice(lb * 128, (lb + 1) * 128)
                wl = slice(half + lb * 128, half + (lb + 1) * 128)
                acc = jnp.zeros((rb, 128), F32)
                for k in range(CONV_K):
                    start = VPAD + r0 + (k - CONV_K // 2) * GRID_W
                    acc = acc + vpad_s[start:start + rb, lanes] * w_ref[k:k + 1, wl]
                y_s[r0:r0 + rb, wl] = acc
    for r0 in range(0, seq_len, rb):
        y = y_s[r0:r0 + rb, :] + b_ref[...]
        yl = _layer_norm(y, g_ref[...], beta_ref[...])
        o_ref[r0:r0 + rb, :] = yl * _sigmoid(yl)


def _conv(proj_b, wts, seq_len, n_seq, first_block, grid):
    full = lambda a: pl.BlockSpec(a.shape, lambda s, _n=a.ndim: (0,) * _n)
    in_specs = [pl.BlockSpec((seq_len, 2 * CW), lambda s: (first_block + s, 0))] + [full(a) for a in wts]
    args = [proj_b] + list(wts)
    return pl.pallas_call(
        functools.partial(_conv_kernel, seq_len=seq_len, grid=grid),
        grid=(n_seq,),
        in_specs=in_specs,
        out_specs=pl.BlockSpec((seq_len, CW), lambda s: (s, 0)),
        out_shape=jax.ShapeDtypeStruct((n_seq * seq_len, CW), F32),
        scratch_shapes=[pltpu.VMEM((seq_len + 2 * CONV_PAD, CW), F32),
                        pltpu.VMEM((seq_len + 2 * VPAD if grid else 8, CW // 2), F32),
                        pltpu.VMEM((seq_len, CW), F32)],
        compiler_params=_cparams(("arbitrary",)),
        name="conv_%d" % seq_len,
    )(*args)


def _outproj_kernel(yac_ref, yal_ref, ybc_ref, ybl_ref, xc_ref, xl_ref, g1_ref, sc2_ref, sh2_ref, wa_ref, wb_ref,
                    lg_ref, lb_ref, x1_ref, h2_ref, *, alpha, ctx_tiles):
    is_ctx = pl.program_id(0) < ctx_tiles
    ya = jnp.where(is_ctx, yac_ref[...], yal_ref[...])
    yb = jnp.where(is_ctx, ybc_ref[...], ybl_ref[...])
    x = jnp.where(is_ctx, xc_ref[...], xl_ref[...])
    mix = _dot(ya.astype(BF16), wa_ref[...]) + _dot(yb.astype(BF16), wb_ref[...])
    x1 = _layer_norm(alpha * x + g1_ref[0] * mix, lg_ref[...], lb_ref[...])
    x1_ref[...] = x1
    h2_ref[...] = x1 * (1.0 + sc2_ref[0]) + sh2_ref[0]


def _out_proj(ya_c, ya_l, yb_c, yb_l, x_ctx, x_lat, g1, sc2, sh2, w_oa, w_ob, ln_g, ln_b, tm, row_of_tile, alpha):
    t_all = x_ctx.shape[0] + x_lat.shape[0]
    ctx_tiles = x_ctx.shape[0] // tm
    tile = lambda w: pl.BlockSpec((tm, w), lambda i: (i, 0))
    modrow = pl.BlockSpec((1, 1, D_MODEL), lambda i: (row_of_tile(i), 0, 0))
    full = lambda a: pl.BlockSpec(a.shape, lambda i, _n=a.ndim: (0,) * _n)
    return pl.pallas_call(
        functools.partial(_outproj_kernel, alpha=alpha, ctx_tiles=ctx_tiles),
        grid=(t_all // tm,),
        in_specs=_two_source_specs(tm, RW, ctx_tiles) + _two_source_specs(tm, CW, ctx_tiles)
        + _two_source_specs(tm, D_MODEL, ctx_tiles)
        + [modrow, modrow, modrow, full(w_oa), full(w_ob), full(ln_g), full(ln_b)],
        out_specs=[tile(D_MODEL), tile(D_MODEL)],
        out_shape=[jax.ShapeDtypeStruct((t_all, D_MODEL), F32), jax.ShapeDtypeStruct((t_all, D_MODEL), F32)],
        compiler_params=_cparams(("arbitrary",)),
        name="out_proj",
    )(ya_c, ya_l, yb_c, yb_l, x_ctx, x_lat, g1, sc2, sh2, w_oa, w_ob, ln_g, ln_b)


def _route_kernel(h_ref, wt_hi_ref, wt_lo_ref, bias_ref, ustrict_ref,
                  eidx_ref, wts_ref, pos_ref, cnt_ref, carry_s, *, tm):
    i = pl.program_id(0)

    @pl.when(i == 0)
    def _():
        carry_s[...] = jnp.zeros_like(carry_s)

    h_hi, h_lo = _split_bf16(h_ref[...])
    logits = _dot_nt(wt_hi_ref[...], h_hi) + _dot_nt(wt_hi_ref[...], h_lo) + _dot_nt(wt_lo_ref[...], h_hi)
    scores = _sigmoid(logits)
    sel = scores + bias_ref[...]
    neg = jnp.float32(-jnp.inf)
    gsz = N_EXPERTS // N_GROUPS
    gs = []
    for g in range(N_GROUPS):
        blk = sel[g * gsz:(g + 1) * gsz]
        m1 = jnp.max(blk, axis=0, keepdims=True)
        eq = blk == m1
        cnt = jnp.sum(eq.astype(F32), axis=0, keepdims=True)
        m2 = jnp.max(jnp.where(eq, neg, blk), axis=0, keepdims=True)
        gs.append(m1 + jnp.where(cnt >= 2.0, m1, m2))
    masked = []
    for g in range(N_GROUPS):
        rank = jnp.zeros_like(gs[g])
        for o in range(N_GROUPS):
            if o == g:
                continue
            beats = (gs[o] > gs[g]) | ((gs[o] == gs[g]) & (o < g))
            rank = rank + beats.astype(F32)
        keep = rank < float(TOPK_GROUPS)
        masked.append(jnp.where(keep, sel[g * gsz:(g + 1) * gsz], neg))
    cur = jnp.concatenate(masked, axis=0)
    iota_e = lax.broadcasted_iota(jnp.int32, cur.shape, 0).astype(F32)
    idxs, ws = [], []
    selmask = jnp.zeros(cur.shape, F32)
    for _ in range(TOP_K):
        m = jnp.max(cur, axis=0, keepdims=True)
        idx = jnp.min(jnp.where(cur == m, iota_e, float(N_EXPERTS)), axis=0, keepdims=True)
        onehot = iota_e == idx
        ws.append(jnp.sum(jnp.where(onehot, scores, 0.0), axis=0, keepdims=True))
        idxs.append(idx)
        cur = jnp.where(onehot, neg, cur)
        selmask = jnp.where(onehot, 1.0, selmask)
    pos = _dot(selmask.astype(BF16), ustrict_ref[...]) + carry_s[...]
    carry_s[...] = carry_s[...] + jnp.sum(selmask, axis=1, keepdims=True)
    cnt_ref[...] = carry_s[...]
    wsum = ws[0]
    for k in range(1, TOP_K):
        wsum = wsum + ws[k]
    pks = [jnp.sum(jnp.where(iota_e == idxs[k], pos, 0.0), axis=0, keepdims=True) for k in range(TOP_K)]
    eidx_ref[...] = jnp.concatenate(idxs, axis=0).astype(jnp.int32)
    wts_ref[...] = jnp.concatenate([w / wsum * ROUTED_SCALE for w in ws], axis=0)
    pos_ref[...] = jnp.concatenate(pks, axis=0).astype(jnp.int32)


def _route(h2, wt_hi, wt_lo, bias_col, tm):
    t_all = h2.shape[0]
    ustrict = jnp.asarray(np.triu(np.ones((tm, tm), np.float32), 1), dtype=BF16)
    full = lambda a: pl.BlockSpec(a.shape, lambda i, _n=a.ndim: (0,) * _n)
    tok = pl.BlockSpec((TOP_K, tm), lambda i: (0, i))
    return pl.pallas_call(
        functools.partial(_route_kernel, tm=tm),
        grid=(t_all // tm,),
        in_specs=[pl.BlockSpec((tm, D_MODEL), lambda i: (i, 0)), full(wt_hi), full(wt_lo), full(bias_col),
                  full(ustrict)],
        out_specs=[tok, tok, tok, pl.BlockSpec((N_EXPERTS, 1), lambda i: (0, 0))],
        out_shape=[jax.ShapeDtypeStruct((TOP_K, t_all), jnp.int32), jax.ShapeDtypeStruct((TOP_K, t_all), F32),
                   jax.ShapeDtypeStruct((TOP_K, t_all), jnp.int32), jax.ShapeDtypeStruct((N_EXPERTS, 1), F32)],
        scratch_shapes=[pltpu.VMEM((N_EXPERTS, 1), F32)],
        compiler_params=_cparams(("arbitrary",)),
        name="route",
    )(h2, wt_hi, wt_lo, bias_col, ustrict)


def _slots_kernel(eidx_ref, pos_ref, pstart_ref, slotx_ref, sloty_ref):
    tm = eidx_ref.shape[1]
    iota_e = lax.broadcasted_iota(jnp.int32, (N_EXPERTS, tm), 0)
    for c, out in enumerate((slotx_ref, sloty_ref)):
        rows = []
        for k in range(TOP_K):
            onehot = iota_e == eidx_ref[k:k + 1, :]
            rows.append(jnp.sum(jnp.where(onehot, pstart_ref[:, c:c + 1], 0.0), axis=0, keepdims=True))
        out[...] = jnp.concatenate(rows, axis=0).astype(jnp.int32) + pos_ref[...]


def _slots(eidx, pos, pstart_cols, tm):
    t_all = eidx.shape[1]
    tok = pl.BlockSpec((TOP_K, tm), lambda i: (0, i))
    return pl.pallas_call(
        _slots_kernel,
        grid=(t_all // tm,),
        in_specs=[tok, tok, pl.BlockSpec((N_EXPERTS, 2), lambda i: (0, 0))],
        out_specs=[tok, tok],
        out_shape=[jax.ShapeDtypeStruct((TOP_K, t_all), jnp.int32), jax.ShapeDtypeStruct((TOP_K, t_all), jnp.int32)],
        compiler_params=_cparams(("arbitrary",)),
        name="slots",
    )(eidx, pos, pstart_cols)


TD_TOK = 512
XS_ALIGN = 8


def _dispatch_kernel(fstart_ref, flen_ref, total_ref, slot_ref, h_ref, xs_out, zero_s, sem, zsem, *, xs_rows):
    i = pl.program_id(0)

    def body(j, carry):
        for k in range(TOP_K):
            pltpu.make_async_copy(h_ref.at[pl.ds(j, 1), :], xs_out.at[pl.ds(slot_ref[0, 0, j * TOP_K + k], 1), :],
                                  sem).start(priority=k % 2)
        return carry

    lax.fori_loop(0, TD_TOK, body, 0)
    for k in range(TOP_K):
        pltpu.make_async_copy(h_ref, xs_out.at[pl.ds(0, TD_TOK), :], sem).wait()

    @pl.when(i == pl.num_programs(0) - 1)
    def _():
        zero_s[...] = jnp.zeros_like(zero_s)
        total = total_ref[0]
        n_tail = (xs_rows - total) // XS_ALIGN

        def gap_copy(e, r):
            return pltpu.make_async_copy(zero_s.at[pl.ds(0, 1), :], xs_out.at[pl.ds(fstart_ref[e] + r, 1), :], zsem)

        def tail_copy(q):
            start = pl.multiple_of(total + q * XS_ALIGN, XS_ALIGN)
            return pltpu.make_async_copy(zero_s, xs_out.at[pl.ds(start, XS_ALIGN), :], zsem)

        def for_gaps(fn):
            def per_expert(e, carry):
                for r in range(XS_ALIGN - 1):
                    @pl.when(r < flen_ref[e])
                    def _():
                        fn(gap_copy(e, r))
                return carry
            lax.fori_loop(0, N_EXPERTS, per_expert, 0)

        def for_tail(fn):
            def per_q(q, carry):
                fn(tail_copy(q))
                return carry
            lax.fori_loop(0, n_tail, per_q, 0)

        for_gaps(lambda cp: cp.start())
        for_tail(lambda cp: cp.start())
        for_gaps(lambda cp: cp.wait())
        for_tail(lambda cp: cp.wait())


def _dispatch(slot3, h2, fstart, flen, total, xs_rows):
    t_all = h2.shape[0]
    grid_spec = pltpu.PrefetchScalarGridSpec(
        num_scalar_prefetch=3,
        grid=(t_all // TD_TOK,),
        in_specs=[pl.BlockSpec((1, 1, TD_TOK * TOP_K), lambda i, a, b, c: (i, 0, 0), memory_space=pltpu.SMEM),
                  pl.BlockSpec((TD_TOK, D_MODEL), lambda i, a, b, c: (i, 0))],
        out_specs=pl.BlockSpec(memory_space=pl.ANY),
        scratch_shapes=[pltpu.VMEM((XS_ALIGN, D_MODEL), F32), pltpu.SemaphoreType.DMA(()),
                        pltpu.SemaphoreType.DMA(())],
    )
    return pl.pallas_call(
        functools.partial(_dispatch_kernel, xs_rows=xs_rows),
        grid_spec=grid_spec,
        out_shape=jax.ShapeDtypeStruct((xs_rows, D_MODEL), F32),
        compiler_params=_cparams(("arbitrary",)),
        name="dispatch",
    )(fstart, flen, total, slot3, h2)


X_RING = 4
BLOCK_DMA_PRIORITY = 1


def _expert_kernel(bstart_ref, nblk_ref, nused_ref, xrow_ref, xs_hbm, wgu_ref, wdn_ref, ys_hbm,
                   xbuf, ybuf, wgu16, wdn16, sem_in, sem_out, *, n_blocks):
    e = pl.program_id(0)
    nused = nused_ref[0]
    n = nblk_ref[e]
    g0 = bstart_ref[e]

    def in_copy(g, s):
        start = pl.multiple_of(xrow_ref[g], XS_ALIGN)
        return pltpu.make_async_copy(xs_hbm.at[pl.ds(start, MOE_BLOCK), :], xbuf.at[s], sem_in.at[s])

    def out_copy(g, s):
        return pltpu.make_async_copy(ybuf.at[s], ys_hbm.at[pl.ds(g * MOE_BLOCK, MOE_BLOCK), :], sem_out.at[s])

    @pl.when(e == 0)
    def _():
        for q in range(X_RING - 1):
            @pl.when(q < nused)
            def _():
                in_copy(q, q).start(priority=BLOCK_DMA_PRIORITY)

    @pl.when(n > 0)
    def _():
        wgu16[...] = wgu_ref[0].astype(BF16)
        wdn16[...] = wdn_ref[0].astype(BF16)

    def block(b, carry):
        g = g0 + b
        s = g % 2
        ahead = g + (X_RING - 1)

        @pl.when(ahead < nused)
        def _():
            in_copy(ahead, ahead % X_RING).start(priority=BLOCK_DMA_PRIORITY)

        in_copy(g, g % X_RING).wait()
        gu = _dot(xbuf[g % X_RING].astype(BF16), wgu16[...])
        gate = gu[:, :EXPERT_DIM]
        act = (gate * _sigmoid(gate) * gu[:, EXPERT_DIM:]).astype(BF16)
        y = _dot(act, wdn16[...])

        @pl.when(g >= 2)
        def _():
            out_copy(g - 2, s).wait()

        ybuf[s] = y
        out_copy(g, s).start(priority=BLOCK_DMA_PRIORITY)
        return carry

    lax.fori_loop(0, n, block, 0)

    @pl.when(e == pl.num_programs(0) - 1)
    def _():
        @pl.when(nused >= 2)
        def _():
            out_copy(nused - 2, nused % 2).wait()

        @pl.when(nused >= 1)
        def _():
            out_copy(nused - 1, (nused - 1) % 2).wait()

        ybuf[0] = jnp.zeros((MOE_BLOCK, D_MODEL), F32)

        def tail(g, carry):
            out_copy(g, 0).start()
            out_copy(g, 0).wait()
            return carry

        lax.fori_loop(nused, n_blocks, tail, 0)


def _experts(xs, bstart, nblk, nused, xrow, w_gu, w_down):
    n_blocks = xrow.shape[0]
    grid_spec = pltpu.PrefetchScalarGridSpec(
        num_scalar_prefetch=4,
        grid=(N_EXPERTS,),
        in_specs=[pl.BlockSpec(memory_space=pl.ANY),
                  pl.BlockSpec((1, D_MODEL, 2 * EXPERT_DIM), lambda e, bs, nb, nu, xr: (e, 0, 0)),
                  pl.BlockSpec((1, EXPERT_DIM, D_MODEL), lambda e, bs, nb, nu, xr: (e, 0, 0))],
        out_specs=pl.BlockSpec(memory_space=pl.ANY),
        scratch_shapes=[pltpu.VMEM((X_RING, MOE_BLOCK, D_MODEL), F32),
                        pltpu.VMEM((2, MOE_BLOCK, D_MODEL), F32),
                        pltpu.VMEM((D_MODEL, 2 * EXPERT_DIM), BF16),
                        pltpu.VMEM((EXPERT_DIM, D_MODEL), BF16),
                        pltpu.SemaphoreType.DMA((X_RING,)), pltpu.SemaphoreType.DMA((2,))],
    )
    return pl.pallas_call(
        functools.partial(_expert_kernel, n_blocks=n_blocks),
        grid_spec=grid_spec,
        out_shape=jax.ShapeDtypeStruct((n_blocks * MOE_BLOCK, D_MODEL), F32),
        compiler_params=_cparams(("arbitrary",)),
        name="experts",
    )(bstart, nblk, nused, xrow, xs, w_gu, w_down)


TC_TOK = 128


def _combine_kernel(slot_ref, slot_next_ref, wt_ref, x1_ref, h2_ref, g2_ref, sgu_ref, sdn_ref, lg_ref, lb_ref,
                    y_hbm, oc_ref, ol_ref, gbuf, sems, *, n_tiles, ctx_tiles, alpha):
    i = pl.program_id(0)

    def row_copy(sref, b, j, k):
        return pltpu.make_async_copy(y_hbm.at[pl.ds(sref[0, 0, j * TOP_K + k], 1), :],
                                     gbuf.at[b, k, pl.ds(j, 1), :], sems.at[b])

    def wait_tile(b):
        for k in range(TOP_K):
            pltpu.make_async_copy(y_hbm.at[pl.ds(0, TC_TOK), :], gbuf.at[b, k], sems.at[b]).wait()

    @pl.when(i == 0)
    def _():
        def body(j, carry):
            for k in range(TOP_K):
                row_copy(slot_ref, 0, j, k).start(priority=k % 2)
            return carry
        lax.fori_loop(0, TC_TOK, body, 0)

    b = i % 2
    nb = 1 - b

    def issue_next(j0, j1):
        for j in range(j0, j1):
            for k in range(TOP_K):
                row_copy(slot_next_ref, nb, j, k).start(priority=k % 2)

    issue_next(0, TC_TOK // 2)
    h = h2_ref[...].astype(BF16)
    su = _dot(h, sgu_ref[...])
    sg = su[:, :EXPERT_DIM]
    act = (sg * _sigmoid(sg) * su[:, EXPERT_DIM:]).astype(BF16)
    ffn = _dot(act, sdn_ref[...])
    wait_tile(b)
    issue_next(TC_TOK // 2, TC_TOK)
    for k in range(TOP_K):
        ffn = ffn + gbuf[b, k] * wt_ref[:, k:k + 1]
    out = _layer_norm(alpha * x1_ref[...] + g2_ref[0] * ffn, lg_ref[...], lb_ref[...])

    @pl.when(i < ctx_tiles)
    def _():
        oc_ref[...] = out

    @pl.when(i >= ctx_tiles)
    def _():
        ol_ref[...] = out

    @pl.when(i == n_tiles - 1)
    def _():
        wait_tile(nb)


def _combine(slot3, wts_t, x1, h2, g2, sh_gu, sh_dn, ln_g, ln_b, y_sorted, row_of_tile, alpha, tok_ctx):
    t_all = x1.shape[0]
    n_tiles = t_all // TC_TOK
    ctx_tiles = tok_ctx // TC_TOK
    tile = lambda w: pl.BlockSpec((TC_TOK, w), lambda i: (i, 0))
    full = lambda a: pl.BlockSpec(a.shape, lambda i, _n=a.ndim: (0,) * _n)
    return pl.pallas_call(
        functools.partial(_combine_kernel, n_tiles=n_tiles, ctx_tiles=ctx_tiles, alpha=alpha),
        grid=(n_tiles,),
        in_specs=[pl.BlockSpec((1, 1, TC_TOK * TOP_K), lambda i: (i, 0, 0), memory_space=pltpu.SMEM),
                  pl.BlockSpec((1, 1, TC_TOK * TOP_K), lambda i: (jnp.minimum(i + 1, n_tiles - 1), 0, 0),
                               memory_space=pltpu.SMEM),
                  tile(TOP_K), tile(D_MODEL), tile(D_MODEL),
                  pl.BlockSpec((1, 1, D_MODEL), lambda i: (row_of_tile(i), 0, 0)),
                  full(sh_gu), full(sh_dn), full(ln_g), full(ln_b),
                  pl.BlockSpec(memory_space=pl.ANY)],
        out_specs=_two_source_specs(TC_TOK, D_MODEL, ctx_tiles),
        out_shape=[jax.ShapeDtypeStruct((tok_ctx, D_MODEL), F32),
                   jax.ShapeDtypeStruct((t_all - tok_ctx, D_MODEL), F32)],
        scratch_shapes=[pltpu.VMEM((2, TOP_K, TC_TOK, D_MODEL), F32), pltpu.SemaphoreType.DMA((2,))],
        compiler_params=_cparams(("arbitrary",)),
        name="combine",
    )(slot3, slot3, wts_t, x1, h2, g2, sh_gu, sh_dn, ln_g, ln_b, y_sorted)


def kernel(x_prompt, x_sample, state_rwkv, c, c_ctx, w_ada, b_ada, w_in, mu_shift, w_decay0, w_decay2, w_a0, w_a2, w_g2, k_k, k_a, r_k, lnx_g, lnx_b, conv_w, conv_b, conv_ln_g, conv_ln_b, w_out, ln1_g, ln1_b, router_w, router_bias, expert_w_gu, expert_w_down, shared_w_gu, shared_w_down, ln2_g, ln2_b):
    depth = w_ada.shape[0]
    assert depth == 1
    alpha = (2.0 * depth) ** 0.25
    n_ctx, t_ctx, _ = x_prompt.shape
    n_lat, t_lat, _ = x_sample.shape
    tok_ctx = n_ctx * t_ctx
    tok_lat = n_lat * t_lat
    t_all = tok_ctx + tok_lat
    tm = 1024
    assert tok_ctx % tm == 0 and t_lat % tm == 0 and t_lat // GRID_W * GRID_W == t_lat
    l = 0

    x_ctx = x_prompt.reshape(tok_ctx, D_MODEL)
    x_lat = x_sample.reshape(tok_lat, D_MODEL)

    cond8 = jnp.zeros((8, D_MODEL), F32).at[0].set(c_ctx).at[1:1 + n_lat].set(c)
    mod = _ada(cond8, w_ada[l], b_ada[l])
    sh1, sc1, g1, sh2, sc2, g2 = [m.reshape(8, 1, D_MODEL) for m in jnp.split(mod, 6, axis=-1)]

    def row_of_tile_for(tile_rows):
        ctx_tiles = tok_ctx // tile_rows
        per_seq = t_lat // tile_rows
        return lambda i: jnp.where(i < ctx_tiles, 0, 1 + (i - ctx_tiles) // per_seq)

    w_in_a = w_in[l][:, :RWKV_IN].astype(BF16)
    w_in_b = w_in[l][:, RWKV_IN:].astype(BF16)
    proj_a, proj_b = _in_proj(x_ctx, x_lat, sc1, sh1, w_in_a, w_in_b, tm, row_of_tile_for(tm))

    zeros64 = jnp.zeros((64, 2 * RW), F32)
    wd2 = jnp.concatenate([w_decay2[l, 0], w_decay2[l, 1]], axis=1)
    wa2 = jnp.concatenate([w_a2[l, 0], w_a2[l, 1]], axis=1)
    w_lora = jnp.concatenate([jnp.concatenate([wd2, zeros64], axis=1),
                              jnp.concatenate([zeros64, wa2], axis=1)], axis=0).astype(BF16)
    row = lambda v: v.reshape(1, -1)
    rwkv_w = [row(mu_shift[l]), w_lora, w_g2[l].astype(BF16), row(w_decay0[l]), row(w_a0[l]),
              row(k_k[l]), row(k_a[l]), row(r_k[l]), row(lnx_g[l]), row(lnx_b[l])]
    consts = _rwkv_consts()

    def to_cat(s):
        b = s.shape[0]
        s = s.reshape(b, 2, 2, 4, HD, HD)
        return jnp.transpose(s, (0, 1, 2, 5, 3, 4)).reshape(b, 4, HD, GW)

    s0_ctx = jnp.zeros((n_ctx, 4, HD, GW), F32)
    s0_lat = to_cat(state_rwkv[:, l].astype(F32))
    ya_c, sfin_ctx = _rwkv(proj_a, s0_ctx, rwkv_w, consts, t_ctx, n_ctx, 0)
    ya_l, _ = _rwkv(proj_a, s0_lat, rwkv_w, consts, t_lat, n_lat, tok_ctx // t_lat)
    sfin = jnp.transpose(sfin_ctx.reshape(n_ctx, 2, 2, HD, 4, HD), (0, 1, 2, 4, 5, 3))
    new_state = sfin.reshape(n_ctx, 1, 2, 2 * 4, HD, HD).astype(x_prompt.dtype)

    conv_wts = [conv_w[l], row(conv_b[l]), row(conv_ln_g[l]), row(conv_ln_b[l])]
    yb_c = _conv(proj_b, conv_wts, t_ctx, n_ctx, 0, False)
    yb_l = _conv(proj_b, conv_wts, t_lat, n_lat, tok_ctx // t_lat, True)

    w_oa = w_out[l][:RW].astype(BF16)
    w_ob = w_out[l][RW:].astype(BF16)
    x1, h2 = _out_proj(ya_c, ya_l, yb_c, yb_l, x_ctx, x_lat, g1, sc2, sh2, w_oa, w_ob, row(ln1_g[l]), row(ln1_b[l]), tm,
                       row_of_tile_for(tm), alpha)

    rwt = router_w[l].T
    rwt_hi = rwt.astype(BF16)
    rwt_lo = (rwt - rwt_hi.astype(F32)).astype(BF16)
    eidx, wts, pos, counts = _route(h2, rwt_hi, rwt_lo, router_bias[l].reshape(N_EXPERTS, 1), 512)

    n_assign = t_all * TOP_K
    n_blocks = (n_assign + N_EXPERTS * (MOE_BLOCK - 1) + MOE_BLOCK - 1) // MOE_BLOCK
    cnt = counts.reshape(N_EXPERTS).astype(jnp.int32)
    padded = (cnt + MOE_BLOCK - 1) // MOE_BLOCK * MOE_BLOCK
    pend = jnp.cumsum(padded)
    pstart = pend - padded
    padded_x = (cnt + XS_ALIGN - 1) // XS_ALIGN * XS_ALIGN
    pend_x = jnp.cumsum(padded_x)
    pstart_x = pend_x - padded_x
    xs_rows = n_assign + N_EXPERTS * (XS_ALIGN - 1) // XS_ALIGN * XS_ALIGN + MOE_BLOCK
    slot_x, slot_y = _slots(eidx, pos, jnp.stack([pstart_x, pstart], axis=1).astype(F32), 512)
    per_tile = lambda s, tok: s.T.reshape(t_all // tok, 1, tok * TOP_K)
    xs = _dispatch(per_tile(slot_x, TD_TOK), h2, (pstart_x + cnt).astype(jnp.int32), (padded_x - cnt).astype(jnp.int32),
                   pend_x[-1].astype(jnp.int32).reshape(1), xs_rows)
    bstart = (pstart // MOE_BLOCK).astype(jnp.int32)
    nblk = (padded // MOE_BLOCK).astype(jnp.int32)
    nused = (pend[-1] // MOE_BLOCK).astype(jnp.int32).reshape(1)
    blk_ids = jnp.arange(n_blocks, dtype=jnp.int32)
    blk_e = jnp.minimum(jnp.sum((pend[None, :] <= (blk_ids * MOE_BLOCK)[:, None]).astype(jnp.int32), axis=1),
                        N_EXPERTS - 1)
    shift = jnp.sum(jnp.where(blk_e[:, None] == jnp.arange(N_EXPERTS, dtype=jnp.int32)[None, :],
                              (pstart_x - pstart)[None, :], 0), axis=1)
    xrow = jnp.clip(blk_ids * MOE_BLOCK + shift, 0, xs_rows - MOE_BLOCK).astype(jnp.int32)
    y_sorted = _experts(xs, bstart, nblk, nused, xrow, expert_w_gu[l], expert_w_down[l])
    y_ctx, y_lat = _combine(per_tile(slot_y, TC_TOK), wts.T, x1, h2, g2, shared_w_gu[l].astype(BF16),
                            shared_w_down[l].astype(BF16), row(ln2_g[l]), row(ln2_b[l]), y_sorted,
                            row_of_tile_for(TC_TOK), alpha, tok_ctx)
    return (y_ctx.reshape(n_ctx, t_ctx, D_MODEL), y_lat.reshape(n_lat, t_lat, D_MODEL), new_state)
```

```python
import functools
import math

import numpy as np
import jax
import jax.numpy as jnp
from jax import lax
from jax.experimental import pallas as pl
from jax.experimental.pallas import tpu as pltpu

F32 = jnp.float32
BF16 = jnp.bfloat16

D_MODEL = 1024
RW = 512
HD = 64
CW = 512
CONV_K = 31
RWKV_IN = 3 * RW + 64 + 64 + 128
N_EXPERTS = 256
TOP_K = 8
N_GROUPS = 8
TOPK_GROUPS = 4
EXPERT_DIM = 256
ROUTED_SCALE = 2.5
MOE_BLOCK = 128
GRID_W = 64
LN_EPS = 1e-5
GN_EPS = 64e-5
CHUNK = 64
GW = 256
SCAN_UNROLL = 2
DECAY_SCALE = math.exp(-0.5)
VMEM_LIMIT = 56 * 1024 * 1024


def _cparams(sem):
    return pltpu.CompilerParams(dimension_semantics=sem, vmem_limit_bytes=VMEM_LIMIT)


def _split_bf16(x):
    hi = x.astype(BF16)
    lo = (x - hi.astype(F32)).astype(BF16)
    return hi, lo


def _dot(a, b):
    return jnp.dot(a, b, preferred_element_type=F32)


def _dot_nt(a, b):
    return lax.dot_general(a, b, (((1,), (1,)), ((), ())), preferred_element_type=F32)


def _dot_tn(a, b):
    return lax.dot_general(a, b, (((0,), (0,)), ((), ())), preferred_element_type=F32)


def _sigmoid(x):
    return 1.0 / (1.0 + jnp.exp(-x))


def _layer_norm(x, g, b):
    mu = jnp.mean(x, axis=-1, keepdims=True)
    d = x - mu
    var = jnp.mean(d * d, axis=-1, keepdims=True)
    return d * lax.rsqrt(var + LN_EPS) * g + b


def _ada_kernel(c_ref, w_ref, b_ref, o_ref):
    c = c_ref[...]
    s = c * _sigmoid(c)
    s_hi, s_lo = _split_bf16(s)
    w_hi, w_lo = _split_bf16(w_ref[...])
    o_ref[...] = _dot(s_hi, w_hi) + _dot(s_lo, w_hi) + _dot(s_hi, w_lo) + b_ref[...]


def _ada(cond8, w_ada, b_ada):
    n = w_ada.shape[1]
    tn = 1536
    return pl.pallas_call(
        _ada_kernel,
        grid=(n // tn,),
        in_specs=[pl.BlockSpec((8, D_MODEL), lambda j: (0, 0)),
                  pl.BlockSpec((D_MODEL, tn), lambda j: (0, j)),
                  pl.BlockSpec((1, tn), lambda j: (0, j))],
        out_specs=pl.BlockSpec((8, tn), lambda j: (0, j)),
        out_shape=jax.ShapeDtypeStruct((8, n), F32),
        compiler_params=_cparams(("arbitrary",)),
        name="ada",
    )(cond8, w_ada, b_ada.reshape(1, n))


def _two_source_specs(tm, width, ctx_tiles):
    return [pl.BlockSpec((tm, width), lambda i: (jnp.minimum(i, ctx_tiles - 1), 0)),
            pl.BlockSpec((tm, width), lambda i: (jnp.maximum(i - ctx_tiles, 0), 0))]


def _inproj_kernel(xc_ref, xl_ref, sc_ref, sh_ref, wa_ref, wb_ref, oa_ref, ob_ref, *, ctx_tiles):
    x = jnp.where(pl.program_id(0) < ctx_tiles, xc_ref[...], xl_ref[...])
    h = (x * (1.0 + sc_ref[0]) + sh_ref[0]).astype(BF16)
    oa_ref[...] = _dot(h, wa_ref[...])
    ob_ref[...] = _dot(h, wb_ref[...])


def _in_proj(x_ctx, x_lat, sc1, sh1, w_a, w_b, tm, row_of_tile):
    t_all = x_ctx.shape[0] + x_lat.shape[0]
    ctx_tiles = x_ctx.shape[0] // tm
    return pl.pallas_call(
        functools.partial(_inproj_kernel, ctx_tiles=ctx_tiles),
        grid=(t_all // tm,),
        in_specs=_two_source_specs(tm, D_MODEL, ctx_tiles) + [
                  pl.BlockSpec((1, 1, D_MODEL), lambda i: (row_of_tile(i), 0, 0)),
                  pl.BlockSpec((1, 1, D_MODEL), lambda i: (row_of_tile(i), 0, 0)),
                  pl.BlockSpec(w_a.shape, lambda i: (0, 0)),
                  pl.BlockSpec(w_b.shape, lambda i: (0, 0))],
        out_specs=[pl.BlockSpec((tm, RWKV_IN), lambda i: (i, 0)),
                   pl.BlockSpec((tm, 2 * CW), lambda i: (i, 0))],
        out_shape=[jax.ShapeDtypeStruct((t_all, RWKV_IN), F32),
                   jax.ShapeDtypeStruct((t_all, 2 * CW), F32)],
        compiler_params=_cparams(("arbitrary",)),
        name="in_proj",
    )(x_ctx, x_lat, sc1, sh1, w_a, w_b)


def _rwkv_consts():
    c = CHUNK
    t = np.arange(c)[:, None]
    s = np.arange(GW)[None, :] % c
    sl = (s < t).astype(np.float32)
    il = (s <= t).astype(np.float32)
    su = (s > t).astype(np.float32)
    iu = (s >= t).astype(np.float32)
    eye_cat = (s == t).astype(np.float32)
    tri = np.stack([sl, il, su, iu, eye_cat])
    r = np.arange(GW)
    mask_bd = (r[:, None] // HD == r[None, :] // HD).astype(np.float32)
    eye = np.eye(GW, dtype=np.float32)
    cum = np.stack([np.tril(np.ones((c, c), np.float32)), np.triu(np.ones((c, c), np.float32))])
    q = np.arange(RW)
    seg = (q[:, None] // HD == q[None, :] // HD).astype(np.float32)
    return (jnp.asarray(tri), jnp.asarray(mask_bd, dtype=BF16), jnp.asarray(mask_bd), jnp.asarray(eye),
            jnp.asarray(cum, dtype=BF16), jnp.asarray(seg, dtype=BF16))


def _rwkv_kernel(p_ref, s0_ref, mu_ref, wlora_ref, wg2_ref, wd0_ref, wa0_ref, kk_ref, ka_ref, rk_ref,
                 lng_ref, lnb_ref, tri_ref, mbd16_ref, mbd_ref, eye_ref, cum_ref, seg_ref,
                 ya_ref, sfin_ref,
                 r_s, v_s, kk_s, g_s, bon_s, a_s, lw_s, kd_s, y_s, st_s, *, seq_len):
    nc = seq_len // CHUNK
    scan_unroll = SCAN_UNROLL
    seg = seg_ref[...]

    def seg_sum(x):
        hi, lo = _split_bf16(x)
        return _dot(hi, seg) + _dot(lo, seg)

    def phase_a(c, carry):
        t0 = pl.multiple_of(c * CHUNK, CHUNK)
        cur = p_ref[pl.ds(t0, CHUNK), :]
        prev_row = p_ref[pl.ds(jnp.maximum(t0 - 1, 0), 1), :] * (t0 > 0).astype(F32)
        next_row = p_ref[pl.ds(jnp.minimum(t0 + CHUNK, seq_len - 1), 1), :] * (t0 + CHUNK < seq_len).astype(F32)
        row = lax.broadcasted_iota(jnp.int32, cur.shape, 0)
        prev = jnp.where(row == 0, prev_row, pltpu.roll(cur, 1, 0))
        nxt = jnp.where(row == CHUNK - 1, next_row, pltpu.roll(cur, CHUNK - 1, 0))
        p = cur + mu_ref[...] * (0.5 * (prev + nxt) - cur)
        r = p[:, 0:RW]
        k = p[:, RW:2 * RW]
        v = p[:, 2 * RW:3 * RW]
        z = p[:, 3 * RW:3 * RW + 128]
        gd = p[:, 3 * RW + 128:3 * RW + 256]
        lane = lax.broadcasted_iota(jnp.int32, z.shape, 1)
        z = jnp.where(lane < 64, jnp.tanh(z), z)
        lora = _dot(z.astype(BF16), wlora_ref[...])
        g = _dot(_sigmoid(gd).astype(BF16), wg2_ref[...])
        kks = k * kk_ref[...]
        kk = kks * lax.rsqrt(seg_sum(kks * kks) + 1e-12)
        rows = pl.ds(t0, CHUNK)
        r_s[rows, :] = r
        v_s[rows, :] = v.astype(BF16)
        kk_s[rows, :] = kk
        g_s[rows, :] = g
        kd_sum = jnp.zeros_like(k)
        for d in range(2):
            w_logit = wd0_ref[:, d * RW:(d + 1) * RW] + lora[:, d * RW:(d + 1) * RW]
            a = _sigmoid(wa0_ref[:, d * RW:(d + 1) * RW] + lora[:, (2 + d) * RW:(3 + d) * RW])
            kd = k * (1.0 + (a - 1.0) * ka_ref[...])
            lw_s[d, rows, :] = -DECAY_SCALE * _sigmoid(w_logit)
            a_s[d, rows, :] = a
            kd_s[d, rows, :] = kd
            kd_sum = kd_sum + kd
        bon_s[rows, :] = seg_sum(r * kd_sum * rk_ref[...]) * v
        return carry

    lax.fori_loop(0, nc, phase_a, 0, unroll=2)

    mbd16 = mbd16_ref[...]
    mbd = mbd_ref[...]
    eye = eye_ref[...]

    def bd(x16):
        return jnp.concatenate([x16, x16, x16, x16], axis=0) * mbd16

    for u in range(4):
        s0 = s0_ref[0, u]
        st_s[u] = jnp.concatenate([s0, s0, s0, s0], axis=0) * mbd

    def prep(d, grp, t0):
        rows = pl.ds(t0, CHUNK)
        lanes = slice(grp * GW, (grp + 1) * GW)
        lw = lw_s[d, rows, lanes]
        hi, lo = _split_bf16(lw)
        cl = _dot(cum_ref[d], hi) + _dot(cum_ref[d], lo)
        e_cl = jnp.exp(cl)
        e_ce = jnp.exp(cl - lw)
        e_ncl = jnp.exp(-cl)
        tot = cl[CHUNK - 1:CHUNK, :] if d == 0 else cl[0:1, :]
        wc = jnp.exp(tot)
        kk = kk_s[rows, lanes]
        rt = r_s[rows, lanes] * e_cl
        bt = a_s[d, rows, lanes] * kk * e_ncl
        kt = kd_s[d, rows, lanes] * e_ncl
        kq16 = (kk * e_ce).astype(BF16)
        return dict(d=d, u=d * 2 + grp, rows=rows, lanes=lanes, wc=wc, rt=rt, kq16=kq16,
                    v16=v_s[rows, lanes], bt16=bt.astype(BF16), kt16=kt.astype(BF16),
                    btw16=(bt * wc).astype(BF16), ktw16=(kt * wc).astype(BF16),
                    g16=jnp.concatenate([kq16, rt.astype(BF16)], axis=0))

    def phase_b(i2, carry):
        us = []
        for j in range(scan_unroll):
            i = i2 * scan_unroll + j
            tf = pl.multiple_of(i * CHUNK, CHUNK)
            tb = pl.multiple_of((nc - 1 - i) * CHUNK, CHUNK)
            us += [prep(0, 0, tf), prep(1, 0, tb), prep(0, 1, tf), prep(1, 1, tb)]
        each = lambda fn: [fn(q) for q in us]
        strict = lambda q: tri_ref[2 * q['d']]
        incl = lambda q: tri_ref[2 * q['d'] + 1]
        out_b = each(lambda q: _dot_nt(q['g16'], bd(q['bt16'])))
        out_k = each(lambda q: _dot_nt(q['g16'], bd(q['kt16'])))
        a_bk = [o[:CHUNK] * strict(q) for o, q in zip(out_b, us)]
        p_rb = [(o[CHUNK:] * incl(q)).astype(BF16) for o, q in zip(out_b, us)]
        a_kk = [(o[:CHUNK] * strict(q)).astype(BF16) for o, q in zip(out_k, us)]
        p_rk = [(o[CHUNK:] * incl(q)).astype(BF16) for o, q in zip(out_k, us)]
        tinv = [tri_ref[4] - a for a in a_bk]
        pw = [a.astype(BF16) for a in a_bk]
        for _ in range(5):
            pw = [_dot(p, bd(p)).astype(BF16) for p in pw]
            tinv = [t + _dot(p, bd(t.astype(BF16))) for t, p in zip(tinv, pw)]
        bdv = each(lambda q: bd(q['v16']))
        akkv = [_dot(a, b) for a, b in zip(a_kk, bdv)]
        x = [_dot(t.astype(BF16), jnp.concatenate([bd(q['kq16']), bd(k.astype(BF16))], axis=1))
             for t, q, k in zip(tinv, us, akkv)]
        qu16 = [z.astype(BF16) for z in x]
        tn1 = [_dot_tn(q['btw16'], z) for q, z in zip(us, qu16)]
        tn2 = each(lambda q: _dot_tn(q['ktw16'], q['v16']))
        r1 = [_dot(p, jnp.concatenate([bd(z[:, :GW]), bd(z[:, GW:])], axis=1)) for p, z in zip(p_rb, qu16)]
        y0 = [_dot(p, b) - r[:, GW:] for p, b, r in zip(p_rk, bdv, r1)]
        for q, t1, t2, r, y in zip(us, tn1, tn2, r1, y0):
            mc = (eye * q['wc'] - t1[:, :GW]) * mbd
            ncm = (t2 - t1[:, GW:]) * mbd
            rh = q['rt'] - r[:, :GW]
            st16 = st_s[q['u']].astype(BF16)
            big = _dot(jnp.concatenate([rh.astype(BF16), mc.astype(BF16)], axis=0), st16)
            y_s[q['d'], q['rows'], q['lanes']] = big[:CHUNK] + y
            st_s[q['u']] = big[CHUNK:] + ncm
        return carry

    lax.fori_loop(0, nc // scan_unroll, phase_b, 0)

    for u in range(4):
        st = st_s[u]
        sfin_ref[0, u] = st[0:HD] + st[HD:2 * HD] + st[2 * HD:3 * HD] + st[3 * HD:4 * HD]

    def phase_c(c, carry):
        rows = pl.ds(pl.multiple_of(c * CHUNK, CHUNK), CHUNK)
        y = y_s[0, rows, :] + y_s[1, rows, :]
        mean = seg_sum(y) * (1.0 / HD)
        dlt = y - mean
        var = seg_sum(dlt * dlt) * (1.0 / HD)
        yn = dlt * lax.rsqrt(var + GN_EPS)
        ya_ref[rows, :] = (yn * lng_ref[...] + lnb_ref[...] + bon_s[rows, :]) * g_s[rows, :]
        return carry

    lax.fori_loop(0, nc, phase_c, 0, unroll=2)


def _rwkv(proj_a, s0_cat, wts, consts, seq_len, n_seq, first_block):
    full = lambda a: pl.BlockSpec(a.shape, lambda s, _n=a.ndim: (0,) * _n)
    p_mode = dict(pipeline_mode=pl.Buffered(1)) if n_seq <= 2 else {}
    in_specs = [pl.BlockSpec((seq_len, RWKV_IN), lambda s: (first_block + s, 0), **p_mode),
                pl.BlockSpec((1, 4, HD, GW), lambda s: (s, 0, 0, 0))]
    in_specs += [full(a) for a in wts] + [full(a) for a in consts]
    args = [proj_a, s0_cat] + list(wts) + list(consts)
    seq = (seq_len, RW)
    return pl.pallas_call(
        functools.partial(_rwkv_kernel, seq_len=seq_len),
        grid=(n_seq,),
        in_specs=in_specs,
        out_specs=[pl.BlockSpec((seq_len, RW), lambda s: (s, 0)),
                   pl.BlockSpec((1, 4, HD, GW), lambda s: (s, 0, 0, 0))],
        out_shape=[jax.ShapeDtypeStruct((n_seq * seq_len, RW), F32),
                   jax.ShapeDtypeStruct((n_seq, 4, HD, GW), F32)],
        scratch_shapes=[pltpu.VMEM(seq, F32), pltpu.VMEM(seq, BF16), pltpu.VMEM(seq, F32), pltpu.VMEM(seq, F32),
                        pltpu.VMEM(seq, F32),
                        pltpu.VMEM((2,) + seq, F32), pltpu.VMEM((2,) + seq, F32), pltpu.VMEM((2,) + seq, F32),
                        pltpu.VMEM((2,) + seq, F32), pltpu.VMEM((4, GW, GW), F32)],
        compiler_params=_cparams(("arbitrary",)),
        name="rwkv_%d" % seq_len,
    )(*args)


CONV_PAD = 16
VPAD = (CONV_K // 2) * GRID_W


def _conv_kernel(u_ref, w_ref, b_ref, g_ref, beta_ref, o_ref, pad_s, vpad_s, y_s, *, seq_len, grid):
    rb = 256
    glu = u_ref[:, :CW] * _sigmoid(u_ref[:, CW:])
    zeros = jnp.zeros((CONV_PAD, CW), F32)
    pad_s[0:CONV_PAD, :] = zeros
    pad_s[CONV_PAD + seq_len:2 * CONV_PAD + seq_len, :] = zeros
    pad_s[CONV_PAD:CONV_PAD + seq_len, :] = glu
    half = CW // 2
    if grid:
        vz = jnp.zeros((VPAD, half), F32)
        vpad_s[0:VPAD, :] = vz
        vpad_s[VPAD + seq_len:2 * VPAD + seq_len, :] = vz
        vpad_s[VPAD:VPAD + seq_len, :] = glu[:, half:]
    n_h_lanes = (half if grid else CW) // 128
    for r0 in range(0, seq_len, rb):
        for lb in range(n_h_lanes):
            lanes = slice(lb * 128, (lb + 1) * 128)
            acc = jnp.zeros((rb, 128), F32)
            if grid:
                col = lax.broadcasted_iota(jnp.int32, (rb, 128), 0) % GRID_W
            span = rb + 2 * CONV_PAD
            xpad = pad_s[r0:r0 + span, lanes]
            for res in range(8):
                xrot = xpad if res == 0 else pltpu.roll(xpad, span - res, 0)
                for k in range(CONV_K):
                    if (k + 1) % 8 != res:
                        continue
                    off = (k + 1) - res
                    term = xrot[off:off + rb] * w_ref[k:k + 1, lanes]
                    if grid:
                        src = col + (k - CONV_K // 2)
                        term = jnp.where((src >= 0) & (src < GRID_W), term, 0.0)
                    acc = acc + term
            y_s[r0:r0 + rb, lanes] = acc
        if grid:
            for lb in range(half // 128):
                lanes = slice(lb * 128, (lb + 1) * 128)
                wl = slice(half + lb * 128, half + (lb + 1) * 128)
                acc = jnp.zeros((rb, 128), F32)
                for k in range(CONV_K):
                    start = VPAD + r0 + (k - CONV_K // 2) * GRID_W
                    acc = acc + vpad_s[start:start + rb, lanes] * w_ref[k:k + 1, wl]
                y_s[r0:r0 + rb, wl] = acc
    for r0 in range(0, seq_len, rb):
        y = y_s[r0:r0 + rb, :] + b_ref[...]
        yl = _layer_norm(y, g_ref[...], beta_ref[...])
        o_ref[r0:r0 + rb, :] = yl * _sigmoid(yl)


def _conv(proj_b, wts, seq_len, n_seq, first_block, grid):
    full = lambda a: pl.BlockSpec(a.shape, lambda s, _n=a.ndim: (0,) * _n)
    in_specs = [pl.BlockSpec((seq_len, 2 * CW), lambda s: (first_block + s, 0))] + [full(a) for a in wts]
    args = [proj_b] + list(wts)
    return pl.pallas_call(
        functools.partial(_conv_kernel, seq_len=seq_len, grid=grid),
        grid=(n_seq,),
        in_specs=in_specs,
        out_specs=pl.BlockSpec((seq_len, CW), lambda s: (s, 0)),
        out_shape=jax.ShapeDtypeStruct((n_seq * seq_len, CW), F32),
        scratch_shapes=[pltpu.VMEM((seq_len + 2 * CONV_PAD, CW), F32),
                        pltpu.VMEM((seq_len + 2 * VPAD if grid else 8, CW // 2), F32),
                        pltpu.VMEM((seq_len, CW), F32)],
        compiler_params=_cparams(("arbitrary",)),
        name="conv_%d" % seq_len,
    )(*args)


def _outproj_kernel(yac_ref, yal_ref, ybc_ref, ybl_ref, xc_ref, xl_ref, g1_ref, sc2_ref, sh2_ref, wa_ref, wb_ref,
                    lg_ref, lb_ref, x1_ref, h2_ref, *, alpha, ctx_tiles):
    is_ctx = pl.program_id(0) < ctx_tiles
    ya = jnp.where(is_ctx, yac_ref[...], yal_ref[...])
    yb = jnp.where(is_ctx, ybc_ref[...], ybl_ref[...])
    x = jnp.where(is_ctx, xc_ref[...], xl_ref[...])
    mix = _dot(ya.astype(BF16), wa_ref[...]) + _dot(yb.astype(BF16), wb_ref[...])
    x1 = _layer_norm(alpha * x + g1_ref[0] * mix, lg_ref[...], lb_ref[...])
    x1_ref[...] = x1
    h2_ref[...] = x1 * (1.0 + sc2_ref[0]) + sh2_ref[0]


def _out_proj(ya_c, ya_l, yb_c, yb_l, x_ctx, x_lat, g1, sc2, sh2, w_oa, w_ob, ln_g, ln_b, tm, row_of_tile, alpha):
    t_all = x_ctx.shape[0] + x_lat.shape[0]
    ctx_tiles = x_ctx.shape[0] // tm
    tile = lambda w: pl.BlockSpec((tm, w), lambda i: (i, 0))
    modrow = pl.BlockSpec((1, 1, D_MODEL), lambda i: (row_of_tile(i), 0, 0))
    full = lambda a: pl.BlockSpec(a.shape, lambda i, _n=a.ndim: (0,) * _n)
    return pl.pallas_call(
        functools.partial(_outproj_kernel, alpha=alpha, ctx_tiles=ctx_tiles),
        grid=(t_all // tm,),
        in_specs=_two_source_specs(tm, RW, ctx_tiles) + _two_source_specs(tm, CW, ctx_tiles)
        + _two_source_specs(tm, D_MODEL, ctx_tiles)
        + [modrow, modrow, modrow, full(w_oa), full(w_ob), full(ln_g), full(ln_b)],
        out_specs=[tile(D_MODEL), tile(D_MODEL)],
        out_shape=[jax.ShapeDtypeStruct((t_all, D_MODEL), F32), jax.ShapeDtypeStruct((t_all, D_MODEL), F32)],
        compiler_params=_cparams(("arbitrary",)),
        name="out_proj",
    )(ya_c, ya_l, yb_c, yb_l, x_ctx, x_lat, g1, sc2, sh2, w_oa, w_ob, ln_g, ln_b)


def _route_kernel(h_ref, wt_hi_ref, wt_lo_ref, bias_ref, ustrict_ref,
                  eidx_ref, wts_ref, pos_ref, cnt_ref, carry_s, *, tm):
    i = pl.program_id(0)

    @pl.when(i == 0)
    def _():
        carry_s[...] = jnp.zeros_like(carry_s)

    h_hi, h_lo = _split_bf16(h_ref[...])
    logits = _dot_nt(wt_hi_ref[...], h_hi) + _dot_nt(wt_hi_ref[...], h_lo) + _dot_nt(wt_lo_ref[...], h_hi)
    scores = _sigmoid(logits)
    sel = scores + bias_ref[...]
    neg = jnp.float32(-jnp.inf)
    gsz = N_EXPERTS // N_GROUPS
    gs = []
    for g in range(N_GROUPS):
        blk = sel[g * gsz:(g + 1) * gsz]
        m1 = jnp.max(blk, axis=0, keepdims=True)
        eq = blk == m1
        cnt = jnp.sum(eq.astype(F32), axis=0, keepdims=True)
        m2 = jnp.max(jnp.where(eq, neg, blk), axis=0, keepdims=True)
        gs.append(m1 + jnp.where(cnt >= 2.0, m1, m2))
    masked = []
    for g in range(N_GROUPS):
        rank = jnp.zeros_like(gs[g])
        for o in range(N_GROUPS):
            if o == g:
                continue
            beats = (gs[o] > gs[g]) | ((gs[o] == gs[g]) & (o < g))
            rank = rank + beats.astype(F32)
        keep = rank < float(TOPK_GROUPS)
        masked.append(jnp.where(keep, sel[g * gsz:(g + 1) * gsz], neg))
    cur = jnp.concatenate(masked, axis=0)
    iota_e = lax.broadcasted_iota(jnp.int32, cur.shape, 0).astype(F32)
    idxs, ws = [], []
    selmask = jnp.zeros(cur.shape, F32)
    for _ in range(TOP_K):
        m = jnp.max(cur, axis=0, keepdims=True)
        idx = jnp.min(jnp.where(cur == m, iota_e, float(N_EXPERTS)), axis=0, keepdims=True)
        onehot = iota_e == idx
        ws.append(jnp.sum(jnp.where(onehot, scores, 0.0), axis=0, keepdims=True))
        idxs.append(idx)
        cur = jnp.where(onehot, neg, cur)
        selmask = jnp.where(onehot, 1.0, selmask)
    pos = _dot(selmask.astype(BF16), ustrict_ref[...]) + carry_s[...]
    carry_s[...] = carry_s[...] + jnp.sum(selmask, axis=1, keepdims=True)
    cnt_ref[...] = carry_s[...]
    wsum = ws[0]
    for k in range(1, TOP_K):
        wsum = wsum + ws[k]
    pks = [jnp.sum(jnp.where(iota_e == idxs[k], pos, 0.0), axis=0, keepdims=True) for k in range(TOP_K)]
    eidx_ref[...] = jnp.concatenate(idxs, axis=0).astype(jnp.int32)
    wts_ref[...] = jnp.concatenate([w / wsum * ROUTED_SCALE for w in ws], axis=0)
    pos_ref[...] = jnp.concatenate(pks, axis=0).astype(jnp.int32)


def _route(h2, wt_hi, wt_lo, bias_col, tm):
    t_all = h2.shape[0]
    ustrict = jnp.asarray(np.triu(np.ones((tm, tm), np.float32), 1), dtype=BF16)
    full = lambda a: pl.BlockSpec(a.shape, lambda i, _n=a.ndim: (0,) * _n)
    tok = pl.BlockSpec((TOP_K, tm), lambda i: (0, i))
    return pl.pallas_call(
        functools.partial(_route_kernel, tm=tm),
        grid=(t_all // tm,),
        in_specs=[pl.BlockSpec((tm, D_MODEL), lambda i: (i, 0)), full(wt_hi), full(wt_lo), full(bias_col),
                  full(ustrict)],
        out_specs=[tok, tok, tok, pl.BlockSpec((N_EXPERTS, 1), lambda i: (0, 0))],
        out_shape=[jax.ShapeDtypeStruct((TOP_K, t_all), jnp.int32), jax.ShapeDtypeStruct((TOP_K, t_all), F32),
                   jax.ShapeDtypeStruct((TOP_K, t_all), jnp.int32), jax.ShapeDtypeStruct((N_EXPERTS, 1), F32)],
        scratch_shapes=[pltpu.VMEM((N_EXPERTS, 1), F32)],
        compiler_params=_cparams(("arbitrary",)),
        name="route",
    )(h2, wt_hi, wt_lo, bias_col, ustrict)


def _slots_kernel(eidx_ref, pos_ref, pstart_ref, slotx_ref, sloty_ref):
    tm = eidx_ref.shape[1]
    iota_e = lax.broadcasted_iota(jnp.int32, (N_EXPERTS, tm), 0)
    for c, out in enumerate((slotx_ref, sloty_ref)):
        rows = []
        for k in range(TOP_K):
            onehot = iota_e == eidx_ref[k:k + 1, :]
            rows.append(jnp.sum(jnp.where(onehot, pstart_ref[:, c:c + 1], 0.0), axis=0, keepdims=True))
        out[...] = jnp.concatenate(rows, axis=0).astype(jnp.int32) + pos_ref[...]


def _slots(eidx, pos, pstart_cols, tm):
    t_all = eidx.shape[1]
    tok = pl.BlockSpec((TOP_K, tm), lambda i: (0, i))
    return pl.pallas_call(
        _slots_kernel,
        grid=(t_all // tm,),
        in_specs=[tok, tok, pl.BlockSpec((N_EXPERTS, 2), lambda i: (0, 0))],
        out_specs=[tok, tok],
        out_shape=[jax.ShapeDtypeStruct((TOP_K, t_all), jnp.int32), jax.ShapeDtypeStruct((TOP_K, t_all), jnp.int32)],
        compiler_params=_cparams(("arbitrary",)),
        name="slots",
    )(eidx, pos, pstart_cols)


TD_TOK = 512
XS_ALIGN = 8


def _dispatch_kernel(fstart_ref, flen_ref, total_ref, slot_ref, h_ref, xs_out, zero_s, sem, zsem, *, xs_rows):
    i = pl.program_id(0)

    def body(j, carry):
        for k in range(TOP_K):
            pltpu.make_async_copy(h_ref.at[pl.ds(j, 1), :], xs_out.at[pl.ds(slot_ref[0, 0, j * TOP_K + k], 1), :],
                                  sem).start(priority=k % 2)
        return carry

    lax.fori_loop(0, TD_TOK, body, 0)
    for k in range(TOP_K):
        pltpu.make_async_copy(h_ref, xs_out.at[pl.ds(0, TD_TOK), :], sem).wait()

    @pl.when(i == pl.num_programs(0) - 1)
    def _():
        zero_s[...] = jnp.zeros_like(zero_s)
        total = total_ref[0]
        n_tail = (xs_rows - total) // XS_ALIGN

        def gap_copy(e, r):
            return pltpu.make_async_copy(zero_s.at[pl.ds(0, 1), :], xs_out.at[pl.ds(fstart_ref[e] + r, 1), :], zsem)

        def tail_copy(q):
            start = pl.multiple_of(total + q * XS_ALIGN, XS_ALIGN)
            return pltpu.make_async_copy(zero_s, xs_out.at[pl.ds(start, XS_ALIGN), :], zsem)

        def for_gaps(fn):
            def per_expert(e, carry):
                for r in range(XS_ALIGN - 1):
                    @pl.when(r < flen_ref[e])
                    def _():
                        fn(gap_copy(e, r))
                return carry
            lax.fori_loop(0, N_EXPERTS, per_expert, 0)

        def for_tail(fn):
            def per_q(q, carry):
                fn(tail_copy(q))
                return carry
            lax.fori_loop(0, n_tail, per_q, 0)

        for_gaps(lambda cp: cp.start())
        for_tail(lambda cp: cp.start())
        for_gaps(lambda cp: cp.wait())
        for_tail(lambda cp: cp.wait())


def _dispatch(slot3, h2, fstart, flen, total, xs_rows):
    t_all = h2.shape[0]
    grid_spec = pltpu.PrefetchScalarGridSpec(
        num_scalar_prefetch=3,
        grid=(t_all // TD_TOK,),
        in_specs=[pl.BlockSpec((1, 1, TD_TOK * TOP_K), lambda i, a, b, c: (i, 0, 0), memory_space=pltpu.SMEM),
                  pl.BlockSpec((TD_TOK, D_MODEL), lambda i, a, b, c: (i, 0))],
        out_specs=pl.BlockSpec(memory_space=pl.ANY),
        scratch_shapes=[pltpu.VMEM((XS_ALIGN, D_MODEL), F32), pltpu.SemaphoreType.DMA(()),
                        pltpu.SemaphoreType.DMA(())],
    )
    return pl.pallas_call(
        functools.partial(_dispatch_kernel, xs_rows=xs_rows),
        grid_spec=grid_spec,
        out_shape=jax.ShapeDtypeStruct((xs_rows, D_MODEL), F32),
        compiler_params=_cparams(("arbitrary",)),
        name="dispatch",
    )(fstart, flen, total, slot3, h2)


X_RING = 4
BLOCK_DMA_PRIORITY = 1
W_BUFFERS = 3


def _expert_kernel(bstart_ref, nblk_ref, nused_ref, xrow_ref, xs_hbm, wgu_hbm, wdn_hbm, ys_hbm,
                   xbuf, ybuf, wgu_f32, wdn_f32, wgu16, wdn16, sem_in, sem_out, sem_w, *, n_blocks):
    e = pl.program_id(0)
    nused = nused_ref[0]
    n = nblk_ref[e]
    g0 = bstart_ref[e]

    def w_copies(ex):
        s = ex % W_BUFFERS
        return (pltpu.make_async_copy(wgu_hbm.at[ex], wgu_f32.at[s], sem_w.at[0, s]),
                pltpu.make_async_copy(wdn_hbm.at[ex], wdn_f32.at[s], sem_w.at[1, s]))

    @pl.when(e == 0)
    def _():
        for q in range(W_BUFFERS - 1):
            for cp in w_copies(q):
                cp.start()

    @pl.when(e + (W_BUFFERS - 1) < N_EXPERTS)
    def _():
        for cp in w_copies(e + (W_BUFFERS - 1)):
            cp.start()

    def in_copy(g, s):
        start = pl.multiple_of(xrow_ref[g], XS_ALIGN)
        return pltpu.make_async_copy(xs_hbm.at[pl.ds(start, MOE_BLOCK), :], xbuf.at[s], sem_in.at[s])

    def out_copy(g, s):
        return pltpu.make_async_copy(ybuf.at[s], ys_hbm.at[pl.ds(g * MOE_BLOCK, MOE_BLOCK), :], sem_out.at[s])

    @pl.when(e == 0)
    def _():
        for q in range(X_RING - 1):
            @pl.when(q < nused)
            def _():
                in_copy(q, q).start(priority=BLOCK_DMA_PRIORITY)

    for cp in w_copies(e):
        cp.wait()

    @pl.when(n > 0)
    def _():
        wgu16[...] = wgu_f32[e % W_BUFFERS].astype(BF16)
        wdn16[...] = wdn_f32[e % W_BUFFERS].astype(BF16)

    def block(b, carry):
        g = g0 + b
        s = g % 2
        ahead = g + (X_RING - 1)

        @pl.when(ahead < nused)
        def _():
            in_copy(ahead, ahead % X_RING).start(priority=BLOCK_DMA_PRIORITY)

        in_copy(g, g % X_RING).wait()
        gu = _dot(xbuf[g % X_RING].astype(BF16), wgu16[...])
        gate = gu[:, :EXPERT_DIM]
        act = (gate * _sigmoid(gate) * gu[:, EXPERT_DIM:]).astype(BF16)
        y = _dot(act, wdn16[...])

        @pl.when(g >= 2)
        def _():
            out_copy(g - 2, s).wait()

        ybuf[s] = y
        out_copy(g, s).start(priority=BLOCK_DMA_PRIORITY)
        return carry

    lax.fori_loop(0, n, block, 0)

    @pl.when(e == pl.num_programs(0) - 1)
    def _():
        @pl.when(nused >= 2)
        def _():
            out_copy(nused - 2, nused % 2).wait()

        @pl.when(nused >= 1)
        def _():
            out_copy(nused - 1, (nused - 1) % 2).wait()

        ybuf[0] = jnp.zeros((MOE_BLOCK, D_MODEL), F32)

        def tail(g, carry):
            out_copy(g, 0).start()
            out_copy(g, 0).wait()
            return carry

        lax.fori_loop(nused, n_blocks, tail, 0)


def _experts(xs, bstart, nblk, nused, xrow, w_gu, w_down):
    n_blocks = xrow.shape[0]
    grid_spec = pltpu.PrefetchScalarGridSpec(
        num_scalar_prefetch=4,
        grid=(N_EXPERTS,),
        in_specs=[pl.BlockSpec(memory_space=pl.ANY), pl.BlockSpec(memory_space=pl.ANY),
                  pl.BlockSpec(memory_space=pl.ANY)],
        out_specs=pl.BlockSpec(memory_space=pl.ANY),
        scratch_shapes=[pltpu.VMEM((X_RING, MOE_BLOCK, D_MODEL), F32),
                        pltpu.VMEM((2, MOE_BLOCK, D_MODEL), F32),
                        pltpu.VMEM((W_BUFFERS, D_MODEL, 2 * EXPERT_DIM), F32),
                        pltpu.VMEM((W_BUFFERS, EXPERT_DIM, D_MODEL), F32),
                        pltpu.VMEM((D_MODEL, 2 * EXPERT_DIM), BF16),
                        pltpu.VMEM((EXPERT_DIM, D_MODEL), BF16),
                        pltpu.SemaphoreType.DMA((X_RING,)), pltpu.SemaphoreType.DMA((2,)),
                        pltpu.SemaphoreType.DMA((2, W_BUFFERS))],
    )
    return pl.pallas_call(
        functools.partial(_expert_kernel, n_blocks=n_blocks),
        grid_spec=grid_spec,
        out_shape=jax.ShapeDtypeStruct((n_blocks * MOE_BLOCK, D_MODEL), F32),
        compiler_params=_cparams(("arbitrary",)),
        name="experts",
    )(bstart, nblk, nused, xrow, xs, w_gu, w_down)


TC_TOK = 128


def _combine_kernel(slot_ref, slot_next_ref, wt_ref, x1_ref, h2_ref, g2_ref, sgu_ref, sdn_ref, lg_ref, lb_ref,
                    y_hbm, oc_ref, ol_ref, gbuf, sems, *, n_tiles, ctx_tiles, alpha):
    i = pl.program_id(0)

    def row_copy(sref, b, j, k):
        return pltpu.make_async_copy(y_hbm.at[pl.ds(sref[0, 0, j * TOP_K + k], 1), :],
                                     gbuf.at[b, k, pl.ds(j, 1), :], sems.at[b])

    def wait_tile(b):
        for k in range(TOP_K):
            pltpu.make_async_copy(y_hbm.at[pl.ds(0, TC_TOK), :], gbuf.at[b, k], sems.at[b]).wait()

    @pl.when(i == 0)
    def _():
        def body(j, carry):
            for k in range(TOP_K):
                row_copy(slot_ref, 0, j, k).start(priority=k % 2)
            return carry
        lax.fori_loop(0, TC_TOK, body, 0)

    b = i % 2
    nb = 1 - b

    def issue_next(j0, j1):
        for j in range(j0, j1):
            for k in range(TOP_K):
                row_copy(slot_next_ref, nb, j, k).start(priority=k % 2)

    issue_next(0, TC_TOK // 2)
    h = h2_ref[...].astype(BF16)
    su = _dot(h, sgu_ref[...])
    sg = su[:, :EXPERT_DIM]
    act = (sg * _sigmoid(sg) * su[:, EXPERT_DIM:]).astype(BF16)
    ffn = _dot(act, sdn_ref[...])
    wait_tile(b)
    issue_next(TC_TOK // 2, TC_TOK)
    for k in range(TOP_K):
        ffn = ffn + gbuf[b, k] * wt_ref[:, k:k + 1]
    out = _layer_norm(alpha * x1_ref[...] + g2_ref[0] * ffn, lg_ref[...], lb_ref[...])

    @pl.when(i < ctx_tiles)
    def _():
        oc_ref[...] = out

    @pl.when(i >= ctx_tiles)
    def _():
        ol_ref[...] = out

    @pl.when(i == n_tiles - 1)
    def _():
        wait_tile(nb)


def _combine(slot3, wts_t, x1, h2, g2, sh_gu, sh_dn, ln_g, ln_b, y_sorted, row_of_tile, alpha, tok_ctx):
    t_all = x1.shape[0]
    n_tiles = t_all // TC_TOK
    ctx_tiles = tok_ctx // TC_TOK
    tile = lambda w: pl.BlockSpec((TC_TOK, w), lambda i: (i, 0))
    full = lambda a: pl.BlockSpec(a.shape, lambda i, _n=a.ndim: (0,) * _n)
    return pl.pallas_call(
        functools.partial(_combine_kernel, n_tiles=n_tiles, ctx_tiles=ctx_tiles, alpha=alpha),
        grid=(n_tiles,),
        in_specs=[pl.BlockSpec((1, 1, TC_TOK * TOP_K), lambda i: (i, 0, 0), memory_space=pltpu.SMEM),
                  pl.BlockSpec((1, 1, TC_TOK * TOP_K), lambda i: (jnp.minimum(i + 1, n_tiles - 1), 0, 0),
                               memory_space=pltpu.SMEM),
                  tile(TOP_K), tile(D_MODEL), tile(D_MODEL),
                  pl.BlockSpec((1, 1, D_MODEL), lambda i: (row_of_tile(i), 0, 0)),
                  full(sh_gu), full(sh_dn), full(ln_g), full(ln_b),
                  pl.BlockSpec(memory_space=pl.ANY)],
        out_specs=_two_source_specs(TC_TOK, D_MODEL, ctx_tiles),
        out_shape=[jax.ShapeDtypeStruct((tok_ctx, D_MODEL), F32),
                   jax.ShapeDtypeStruct((t_all - tok_ctx, D_MODEL), F32)],
        scratch_shapes=[pltpu.VMEM((2, TOP_K, TC_TOK, D_MODEL), F32), pltpu.SemaphoreType.DMA((2,))],
        compiler_params=_cparams(("arbitrary",)),
        name="combine",
    )(slot3, slot3, wts_t, x1, h2, g2, sh_gu, sh_dn, ln_g, ln_b, y_sorted)


def kernel(x_prompt, x_sample, state_rwkv, c, c_ctx, w_ada, b_ada, w_in, mu_shift, w_decay0, w_decay2, w_a0, w_a2, w_g2, k_k, k_a, r_k, lnx_g, lnx_b, conv_w, conv_b, conv_ln_g, conv_ln_b, w_out, ln1_g, ln1_b, router_w, router_bias, expert_w_gu, expert_w_down, shared_w_gu, shared_w_down, ln2_g, ln2_b):
    depth = w_ada.shape[0]
    assert depth == 1
    alpha = (2.0 * depth) ** 0.25
    n_ctx, t_ctx, _ = x_prompt.shape
    n_lat, t_lat, _ = x_sample.shape
    tok_ctx = n_ctx * t_ctx
    tok_lat = n_lat * t_lat
    t_all = tok_ctx + tok_lat
    tm = 1024
    assert tok_ctx % tm == 0 and t_lat % tm == 0 and t_lat // GRID_W * GRID_W == t_lat
    l = 0

    x_ctx = x_prompt.reshape(tok_ctx, D_MODEL)
    x_lat = x_sample.reshape(tok_lat, D_MODEL)

    cond8 = jnp.zeros((8, D_MODEL), F32).at[0].set(c_ctx).at[1:1 + n_lat].set(c)
    mod = _ada(cond8, w_ada[l], b_ada[l])
    sh1, sc1, g1, sh2, sc2, g2 = [m.reshape(8, 1, D_MODEL) for m in jnp.split(mod, 6, axis=-1)]

    def row_of_tile_for(tile_rows):
        ctx_tiles = tok_ctx // tile_rows
        per_seq = t_lat // tile_rows
        return lambda i: jnp.where(i < ctx_tiles, 0, 1 + (i - ctx_tiles) // per_seq)

    w_in_a = w_in[l][:, :RWKV_IN].astype(BF16)
    w_in_b = w_in[l][:, RWKV_IN:].astype(BF16)
    proj_a, proj_b = _in_proj(x_ctx, x_lat, sc1, sh1, w_in_a, w_in_b, tm, row_of_tile_for(tm))

    zeros64 = jnp.zeros((64, 2 * RW), F32)
    wd2 = jnp.concatenate([w_decay2[l, 0], w_decay2[l, 1]], axis=1)
    wa2 = jnp.concatenate([w_a2[l, 0], w_a2[l, 1]], axis=1)
    w_lora = jnp.concatenate([jnp.concatenate([wd2, zeros64], axis=1),
                              jnp.concatenate([zeros64, wa2], axis=1)], axis=0).astype(BF16)
    row = lambda v: v.reshape(1, -1)
    rwkv_w = [row(mu_shift[l]), w_lora, w_g2[l].astype(BF16), row(w_decay0[l]), row(w_a0[l]),
              row(k_k[l]), row(k_a[l]), row(r_k[l]), row(lnx_g[l]), row(lnx_b[l])]
    consts = _rwkv_consts()

    def to_cat(s):
        b = s.shape[0]
        s = s.reshape(b, 2, 2, 4, HD, HD)
        return jnp.transpose(s, (0, 1, 2, 5, 3, 4)).reshape(b, 4, HD, GW)

    s0_ctx = jnp.zeros((n_ctx, 4, HD, GW), F32)
    s0_lat = to_cat(state_rwkv[:, l].astype(F32))
    ya_c, sfin_ctx = _rwkv(proj_a, s0_ctx, rwkv_w, consts, t_ctx, n_ctx, 0)
    ya_l, _ = _rwkv(proj_a, s0_lat, rwkv_w, consts, t_lat, n_lat, tok_ctx // t_lat)
    sfin = jnp.transpose(sfin_ctx.reshape(n_ctx, 2, 2, HD, 4, HD), (0, 1, 2, 4, 5, 3))
    new_state = sfin.reshape(n_ctx, 1, 2, 2 * 4, HD, HD).astype(x_prompt.dtype)

    conv_wts = [conv_w[l], row(conv_b[l]), row(conv_ln_g[l]), row(conv_ln_b[l])]
    yb_c = _conv(proj_b, conv_wts, t_ctx, n_ctx, 0, False)
    yb_l = _conv(proj_b, conv_wts, t_lat, n_lat, tok_ctx // t_lat, True)

    w_oa = w_out[l][:RW].astype(BF16)
    w_ob = w_out[l][RW:].astype(BF16)
    x1, h2 = _out_proj(ya_c, ya_l, yb_c, yb_l, x_ctx, x_lat, g1, sc2, sh2, w_oa, w_ob, row(ln1_g[l]), row(ln1_b[l]), tm,
                       row_of_tile_for(tm), alpha)

    rwt = router_w[l].T
    rwt_hi = rwt.astype(BF16)
    rwt_lo = (rwt - rwt_hi.astype(F32)).astype(BF16)
    eidx, wts, pos, counts = _route(h2, rwt_hi, rwt_lo, router_bias[l].reshape(N_EXPERTS, 1), 512)

    n_assign = t_all * TOP_K
    n_blocks = (n_assign + N_EXPERTS * (MOE_BLOCK - 1) + MOE_BLOCK - 1) // MOE_BLOCK
    cnt = counts.reshape(N_EXPERTS).astype(jnp.int32)
    padded = (cnt + MOE_BLOCK - 1) // MOE_BLOCK * MOE_BLOCK
    pend = jnp.cumsum(padded)
    pstart = pend - padded
    padded_x = (cnt + XS_ALIGN - 1) // XS_ALIGN * XS_ALIGN
    pend_x = jnp.cumsum(padded_x)
    pstart_x = pend_x - padded_x
    xs_rows = n_assign + N_EXPERTS * (XS_ALIGN - 1) // XS_ALIGN * XS_ALIGN + MOE_BLOCK
    slot_x, slot_y = _slots(eidx, pos, jnp.stack([pstart_x, pstart], axis=1).astype(F32), 512)
    per_tile = lambda s, tok: s.T.reshape(t_all // tok, 1, tok * TOP_K)
    xs = _dispatch(per_tile(slot_x, TD_TOK), h2, (pstart_x + cnt).astype(jnp.int32), (padded_x - cnt).astype(jnp.int32),
                   pend_x[-1].astype(jnp.int32).reshape(1), xs_rows)
    bstart = (pstart // MOE_BLOCK).astype(jnp.int32)
    nblk = (padded // MOE_BLOCK).astype(jnp.int32)
    nused = (pend[-1] // MOE_BLOCK).astype(jnp.int32).reshape(1)
    blk_ids = jnp.arange(n_blocks, dtype=jnp.int32)
    blk_e = jnp.minimum(jnp.sum((pend[None, :] <= (blk_ids * MOE_BLOCK)[:, None]).astype(jnp.int32), axis=1),
                        N_EXPERTS - 1)
    shift = jnp.sum(jnp.where(blk_e[:, None] == jnp.arange(N_EXPERTS, dtype=jnp.int32)[None, :],
                              (pstart_x - pstart)[None, :], 0), axis=1)
    xrow = jnp.clip(blk_ids * MOE_BLOCK + shift, 0, xs_rows - MOE_BLOCK).astype(jnp.int32)
    y_sorted = _experts(xs, bstart, nblk, nused, xrow, expert_w_gu[l], expert_w_down[l])
    y_ctx, y_lat = _combine(per_tile(slot_y, TC_TOK), wts.T, x1, h2, g2, shared_w_gu[l].astype(BF16),
                            shared_w_down[l].astype(BF16), row(ln2_g[l]), row(ln2_b[l]), y_sorted,
                            row_of_tile_for(TC_TOK), alpha, tok_ctx)
    return (y_ctx.reshape(n_ctx, t_ctx, D_MODEL), y_lat.reshape(n_lat, t_lat, D_MODEL), new_state)
```

```python
import functools
import math

import numpy as np
import jax
import jax.numpy as jnp
from jax import lax
from jax.experimental import pallas as pl
from jax.experimental.pallas import tpu as pltpu

F32 = jnp.float32
BF16 = jnp.bfloat16

D_MODEL = 1024
RW = 512
HD = 64
CW = 512
CONV_K = 31
RWKV_IN = 3 * RW + 64 + 64 + 128
N_EXPERTS = 256
TOP_K = 8
N_GROUPS = 8
TOPK_GROUPS = 4
EXPERT_DIM = 256
ROUTED_SCALE = 2.5
MOE_BLOCK = 128
GRID_W = 64
LN_EPS = 1e-5
GN_EPS = 64e-5
CHUNK = 64
GW = 256
SCAN_UNROLL = 2
DECAY_SCALE = math.exp(-0.5)
VMEM_LIMIT = 56 * 1024 * 1024


def _cparams(sem):
    return pltpu.CompilerParams(dimension_semantics=sem, vmem_limit_bytes=VMEM_LIMIT)


def _split_bf16(x):
    hi = x.astype(BF16)
    lo = (x - hi.astype(F32)).astype(BF16)
    return hi, lo


def _dot(a, b):
    return jnp.dot(a, b, preferred_element_type=F32)


def _dot_nt(a, b):
    return lax.dot_general(a, b, (((1,), (1,)), ((), ())), preferred_element_type=F32)


def _dot_tn(a, b):
    return lax.dot_general(a, b, (((0,), (0,)), ((), ())), preferred_element_type=F32)


def _sigmoid(x):
    return 1.0 / (1.0 + jnp.exp(-x))


def _layer_norm(x, g, b):
    mu = jnp.mean(x, axis=-1, keepdims=True)
    d = x - mu
    var = jnp.mean(d * d, axis=-1, keepdims=True)
    return d * lax.rsqrt(var + LN_EPS) * g + b


def _ada_kernel(c_ref, w_ref, b_ref, o_ref):
    c = c_ref[...]
    s = c * _sigmoid(c)
    s_hi, s_lo = _split_bf16(s)
    w_hi, w_lo = _split_bf16(w_ref[...])
    o_ref[...] = _dot(s_hi, w_hi) + _dot(s_lo, w_hi) + _dot(s_hi, w_lo) + b_ref[...]


def _ada(cond8, w_ada, b_ada):
    n = w_ada.shape[1]
    tn = 1536
    return pl.pallas_call(
        _ada_kernel,
        grid=(n // tn,),
        in_specs=[pl.BlockSpec((8, D_MODEL), lambda j: (0, 0)),
                  pl.BlockSpec((D_MODEL, tn), lambda j: (0, j)),
                  pl.BlockSpec((1, tn), lambda j: (0, j))],
        out_specs=pl.BlockSpec((8, tn), lambda j: (0, j)),
        out_shape=jax.ShapeDtypeStruct((8, n), F32),
        compiler_params=_cparams(("arbitrary",)),
        name="ada",
    )(cond8, w_ada, b_ada.reshape(1, n))


def _two_source_specs(tm, width, ctx_tiles):
    return [pl.BlockSpec((tm, width), lambda i: (jnp.minimum(i, ctx_tiles - 1), 0)),
            pl.BlockSpec((tm, width), lambda i: (jnp.maximum(i - ctx_tiles, 0), 0))]


def _inproj_kernel(xc_ref, xl_ref, sc_ref, sh_ref, wa_ref, wb_ref, oa_ref, ob_ref, *, ctx_tiles):
    x = jnp.where(pl.program_id(0) < ctx_tiles, xc_ref[...], xl_ref[...])
    h = (x * (1.0 + sc_ref[0]) + sh_ref[0]).astype(BF16)
    oa_ref[...] = _dot(h, wa_ref[...])
    ob_ref[...] = _dot(h, wb_ref[...])


def _in_proj(x_ctx, x_lat, sc1, sh1, w_a, w_b, tm, row_of_tile):
    t_all = x_ctx.shape[0] + x_lat.shape[0]
    ctx_tiles = x_ctx.shape[0] // tm
    return pl.pallas_call(
        functools.partial(_inproj_kernel, ctx_tiles=ctx_tiles),
        grid=(t_all // tm,),
        in_specs=_two_source_specs(tm, D_MODEL, ctx_tiles) + [
                  pl.BlockSpec((1, 1, D_MODEL), lambda i: (row_of_tile(i), 0, 0)),
                  pl.BlockSpec((1, 1, D_MODEL), lambda i: (row_of_tile(i), 0, 0)),
                  pl.BlockSpec(w_a.shape, lambda i: (0, 0)),
                  pl.BlockSpec(w_b.shape, lambda i: (0, 0))],
        out_specs=[pl.BlockSpec((tm, RWKV_IN), lambda i: (i, 0)),
                   pl.BlockSpec((tm, 2 * CW), lambda i: (i, 0))],
        out_shape=[jax.ShapeDtypeStruct((t_all, RWKV_IN), F32),
                   jax.ShapeDtypeStruct((t_all, 2 * CW), F32)],
        compiler_params=_cparams(("arbitrary",)),
        name="in_proj",
    )(x_ctx, x_lat, sc1, sh1, w_a, w_b)


def _rwkv_consts():
    c = CHUNK
    t = np.arange(c)[:, None]
    s = np.arange(GW)[None, :] % c
    sl = (s < t).astype(np.float32)
    il = (s <= t).astype(np.float32)
    su = (s > t).astype(np.float32)
    iu = (s >= t).astype(np.float32)
    eye_cat = (s == t).astype(np.float32)
    tri = np.stack([sl, il, su, iu, eye_cat])
    r = np.arange(GW)
    mask_bd = (r[:, None] // HD == r[None, :] // HD).astype(np.float32)
    eye = np.eye(GW, dtype=np.float32)
    cum = np.stack([np.tril(np.ones((c, c), np.float32)), np.triu(np.ones((c, c), np.float32))])
    q = np.arange(RW)
    seg = (q[:, None] // HD == q[None, :] // HD).astype(np.float32)
    return (jnp.asarray(tri), jnp.asarray(mask_bd, dtype=BF16), jnp.asarray(mask_bd), jnp.asarray(eye),
            jnp.asarray(cum, dtype=BF16), jnp.asarray(seg, dtype=BF16))


def _rwkv_kernel(p_ref, s0_ref, mu_ref, wlora_ref, wg2_ref, wd0_ref, wa0_ref, kk_ref, ka_ref, rk_ref,
                 lng_ref, lnb_ref, tri_ref, mbd16_ref, mbd_ref, eye_ref, cum_ref, seg_ref,
                 ya_ref, sfin_ref,
                 r_s, v_s, kk_s, g_s, bon_s, a_s, lw_s, kd_s, y_s, st_s, *, seq_len):
    nc = seq_len // CHUNK
    scan_unroll = SCAN_UNROLL
    seg = seg_ref[...]

    def seg_sum(x):
        hi, lo = _split_bf16(x)
        return _dot(hi, seg) + _dot(lo, seg)

    def phase_a(c, carry):
        t0 = pl.multiple_of(c * CHUNK, CHUNK)
        cur = p_ref[pl.ds(t0, CHUNK), :]
        prev_row = p_ref[pl.ds(jnp.maximum(t0 - 1, 0), 1), :] * (t0 > 0).astype(F32)
        next_row = p_ref[pl.ds(jnp.minimum(t0 + CHUNK, seq_len - 1), 1), :] * (t0 + CHUNK < seq_len).astype(F32)
        row = lax.broadcasted_iota(jnp.int32, cur.shape, 0)
        prev = jnp.where(row == 0, prev_row, pltpu.roll(cur, 1, 0))
        nxt = jnp.where(row == CHUNK - 1, next_row, pltpu.roll(cur, CHUNK - 1, 0))
        p = cur + mu_ref[...] * (0.5 * (prev + nxt) - cur)
        r = p[:, 0:RW]
        k = p[:, RW:2 * RW]
        v = p[:, 2 * RW:3 * RW]
        z = p[:, 3 * RW:3 * RW + 128]
        gd = p[:, 3 * RW + 128:3 * RW + 256]
        lane = lax.broadcasted_iota(jnp.int32, z.shape, 1)
        z = jnp.where(lane < 64, jnp.tanh(z), z)
        lora = _dot(z.astype(BF16), wlora_ref[...])
        g = _dot(_sigmoid(gd).astype(BF16), wg2_ref[...])
        kks = k * kk_ref[...]
        kk = kks * lax.rsqrt(seg_sum(kks * kks) + 1e-12)
        rows = pl.ds(t0, CHUNK)
        r_s[rows, :] = r
        v_s[rows, :] = v.astype(BF16)
        kk_s[rows, :] = kk
        g_s[rows, :] = g
        kd_sum = jnp.zeros_like(k)
        for d in range(2):
            w_logit = wd0_ref[:, d * RW:(d + 1) * RW] + lora[:, d * RW:(d + 1) * RW]
            a = _sigmoid(wa0_ref[:, d * RW:(d + 1) * RW] + lora[:, (2 + d) * RW:(3 + d) * RW])
            kd = k * (1.0 + (a - 1.0) * ka_ref[...])
            lw_s[d, rows, :] = -DECAY_SCALE * _sigmoid(w_logit)
            a_s[d, rows, :] = a
            kd_s[d, rows, :] = kd
            kd_sum = kd_sum + kd
        bon_s[rows, :] = seg_sum(r * kd_sum * rk_ref[...]) * v
        return carry

    lax.fori_loop(0, nc, phase_a, 0, unroll=2)

    mbd16 = mbd16_ref[...]
    mbd = mbd_ref[...]
    eye = eye_ref[...]

    def bd(x16):
        return jnp.concatenate([x16, x16, x16, x16], axis=0) * mbd16

    for u in range(4):
        s0 = s0_ref[0, u]
        st_s[u] = jnp.concatenate([s0, s0, s0, s0], axis=0) * mbd

    def prep(d, grp, t0):
        rows = pl.ds(t0, CHUNK)
        lanes = slice(grp * GW, (grp + 1) * GW)
        lw = lw_s[d, rows, lanes]
        hi, lo = _split_bf16(lw)
        cl = _dot(cum_ref[d], hi) + _dot(cum_ref[d], lo)
        e_cl = jnp.exp(cl)
        e_ce = jnp.exp(cl - lw)
        e_ncl = jnp.exp(-cl)
        tot = cl[CHUNK - 1:CHUNK, :] if d == 0 else cl[0:1, :]
        wc = jnp.exp(tot)
        kk = kk_s[rows, lanes]
        rt = r_s[rows, lanes] * e_cl
        bt = a_s[d, rows, lanes] * kk * e_ncl
        kt = kd_s[d, rows, lanes] * e_ncl
        kq16 = (kk * e_ce).astype(BF16)
        return dict(d=d, u=d * 2 + grp, rows=rows, lanes=lanes, wc=wc, rt=rt, kq16=kq16,
                    v16=v_s[rows, lanes], bt16=bt.astype(BF16), kt16=kt.astype(BF16),
                    btw16=(bt * wc).astype(BF16), ktw16=(kt * wc).astype(BF16),
                    g16=jnp.concatenate([kq16, rt.astype(BF16)], axis=0))

    def phase_b(i2, carry):
        us = []
        for j in range(scan_unroll):
            i = i2 * scan_unroll + j
            tf = pl.multiple_of(i * CHUNK, CHUNK)
            tb = pl.multiple_of((nc - 1 - i) * CHUNK, CHUNK)
            us += [prep(0, 0, tf), prep(1, 0, tb), prep(0, 1, tf), prep(1, 1, tb)]
        each = lambda fn: [fn(q) for q in us]
        strict = lambda q: tri_ref[2 * q['d']]
        incl = lambda q: tri_ref[2 * q['d'] + 1]
        out_b = each(lambda q: _dot_nt(q['g16'], bd(q['bt16'])))
        out_k = each(lambda q: _dot_nt(q['g16'], bd(q['kt16'])))
        a_bk = [o[:CHUNK] * strict(q) for o, q in zip(out_b, us)]
        p_rb = [(o[CHUNK:] * incl(q)).astype(BF16) for o, q in zip(out_b, us)]
        a_kk = [(o[:CHUNK] * strict(q)).astype(BF16) for o, q in zip(out_k, us)]
        p_rk = [(o[CHUNK:] * incl(q)).astype(BF16) for o, q in zip(out_k, us)]
        tinv = [tri_ref[4] - a for a in a_bk]
        pw = [a.astype(BF16) for a in a_bk]
        for _ in range(5):
            pw = [_dot(p, bd(p)).astype(BF16) for p in pw]
            tinv = [t + _dot(p, bd(t.astype(BF16))) for t, p in zip(tinv, pw)]
        bdv = each(lambda q: bd(q['v16']))
        akkv = [_dot(a, b) for a, b in zip(a_kk, bdv)]
        x = [_dot(t.astype(BF16), jnp.concatenate([bd(q['kq16']), bd(k.astype(BF16))], axis=1))
             for t, q, k in zip(tinv, us, akkv)]
        qu16 = [z.astype(BF16) for z in x]
        tn1 = [_dot_tn(q['btw16'], z) for q, z in zip(us, qu16)]
        tn2 = each(lambda q: _dot_tn(q['ktw16'], q['v16']))
        r1 = [_dot(p, jnp.concatenate([bd(z[:, :GW]), bd(z[:, GW:])], axis=1)) for p, z in zip(p_rb, qu16)]
        y0 = [_dot(p, b) - r[:, GW:] for p, b, r in zip(p_rk, bdv, r1)]
        for q, t1, t2, r, y in zip(us, tn1, tn2, r1, y0):
            mc = (eye * q['wc'] - t1[:, :GW]) * mbd
            ncm = (t2 - t1[:, GW:]) * mbd
            rh = q['rt'] - r[:, :GW]
            st16 = st_s[q['u']].astype(BF16)
            big = _dot(jnp.concatenate([rh.astype(BF16), mc.astype(BF16)], axis=0), st16)
            y_s[q['d'], q['rows'], q['lanes']] = big[:CHUNK] + y
            st_s[q['u']] = big[CHUNK:] + ncm
        return carry

    lax.fori_loop(0, nc // scan_unroll, phase_b, 0)

    for u in range(4):
        st = st_s[u]
        sfin_ref[0, u] = st[0:HD] + st[HD:2 * HD] + st[2 * HD:3 * HD] + st[3 * HD:4 * HD]

    def phase_c(c, carry):
        rows = pl.ds(pl.multiple_of(c * CHUNK, CHUNK), CHUNK)
        y = y_s[0, rows, :] + y_s[1, rows, :]
        mean = seg_sum(y) * (1.0 / HD)
        dlt = y - mean
        var = seg_sum(dlt * dlt) * (1.0 / HD)
        yn = dlt * lax.rsqrt(var + GN_EPS)
        ya_ref[rows, :] = (yn * lng_ref[...] + lnb_ref[...] + bon_s[rows, :]) * g_s[rows, :]
        return carry

    lax.fori_loop(0, nc, phase_c, 0, unroll=2)


def _rwkv(proj_a, s0_cat, wts, consts, seq_len, n_seq, first_block):
    full = lambda a: pl.BlockSpec(a.shape, lambda s, _n=a.ndim: (0,) * _n)
    p_mode = dict(pipeline_mode=pl.Buffered(1)) if n_seq <= 2 else {}
    in_specs = [pl.BlockSpec((seq_len, RWKV_IN), lambda s: (first_block + s, 0), **p_mode),
                pl.BlockSpec((1, 4, HD, GW), lambda s: (s, 0, 0, 0))]
    in_specs += [full(a) for a in wts] + [full(a) for a in consts]
    args = [proj_a, s0_cat] + list(wts) + list(consts)
    seq = (seq_len, RW)
    return pl.pallas_call(
        functools.partial(_rwkv_kernel, seq_len=seq_len),
        grid=(n_seq,),
        in_specs=in_specs,
        out_specs=[pl.BlockSpec((seq_len, RW), lambda s: (s, 0)),
                   pl.BlockSpec((1, 4, HD, GW), lambda s: (s, 0, 0, 0))],
        out_shape=[jax.ShapeDtypeStruct((n_seq * seq_len, RW), F32),
                   jax.ShapeDtypeStruct((n_seq, 4, HD, GW), F32)],
        scratch_shapes=[pltpu.VMEM(seq, F32), pltpu.VMEM(seq, BF16), pltpu.VMEM(seq, F32), pltpu.VMEM(seq, F32),
                        pltpu.VMEM(seq, F32),
                        pltpu.VMEM((2,) + seq, F32), pltpu.VMEM((2,) + seq, F32), pltpu.VMEM((2,) + seq, F32),
                        pltpu.VMEM((2,) + seq, F32), pltpu.VMEM((4, GW, GW), F32)],
        compiler_params=_cparams(("arbitrary",)),
        name="rwkv_%d" % seq_len,
    )(*args)


CONV_PAD = 16
VPAD = (CONV_K // 2) * GRID_W


def _conv_kernel(u_ref, w_ref, b_ref, g_ref, beta_ref, o_ref, pad_s, vpad_s, y_s, *, seq_len, grid):
    rb = 256
    glu = u_ref[:, :CW] * _sigmoid(u_ref[:, CW:])
    zeros = jnp.zeros((CONV_PAD, CW), F32)
    pad_s[0:CONV_PAD, :] = zeros
    pad_s[CONV_PAD + seq_len:2 * CONV_PAD + seq_len, :] = zeros
    pad_s[CONV_PAD:CONV_PAD + seq_len, :] = glu
    half = CW // 2
    if grid:
        vz = jnp.zeros((VPAD, half), F32)
        vpad_s[0:VPAD, :] = vz
        vpad_s[VPAD + seq_len:2 * VPAD + seq_len, :] = vz
        vpad_s[VPAD:VPAD + seq_len, :] = glu[:, half:]
    n_h_lanes = (half if grid else CW) // 128
    for r0 in range(0, seq_len, rb):
        for lb in range(n_h_lanes):
            lanes = slice(lb * 128, (lb + 1) * 128)
            acc = jnp.zeros((rb, 128), F32)
            if grid:
                col = lax.broadcasted_iota(jnp.int32, (rb, 128), 0) % GRID_W
            span = rb + 2 * CONV_PAD
            xpad = pad_s[r0:r0 + span, lanes]
            for res in range(8):
                xrot = xpad if res == 0 else pltpu.roll(xpad, span - res, 0)
                for k in range(CONV_K):
                    if (k + 1) % 8 != res:
                        continue
                    off = (k + 1) - res
                    term = xrot[off:off + rb] * w_ref[k:k + 1, lanes]
                    if grid:
                        src = col + (k - CONV_K // 2)
                        term = jnp.where((src >= 0) & (src < GRID_W), term, 0.0)
                    acc = acc + term
            y_s[r0:r0 + rb, lanes] = acc
        if grid:
            for lb in range(half // 128):
                lanes = slice(lb * 128, (lb + 1) * 128)
                wl = slice(half + lb * 128, half + (lb + 1) * 128)
                acc = jnp.zeros((rb, 128), F32)
                for k in range(CONV_K):
                    start = VPAD + r0 + (k - CONV_K // 2) * GRID_W
                    acc = acc + vpad_s[start:start + rb, lanes] * w_ref[k:k + 1, wl]
                y_s[r0:r0 + rb, wl] = acc
    for r0 in range(0, seq_len, rb):
        y = y_s[r0:r0 + rb, :] + b_ref[...]
        yl = _layer_norm(y, g_ref[...], beta_ref[...])
        o_ref[r0:r0 + rb, :] = yl * _sigmoid(yl)


def _conv(proj_b, wts, seq_len, n_seq, first_block, grid):
    full = lambda a: pl.BlockSpec(a.shape, lambda s, _n=a.ndim: (0,) * _n)
    in_specs = [pl.BlockSpec((seq_len, 2 * CW), lambda s: (first_block + s, 0))] + [full(a) for a in wts]
    args = [proj_b] + list(wts)
    return pl.pallas_call(
        functools.partial(_conv_kernel, seq_len=seq_len, grid=grid),
        grid=(n_seq,),
        in_specs=in_specs,
        out_specs=pl.BlockSpec((seq_len, CW), lambda s: (s, 0)),
        out_shape=jax.ShapeDtypeStruct((n_seq * seq_len, CW), F32),
        scratch_shapes=[pltpu.VMEM((seq_len + 2 * CONV_PAD, CW), F32),
                        pltpu.VMEM((seq_len + 2 * VPAD if grid else 8, CW // 2), F32),
                        pltpu.VMEM((seq_len, CW), F32)],
        compiler_params=_cparams(("arbitrary",)),
        name="conv_%d" % seq_len,
    )(*args)


def _outproj_kernel(yac_ref, yal_ref, ybc_ref, ybl_ref, xc_ref, xl_ref, g1_ref, sc2_ref, sh2_ref, wa_ref, wb_ref,
                    lg_ref, lb_ref, x1_ref, h2_ref, *, alpha, ctx_tiles):
    is_ctx = pl.program_id(0) < ctx_tiles
    ya = jnp.where(is_ctx, yac_ref[...], yal_ref[...])
    yb = jnp.where(is_ctx, ybc_ref[...], ybl_ref[...])
    x = jnp.where(is_ctx, xc_ref[...], xl_ref[...])
    mix = _dot(ya.astype(BF16), wa_ref[...]) + _dot(yb.astype(BF16), wb_ref[...])
    x1 = _layer_norm(alpha * x + g1_ref[0] * mix, lg_ref[...], lb_ref[...])
    x1_ref[...] = x1
    h2_ref[...] = x1 * (1.0 + sc2_ref[0]) + sh2_ref[0]


def _out_proj(ya_c, ya_l, yb_c, yb_l, x_ctx, x_lat, g1, sc2, sh2, w_oa, w_ob, ln_g, ln_b, tm, row_of_tile, alpha):
    t_all = x_ctx.shape[0] + x_lat.shape[0]
    ctx_tiles = x_ctx.shape[0] // tm
    tile = lambda w: pl.BlockSpec((tm, w), lambda i: (i, 0))
    modrow = pl.BlockSpec((1, 1, D_MODEL), lambda i: (row_of_tile(i), 0, 0))
    full = lambda a: pl.BlockSpec(a.shape, lambda i, _n=a.ndim: (0,) * _n)
    return pl.pallas_call(
        functools.partial(_outproj_kernel, alpha=alpha, ctx_tiles=ctx_tiles),
        grid=(t_all // tm,),
        in_specs=_two_source_specs(tm, RW, ctx_tiles) + _two_source_specs(tm, CW, ctx_tiles)
        + _two_source_specs(tm, D_MODEL, ctx_tiles)
        + [modrow, modrow, modrow, full(w_oa), full(w_ob), full(ln_g), full(ln_b)],
        out_specs=[tile(D_MODEL), tile(D_MODEL)],
        out_shape=[jax.ShapeDtypeStruct((t_all, D_MODEL), F32), jax.ShapeDtypeStruct((t_all, D_MODEL), F32)],
        compiler_params=_cparams(("arbitrary",)),
        name="out_proj",
    )(ya_c, ya_l, yb_c, yb_l, x_ctx, x_lat, g1, sc2, sh2, w_oa, w_ob, ln_g, ln_b)


def _route_kernel(h_ref, wt_hi_ref, wt_lo_ref, bias_ref, ustrict_ref,
                  eidx_ref, wts_ref, pos_ref, cnt_ref, carry_s, *, tm):
    i = pl.program_id(0)

    @pl.when(i == 0)
    def _():
        carry_s[...] = jnp.zeros_like(carry_s)

    h_hi, h_lo = _split_bf16(h_ref[...])
    logits = _dot_nt(wt_hi_ref[...], h_hi) + _dot_nt(wt_hi_ref[...], h_lo) + _dot_nt(wt_lo_ref[...], h_hi)
    scores = _sigmoid(logits)
    sel = scores + bias_ref[...]
    neg = jnp.float32(-jnp.inf)
    gsz = N_EXPERTS // N_GROUPS
    gs = []
    for g in range(N_GROUPS):
        blk = sel[g * gsz:(g + 1) * gsz]
        m1 = jnp.max(blk, axis=0, keepdims=True)
        eq = blk == m1
        cnt = jnp.sum(eq.astype(F32), axis=0, keepdims=True)
        m2 = jnp.max(jnp.where(eq, neg, blk), axis=0, keepdims=True)
        gs.append(m1 + jnp.where(cnt >= 2.0, m1, m2))
    masked = []
    for g in range(N_GROUPS):
        rank = jnp.zeros_like(gs[g])
        for o in range(N_GROUPS):
            if o == g:
                continue
            beats = (gs[o] > gs[g]) | ((gs[o] == gs[g]) & (o < g))
            rank = rank + beats.astype(F32)
        keep = rank < float(TOPK_GROUPS)
        masked.append(jnp.where(keep, sel[g * gsz:(g + 1) * gsz], neg))
    cur = jnp.concatenate(masked, axis=0)
    iota_e = lax.broadcasted_iota(jnp.int32, cur.shape, 0).astype(F32)
    idxs, ws = [], []
    selmask = jnp.zeros(cur.shape, F32)
    for _ in range(TOP_K):
        m = jnp.max(cur, axis=0, keepdims=True)
        idx = jnp.min(jnp.where(cur == m, iota_e, float(N_EXPERTS)), axis=0, keepdims=True)
        onehot = iota_e == idx
        ws.append(jnp.sum(jnp.where(onehot, scores, 0.0), axis=0, keepdims=True))
        idxs.append(idx)
        cur = jnp.where(onehot, neg, cur)
        selmask = jnp.where(onehot, 1.0, selmask)
    pos = _dot(selmask.astype(BF16), ustrict_ref[...]) + carry_s[...]
    carry_s[...] = carry_s[...] + jnp.sum(selmask, axis=1, keepdims=True)
    cnt_ref[...] = carry_s[...]
    wsum = ws[0]
    for k in range(1, TOP_K):
        wsum = wsum + ws[k]
    pks = [jnp.sum(jnp.where(iota_e == idxs[k], pos, 0.0), axis=0, keepdims=True) for k in range(TOP_K)]
    eidx_ref[...] = jnp.concatenate(idxs, axis=0).astype(jnp.int32)
    wts_ref[...] = jnp.concatenate([w / wsum * ROUTED_SCALE for w in ws], axis=0)
    pos_ref[...] = jnp.concatenate(pks, axis=0).astype(jnp.int32)


def _route(h2, wt_hi, wt_lo, bias_col, tm):
    t_all = h2.shape[0]
    ustrict = jnp.asarray(np.triu(np.ones((tm, tm), np.float32), 1), dtype=BF16)
    full = lambda a: pl.BlockSpec(a.shape, lambda i, _n=a.ndim: (0,) * _n)
    tok = pl.BlockSpec((TOP_K, tm), lambda i: (0, i))
    return pl.pallas_call(
        functools.partial(_route_kernel, tm=tm),
        grid=(t_all // tm,),
        in_specs=[pl.BlockSpec((tm, D_MODEL), lambda i: (i, 0)), full(wt_hi), full(wt_lo), full(bias_col),
                  full(ustrict)],
        out_specs=[tok, tok, tok, pl.BlockSpec((N_EXPERTS, 1), lambda i: (0, 0))],
        out_shape=[jax.ShapeDtypeStruct((TOP_K, t_all), jnp.int32), jax.ShapeDtypeStruct((TOP_K, t_all), F32),
                   jax.ShapeDtypeStruct((TOP_K, t_all), jnp.int32), jax.ShapeDtypeStruct((N_EXPERTS, 1), F32)],
        scratch_shapes=[pltpu.VMEM((N_EXPERTS, 1), F32)],
        compiler_params=_cparams(("arbitrary",)),
        name="route",
    )(h2, wt_hi, wt_lo, bias_col, ustrict)


def _slots_kernel(eidx_ref, pos_ref, pstart_ref, slotx_ref, sloty_ref):
    tm = eidx_ref.shape[1]
    iota_e = lax.broadcasted_iota(jnp.int32, (N_EXPERTS, tm), 0)
    for c, out in enumerate((slotx_ref, sloty_ref)):
        rows = []
        for k in range(TOP_K):
            onehot = iota_e == eidx_ref[k:k + 1, :]
            rows.append(jnp.sum(jnp.where(onehot, pstart_ref[:, c:c + 1], 0.0), axis=0, keepdims=True))
        out[...] = jnp.concatenate(rows, axis=0).astype(jnp.int32) + pos_ref[...]


def _slots(eidx, pos, pstart_cols, tm):
    t_all = eidx.shape[1]
    tok = pl.BlockSpec((TOP_K, tm), lambda i: (0, i))
    return pl.pallas_call(
        _slots_kernel,
        grid=(t_all // tm,),
        in_specs=[tok, tok, pl.BlockSpec((N_EXPERTS, 2), lambda i: (0, 0))],
        out_specs=[tok, tok],
        out_shape=[jax.ShapeDtypeStruct((TOP_K, t_all), jnp.int32), jax.ShapeDtypeStruct((TOP_K, t_all), jnp.int32)],
        compiler_params=_cparams(("arbitrary",)),
        name="slots",
    )(eidx, pos, pstart_cols)


TD_TOK = 512
XS_ALIGN = 8


def _dispatch_kernel(fstart_ref, flen_ref, total_ref, slot_ref, h_ref, xs_out, zero_s, sem, zsem, *, xs_rows):
    i = pl.program_id(0)

    def body(j, carry):
        for k in range(TOP_K):
            pltpu.make_async_copy(h_ref.at[pl.ds(j, 1), :], xs_out.at[pl.ds(slot_ref[0, 0, j * TOP_K + k], 1), :],
                                  sem).start(priority=k % 2)
        return carry

    lax.fori_loop(0, TD_TOK, body, 0)
    for k in range(TOP_K):
        pltpu.make_async_copy(h_ref, xs_out.at[pl.ds(0, TD_TOK), :], sem).wait()

    @pl.when(i == pl.num_programs(0) - 1)
    def _():
        zero_s[...] = jnp.zeros_like(zero_s)
        total = total_ref[0]
        n_tail = (xs_rows - total) // XS_ALIGN

        def gap_copy(e, r):
            return pltpu.make_async_copy(zero_s.at[pl.ds(0, 1), :], xs_out.at[pl.ds(fstart_ref[e] + r, 1), :], zsem)

        def tail_copy(q):
            start = pl.multiple_of(total + q * XS_ALIGN, XS_ALIGN)
            return pltpu.make_async_copy(zero_s, xs_out.at[pl.ds(start, XS_ALIGN), :], zsem)

        def for_gaps(fn):
            def per_expert(e, carry):
                for r in range(XS_ALIGN - 1):
                    @pl.when(r < flen_ref[e])
                    def _():
                        fn(gap_copy(e, r))
                return carry
            lax.fori_loop(0, N_EXPERTS, per_expert, 0)

        def for_tail(fn):
            def per_q(q, carry):
                fn(tail_copy(q))
                return carry
            lax.fori_loop(0, n_tail, per_q, 0)

        for_gaps(lambda cp: cp.start())
        for_tail(lambda cp: cp.start())
        for_gaps(lambda cp: cp.wait())
        for_tail(lambda cp: cp.wait())


def _dispatch(slot3, h2, fstart, flen, total, xs_rows):
    t_all = h2.shape[0]
    grid_spec = pltpu.PrefetchScalarGridSpec(
        num_scalar_prefetch=3,
        grid=(t_all // TD_TOK,),
        in_specs=[pl.BlockSpec((1, 1, TD_TOK * TOP_K), lambda i, a, b, c: (i, 0, 0), memory_space=pltpu.SMEM),
                  pl.BlockSpec((TD_TOK, D_MODEL), lambda i, a, b, c: (i, 0))],
        out_specs=pl.BlockSpec(memory_space=pl.ANY),
        scratch_shapes=[pltpu.VMEM((XS_ALIGN, D_MODEL), F32), pltpu.SemaphoreType.DMA(()),
                        pltpu.SemaphoreType.DMA(())],
    )
    return pl.pallas_call(
        functools.partial(_dispatch_kernel, xs_rows=xs_rows),
        grid_spec=grid_spec,
        out_shape=jax.ShapeDtypeStruct((xs_rows, D_MODEL), F32),
        compiler_params=_cparams(("arbitrary",)),
        name="dispatch",
    )(fstart, flen, total, slot3, h2)


IV_TOK = 512


def _invert_kernel(slot_ref, zero_hbm, out_hbm, tbl, sem):
    i = pl.program_id(0)

    @pl.when(i == 0)
    def _():
        cp = pltpu.make_async_copy(zero_hbm, tbl, sem)
        cp.start()
        cp.wait()

    base = i * IV_TOK

    def body(j, carry):
        for k in range(TOP_K):
            tbl[slot_ref[0, 0, j * TOP_K + k]] = base + j
        return carry

    lax.fori_loop(0, IV_TOK, body, 0, unroll=2)

    @pl.when(i == pl.num_programs(0) - 1)
    def _():
        cp = pltpu.make_async_copy(tbl, out_hbm, sem)
        cp.start()
        cp.wait()


def _invert(slot3, n_rows):
    n_tiles = slot3.shape[0]
    zeros = jnp.zeros((n_rows,), jnp.int32)
    return pl.pallas_call(
        _invert_kernel,
        grid=(n_tiles,),
        in_specs=[pl.BlockSpec((1, 1, IV_TOK * TOP_K), lambda i: (i, 0, 0), memory_space=pltpu.SMEM),
                  pl.BlockSpec(memory_space=pl.ANY)],
        out_specs=pl.BlockSpec(memory_space=pl.ANY),
        out_shape=jax.ShapeDtypeStruct((n_rows,), jnp.int32),
        scratch_shapes=[pltpu.SMEM((n_rows,), jnp.int32), pltpu.SemaphoreType.DMA(())],
        compiler_params=_cparams(("arbitrary",)),
        name="invert",
    )(slot3, zeros)


X_RING = 4
X_AHEAD = X_RING - 1
RT_RING = 8
RT_AHEAD = 6
BLOCK_DMA_PRIORITY = 1
W_BUFFERS = 3


def _expert_kernel(bstart_ref, nblk_ref, nused_ref, rt_hbm, h_hbm, wgu_hbm, wdn_hbm, ys_hbm,
                   xbuf, ybuf, wgu_f32, wdn_f32, wgu16, wdn16, rt_smem, sem_in, sem_out, sem_w, sem_rt,
                   *, n_blocks):
    e = pl.program_id(0)
    nused = nused_ref[0]
    n = nblk_ref[e]
    g0 = bstart_ref[e]

    def w_copies(ex):
        s = ex % W_BUFFERS
        return (pltpu.make_async_copy(wgu_hbm.at[ex], wgu_f32.at[s], sem_w.at[0, s]),
                pltpu.make_async_copy(wdn_hbm.at[ex], wdn_f32.at[s], sem_w.at[1, s]))

    @pl.when(e == 0)
    def _():
        for q in range(W_BUFFERS - 1):
            for cp in w_copies(q):
                cp.start()

    @pl.when(e + (W_BUFFERS - 1) < N_EXPERTS)
    def _():
        for cp in w_copies(e + (W_BUFFERS - 1)):
            cp.start()

    def rt_copy(q):
        s = q % RT_RING
        return pltpu.make_async_copy(rt_hbm.at[pl.ds(q, 1), :], rt_smem.at[pl.ds(s, 1), :], sem_rt.at[s])

    def gather_block(q):
        src = jnp.minimum(q, nused - 1) % RT_RING
        dst = q % X_RING
        for j in range(MOE_BLOCK):
            pltpu.make_async_copy(h_hbm.at[pl.ds(rt_smem[src, j], 1), :], xbuf.at[dst, pl.ds(j, 1), :],
                                  sem_in.at[dst]).start(priority=j % 2)

    def wait_block(q):
        s = q % X_RING
        pltpu.make_async_copy(h_hbm.at[pl.ds(0, MOE_BLOCK), :], xbuf.at[s], sem_in.at[s]).wait()

    def out_copy(g, s):
        return pltpu.make_async_copy(ybuf.at[s], ys_hbm.at[pl.ds(g * MOE_BLOCK, MOE_BLOCK), :], sem_out.at[s])

    @pl.when(e == 0)
    def _():
        for q in range(RT_AHEAD):
            @pl.when(q < nused)
            def _():
                rt_copy(q).start()
        for q in range(X_AHEAD):
            @pl.when(q < nused)
            def _():
                rt_copy(q).wait()
                gather_block(q)

    for cp in w_copies(e):
        cp.wait()

    @pl.when(n > 0)
    def _():
        wgu16[...] = wgu_f32[e % W_BUFFERS].astype(BF16)
        wdn16[...] = wdn_f32[e % W_BUFFERS].astype(BF16)

    def block(b, carry):
        g = g0 + b
        s = g % 2

        @pl.when(g + RT_AHEAD < nused)
        def _():
            rt_copy(g + RT_AHEAD).start()

        @pl.when(g + X_AHEAD < nused)
        def _():
            rt_copy(g + X_AHEAD).wait()

        wait_block(g)
        gather_block(g + X_AHEAD)
        gu = _dot(xbuf[g % X_RING].astype(BF16), wgu16[...])
        gate = gu[:, :EXPERT_DIM]
        act = (gate * _sigmoid(gate) * gu[:, EXPERT_DIM:]).astype(BF16)
        y = _dot(act, wdn16[...])

        @pl.when(g >= 2)
        def _():
            out_copy(g - 2, s).wait()

        ybuf[s] = y
        out_copy(g, s).start(priority=BLOCK_DMA_PRIORITY)
        return carry

    lax.fori_loop(0, n, block, 0)

    @pl.when(e == pl.num_programs(0) - 1)
    def _():
        for d in range(X_AHEAD):
            @pl.when(nused + d >= X_AHEAD)
            def _():
                wait_block(nused + d)

        @pl.when(nused >= 2)
        def _():
            out_copy(nused - 2, nused % 2).wait()

        @pl.when(nused >= 1)
        def _():
            out_copy(nused - 1, (nused - 1) % 2).wait()

        ybuf[0] = jnp.zeros((MOE_BLOCK, D_MODEL), F32)

        def tail(g, carry):
            out_copy(g, 0).start()
            out_copy(g, 0).wait()
            return carry

        lax.fori_loop(nused, n_blocks, tail, 0)


def _experts(row_tok, h2, bstart, nblk, nused, w_gu, w_down):
    n_blocks = row_tok.shape[0]
    any_spec = pl.BlockSpec(memory_space=pl.ANY)
    grid_spec = pltpu.PrefetchScalarGridSpec(
        num_scalar_prefetch=3,
        grid=(N_EXPERTS,),
        in_specs=[any_spec, any_spec, any_spec, any_spec],
        out_specs=any_spec,
        scratch_shapes=[pltpu.VMEM((X_RING, MOE_BLOCK, D_MODEL), F32),
                        pltpu.VMEM((2, MOE_BLOCK, D_MODEL), F32),
                        pltpu.VMEM((W_BUFFERS, D_MODEL, 2 * EXPERT_DIM), F32),
                        pltpu.VMEM((W_BUFFERS, EXPERT_DIM, D_MODEL), F32),
                        pltpu.VMEM((D_MODEL, 2 * EXPERT_DIM), BF16),
                        pltpu.VMEM((EXPERT_DIM, D_MODEL), BF16),
                        pltpu.SMEM((RT_RING, MOE_BLOCK), jnp.int32),
                        pltpu.SemaphoreType.DMA((X_RING,)), pltpu.SemaphoreType.DMA((2,)),
                        pltpu.SemaphoreType.DMA((2, W_BUFFERS)), pltpu.SemaphoreType.DMA((RT_RING,))],
    )
    return pl.pallas_call(
        functools.partial(_expert_kernel, n_blocks=n_blocks),
        grid_spec=grid_spec,
        out_shape=jax.ShapeDtypeStruct((n_blocks * MOE_BLOCK, D_MODEL), F32),
        compiler_params=_cparams(("arbitrary",)),
        name="experts",
    )(bstart, nblk, nused, row_tok, h2, w_gu, w_down)


TC_TOK = 128


def _combine_kernel(slot_ref, slot_next_ref, wt_ref, x1_ref, h2_ref, g2_ref, sgu_ref, sdn_ref, lg_ref, lb_ref,
                    y_hbm, oc_ref, ol_ref, gbuf, sems, *, n_tiles, ctx_tiles, alpha):
    i = pl.program_id(0)

    def row_copy(sref, b, j, k):
        return pltpu.make_async_copy(y_hbm.at[pl.ds(sref[0, 0, j * TOP_K + k], 1), :],
                                     gbuf.at[b, k, pl.ds(j, 1), :], sems.at[b])

    def wait_tile(b):
        for k in range(TOP_K):
            pltpu.make_async_copy(y_hbm.at[pl.ds(0, TC_TOK), :], gbuf.at[b, k], sems.at[b]).wait()

    @pl.when(i == 0)
    def _():
        def body(j, carry):
            for k in range(TOP_K):
                row_copy(slot_ref, 0, j, k).start(priority=k % 2)
            return carry
        lax.fori_loop(0, TC_TOK, body, 0)

    b = i % 2
    nb = 1 - b

    def issue_next(j0, j1):
        for j in range(j0, j1):
            for k in range(TOP_K):
                row_copy(slot_next_ref, nb, j, k).start(priority=k % 2)

    issue_next(0, TC_TOK // 2)
    h = h2_ref[...].astype(BF16)
    su = _dot(h, sgu_ref[...])
    sg = su[:, :EXPERT_DIM]
    act = (sg * _sigmoid(sg) * su[:, EXPERT_DIM:]).astype(BF16)
    ffn = _dot(act, sdn_ref[...])
    wait_tile(b)
    issue_next(TC_TOK // 2, TC_TOK)
    for k in range(TOP_K):
        ffn = ffn + gbuf[b, k] * wt_ref[:, k:k + 1]
    out = _layer_norm(alpha * x1_ref[...] + g2_ref[0] * ffn, lg_ref[...], lb_ref[...])

    @pl.when(i < ctx_tiles)
    def _():
        oc_ref[...] = out

    @pl.when(i >= ctx_tiles)
    def _():
        ol_ref[...] = out

    @pl.when(i == n_tiles - 1)
    def _():
        wait_tile(nb)


def _combine(slot3, wts_t, x1, h2, g2, sh_gu, sh_dn, ln_g, ln_b, y_sorted, row_of_tile, alpha, tok_ctx):
    t_all = x1.shape[0]
    n_tiles = t_all // TC_TOK
    ctx_tiles = tok_ctx // TC_TOK
    tile = lambda w: pl.BlockSpec((TC_TOK, w), lambda i: (i, 0))
    full = lambda a: pl.BlockSpec(a.shape, lambda i, _n=a.ndim: (0,) * _n)
    return pl.pallas_call(
        functools.partial(_combine_kernel, n_tiles=n_tiles, ctx_tiles=ctx_tiles, alpha=alpha),
        grid=(n_tiles,),
        in_specs=[pl.BlockSpec((1, 1, TC_TOK * TOP_K), lambda i: (i, 0, 0), memory_space=pltpu.SMEM),
                  pl.BlockSpec((1, 1, TC_TOK * TOP_K), lambda i: (jnp.minimum(i + 1, n_tiles - 1), 0, 0),
                               memory_space=pltpu.SMEM),
                  tile(TOP_K), tile(D_MODEL), tile(D_MODEL),
                  pl.BlockSpec((1, 1, D_MODEL), lambda i: (row_of_tile(i), 0, 0)),
                  full(sh_gu), full(sh_dn), full(ln_g), full(ln_b),
                  pl.BlockSpec(memory_space=pl.ANY)],
        out_specs=_two_source_specs(TC_TOK, D_MODEL, ctx_tiles),
        out_shape=[jax.ShapeDtypeStruct((tok_ctx, D_MODEL), F32),
                   jax.ShapeDtypeStruct((t_all - tok_ctx, D_MODEL), F32)],
        scratch_shapes=[pltpu.VMEM((2, TOP_K, TC_TOK, D_MODEL), F32), pltpu.SemaphoreType.DMA((2,))],
        compiler_params=_cparams(("arbitrary",)),
        name="combine",
    )(slot3, slot3, wts_t, x1, h2, g2, sh_gu, sh_dn, ln_g, ln_b, y_sorted)


def kernel(x_prompt, x_sample, state_rwkv, c, c_ctx, w_ada, b_ada, w_in, mu_shift, w_decay0, w_decay2, w_a0, w_a2, w_g2, k_k, k_a, r_k, lnx_g, lnx_b, conv_w, conv_b, conv_ln_g, conv_ln_b, w_out, ln1_g, ln1_b, router_w, router_bias, expert_w_gu, expert_w_down, shared_w_gu, shared_w_down, ln2_g, ln2_b):
    depth = w_ada.shape[0]
    assert depth == 1
    alpha = (2.0 * depth) ** 0.25
    n_ctx, t_ctx, _ = x_prompt.shape
    n_lat, t_lat, _ = x_sample.shape
    tok_ctx = n_ctx * t_ctx
    tok_lat = n_lat * t_lat
    t_all = tok_ctx + tok_lat
    tm = 1024
    assert tok_ctx % tm == 0 and t_lat % tm == 0 and t_lat // GRID_W * GRID_W == t_lat
    l = 0

    x_ctx = x_prompt.reshape(tok_ctx, D_MODEL)
    x_lat = x_sample.reshape(tok_lat, D_MODEL)

    cond8 = jnp.zeros((8, D_MODEL), F32).at[0].set(c_ctx).at[1:1 + n_lat].set(c)
    mod = _ada(cond8, w_ada[l], b_ada[l])
    sh1, sc1, g1, sh2, sc2, g2 = [m.reshape(8, 1, D_MODEL) for m in jnp.split(mod, 6, axis=-1)]

    def row_of_tile_for(tile_rows):
        ctx_tiles = tok_ctx // tile_rows
        per_seq = t_lat // tile_rows
        return lambda i: jnp.where(i < ctx_tiles, 0, 1 + (i - ctx_tiles) // per_seq)

    w_in_a = w_in[l][:, :RWKV_IN].astype(BF16)
    w_in_b = w_in[l][:, RWKV_IN:].astype(BF16)
    proj_a, proj_b = _in_proj(x_ctx, x_lat, sc1, sh1, w_in_a, w_in_b, tm, row_of_tile_for(tm))

    zeros64 = jnp.zeros((64, 2 * RW), F32)
    wd2 = jnp.concatenate([w_decay2[l, 0], w_decay2[l, 1]], axis=1)
    wa2 = jnp.concatenate([w_a2[l, 0], w_a2[l, 1]], axis=1)
    w_lora = jnp.concatenate([jnp.concatenate([wd2, zeros64], axis=1),
                              jnp.concatenate([zeros64, wa2], axis=1)], axis=0).astype(BF16)
    row = lambda v: v.reshape(1, -1)
    rwkv_w = [row(mu_shift[l]), w_lora, w_g2[l].astype(BF16), row(w_decay0[l]), row(w_a0[l]),
              row(k_k[l]), row(k_a[l]), row(r_k[l]), row(lnx_g[l]), row(lnx_b[l])]
    consts = _rwkv_consts()

    def to_cat(s):
        b = s.shape[0]
        s = s.reshape(b, 2, 2, 4, HD, HD)
        return jnp.transpose(s, (0, 1, 2, 5, 3, 4)).reshape(b, 4, HD, GW)

    s0_ctx = jnp.zeros((n_ctx, 4, HD, GW), F32)
    s0_lat = to_cat(state_rwkv[:, l].astype(F32))
    ya_c, sfin_ctx = _rwkv(proj_a, s0_ctx, rwkv_w, consts, t_ctx, n_ctx, 0)
    ya_l, _ = _rwkv(proj_a, s0_lat, rwkv_w, consts, t_lat, n_lat, tok_ctx // t_lat)
    sfin = jnp.transpose(sfin_ctx.reshape(n_ctx, 2, 2, HD, 4, HD), (0, 1, 2, 4, 5, 3))
    new_state = sfin.reshape(n_ctx, 1, 2, 2 * 4, HD, HD).astype(x_prompt.dtype)

    conv_wts = [conv_w[l], row(conv_b[l]), row(conv_ln_g[l]), row(conv_ln_b[l])]
    yb_c = _conv(proj_b, conv_wts, t_ctx, n_ctx, 0, False)
    yb_l = _conv(proj_b, conv_wts, t_lat, n_lat, tok_ctx // t_lat, True)

    w_oa = w_out[l][:RW].astype(BF16)
    w_ob = w_out[l][RW:].astype(BF16)
    x1, h2 = _out_proj(ya_c, ya_l, yb_c, yb_l, x_ctx, x_lat, g1, sc2, sh2, w_oa, w_ob, row(ln1_g[l]), row(ln1_b[l]), tm,
                       row_of_tile_for(tm), alpha)

    rwt = router_w[l].T
    rwt_hi = rwt.astype(BF16)
    rwt_lo = (rwt - rwt_hi.astype(F32)).astype(BF16)
    eidx, wts, pos, counts = _route(h2, rwt_hi, rwt_lo, router_bias[l].reshape(N_EXPERTS, 1), 512)

    n_assign = t_all * TOP_K
    n_blocks = (n_assign + N_EXPERTS * (MOE_BLOCK - 1) + MOE_BLOCK - 1) // MOE_BLOCK
    cnt = counts.reshape(N_EXPERTS).astype(jnp.int32)
    padded = (cnt + MOE_BLOCK - 1) // MOE_BLOCK * MOE_BLOCK
    pend = jnp.cumsum(padded)
    pstart = pend - padded
    padded_x = (cnt + XS_ALIGN - 1) // XS_ALIGN * XS_ALIGN
    pend_x = jnp.cumsum(padded_x)
    pstart_x = pend_x - padded_x
    xs_rows = n_assign + N_EXPERTS * (XS_ALIGN - 1) // XS_ALIGN * XS_ALIGN + MOE_BLOCK
    slot_x, slot_y = _slots(eidx, pos, jnp.stack([pstart_x, pstart], axis=1).astype(F32), 512)
    per_tile = lambda s, tok: s.T.reshape(t_all // tok, 1, tok * TOP_K)
    row_tok = _invert(per_tile(slot_y, IV_TOK), n_blocks * MOE_BLOCK).reshape(n_blocks, MOE_BLOCK)
    bstart = (pstart // MOE_BLOCK).astype(jnp.int32)
    nblk = (padded // MOE_BLOCK).astype(jnp.int32)
    nused = (pend[-1] // MOE_BLOCK).astype(jnp.int32).reshape(1)
    y_sorted = _experts(row_tok, h2, bstart, nblk, nused, expert_w_gu[l], expert_w_down[l])
    y_ctx, y_lat = _combine(per_tile(slot_y, TC_TOK), wts.T, x1, h2, g2, shared_w_gu[l].astype(BF16),
                            shared_w_down[l].astype(BF16), row(ln2_g[l]), row(ln2_b[l]), y_sorted,
                            row_of_tile_for(TC_TOK), alpha, tok_ctx)
    return (y_ctx.reshape(n_ctx, t_ctx, D_MODEL), y_lat.reshape(n_lat, t_lat, D_MODEL), new_state)
```

```python
import functools
import math

import numpy as np
import jax
import jax.numpy as jnp
from jax import lax
from jax.experimental import pallas as pl
from jax.experimental.pallas import tpu as pltpu

F32 = jnp.float32
BF16 = jnp.bfloat16

D_MODEL = 1024
RW = 512
HD = 64
CW = 512
CONV_K = 31
RWKV_IN = 3 * RW + 64 + 64 + 128
N_EXPERTS = 256
TOP_K = 8
N_GROUPS = 8
TOPK_GROUPS = 4
EXPERT_DIM = 256
ROUTED_SCALE = 2.5
MOE_BLOCK = 128
GRID_W = 64
LN_EPS = 1e-5
GN_EPS = 64e-5
CHUNK = 64
GW = 256
SCAN_UNROLL = 2
DECAY_SCALE = math.exp(-0.5)
VMEM_LIMIT = 56 * 1024 * 1024


def _cparams(sem):
    return pltpu.CompilerParams(dimension_semantics=sem, vmem_limit_bytes=VMEM_LIMIT)


def _split_bf16(x):
    hi = x.astype(BF16)
    lo = (x - hi.astype(F32)).astype(BF16)
    return hi, lo


def _dot(a, b):
    return jnp.dot(a, b, preferred_element_type=F32)


def _dot_nt(a, b):
    return lax.dot_general(a, b, (((1,), (1,)), ((), ())), preferred_element_type=F32)


def _dot_tn(a, b):
    return lax.dot_general(a, b, (((0,), (0,)), ((), ())), preferred_element_type=F32)


def _sigmoid(x):
    return 1.0 / (1.0 + jnp.exp(-x))


def _layer_norm(x, g, b):
    mu = jnp.mean(x, axis=-1, keepdims=True)
    d = x - mu
    var = jnp.mean(d * d, axis=-1, keepdims=True)
    return d * lax.rsqrt(var + LN_EPS) * g + b


def _ada_kernel(c_ref, w_ref, b_ref, o_ref):
    c = c_ref[...]
    s = c * _sigmoid(c)
    s_hi, s_lo = _split_bf16(s)
    w_hi, w_lo = _split_bf16(w_ref[...])
    o_ref[...] = _dot(s_hi, w_hi) + _dot(s_lo, w_hi) + _dot(s_hi, w_lo) + b_ref[...]


def _ada(cond8, w_ada, b_ada):
    n = w_ada.shape[1]
    tn = 1536
    return pl.pallas_call(
        _ada_kernel,
        grid=(n // tn,),
        in_specs=[pl.BlockSpec((8, D_MODEL), lambda j: (0, 0)),
                  pl.BlockSpec((D_MODEL, tn), lambda j: (0, j)),
                  pl.BlockSpec((1, tn), lambda j: (0, j))],
        out_specs=pl.BlockSpec((8, tn), lambda j: (0, j)),
        out_shape=jax.ShapeDtypeStruct((8, n), F32),
        compiler_params=_cparams(("arbitrary",)),
        name="ada",
    )(cond8, w_ada, b_ada.reshape(1, n))


def _two_source_specs(tm, width, ctx_tiles):
    return [pl.BlockSpec((tm, width), lambda i: (jnp.minimum(i, ctx_tiles - 1), 0)),
            pl.BlockSpec((tm, width), lambda i: (jnp.maximum(i - ctx_tiles, 0), 0))]


def _inproj_kernel(xc_ref, xl_ref, sc_ref, sh_ref, wa_ref, wb_ref, oa_ref, ob_ref, *, ctx_tiles):
    x = jnp.where(pl.program_id(0) < ctx_tiles, xc_ref[...], xl_ref[...])
    h = (x * (1.0 + sc_ref[0]) + sh_ref[0]).astype(BF16)
    oa_ref[...] = _dot(h, wa_ref[...])
    ob_ref[...] = _dot(h, wb_ref[...])


def _in_proj(x_ctx, x_lat, sc1, sh1, w_a, w_b, tm, row_of_tile):
    t_all = x_ctx.shape[0] + x_lat.shape[0]
    ctx_tiles = x_ctx.shape[0] // tm
    return pl.pallas_call(
        functools.partial(_inproj_kernel, ctx_tiles=ctx_tiles),
        grid=(t_all // tm,),
        in_specs=_two_source_specs(tm, D_MODEL, ctx_tiles) + [
                  pl.BlockSpec((1, 1, D_MODEL), lambda i: (row_of_tile(i), 0, 0)),
                  pl.BlockSpec((1, 1, D_MODEL), lambda i: (row_of_tile(i), 0, 0)),
                  pl.BlockSpec(w_a.shape, lambda i: (0, 0)),
                  pl.BlockSpec(w_b.shape, lambda i: (0, 0))],
        out_specs=[pl.BlockSpec((tm, RWKV_IN), lambda i: (i, 0)),
                   pl.BlockSpec((tm, 2 * CW), lambda i: (i, 0))],
        out_shape=[jax.ShapeDtypeStruct((t_all, RWKV_IN), F32),
                   jax.ShapeDtypeStruct((t_all, 2 * CW), F32)],
        compiler_params=_cparams(("arbitrary",)),
        name="in_proj",
    )(x_ctx, x_lat, sc1, sh1, w_a, w_b)


def _rwkv_consts():
    c = CHUNK
    t = np.arange(c)[:, None]
    s = np.arange(GW)[None, :] % c
    sl = (s < t).astype(np.float32)
    il = (s <= t).astype(np.float32)
    su = (s > t).astype(np.float32)
    iu = (s >= t).astype(np.float32)
    eye_cat = (s == t).astype(np.float32)
    tri = np.stack([sl, il, su, iu, eye_cat])
    r = np.arange(GW)
    mask_bd = (r[:, None] // HD == r[None, :] // HD).astype(np.float32)
    eye = np.eye(GW, dtype=np.float32)
    cum = np.stack([np.tril(np.ones((c, c), np.float32)), np.triu(np.ones((c, c), np.float32))])
    q = np.arange(RW)
    seg = (q[:, None] // HD == q[None, :] // HD).astype(np.float32)
    return (jnp.asarray(tri), jnp.asarray(mask_bd, dtype=BF16), jnp.asarray(mask_bd), jnp.asarray(eye),
            jnp.asarray(cum, dtype=BF16), jnp.asarray(seg, dtype=BF16))


def _rwkv_kernel(p_ref, s0_ref, mu_ref, wlora_ref, wg2_ref, wd0_ref, wa0_ref, kk_ref, ka_ref, rk_ref,
                 lng_ref, lnb_ref, tri_ref, mbd16_ref, mbd_ref, eye_ref, cum_ref, seg_ref,
                 ya_ref, sfin_ref,
                 r_s, v_s, kk_s, g_s, bon_s, a_s, lw_s, kd_s, y_s, st_s, *, seq_len):
    nc = seq_len // CHUNK
    scan_unroll = SCAN_UNROLL
    seg = seg_ref[...]

    def seg_sum(x):
        hi, lo = _split_bf16(x)
        return _dot(hi, seg) + _dot(lo, seg)

    def phase_a(c, carry):
        t0 = pl.multiple_of(c * CHUNK, CHUNK)
        cur = p_ref[pl.ds(t0, CHUNK), :]
        prev_row = p_ref[pl.ds(jnp.maximum(t0 - 1, 0), 1), :] * (t0 > 0).astype(F32)
        next_row = p_ref[pl.ds(jnp.minimum(t0 + CHUNK, seq_len - 1), 1), :] * (t0 + CHUNK < seq_len).astype(F32)
        row = lax.broadcasted_iota(jnp.int32, cur.shape, 0)
        prev = jnp.where(row == 0, prev_row, pltpu.roll(cur, 1, 0))
        nxt = jnp.where(row == CHUNK - 1, next_row, pltpu.roll(cur, CHUNK - 1, 0))
        p = cur + mu_ref[...] * (0.5 * (prev + nxt) - cur)
        r = p[:, 0:RW]
        k = p[:, RW:2 * RW]
        v = p[:, 2 * RW:3 * RW]
        z = p[:, 3 * RW:3 * RW + 128]
        gd = p[:, 3 * RW + 128:3 * RW + 256]
        lane = lax.broadcasted_iota(jnp.int32, z.shape, 1)
        z = jnp.where(lane < 64, jnp.tanh(z), z)
        lora = _dot(z.astype(BF16), wlora_ref[...])
        g = _dot(_sigmoid(gd).astype(BF16), wg2_ref[...])
        kks = k * kk_ref[...]
        kk = kks * lax.rsqrt(seg_sum(kks * kks) + 1e-12)
        rows = pl.ds(t0, CHUNK)
        r_s[rows, :] = r
        v_s[rows, :] = v.astype(BF16)
        kk_s[rows, :] = kk
        g_s[rows, :] = g
        kd_sum = jnp.zeros_like(k)
        for d in range(2):
            w_logit = wd0_ref[:, d * RW:(d + 1) * RW] + lora[:, d * RW:(d + 1) * RW]
            a = _sigmoid(wa0_ref[:, d * RW:(d + 1) * RW] + lora[:, (2 + d) * RW:(3 + d) * RW])
            kd = k * (1.0 + (a - 1.0) * ka_ref[...])
            lw_s[d, rows, :] = -DECAY_SCALE * _sigmoid(w_logit)
            a_s[d, rows, :] = a
            kd_s[d, rows, :] = kd
            kd_sum = kd_sum + kd
        bon_s[rows, :] = seg_sum(r * kd_sum * rk_ref[...]) * v
        return carry

    lax.fori_loop(0, nc, phase_a, 0, unroll=2)

    mbd16 = mbd16_ref[...]
    mbd = mbd_ref[...]
    eye = eye_ref[...]

    def bd(x16):
        return jnp.concatenate([x16, x16, x16, x16], axis=0) * mbd16

    for u in range(4):
        s0 = s0_ref[0, u]
        st_s[u] = jnp.concatenate([s0, s0, s0, s0], axis=0) * mbd

    def prep(d, grp, t0):
        rows = pl.ds(t0, CHUNK)
        lanes = slice(grp * GW, (grp + 1) * GW)
        lw = lw_s[d, rows, lanes]
        hi, lo = _split_bf16(lw)
        cl = _dot(cum_ref[d], hi) + _dot(cum_ref[d], lo)
        e_cl = jnp.exp(cl)
        e_ce = jnp.exp(cl - lw)
        e_ncl = jnp.exp(-cl)
        tot = cl[CHUNK - 1:CHUNK, :] if d == 0 else cl[0:1, :]
        wc = jnp.exp(tot)
        kk = kk_s[rows, lanes]
        rt = r_s[rows, lanes] * e_cl
        bt = a_s[d, rows, lanes] * kk * e_ncl
        kt = kd_s[d, rows, lanes] * e_ncl
        kq16 = (kk * e_ce).astype(BF16)
        return dict(d=d, u=d * 2 + grp, rows=rows, lanes=lanes, wc=wc, rt=rt, kq16=kq16,
                    v16=v_s[rows, lanes], bt16=bt.astype(BF16), kt16=kt.astype(BF16),
                    btw16=(bt * wc).astype(BF16), ktw16=(kt * wc).astype(BF16),
                    g16=jnp.concatenate([kq16, rt.astype(BF16)], axis=0))

    def phase_b(i2, carry):
        us = []
        for j in range(scan_unroll):
            i = i2 * scan_unroll + j
            tf = pl.multiple_of(i * CHUNK, CHUNK)
            tb = pl.multiple_of((nc - 1 - i) * CHUNK, CHUNK)
            us += [prep(0, 0, tf), prep(1, 0, tb), prep(0, 1, tf), prep(1, 1, tb)]
        each = lambda fn: [fn(q) for q in us]
        strict = lambda q: tri_ref[2 * q['d']]
        incl = lambda q: tri_ref[2 * q['d'] + 1]
        out_b = each(lambda q: _dot_nt(q['g16'], bd(q['bt16'])))
        out_k = each(lambda q: _dot_nt(q['g16'], bd(q['kt16'])))
        a_bk = [o[:CHUNK] * strict(q) for o, q in zip(out_b, us)]
        p_rb = [(o[CHUNK:] * incl(q)).astype(BF16) for o, q in zip(out_b, us)]
        a_kk = [(o[:CHUNK] * strict(q)).astype(BF16) for o, q in zip(out_k, us)]
        p_rk = [(o[CHUNK:] * incl(q)).astype(BF16) for o, q in zip(out_k, us)]
        tinv = [tri_ref[4] - a for a in a_bk]
        pw = [a.astype(BF16) for a in a_bk]
        for _ in range(5):
            pw = [_dot(p, bd(p)).astype(BF16) for p in pw]
            tinv = [t + _dot(p, bd(t.astype(BF16))) for t, p in zip(tinv, pw)]
        bdv = each(lambda q: bd(q['v16']))
        akkv = [_dot(a, b) for a, b in zip(a_kk, bdv)]
        x = [_dot(t.astype(BF16), jnp.concatenate([bd(q['kq16']), bd(k.astype(BF16))], axis=1))
             for t, q, k in zip(tinv, us, akkv)]
        qu16 = [z.astype(BF16) for z in x]
        tn1 = [_dot_tn(q['btw16'], z) for q, z in zip(us, qu16)]
        tn2 = each(lambda q: _dot_tn(q['ktw16'], q['v16']))
        r1 = [_dot(p, jnp.concatenate([bd(z[:, :GW]), bd(z[:, GW:])], axis=1)) for p, z in zip(p_rb, qu16)]
        y0 = [_dot(p, b) - r[:, GW:] for p, b, r in zip(p_rk, bdv, r1)]
        for q, t1, t2, r, y in zip(us, tn1, tn2, r1, y0):
            mc = (eye * q['wc'] - t1[:, :GW]) * mbd
            ncm = (t2 - t1[:, GW:]) * mbd
            rh = q['rt'] - r[:, :GW]
            st16 = st_s[q['u']].astype(BF16)
            big = _dot(jnp.concatenate([rh.astype(BF16), mc.astype(BF16)], axis=0), st16)
            y_s[q['d'], q['rows'], q['lanes']] = big[:CHUNK] + y
            st_s[q['u']] = big[CHUNK:] + ncm
        return carry

    lax.fori_loop(0, nc // scan_unroll, phase_b, 0)

    for u in range(4):
        st = st_s[u]
        sfin_ref[0, u] = st[0:HD] + st[HD:2 * HD] + st[2 * HD:3 * HD] + st[3 * HD:4 * HD]

    def phase_c(c, carry):
        rows = pl.ds(pl.multiple_of(c * CHUNK, CHUNK), CHUNK)
        y = y_s[0, rows, :] + y_s[1, rows, :]
        mean = seg_sum(y) * (1.0 / HD)
        dlt = y - mean
        var = seg_sum(dlt * dlt) * (1.0 / HD)
        yn = dlt * lax.rsqrt(var + GN_EPS)
        ya_ref[rows, :] = (yn * lng_ref[...] + lnb_ref[...] + bon_s[rows, :]) * g_s[rows, :]
        return carry

    lax.fori_loop(0, nc, phase_c, 0, unroll=2)


def _rwkv(proj_a, s0_cat, wts, consts, seq_len, n_seq, first_block):
    full = lambda a: pl.BlockSpec(a.shape, lambda s, _n=a.ndim: (0,) * _n)
    p_mode = dict(pipeline_mode=pl.Buffered(1)) if n_seq <= 2 else {}
    in_specs = [pl.BlockSpec((seq_len, RWKV_IN), lambda s: (first_block + s, 0), **p_mode),
                pl.BlockSpec((1, 4, HD, GW), lambda s: (s, 0, 0, 0))]
    in_specs += [full(a) for a in wts] + [full(a) for a in consts]
    args = [proj_a, s0_cat] + list(wts) + list(consts)
    seq = (seq_len, RW)
    return pl.pallas_call(
        functools.partial(_rwkv_kernel, seq_len=seq_len),
        grid=(n_seq,),
        in_specs=in_specs,
        out_specs=[pl.BlockSpec((seq_len, RW), lambda s: (s, 0)),
                   pl.BlockSpec((1, 4, HD, GW), lambda s: (s, 0, 0, 0))],
        out_shape=[jax.ShapeDtypeStruct((n_seq * seq_len, RW), F32),
                   jax.ShapeDtypeStruct((n_seq, 4, HD, GW), F32)],
        scratch_shapes=[pltpu.VMEM(seq, F32), pltpu.VMEM(seq, BF16), pltpu.VMEM(seq, F32), pltpu.VMEM(seq, F32),
                        pltpu.VMEM(seq, F32),
                        pltpu.VMEM((2,) + seq, F32), pltpu.VMEM((2,) + seq, F32), pltpu.VMEM((2,) + seq, F32),
                        pltpu.VMEM((2,) + seq, F32), pltpu.VMEM((4, GW, GW), F32)],
        compiler_params=_cparams(("arbitrary",)),
        name="rwkv_%d" % seq_len,
    )(*args)


CONV_PAD = 16
VPAD = (CONV_K // 2) * GRID_W


def _conv_kernel(u_ref, w_ref, b_ref, g_ref, beta_ref, o_ref, pad_s, vpad_s, y_s, *, seq_len, grid):
    rb = 256
    glu = u_ref[:, :CW] * _sigmoid(u_ref[:, CW:])
    zeros = jnp.zeros((CONV_PAD, CW), F32)
    pad_s[0:CONV_PAD, :] = zeros
    pad_s[CONV_PAD + seq_len:2 * CONV_PAD + seq_len, :] = zeros
    pad_s[CONV_PAD:CONV_PAD + seq_len, :] = glu
    half = CW // 2
    if grid:
        vz = jnp.zeros((VPAD, half), F32)
        vpad_s[0:VPAD, :] = vz
        vpad_s[VPAD + seq_len:2 * VPAD + seq_len, :] = vz
        vpad_s[VPAD:VPAD + seq_len, :] = glu[:, half:]
    n_h_lanes = (half if grid else CW) // 128
    for r0 in range(0, seq_len, rb):
        for lb in range(n_h_lanes):
            lanes = slice(lb * 128, (lb + 1) * 128)
            acc = jnp.zeros((rb, 128), F32)
            if grid:
                col = lax.broadcasted_iota(jnp.int32, (rb, 128), 0) % GRID_W
            span = rb + 2 * CONV_PAD
            xpad = pad_s[r0:r0 + span, lanes]
            for res in range(8):
                xrot = xpad if res == 0 else pltpu.roll(xpad, span - res, 0)
                for k in range(CONV_K):
                    if (k + 1) % 8 != res:
                        continue
                    off = (k + 1) - res
                    term = xrot[off:off + rb] * w_ref[k:k + 1, lanes]
                    if grid:
                        src = col + (k - CONV_K // 2)
                        term = jnp.where((src >= 0) & (src < GRID_W), term, 0.0)
                    acc = acc + term
            y_s[r0:r0 + rb, lanes] = acc
        if grid:
            for lb in range(half // 128):
                lanes = slice(lb * 128, (lb + 1) * 128)
                wl = slice(half + lb * 128, half + (lb + 1) * 128)
                acc = jnp.zeros((rb, 128), F32)
                for k in range(CONV_K):
                    start = VPAD + r0 + (k - CONV_K // 2) * GRID_W
                    acc = acc + vpad_s[start:start + rb, lanes] * w_ref[k:k + 1, wl]
                y_s[r0:r0 + rb, wl] = acc
    for r0 in range(0, seq_len, rb):
        y = y_s[r0:r0 + rb, :] + b_ref[...]
        yl = _layer_norm(y, g_ref[...], beta_ref[...])
        o_ref[r0:r0 + rb, :] = yl * _sigmoid(yl)


def _conv(proj_b, wts, seq_len, n_seq, first_block, grid):
    full = lambda a: pl.BlockSpec(a.shape, lambda s, _n=a.ndim: (0,) * _n)
    in_specs = [pl.BlockSpec((seq_len, 2 * CW), lambda s: (first_block + s, 0))] + [full(a) for a in wts]
    args = [proj_b] + list(wts)
    return pl.pallas_call(
        functools.partial(_conv_kernel, seq_len=seq_len, grid=grid),
        grid=(n_seq,),
        in_specs=in_specs,
        out_specs=pl.BlockSpec((seq_len, CW), lambda s: (s, 0)),
        out_shape=jax.ShapeDtypeStruct((n_seq * seq_len, CW), F32),
        scratch_shapes=[pltpu.VMEM((seq_len + 2 * CONV_PAD, CW), F32),
                        pltpu.VMEM((seq_len + 2 * VPAD if grid else 8, CW // 2), F32),
                        pltpu.VMEM((seq_len, CW), F32)],
        compiler_params=_cparams(("arbitrary",)),
        name="conv_%d" % seq_len,
    )(*args)


def _outproj_kernel(yac_ref, yal_ref, ybc_ref, ybl_ref, xc_ref, xl_ref, g1_ref, sc2_ref, sh2_ref, wa_ref, wb_ref,
                    lg_ref, lb_ref, x1_ref, h2_ref, *, alpha, ctx_tiles):
    is_ctx = pl.program_id(0) < ctx_tiles
    ya = jnp.where(is_ctx, yac_ref[...], yal_ref[...])
    yb = jnp.where(is_ctx, ybc_ref[...], ybl_ref[...])
    x = jnp.where(is_ctx, xc_ref[...], xl_ref[...])
    mix = _dot(ya.astype(BF16), wa_ref[...]) + _dot(yb.astype(BF16), wb_ref[...])
    x1 = _layer_norm(alpha * x + g1_ref[0] * mix, lg_ref[...], lb_ref[...])
    x1_ref[...] = x1
    h2_ref[...] = x1 * (1.0 + sc2_ref[0]) + sh2_ref[0]


def _out_proj(ya_c, ya_l, yb_c, yb_l, x_ctx, x_lat, g1, sc2, sh2, w_oa, w_ob, ln_g, ln_b, tm, row_of_tile, alpha):
    t_all = x_ctx.shape[0] + x_lat.shape[0]
    ctx_tiles = x_ctx.shape[0] // tm
    tile = lambda w: pl.BlockSpec((tm, w), lambda i: (i, 0))
    modrow = pl.BlockSpec((1, 1, D_MODEL), lambda i: (row_of_tile(i), 0, 0))
    full = lambda a: pl.BlockSpec(a.shape, lambda i, _n=a.ndim: (0,) * _n)
    return pl.pallas_call(
        functools.partial(_outproj_kernel, alpha=alpha, ctx_tiles=ctx_tiles),
        grid=(t_all // tm,),
        in_specs=_two_source_specs(tm, RW, ctx_tiles) + _two_source_specs(tm, CW, ctx_tiles)
        + _two_source_specs(tm, D_MODEL, ctx_tiles)
        + [modrow, modrow, modrow, full(w_oa), full(w_ob), full(ln_g), full(ln_b)],
        out_specs=[tile(D_MODEL), tile(D_MODEL)],
        out_shape=[jax.ShapeDtypeStruct((t_all, D_MODEL), F32), jax.ShapeDtypeStruct((t_all, D_MODEL), F32)],
        compiler_params=_cparams(("arbitrary",)),
        name="out_proj",
    )(ya_c, ya_l, yb_c, yb_l, x_ctx, x_lat, g1, sc2, sh2, w_oa, w_ob, ln_g, ln_b)


def _route_kernel(h_ref, wt_hi_ref, wt_lo_ref, bias_ref, ustrict_ref,
                  eidx_ref, wts_ref, pos_ref, cnt_ref, carry_s, *, tm):
    i = pl.program_id(0)

    @pl.when(i == 0)
    def _():
        carry_s[...] = jnp.zeros_like(carry_s)

    h_hi, h_lo = _split_bf16(h_ref[...])
    logits = _dot_nt(wt_hi_ref[...], h_hi) + _dot_nt(wt_hi_ref[...], h_lo) + _dot_nt(wt_lo_ref[...], h_hi)
    scores = _sigmoid(logits)
    sel = scores + bias_ref[...]
    neg = jnp.float32(-jnp.inf)
    gsz = N_EXPERTS // N_GROUPS
    gs = []
    for g in range(N_GROUPS):
        blk = sel[g * gsz:(g + 1) * gsz]
        m1 = jnp.max(blk, axis=0, keepdims=True)
        eq = blk == m1
        cnt = jnp.sum(eq.astype(F32), axis=0, keepdims=True)
        m2 = jnp.max(jnp.where(eq, neg, blk), axis=0, keepdims=True)
        gs.append(m1 + jnp.where(cnt >= 2.0, m1, m2))
    masked = []
    for g in range(N_GROUPS):
        rank = jnp.zeros_like(gs[g])
        for o in range(N_GROUPS):
            if o == g:
                continue
            beats = (gs[o] > gs[g]) | ((gs[o] == gs[g]) & (o < g))
            rank = rank + beats.astype(F32)
        keep = rank < float(TOPK_GROUPS)
        masked.append(jnp.where(keep, sel[g * gsz:(g + 1) * gsz], neg))
    cur = jnp.concatenate(masked, axis=0)
    iota_e = lax.broadcasted_iota(jnp.int32, cur.shape, 0).astype(F32)
    idxs, ws = [], []
    selmask = jnp.zeros(cur.shape, F32)
    for _ in range(TOP_K):
        m = jnp.max(cur, axis=0, keepdims=True)
        idx = jnp.min(jnp.where(cur == m, iota_e, float(N_EXPERTS)), axis=0, keepdims=True)
        onehot = iota_e == idx
        ws.append(jnp.sum(jnp.where(onehot, scores, 0.0), axis=0, keepdims=True))
        idxs.append(idx)
        cur = jnp.where(onehot, neg, cur)
        selmask = jnp.where(onehot, 1.0, selmask)
    pos = _dot(selmask.astype(BF16), ustrict_ref[...]) + carry_s[...]
    carry_s[...] = carry_s[...] + jnp.sum(selmask, axis=1, keepdims=True)
    cnt_ref[...] = carry_s[...]
    wsum = ws[0]
    for k in range(1, TOP_K):
        wsum = wsum + ws[k]
    pks = [jnp.sum(jnp.where(iota_e == idxs[k], pos, 0.0), axis=0, keepdims=True) for k in range(TOP_K)]
    eidx_ref[...] = jnp.concatenate(idxs, axis=0).astype(jnp.int32)
    wts_ref[...] = jnp.concatenate([w / wsum * ROUTED_SCALE for w in ws], axis=0)
    pos_ref[...] = jnp.concatenate(pks, axis=0).astype(jnp.int32)


def _route(h2, wt_hi, wt_lo, bias_col, tm):
    t_all = h2.shape[0]
    ustrict = jnp.asarray(np.triu(np.ones((tm, tm), np.float32), 1), dtype=BF16)
    full = lambda a: pl.BlockSpec(a.shape, lambda i, _n=a.ndim: (0,) * _n)
    tok = pl.BlockSpec((TOP_K, tm), lambda i: (0, i))
    return pl.pallas_call(
        functools.partial(_route_kernel, tm=tm),
        grid=(t_all // tm,),
        in_specs=[pl.BlockSpec((tm, D_MODEL), lambda i: (i, 0)), full(wt_hi), full(wt_lo), full(bias_col),
                  full(ustrict)],
        out_specs=[tok, tok, tok, pl.BlockSpec((N_EXPERTS, 1), lambda i: (0, 0))],
        out_shape=[jax.ShapeDtypeStruct((TOP_K, t_all), jnp.int32), jax.ShapeDtypeStruct((TOP_K, t_all), F32),
                   jax.ShapeDtypeStruct((TOP_K, t_all), jnp.int32), jax.ShapeDtypeStruct((N_EXPERTS, 1), F32)],
        scratch_shapes=[pltpu.VMEM((N_EXPERTS, 1), F32)],
        compiler_params=_cparams(("arbitrary",)),
        name="route",
    )(h2, wt_hi, wt_lo, bias_col, ustrict)


def _slots_kernel(eidx_ref, pos_ref, pstart_ref, slotx_ref, sloty_ref):
    tm = eidx_ref.shape[1]
    iota_e = lax.broadcasted_iota(jnp.int32, (N_EXPERTS, tm), 0)
    for c, out in enumerate((slotx_ref, sloty_ref)):
        rows = []
        for k in range(TOP_K):
            onehot = iota_e == eidx_ref[k:k + 1, :]
            rows.append(jnp.sum(jnp.where(onehot, pstart_ref[:, c:c + 1], 0.0), axis=0, keepdims=True))
        out[...] = jnp.concatenate(rows, axis=0).astype(jnp.int32) + pos_ref[...]


def _slots(eidx, pos, pstart_cols, tm):
    t_all = eidx.shape[1]
    tok = pl.BlockSpec((TOP_K, tm), lambda i: (0, i))
    return pl.pallas_call(
        _slots_kernel,
        grid=(t_all // tm,),
        in_specs=[tok, tok, pl.BlockSpec((N_EXPERTS, 2), lambda i: (0, 0))],
        out_specs=[tok, tok],
        out_shape=[jax.ShapeDtypeStruct((TOP_K, t_all), jnp.int32), jax.ShapeDtypeStruct((TOP_K, t_all), jnp.int32)],
        compiler_params=_cparams(("arbitrary",)),
        name="slots",
    )(eidx, pos, pstart_cols)


TD_TOK = 512
XS_ALIGN = 8


def _dispatch_kernel(fstart_ref, flen_ref, total_ref, slot_ref, h_ref, xs_out, zero_s, sem, zsem, *, xs_rows):
    i = pl.program_id(0)

    def body(j, carry):
        for k in range(TOP_K):
            pltpu.make_async_copy(h_ref.at[pl.ds(j, 1), :], xs_out.at[pl.ds(slot_ref[0, 0, j * TOP_K + k], 1), :],
                                  sem).start(priority=k % 2)
        return carry

    lax.fori_loop(0, TD_TOK, body, 0)
    for k in range(TOP_K):
        pltpu.make_async_copy(h_ref, xs_out.at[pl.ds(0, TD_TOK), :], sem).wait()

    @pl.when(i == pl.num_programs(0) - 1)
    def _():
        zero_s[...] = jnp.zeros_like(zero_s)
        total = total_ref[0]
        n_tail = (xs_rows - total) // XS_ALIGN

        def gap_copy(e, r):
            return pltpu.make_async_copy(zero_s.at[pl.ds(0, 1), :], xs_out.at[pl.ds(fstart_ref[e] + r, 1), :], zsem)

        def tail_copy(q):
            start = pl.multiple_of(total + q * XS_ALIGN, XS_ALIGN)
            return pltpu.make_async_copy(zero_s, xs_out.at[pl.ds(start, XS_ALIGN), :], zsem)

        def for_gaps(fn):
            def per_expert(e, carry):
                for r in range(XS_ALIGN - 1):
                    @pl.when(r < flen_ref[e])
                    def _():
                        fn(gap_copy(e, r))
                return carry
            lax.fori_loop(0, N_EXPERTS, per_expert, 0)

        def for_tail(fn):
            def per_q(q, carry):
                fn(tail_copy(q))
                return carry
            lax.fori_loop(0, n_tail, per_q, 0)

        for_gaps(lambda cp: cp.start())
        for_tail(lambda cp: cp.start())
        for_gaps(lambda cp: cp.wait())
        for_tail(lambda cp: cp.wait())


def _dispatch(slot3, h2, fstart, flen, total, xs_rows):
    t_all = h2.shape[0]
    grid_spec = pltpu.PrefetchScalarGridSpec(
        num_scalar_prefetch=3,
        grid=(t_all // TD_TOK,),
        in_specs=[pl.BlockSpec((1, 1, TD_TOK * TOP_K), lambda i, a, b, c: (i, 0, 0), memory_space=pltpu.SMEM),
                  pl.BlockSpec((TD_TOK, D_MODEL), lambda i, a, b, c: (i, 0))],
        out_specs=pl.BlockSpec(memory_space=pl.ANY),
        scratch_shapes=[pltpu.VMEM((XS_ALIGN, D_MODEL), F32), pltpu.SemaphoreType.DMA(()),
                        pltpu.SemaphoreType.DMA(())],
    )
    return pl.pallas_call(
        functools.partial(_dispatch_kernel, xs_rows=xs_rows),
        grid_spec=grid_spec,
        out_shape=jax.ShapeDtypeStruct((xs_rows, D_MODEL), F32),
        compiler_params=_cparams(("arbitrary",)),
        name="dispatch",
    )(fstart, flen, total, slot3, h2)


X_RING = 6
Y_RING = 3
W_BUFFERS = 3


class _SplitCopy:
    def __init__(self, parts):
        self.parts = parts

    def start(self):
        for queue, cp in enumerate(self.parts):
            cp.start(priority=queue)

    def wait(self):
        for cp in self.parts:
            cp.wait()


def _expert_kernel(bstart_ref, nblk_ref, nused_ref, xrow_ref, xs_hbm, wgu_hbm, wdn_hbm, ys_hbm,
                   xbuf, ybuf, wgu_f32, wdn_f32, wgu16, wdn16, sem_in, sem_out, sem_w, *, n_blocks):
    e = pl.program_id(0)
    nused = nused_ref[0]
    n = nblk_ref[e]
    g0 = bstart_ref[e]

    def halves(src, dst, sem, rows):
        h = rows // 2
        return _SplitCopy([pltpu.make_async_copy(src.at[pl.ds(r0, h), :], dst.at[pl.ds(r0, h), :], sem)
                           for r0 in (0, h)])

    def w_copies(ex):
        s = ex % W_BUFFERS
        return (halves(wgu_hbm.at[ex], wgu_f32.at[s], sem_w.at[0, s], D_MODEL),
                halves(wdn_hbm.at[ex], wdn_f32.at[s], sem_w.at[1, s], EXPERT_DIM))

    @pl.when(e == 0)
    def _():
        for q in range(W_BUFFERS - 1):
            for cp in w_copies(q):
                cp.start()

    @pl.when(e + (W_BUFFERS - 1) < N_EXPERTS)
    def _():
        for cp in w_copies(e + (W_BUFFERS - 1)):
            cp.start()

    def in_copy(g, s):
        start = pl.multiple_of(xrow_ref[g], XS_ALIGN)
        return halves(xs_hbm.at[pl.ds(start, MOE_BLOCK), :], xbuf.at[s], sem_in.at[s], MOE_BLOCK)

    def out_copy(g, s):
        return halves(ybuf.at[s], ys_hbm.at[pl.ds(g * MOE_BLOCK, MOE_BLOCK), :], sem_out.at[s], MOE_BLOCK)

    @pl.when(e == 0)
    def _():
        for q in range(X_RING - 1):
            @pl.when(q < nused)
            def _():
                in_copy(q, q).start()

    for cp in w_copies(e):
        cp.wait()

    @pl.when(n > 0)
    def _():
        wgu16[...] = wgu_f32[e % W_BUFFERS].astype(BF16)
        wdn16[...] = wdn_f32[e % W_BUFFERS].astype(BF16)

    def block(b, carry):
        g = g0 + b
        s = g % Y_RING
        ahead = g + (X_RING - 1)

        @pl.when(ahead < nused)
        def _():
            in_copy(ahead, ahead % X_RING).start()

        in_copy(g, g % X_RING).wait()
        gu = _dot(xbuf[g % X_RING].astype(BF16), wgu16[...])
        gate = gu[:, :EXPERT_DIM]
        act = (gate * _sigmoid(gate) * gu[:, EXPERT_DIM:]).astype(BF16)
        y = _dot(act, wdn16[...])

        @pl.when(g >= Y_RING)
        def _():
            out_copy(g - Y_RING, s).wait()

        ybuf[s] = y
        out_copy(g, s).start()
        return carry

    lax.fori_loop(0, n, block, 0)

    @pl.when(e == pl.num_programs(0) - 1)
    def _():
        for d in range(1, Y_RING + 1):
            @pl.when(nused >= d)
            def _():
                out_copy(nused - d, (nused - d) % Y_RING).wait()

        ybuf[0] = jnp.zeros((MOE_BLOCK, D_MODEL), F32)

        def tail(g, carry):
            out_copy(g, 0).start()
            out_copy(g, 0).wait()
            return carry

        lax.fori_loop(nused, n_blocks, tail, 0)


def _experts(xs, bstart, nblk, nused, xrow, w_gu, w_down):
    n_blocks = xrow.shape[0]
    grid_spec = pltpu.PrefetchScalarGridSpec(
        num_scalar_prefetch=4,
        grid=(N_EXPERTS,),
        in_specs=[pl.BlockSpec(memory_space=pl.ANY), pl.BlockSpec(memory_space=pl.ANY),
                  pl.BlockSpec(memory_space=pl.ANY)],
        out_specs=pl.BlockSpec(memory_space=pl.ANY),
        scratch_shapes=[pltpu.VMEM((X_RING, MOE_BLOCK, D_MODEL), F32),
                        pltpu.VMEM((Y_RING, MOE_BLOCK, D_MODEL), F32),
                        pltpu.VMEM((W_BUFFERS, D_MODEL, 2 * EXPERT_DIM), F32),
                        pltpu.VMEM((W_BUFFERS, EXPERT_DIM, D_MODEL), F32),
                        pltpu.VMEM((D_MODEL, 2 * EXPERT_DIM), BF16),
                        pltpu.VMEM((EXPERT_DIM, D_MODEL), BF16),
                        pltpu.SemaphoreType.DMA((X_RING,)), pltpu.SemaphoreType.DMA((Y_RING,)),
                        pltpu.SemaphoreType.DMA((2, W_BUFFERS))],
    )
    return pl.pallas_call(
        functools.partial(_expert_kernel, n_blocks=n_blocks),
        grid_spec=grid_spec,
        out_shape=jax.ShapeDtypeStruct((n_blocks * MOE_BLOCK, D_MODEL), F32),
        compiler_params=_cparams(("arbitrary",)),
        name="experts",
    )(bstart, nblk, nused, xrow, xs, w_gu, w_down)


TC_TOK = 128


def _combine_kernel(slot_ref, slot_next_ref, wt_ref, x1_ref, h2_ref, g2_ref, sgu_ref, sdn_ref, lg_ref, lb_ref,
                    y_hbm, oc_ref, ol_ref, gbuf, sems, *, n_tiles, ctx_tiles, alpha):
    i = pl.program_id(0)

    def row_copy(sref, b, j, k):
        return pltpu.make_async_copy(y_hbm.at[pl.ds(sref[0, 0, j * TOP_K + k], 1), :],
                                     gbuf.at[b, k, pl.ds(j, 1), :], sems.at[b])

    def wait_tile(b):
        for k in range(TOP_K):
            pltpu.make_async_copy(y_hbm.at[pl.ds(0, TC_TOK), :], gbuf.at[b, k], sems.at[b]).wait()

    @pl.when(i == 0)
    def _():
        def body(j, carry):
            for k in range(TOP_K):
                row_copy(slot_ref, 0, j, k).start(priority=k % 2)
            return carry
        lax.fori_loop(0, TC_TOK, body, 0)

    b = i % 2
    nb = 1 - b

    def issue_next(j0, j1):
        for j in range(j0, j1):
            for k in range(TOP_K):
                row_copy(slot_next_ref, nb, j, k).start(priority=k % 2)

    issue_next(0, TC_TOK // 2)
    h = h2_ref[...].astype(BF16)
    su = _dot(h, sgu_ref[...])
    sg = su[:, :EXPERT_DIM]
    act = (sg * _sigmoid(sg) * su[:, EXPERT_DIM:]).astype(BF16)
    ffn = _dot(act, sdn_ref[...])
    wait_tile(b)
    issue_next(TC_TOK // 2, TC_TOK)
    for k in range(TOP_K):
        ffn = ffn + gbuf[b, k] * wt_ref[:, k:k + 1]
    out = _layer_norm(alpha * x1_ref[...] + g2_ref[0] * ffn, lg_ref[...], lb_ref[...])

    @pl.when(i < ctx_tiles)
    def _():
        oc_ref[...] = out

    @pl.when(i >= ctx_tiles)
    def _():
        ol_ref[...] = out

    @pl.when(i == n_tiles - 1)
    def _():
        wait_tile(nb)


def _combine(slot3, wts_t, x1, h2, g2, sh_gu, sh_dn, ln_g, ln_b, y_sorted, row_of_tile, alpha, tok_ctx):
    t_all = x1.shape[0]
    n_tiles = t_all // TC_TOK
    ctx_tiles = tok_ctx // TC_TOK
    tile = lambda w: pl.BlockSpec((TC_TOK, w), lambda i: (i, 0))
    full = lambda a: pl.BlockSpec(a.shape, lambda i, _n=a.ndim: (0,) * _n)
    return pl.pallas_call(
        functools.partial(_combine_kernel, n_tiles=n_tiles, ctx_tiles=ctx_tiles, alpha=alpha),
        grid=(n_tiles,),
        in_specs=[pl.BlockSpec((1, 1, TC_TOK * TOP_K), lambda i: (i, 0, 0), memory_space=pltpu.SMEM),
                  pl.BlockSpec((1, 1, TC_TOK * TOP_K), lambda i: (jnp.minimum(i + 1, n_tiles - 1), 0, 0),
                               memory_space=pltpu.SMEM),
                  tile(TOP_K), tile(D_MODEL), tile(D_MODEL),
                  pl.BlockSpec((1, 1, D_MODEL), lambda i: (row_of_tile(i), 0, 0)),
                  full(sh_gu), full(sh_dn), full(ln_g), full(ln_b),
                  pl.BlockSpec(memory_space=pl.ANY)],
        out_specs=_two_source_specs(TC_TOK, D_MODEL, ctx_tiles),
        out_shape=[jax.ShapeDtypeStruct((tok_ctx, D_MODEL), F32),
                   jax.ShapeDtypeStruct((t_all - tok_ctx, D_MODEL), F32)],
        scratch_shapes=[pltpu.VMEM((2, TOP_K, TC_TOK, D_MODEL), F32), pltpu.SemaphoreType.DMA((2,))],
        compiler_params=_cparams(("arbitrary",)),
        name="combine",
    )(slot3, slot3, wts_t, x1, h2, g2, sh_gu, sh_dn, ln_g, ln_b, y_sorted)


def kernel(x_prompt, x_sample, state_rwkv, c, c_ctx, w_ada, b_ada, w_in, mu_shift, w_decay0, w_decay2, w_a0, w_a2, w_g2, k_k, k_a, r_k, lnx_g, lnx_b, conv_w, conv_b, conv_ln_g, conv_ln_b, w_out, ln1_g, ln1_b, router_w, router_bias, expert_w_gu, expert_w_down, shared_w_gu, shared_w_down, ln2_g, ln2_b):
    depth = w_ada.shape[0]
    assert depth == 1
    alpha = (2.0 * depth) ** 0.25
    n_ctx, t_ctx, _ = x_prompt.shape
    n_lat, t_lat, _ = x_sample.shape
    tok_ctx = n_ctx * t_ctx
    tok_lat = n_lat * t_lat
    t_all = tok_ctx + tok_lat
    tm = 1024
    assert tok_ctx % tm == 0 and t_lat % tm == 0 and t_lat // GRID_W * GRID_W == t_lat
    l = 0

    x_ctx = x_prompt.reshape(tok_ctx, D_MODEL)
    x_lat = x_sample.reshape(tok_lat, D_MODEL)

    cond8 = jnp.zeros((8, D_MODEL), F32).at[0].set(c_ctx).at[1:1 + n_lat].set(c)
    mod = _ada(cond8, w_ada[l], b_ada[l])
    sh1, sc1, g1, sh2, sc2, g2 = [m.reshape(8, 1, D_MODEL) for m in jnp.split(mod, 6, axis=-1)]

    def row_of_tile_for(tile_rows):
        ctx_tiles = tok_ctx // tile_rows
        per_seq = t_lat // tile_rows
        return lambda i: jnp.where(i < ctx_tiles, 0, 1 + (i - ctx_tiles) // per_seq)

    w_in_a = w_in[l][:, :RWKV_IN].astype(BF16)
    w_in_b = w_in[l][:, RWKV_IN:].astype(BF16)
    proj_a, proj_b = _in_proj(x_ctx, x_lat, sc1, sh1, w_in_a, w_in_b, tm, row_of_tile_for(tm))

    zeros64 = jnp.zeros((64, 2 * RW), F32)
    wd2 = jnp.concatenate([w_decay2[l, 0], w_decay2[l, 1]], axis=1)
    wa2 = jnp.concatenate([w_a2[l, 0], w_a2[l, 1]], axis=1)
    w_lora = jnp.concatenate([jnp.concatenate([wd2, zeros64], axis=1),
                              jnp.concatenate([zeros64, wa2], axis=1)], axis=0).astype(BF16)
    row = lambda v: v.reshape(1, -1)
    rwkv_w = [row(mu_shift[l]), w_lora, w_g2[l].astype(BF16), row(w_decay0[l]), row(w_a0[l]),
              row(k_k[l]), row(k_a[l]), row(r_k[l]), row(lnx_g[l]), row(lnx_b[l])]
    consts = _rwkv_consts()

    def to_cat(s):
        b = s.shape[0]
        s = s.reshape(b, 2, 2, 4, HD, HD)
        return jnp.transpose(s, (0, 1, 2, 5, 3, 4)).reshape(b, 4, HD, GW)

    s0_ctx = jnp.zeros((n_ctx, 4, HD, GW), F32)
    s0_lat = to_cat(state_rwkv[:, l].astype(F32))
    ya_c, sfin_ctx = _rwkv(proj_a, s0_ctx, rwkv_w, consts, t_ctx, n_ctx, 0)
    ya_l, _ = _rwkv(proj_a, s0_lat, rwkv_w, consts, t_lat, n_lat, tok_ctx // t_lat)
    sfin = jnp.transpose(sfin_ctx.reshape(n_ctx, 2, 2, HD, 4, HD), (0, 1, 2, 4, 5, 3))
    new_state = sfin.reshape(n_ctx, 1, 2, 2 * 4, HD, HD).astype(x_prompt.dtype)

    conv_wts = [conv_w[l], row(conv_b[l]), row(conv_ln_g[l]), row(conv_ln_b[l])]
    yb_c = _conv(proj_b, conv_wts, t_ctx, n_ctx, 0, False)
    yb_l = _conv(proj_b, conv_wts, t_lat, n_lat, tok_ctx // t_lat, True)

    w_oa = w_out[l][:RW].astype(BF16)
    w_ob = w_out[l][RW:].astype(BF16)
    x1, h2 = _out_proj(ya_c, ya_l, yb_c, yb_l, x_ctx, x_lat, g1, sc2, sh2, w_oa, w_ob, row(ln1_g[l]), row(ln1_b[l]), tm,
                       row_of_tile_for(tm), alpha)

    rwt = router_w[l].T
    rwt_hi = rwt.astype(BF16)
    rwt_lo = (rwt - rwt_hi.astype(F32)).astype(BF16)
    eidx, wts, pos, counts = _route(h2, rwt_hi, rwt_lo, router_bias[l].reshape(N_EXPERTS, 1), 512)

    n_assign = t_all * TOP_K
    n_blocks = (n_assign + N_EXPERTS * (MOE_BLOCK - 1) + MOE_BLOCK - 1) // MOE_BLOCK
    cnt = counts.reshape(N_EXPERTS).astype(jnp.int32)
    padded = (cnt + MOE_BLOCK - 1) // MOE_BLOCK * MOE_BLOCK
    pend = jnp.cumsum(padded)
    pstart = pend - padded
    padded_x = (cnt + XS_ALIGN - 1) // XS_ALIGN * XS_ALIGN
    pend_x = jnp.cumsum(padded_x)
    pstart_x = pend_x - padded_x
    xs_rows = n_assign + N_EXPERTS * (XS_ALIGN - 1) // XS_ALIGN * XS_ALIGN + MOE_BLOCK
    slot_x, slot_y = _slots(eidx, pos, jnp.stack([pstart_x, pstart], axis=1).astype(F32), 512)
    per_tile = lambda s, tok: s.T.reshape(t_all // tok, 1, tok * TOP_K)
    xs = _dispatch(per_tile(slot_x, TD_TOK), h2, (pstart_x + cnt).astype(jnp.int32), (padded_x - cnt).astype(jnp.int32),
                   pend_x[-1].astype(jnp.int32).reshape(1), xs_rows)
    bstart = (pstart // MOE_BLOCK).astype(jnp.int32)
    nblk = (padded // MOE_BLOCK).astype(jnp.int32)
    nused = (pend[-1] // MOE_BLOCK).astype(jnp.int32).reshape(1)
    blk_ids = jnp.arange(n_blocks, dtype=jnp.int32)
    blk_e = jnp.minimum(jnp.sum((pend[None, :] <= (blk_ids * MOE_BLOCK)[:, None]).astype(jnp.int32), axis=1),
                        N_EXPERTS - 1)
    shift = jnp.sum(jnp.where(blk_e[:, None] == jnp.arange(N_EXPERTS, dtype=jnp.int32)[None, :],
                              (pstart_x - pstart)[None, :], 0), axis=1)
    xrow = jnp.clip(blk_ids * MOE_BLOCK + shift, 0, xs_rows - MOE_BLOCK).astype(jnp.int32)
    y_sorted = _experts(xs, bstart, nblk, nused, xrow, expert_w_gu[l], expert_w_down[l])
    y_ctx, y_lat = _combine(per_tile(slot_y, TC_TOK), wts.T, x1, h2, g2, shared_w_gu[l].astype(BF16),
                            shared_w_down[l].astype(BF16), row(ln2_g[l]), row(ln2_b[l]), y_sorted,
                            row_of_tile_for(TC_TOK), alpha, tok_ctx)
    return (y_ctx.reshape(n_ctx, t_ctx, D_MODEL), y_lat.reshape(n_lat, t_lat, D_MODEL), new_state)
```

```python
import functools
import math

import numpy as np
import jax
import jax.numpy as jnp
from jax import lax
from jax.experimental import pallas as pl
from jax.experimental.pallas import tpu as pltpu

F32 = jnp.float32
BF16 = jnp.bfloat16

D_MODEL = 1024
RW = 512
HD = 64
CW = 512
CONV_K = 31
RWKV_IN = 3 * RW + 64 + 64 + 128
N_EXPERTS = 256
TOP_K = 8
N_GROUPS = 8
TOPK_GROUPS = 4
EXPERT_DIM = 256
ROUTED_SCALE = 2.5
MOE_BLOCK = 128
GRID_W = 64
LN_EPS = 1e-5
GN_EPS = 64e-5
CHUNK = 64
GW = 256
SCAN_UNROLL = 2
DECAY_SCALE = math.exp(-0.5)
VMEM_LIMIT = 56 * 1024 * 1024


def _cparams(sem):
    return pltpu.CompilerParams(dimension_semantics=sem, vmem_limit_bytes=VMEM_LIMIT)


def _split_bf16(x):
    hi = x.astype(BF16)
    lo = (x - hi.astype(F32)).astype(BF16)
    return hi, lo


def _dot(a, b):
    return jnp.dot(a, b, preferred_element_type=F32)


def _dot_nt(a, b):
    return lax.dot_general(a, b, (((1,), (1,)), ((), ())), preferred_element_type=F32)


def _dot_tn(a, b):
    return lax.dot_general(a, b, (((0,), (0,)), ((), ())), preferred_element_type=F32)


def _sigmoid(x):
    return 1.0 / (1.0 + jnp.exp(-x))


def _layer_norm(x, g, b):
    mu = jnp.mean(x, axis=-1, keepdims=True)
    d = x - mu
    var = jnp.mean(d * d, axis=-1, keepdims=True)
    return d * lax.rsqrt(var + LN_EPS) * g + b


def _ada_kernel(c_ref, w_ref, b_ref, o_ref):
    c = c_ref[...]
    s = c * _sigmoid(c)
    s_hi, s_lo = _split_bf16(s)
    w_hi, w_lo = _split_bf16(w_ref[...])
    o_ref[...] = _dot(s_hi, w_hi) + _dot(s_lo, w_hi) + _dot(s_hi, w_lo) + b_ref[...]


def _ada(cond8, w_ada, b_ada):
    n = w_ada.shape[1]
    tn = 1536
    return pl.pallas_call(
        _ada_kernel,
        grid=(n // tn,),
        in_specs=[pl.BlockSpec((8, D_MODEL), lambda j: (0, 0)),
                  pl.BlockSpec((D_MODEL, tn), lambda j: (0, j)),
                  pl.BlockSpec((1, tn), lambda j: (0, j))],
        out_specs=pl.BlockSpec((8, tn), lambda j: (0, j)),
        out_shape=jax.ShapeDtypeStruct((8, n), F32),
        compiler_params=_cparams(("arbitrary",)),
        name="ada",
    )(cond8, w_ada, b_ada.reshape(1, n))


def _two_source_specs(tm, width, ctx_tiles):
    return [pl.BlockSpec((tm, width), lambda i: (jnp.minimum(i, ctx_tiles - 1), 0)),
            pl.BlockSpec((tm, width), lambda i: (jnp.maximum(i - ctx_tiles, 0), 0))]


def _inproj_kernel(xc_ref, xl_ref, sc_ref, sh_ref, wa_ref, wb_ref, oa_ref, ob_ref, *, ctx_tiles):
    x = jnp.where(pl.program_id(0) < ctx_tiles, xc_ref[...], xl_ref[...])
    h = (x * (1.0 + sc_ref[0]) + sh_ref[0]).astype(BF16)
    oa_ref[...] = _dot(h, wa_ref[...])
    ob_ref[...] = _dot(h, wb_ref[...])


def _in_proj(x_ctx, x_lat, sc1, sh1, w_a, w_b, tm, row_of_tile):
    t_all = x_ctx.shape[0] + x_lat.shape[0]
    ctx_tiles = x_ctx.shape[0] // tm
    return pl.pallas_call(
        functools.partial(_inproj_kernel, ctx_tiles=ctx_tiles),
        grid=(t_all // tm,),
        in_specs=_two_source_specs(tm, D_MODEL, ctx_tiles) + [
                  pl.BlockSpec((1, 1, D_MODEL), lambda i: (row_of_tile(i), 0, 0)),
                  pl.BlockSpec((1, 1, D_MODEL), lambda i: (row_of_tile(i), 0, 0)),
                  pl.BlockSpec(w_a.shape, lambda i: (0, 0)),
                  pl.BlockSpec(w_b.shape, lambda i: (0, 0))],
        out_specs=[pl.BlockSpec((tm, RWKV_IN), lambda i: (i, 0)),
                   pl.BlockSpec((tm, 2 * CW), lambda i: (i, 0))],
        out_shape=[jax.ShapeDtypeStruct((t_all, RWKV_IN), F32),
                   jax.ShapeDtypeStruct((t_all, 2 * CW), F32)],
        compiler_params=_cparams(("arbitrary",)),
        name="in_proj",
    )(x_ctx, x_lat, sc1, sh1, w_a, w_b)


def _rwkv_consts():
    c = CHUNK
    t = np.arange(c)[:, None]
    s = np.arange(GW)[None, :] % c
    sl = (s < t).astype(np.float32)
    il = (s <= t).astype(np.float32)
    su = (s > t).astype(np.float32)
    iu = (s >= t).astype(np.float32)
    eye_cat = (s == t).astype(np.float32)
    tri = np.stack([sl, il, su, iu, eye_cat])
    r = np.arange(GW)
    mask_bd = (r[:, None] // HD == r[None, :] // HD).astype(np.float32)
    eye = np.eye(GW, dtype=np.float32)
    cum = np.stack([np.tril(np.ones((c, c), np.float32)), np.triu(np.ones((c, c), np.float32))])
    q = np.arange(RW)
    seg = (q[:, None] // HD == q[None, :] // HD).astype(np.float32)
    return (jnp.asarray(tri), jnp.asarray(mask_bd, dtype=BF16), jnp.asarray(mask_bd), jnp.asarray(eye),
            jnp.asarray(cum, dtype=BF16), jnp.asarray(seg, dtype=BF16))


def _rwkv_kernel(p_ref, s0_ref, mu_ref, wlora_ref, wg2_ref, wd0_ref, wa0_ref, kk_ref, ka_ref, rk_ref,
                 lng_ref, lnb_ref, tri_ref, mbd16_ref, mbd_ref, eye_ref, cum_ref, seg_ref,
                 ya_ref, sfin_ref,
                 r_s, v_s, kk_s, g_s, bon_s, a_s, lw_s, kd_s, y_s, st_s, *, seq_len):
    nc = seq_len // CHUNK
    scan_unroll = SCAN_UNROLL
    seg = seg_ref[...]

    def seg_sum(x):
        hi, lo = _split_bf16(x)
        return _dot(hi, seg) + _dot(lo, seg)

    def phase_a(c, carry):
        t0 = pl.multiple_of(c * CHUNK, CHUNK)
        cur = p_ref[pl.ds(t0, CHUNK), :]
        prev_row = p_ref[pl.ds(jnp.maximum(t0 - 1, 0), 1), :] * (t0 > 0).astype(F32)
        next_row = p_ref[pl.ds(jnp.minimum(t0 + CHUNK, seq_len - 1), 1), :] * (t0 + CHUNK < seq_len).astype(F32)
        row = lax.broadcasted_iota(jnp.int32, cur.shape, 0)
        prev = jnp.where(row == 0, prev_row, pltpu.roll(cur, 1, 0))
        nxt = jnp.where(row == CHUNK - 1, next_row, pltpu.roll(cur, CHUNK - 1, 0))
        p = cur + mu_ref[...] * (0.5 * (prev + nxt) - cur)
        r = p[:, 0:RW]
        k = p[:, RW:2 * RW]
        v = p[:, 2 * RW:3 * RW]
        z = p[:, 3 * RW:3 * RW + 128]
        gd = p[:, 3 * RW + 128:3 * RW + 256]
        lane = lax.broadcasted_iota(jnp.int32, z.shape, 1)
        z = jnp.where(lane < 64, jnp.tanh(z), z)
        lora = _dot(z.astype(BF16), wlora_ref[...])
        g = _dot(_sigmoid(gd).astype(BF16), wg2_ref[...])
        kks = k * kk_ref[...]
        kk = kks * lax.rsqrt(seg_sum(kks * kks) + 1e-12)
        rows = pl.ds(t0, CHUNK)
        r_s[rows, :] = r
        v_s[rows, :] = v.astype(BF16)
        kk_s[rows, :] = kk
        g_s[rows, :] = g
        kd_sum = jnp.zeros_like(k)
        for d in range(2):
            w_logit = wd0_ref[:, d * RW:(d + 1) * RW] + lora[:, d * RW:(d + 1) * RW]
            a = _sigmoid(wa0_ref[:, d * RW:(d + 1) * RW] + lora[:, (2 + d) * RW:(3 + d) * RW])
            kd = k * (1.0 + (a - 1.0) * ka_ref[...])
            lw_s[d, rows, :] = -DECAY_SCALE * _sigmoid(w_logit)
            a_s[d, rows, :] = a
            kd_s[d, rows, :] = kd
            kd_sum = kd_sum + kd
        bon_s[rows, :] = seg_sum(r * kd_sum * rk_ref[...]) * v
        return carry

    lax.fori_loop(0, nc, phase_a, 0, unroll=2)

    mbd16 = mbd16_ref[...]
    mbd = mbd_ref[...]
    eye = eye_ref[...]

    def bd(x16):
        return jnp.concatenate([x16, x16, x16, x16], axis=0) * mbd16

    for u in range(4):
        s0 = s0_ref[0, u]
        st_s[u] = jnp.concatenate([s0, s0, s0, s0], axis=0) * mbd

    def prep(d, grp, t0):
        rows = pl.ds(t0, CHUNK)
        lanes = slice(grp * GW, (grp + 1) * GW)
        lw = lw_s[d, rows, lanes]
        hi, lo = _split_bf16(lw)
        cl = _dot(cum_ref[d], hi) + _dot(cum_ref[d], lo)
        e_cl = jnp.exp(cl)
        e_ce = jnp.exp(cl - lw)
        e_ncl = jnp.exp(-cl)
        tot = cl[CHUNK - 1:CHUNK, :] if d == 0 else cl[0:1, :]
        wc = jnp.exp(tot)
        kk = kk_s[rows, lanes]
        rt = r_s[rows, lanes] * e_cl
        bt = a_s[d, rows, lanes] * kk * e_ncl
        kt = kd_s[d, rows, lanes] * e_ncl
        kq16 = (kk * e_ce).astype(BF16)
        return dict(d=d, u=d * 2 + grp, rows=rows, lanes=lanes, wc=wc, rt=rt, kq16=kq16,
                    v16=v_s[rows, lanes], bt16=bt.astype(BF16), kt16=kt.astype(BF16),
                    btw16=(bt * wc).astype(BF16), ktw16=(kt * wc).astype(BF16),
                    g16=jnp.concatenate([kq16, rt.astype(BF16)], axis=0))

    def phase_b(i2, carry):
        us = []
        for j in range(scan_unroll):
            i = i2 * scan_unroll + j
            tf = pl.multiple_of(i * CHUNK, CHUNK)
            tb = pl.multiple_of((nc - 1 - i) * CHUNK, CHUNK)
            us += [prep(0, 0, tf), prep(1, 0, tb), prep(0, 1, tf), prep(1, 1, tb)]
        each = lambda fn: [fn(q) for q in us]
        strict = lambda q: tri_ref[2 * q['d']]
        incl = lambda q: tri_ref[2 * q['d'] + 1]
        out_b = each(lambda q: _dot_nt(q['g16'], bd(q['bt16'])))
        out_k = each(lambda q: _dot_nt(q['g16'], bd(q['kt16'])))
        a_bk = [o[:CHUNK] * strict(q) for o, q in zip(out_b, us)]
        p_rb = [(o[CHUNK:] * incl(q)).astype(BF16) for o, q in zip(out_b, us)]
        a_kk = [(o[:CHUNK] * strict(q)).astype(BF16) for o, q in zip(out_k, us)]
        p_rk = [(o[CHUNK:] * incl(q)).astype(BF16) for o, q in zip(out_k, us)]
        tinv = [tri_ref[4] - a for a in a_bk]
        pw = [a.astype(BF16) for a in a_bk]
        for _ in range(5):
            pw = [_dot(p, bd(p)).astype(BF16) for p in pw]
            tinv = [t + _dot(p, bd(t.astype(BF16))) for t, p in zip(tinv, pw)]
        bdv = each(lambda q: bd(q['v16']))
        akkv = [_dot(a, b) for a, b in zip(a_kk, bdv)]
        x = [_dot(t.astype(BF16), jnp.concatenate([bd(q['kq16']), bd(k.astype(BF16))], axis=1))
             for t, q, k in zip(tinv, us, akkv)]
        qu16 = [z.astype(BF16) for z in x]
        tn1 = [_dot_tn(q['btw16'], z) for q, z in zip(us, qu16)]
        tn2 = each(lambda q: _dot_tn(q['ktw16'], q['v16']))
        r1 = [_dot(p, jnp.concatenate([bd(z[:, :GW]), bd(z[:, GW:])], axis=1)) for p, z in zip(p_rb, qu16)]
        y0 = [_dot(p, b) - r[:, GW:] for p, b, r in zip(p_rk, bdv, r1)]
        for q, t1, t2, r, y in zip(us, tn1, tn2, r1, y0):
            mc = (eye * q['wc'] - t1[:, :GW]) * mbd
            ncm = (t2 - t1[:, GW:]) * mbd
            rh = q['rt'] - r[:, :GW]
            st16 = st_s[q['u']].astype(BF16)
            big = _dot(jnp.concatenate([rh.astype(BF16), mc.astype(BF16)], axis=0), st16)
            y_s[q['d'], q['rows'], q['lanes']] = big[:CHUNK] + y
            st_s[q['u']] = big[CHUNK:] + ncm
        return carry

    lax.fori_loop(0, nc // scan_unroll, phase_b, 0)

    for u in range(4):
        st = st_s[u]
        sfin_ref[0, u] = st[0:HD] + st[HD:2 * HD] + st[2 * HD:3 * HD] + st[3 * HD:4 * HD]

    def phase_c(c, carry):
        rows = pl.ds(pl.multiple_of(c * CHUNK, CHUNK), CHUNK)
        y = y_s[0, rows, :] + y_s[1, rows, :]
        mean = seg_sum(y) * (1.0 / HD)
        dlt = y - mean
        var = seg_sum(dlt * dlt) * (1.0 / HD)
        yn = dlt * lax.rsqrt(var + GN_EPS)
        ya_ref[rows, :] = (yn * lng_ref[...] + lnb_ref[...] + bon_s[rows, :]) * g_s[rows, :]
        return carry

    lax.fori_loop(0, nc, phase_c, 0, unroll=2)


def _rwkv(proj_a, s0_cat, wts, consts, seq_len, n_seq, first_block):
    full = lambda a: pl.BlockSpec(a.shape, lambda s, _n=a.ndim: (0,) * _n)
    p_mode = dict(pipeline_mode=pl.Buffered(1)) if n_seq <= 2 else {}
    in_specs = [pl.BlockSpec((seq_len, RWKV_IN), lambda s: (first_block + s, 0), **p_mode),
                pl.BlockSpec((1, 4, HD, GW), lambda s: (s, 0, 0, 0))]
    in_specs += [full(a) for a in wts] + [full(a) for a in consts]
    args = [proj_a, s0_cat] + list(wts) + list(consts)
    seq = (seq_len, RW)
    return pl.pallas_call(
        functools.partial(_rwkv_kernel, seq_len=seq_len),
        grid=(n_seq,),
        in_specs=in_specs,
        out_specs=[pl.BlockSpec((seq_len, RW), lambda s: (s, 0)),
                   pl.BlockSpec((1, 4, HD, GW), lambda s: (s, 0, 0, 0))],
        out_shape=[jax.ShapeDtypeStruct((n_seq * seq_len, RW), F32),
                   jax.ShapeDtypeStruct((n_seq, 4, HD, GW), F32)],
        scratch_shapes=[pltpu.VMEM(seq, F32), pltpu.VMEM(seq, BF16), pltpu.VMEM(seq, F32), pltpu.VMEM(seq, F32),
                        pltpu.VMEM(seq, F32),
                        pltpu.VMEM((2,) + seq, F32), pltpu.VMEM((2,) + seq, F32), pltpu.VMEM((2,) + seq, F32),
                        pltpu.VMEM((2,) + seq, F32), pltpu.VMEM((4, GW, GW), F32)],
        compiler_params=_cparams(("arbitrary",)),
        name="rwkv_%d" % seq_len,
    )(*args)


CONV_PAD = 16
VPAD = (CONV_K // 2) * GRID_W


def _conv_kernel(u_ref, w_ref, b_ref, g_ref, beta_ref, o_ref, pad_s, vpad_s, y_s, *, seq_len, grid):
    rb = 256
    glu = u_ref[:, :CW] * _sigmoid(u_ref[:, CW:])
    zeros = jnp.zeros((CONV_PAD, CW), F32)
    pad_s[0:CONV_PAD, :] = zeros
    pad_s[CONV_PAD + seq_len:2 * CONV_PAD + seq_len, :] = zeros
    pad_s[CONV_PAD:CONV_PAD + seq_len, :] = glu
    half = CW // 2
    if grid:
        vz = jnp.zeros((VPAD, half), F32)
        vpad_s[0:VPAD, :] = vz
        vpad_s[VPAD + seq_len:2 * VPAD + seq_len, :] = vz
        vpad_s[VPAD:VPAD + seq_len, :] = glu[:, half:]
    n_h_lanes = (half if grid else CW) // 128
    for r0 in range(0, seq_len, rb):
        for lb in range(n_h_lanes):
            lanes = slice(lb * 128, (lb + 1) * 128)
            acc = jnp.zeros((rb, 128), F32)
            if grid:
                col = lax.broadcasted_iota(jnp.int32, (rb, 128), 0) % GRID_W
            span = rb + 2 * CONV_PAD
            xpad = pad_s[r0:r0 + span, lanes]
            for res in range(8):
                xrot = xpad if res == 0 else pltpu.roll(xpad, span - res, 0)
                for k in range(CONV_K):
                    if (k + 1) % 8 != res:
                        continue
                    off = (k + 1) - res
                    term = xrot[off:off + rb] * w_ref[k:k + 1, lanes]
                    if grid:
                        src = col + (k - CONV_K // 2)
                        term = jnp.where((src >= 0) & (src < GRID_W), term, 0.0)
                    acc = acc + term
            y_s[r0:r0 + rb, lanes] = acc
        if grid:
            for lb in range(half // 128):
                lanes = slice(lb * 128, (lb + 1) * 128)
                wl = slice(half + lb * 128, half + (lb + 1) * 128)
                acc = jnp.zeros((rb, 128), F32)
                for k in range(CONV_K):
                    start = VPAD + r0 + (k - CONV_K // 2) * GRID_W
                    acc = acc + vpad_s[start:start + rb, lanes] * w_ref[k:k + 1, wl]
                y_s[r0:r0 + rb, wl] = acc
    for r0 in range(0, seq_len, rb):
        y = y_s[r0:r0 + rb, :] + b_ref[...]
        yl = _layer_norm(y, g_ref[...], beta_ref[...])
        o_ref[r0:r0 + rb, :] = yl * _sigmoid(yl)


def _conv(proj_b, wts, seq_len, n_seq, first_block, grid):
    full = lambda a: pl.BlockSpec(a.shape, lambda s, _n=a.ndim: (0,) * _n)
    in_specs = [pl.BlockSpec((seq_len, 2 * CW), lambda s: (first_block + s, 0))] + [full(a) for a in wts]
    args = [proj_b] + list(wts)
    return pl.pallas_call(
        functools.partial(_conv_kernel, seq_len=seq_len, grid=grid),
        grid=(n_seq,),
        in_specs=in_specs,
        out_specs=pl.BlockSpec((seq_len, CW), lambda s: (s, 0)),
        out_shape=jax.ShapeDtypeStruct((n_seq * seq_len, CW), F32),
        scratch_shapes=[pltpu.VMEM((seq_len + 2 * CONV_PAD, CW), F32),
                        pltpu.VMEM((seq_len + 2 * VPAD if grid else 8, CW // 2), F32),
                        pltpu.VMEM((seq_len, CW), F32)],
        compiler_params=_cparams(("arbitrary",)),
        name="conv_%d" % seq_len,
    )(*args)


def _outproj_kernel(yac_ref, yal_ref, ybc_ref, ybl_ref, xc_ref, xl_ref, g1_ref, sc2_ref, sh2_ref, wa_ref, wb_ref,
                    lg_ref, lb_ref, x1_ref, h2_ref, *, alpha, ctx_tiles):
    is_ctx = pl.program_id(0) < ctx_tiles
    ya = jnp.where(is_ctx, yac_ref[...], yal_ref[...])
    yb = jnp.where(is_ctx, ybc_ref[...], ybl_ref[...])
    x = jnp.where(is_ctx, xc_ref[...], xl_ref[...])
    mix = _dot(ya.astype(BF16), wa_ref[...]) + _dot(yb.astype(BF16), wb_ref[...])
    x1 = _layer_norm(alpha * x + g1_ref[0] * mix, lg_ref[...], lb_ref[...])
    x1_ref[...] = x1
    h2_ref[...] = x1 * (1.0 + sc2_ref[0]) + sh2_ref[0]


def _out_proj(ya_c, ya_l, yb_c, yb_l, x_ctx, x_lat, g1, sc2, sh2, w_oa, w_ob, ln_g, ln_b, tm, row_of_tile, alpha):
    t_all = x_ctx.shape[0] + x_lat.shape[0]
    ctx_tiles = x_ctx.shape[0] // tm
    tile = lambda w: pl.BlockSpec((tm, w), lambda i: (i, 0))
    modrow = pl.BlockSpec((1, 1, D_MODEL), lambda i: (row_of_tile(i), 0, 0))
    full = lambda a: pl.BlockSpec(a.shape, lambda i, _n=a.ndim: (0,) * _n)
    return pl.pallas_call(
        functools.partial(_outproj_kernel, alpha=alpha, ctx_tiles=ctx_tiles),
        grid=(t_all // tm,),
        in_specs=_two_source_specs(tm, RW, ctx_tiles) + _two_source_specs(tm, CW, ctx_tiles)
        + _two_source_specs(tm, D_MODEL, ctx_tiles)
        + [modrow, modrow, modrow, full(w_oa), full(w_ob), full(ln_g), full(ln_b)],
        out_specs=[tile(D_MODEL), tile(D_MODEL)],
        out_shape=[jax.ShapeDtypeStruct((t_all, D_MODEL), F32), jax.ShapeDtypeStruct((t_all, D_MODEL), F32)],
        compiler_params=_cparams(("arbitrary",)),
        name="out_proj",
    )(ya_c, ya_l, yb_c, yb_l, x_ctx, x_lat, g1, sc2, sh2, w_oa, w_ob, ln_g, ln_b)


def _route_kernel(h_ref, wt_hi_ref, wt_lo_ref, bias_ref, ustrict_ref,
                  eidx_ref, wts_ref, pos_ref, cnt_ref, carry_s, *, tm):
    i = pl.program_id(0)

    @pl.when(i == 0)
    def _():
        carry_s[...] = jnp.zeros_like(carry_s)

    h_hi, h_lo = _split_bf16(h_ref[...])
    logits = _dot_nt(wt_hi_ref[...], h_hi) + _dot_nt(wt_hi_ref[...], h_lo) + _dot_nt(wt_lo_ref[...], h_hi)
    scores = _sigmoid(logits)
    sel = scores + bias_ref[...]
    neg = jnp.float32(-jnp.inf)
    gsz = N_EXPERTS // N_GROUPS
    gs = []
    for g in range(N_GROUPS):
        blk = sel[g * gsz:(g + 1) * gsz]
        m1 = jnp.max(blk, axis=0, keepdims=True)
        eq = blk == m1
        cnt = jnp.sum(eq.astype(F32), axis=0, keepdims=True)
        m2 = jnp.max(jnp.where(eq, neg, blk), axis=0, keepdims=True)
        gs.append(m1 + jnp.where(cnt >= 2.0, m1, m2))
    masked = []
    for g in range(N_GROUPS):
        rank = jnp.zeros_like(gs[g])
        for o in range(N_GROUPS):
            if o == g:
                continue
            beats = (gs[o] > gs[g]) | ((gs[o] == gs[g]) & (o < g))
            rank = rank + beats.astype(F32)
        keep = rank < float(TOPK_GROUPS)
        masked.append(jnp.where(keep, sel[g * gsz:(g + 1) * gsz], neg))
    cur = jnp.concatenate(masked, axis=0)
    iota_e = lax.broadcasted_iota(jnp.int32, cur.shape, 0).astype(F32)
    idxs, ws = [], []
    selmask = jnp.zeros(cur.shape, F32)
    for _ in range(TOP_K):
        m = jnp.max(cur, axis=0, keepdims=True)
        idx = jnp.min(jnp.where(cur == m, iota_e, float(N_EXPERTS)), axis=0, keepdims=True)
        onehot = iota_e == idx
        ws.append(jnp.sum(jnp.where(onehot, scores, 0.0), axis=0, keepdims=True))
        idxs.append(idx)
        cur = jnp.where(onehot, neg, cur)
        selmask = jnp.where(onehot, 1.0, selmask)
    pos = _dot(selmask.astype(BF16), ustrict_ref[...]) + carry_s[...]
    carry_s[...] = carry_s[...] + jnp.sum(selmask, axis=1, keepdims=True)
    cnt_ref[...] = carry_s[...]
    wsum = ws[0]
    for k in range(1, TOP_K):
        wsum = wsum + ws[k]
    pks = [jnp.sum(jnp.where(iota_e == idxs[k], pos, 0.0), axis=0, keepdims=True) for k in range(TOP_K)]
    eidx_ref[...] = jnp.concatenate(idxs, axis=0).astype(jnp.int32)
    wts_ref[...] = jnp.concatenate([w / wsum * ROUTED_SCALE for w in ws], axis=0)
    pos_ref[...] = jnp.concatenate(pks, axis=0).astype(jnp.int32)


def _route(h2, wt_hi, wt_lo, bias_col, tm):
    t_all = h2.shape[0]
    ustrict = jnp.asarray(np.triu(np.ones((tm, tm), np.float32), 1), dtype=BF16)
    full = lambda a: pl.BlockSpec(a.shape, lambda i, _n=a.ndim: (0,) * _n)
    tok = pl.BlockSpec((TOP_K, tm), lambda i: (0, i))
    return pl.pallas_call(
        functools.partial(_route_kernel, tm=tm),
        grid=(t_all // tm,),
        in_specs=[pl.BlockSpec((tm, D_MODEL), lambda i: (i, 0)), full(wt_hi), full(wt_lo), full(bias_col),
                  full(ustrict)],
        out_specs=[tok, tok, tok, pl.BlockSpec((N_EXPERTS, 1), lambda i: (0, 0))],
        out_shape=[jax.ShapeDtypeStruct((TOP_K, t_all), jnp.int32), jax.ShapeDtypeStruct((TOP_K, t_all), F32),
                   jax.ShapeDtypeStruct((TOP_K, t_all), jnp.int32), jax.ShapeDtypeStruct((N_EXPERTS, 1), F32)],
        scratch_shapes=[pltpu.VMEM((N_EXPERTS, 1), F32)],
        compiler_params=_cparams(("arbitrary",)),
        name="route",
    )(h2, wt_hi, wt_lo, bias_col, ustrict)


def _slots_kernel(eidx_ref, pos_ref, pstart_ref, slotx_ref, sloty_ref):
    tm = eidx_ref.shape[1]
    iota_e = lax.broadcasted_iota(jnp.int32, (N_EXPERTS, tm), 0)
    for c, out in enumerate((slotx_ref, sloty_ref)):
        rows = []
        for k in range(TOP_K):
            onehot = iota_e == eidx_ref[k:k + 1, :]
            rows.append(jnp.sum(jnp.where(onehot, pstart_ref[:, c:c + 1], 0.0), axis=0, keepdims=True))
        out[...] = jnp.concatenate(rows, axis=0).astype(jnp.int32) + pos_ref[...]


def _slots(eidx, pos, pstart_cols, tm):
    t_all = eidx.shape[1]
    tok = pl.BlockSpec((TOP_K, tm), lambda i: (0, i))
    return pl.pallas_call(
        _slots_kernel,
        grid=(t_all // tm,),
        in_specs=[tok, tok, pl.BlockSpec((N_EXPERTS, 2), lambda i: (0, 0))],
        out_specs=[tok, tok],
        out_shape=[jax.ShapeDtypeStruct((TOP_K, t_all), jnp.int32), jax.ShapeDtypeStruct((TOP_K, t_all), jnp.int32)],
        compiler_params=_cparams(("arbitrary",)),
        name="slots",
    )(eidx, pos, pstart_cols)


TD_TOK = 512
XS_ALIGN = 8


def _dispatch_kernel(fstart_ref, flen_ref, total_ref, slot_ref, h_ref, xs_out, zero_s, sem, zsem, *, xs_rows):
    i = pl.program_id(0)

    def body(j, carry):
        for k in range(TOP_K):
            pltpu.make_async_copy(h_ref.at[pl.ds(j, 1), :], xs_out.at[pl.ds(slot_ref[0, 0, j * TOP_K + k], 1), :],
                                  sem).start(priority=k % 2)
        return carry

    lax.fori_loop(0, TD_TOK, body, 0)
    for k in range(TOP_K):
        pltpu.make_async_copy(h_ref, xs_out.at[pl.ds(0, TD_TOK), :], sem).wait()

    @pl.when(i == pl.num_programs(0) - 1)
    def _():
        zero_s[...] = jnp.zeros_like(zero_s)
        total = total_ref[0]
        n_tail = (xs_rows - total) // XS_ALIGN

        def gap_copy(e, r):
            return pltpu.make_async_copy(zero_s.at[pl.ds(0, 1), :], xs_out.at[pl.ds(fstart_ref[e] + r, 1), :], zsem)

        def tail_copy(q):
            start = pl.multiple_of(total + q * XS_ALIGN, XS_ALIGN)
            return pltpu.make_async_copy(zero_s, xs_out.at[pl.ds(start, XS_ALIGN), :], zsem)

        def for_gaps(fn):
            def per_expert(e, carry):
                for r in range(XS_ALIGN - 1):
                    @pl.when(r < flen_ref[e])
                    def _():
                        fn(gap_copy(e, r))
                return carry
            lax.fori_loop(0, N_EXPERTS, per_expert, 0)

        def for_tail(fn):
            def per_q(q, carry):
                fn(tail_copy(q))
                return carry
            lax.fori_loop(0, n_tail, per_q, 0)

        for_gaps(lambda cp: cp.start())
        for_tail(lambda cp: cp.start())
        for_gaps(lambda cp: cp.wait())
        for_tail(lambda cp: cp.wait())


def _dispatch(slot3, h2, fstart, flen, total, xs_rows):
    t_all = h2.shape[0]
    grid_spec = pltpu.PrefetchScalarGridSpec(
        num_scalar_prefetch=3,
        grid=(t_all // TD_TOK,),
        in_specs=[pl.BlockSpec((1, 1, TD_TOK * TOP_K), lambda i, a, b, c: (i, 0, 0), memory_space=pltpu.SMEM),
                  pl.BlockSpec((TD_TOK, D_MODEL), lambda i, a, b, c: (i, 0))],
        out_specs=pl.BlockSpec(memory_space=pl.ANY),
        scratch_shapes=[pltpu.VMEM((XS_ALIGN, D_MODEL), F32), pltpu.SemaphoreType.DMA(()),
                        pltpu.SemaphoreType.DMA(())],
    )
    return pl.pallas_call(
        functools.partial(_dispatch_kernel, xs_rows=xs_rows),
        grid_spec=grid_spec,
        out_shape=jax.ShapeDtypeStruct((xs_rows, D_MODEL), F32),
        compiler_params=_cparams(("arbitrary",)),
        name="dispatch",
    )(fstart, flen, total, slot3, h2)


X_RING = 6
Y_RING = 3
W_BUFFERS = 3


COPY_PARTS = 4


class _SplitCopy:
    def __init__(self, parts):
        self.parts = parts

    def start(self):
        for i, cp in enumerate(self.parts):
            cp.start(priority=i % 2)

    def wait(self):
        for cp in self.parts:
            cp.wait()


def _expert_kernel(bstart_ref, nblk_ref, nused_ref, xrow_ref, xs_hbm, wgu_hbm, wdn_hbm, ys_hbm,
                   xbuf, ybuf, wgu_f32, wdn_f32, wgu16, wdn16, sem_in, sem_out, sem_w, *, n_blocks):
    e = pl.program_id(0)
    nused = nused_ref[0]
    n = nblk_ref[e]
    g0 = bstart_ref[e]

    def split_copy(src, dst, sem, rows):
        h = rows // COPY_PARTS
        return _SplitCopy([pltpu.make_async_copy(src.at[pl.ds(r0, h), :], dst.at[pl.ds(r0, h), :], sem)
                           for r0 in range(0, rows, h)])

    def w_copies(ex):
        s = ex % W_BUFFERS
        return (split_copy(wgu_hbm.at[ex], wgu_f32.at[s], sem_w.at[0, s], D_MODEL),
                split_copy(wdn_hbm.at[ex], wdn_f32.at[s], sem_w.at[1, s], EXPERT_DIM))

    @pl.when(e == 0)
    def _():
        for q in range(W_BUFFERS - 1):
            for cp in w_copies(q):
                cp.start()

    @pl.when(e + (W_BUFFERS - 1) < N_EXPERTS)
    def _():
        for cp in w_copies(e + (W_BUFFERS - 1)):
            cp.start()

    def in_copy(g, s):
        start = pl.multiple_of(xrow_ref[g], XS_ALIGN)
        return split_copy(xs_hbm.at[pl.ds(start, MOE_BLOCK), :], xbuf.at[s], sem_in.at[s], MOE_BLOCK)

    def out_copy(g, s):
        return split_copy(ybuf.at[s], ys_hbm.at[pl.ds(g * MOE_BLOCK, MOE_BLOCK), :], sem_out.at[s], MOE_BLOCK)

    @pl.when(e == 0)
    def _():
        for q in range(X_RING - 1):
            @pl.when(q < nused)
            def _():
                in_copy(q, q).start()

    for cp in w_copies(e):
        cp.wait()

    @pl.when(n > 0)
    def _():
        wgu16[...] = wgu_f32[e % W_BUFFERS].astype(BF16)
        wdn16[...] = wdn_f32[e % W_BUFFERS].astype(BF16)

    def block(b, carry):
        g = g0 + b
        s = g % Y_RING
        ahead = g + (X_RING - 1)

        @pl.when(ahead < nused)
        def _():
            in_copy(ahead, ahead % X_RING).start()

        in_copy(g, g % X_RING).wait()
        gu = _dot(xbuf[g % X_RING].astype(BF16), wgu16[...])
        gate = gu[:, :EXPERT_DIM]
        act = (gate * _sigmoid(gate) * gu[:, EXPERT_DIM:]).astype(BF16)
        y = _dot(act, wdn16[...])

        @pl.when(g >= Y_RING)
        def _():
            out_copy(g - Y_RING, s).wait()

        ybuf[s] = y
        out_copy(g, s).start()
        return carry

    lax.fori_loop(0, n, block, 0)

    @pl.when(e == pl.num_programs(0) - 1)
    def _():
        for d in range(1, Y_RING + 1):
            @pl.when(nused >= d)
            def _():
                out_copy(nused - d, (nused - d) % Y_RING).wait()

        ybuf[0] = jnp.zeros((MOE_BLOCK, D_MODEL), F32)

        def tail(g, carry):
            out_copy(g, 0).start()
            out_copy(g, 0).wait()
            return carry

        lax.fori_loop(nused, n_blocks, tail, 0)


def _experts(xs, bstart, nblk, nused, xrow, w_gu, w_down):
    n_blocks = xrow.shape[0]
    grid_spec = pltpu.PrefetchScalarGridSpec(
        num_scalar_prefetch=4,
        grid=(N_EXPERTS,),
        in_specs=[pl.BlockSpec(memory_space=pl.ANY), pl.BlockSpec(memory_space=pl.ANY),
                  pl.BlockSpec(memory_space=pl.ANY)],
        out_specs=pl.BlockSpec(memory_space=pl.ANY),
        scratch_shapes=[pltpu.VMEM((X_RING, MOE_BLOCK, D_MODEL), F32),
                        pltpu.VMEM((Y_RING, MOE_BLOCK, D_MODEL), F32),
                        pltpu.VMEM((W_BUFFERS, D_MODEL, 2 * EXPERT_DIM), F32),
                        pltpu.VMEM((W_BUFFERS, EXPERT_DIM, D_MODEL), F32),
                        pltpu.VMEM((D_MODEL, 2 * EXPERT_DIM), BF16),
                        pltpu.VMEM((EXPERT_DIM, D_MODEL), BF16),
                        pltpu.SemaphoreType.DMA((X_RING,)), pltpu.SemaphoreType.DMA((Y_RING,)),
                        pltpu.SemaphoreType.DMA((2, W_BUFFERS))],
    )
    return pl.pallas_call(
        functools.partial(_expert_kernel, n_blocks=n_blocks),
        grid_spec=grid_spec,
        out_shape=jax.ShapeDtypeStruct((n_blocks * MOE_BLOCK, D_MODEL), F32),
        compiler_params=_cparams(("arbitrary",)),
        name="experts",
    )(bstart, nblk, nused, xrow, xs, w_gu, w_down)


TC_TOK = 128


def _combine_kernel(slot_ref, slot_next_ref, wt_ref, x1_ref, h2_ref, g2_ref, sgu_ref, sdn_ref, lg_ref, lb_ref,
                    y_hbm, oc_ref, ol_ref, gbuf, sems, *, n_tiles, ctx_tiles, alpha):
    i = pl.program_id(0)

    def row_copy(sref, b, j, k):
        return pltpu.make_async_copy(y_hbm.at[pl.ds(sref[0, 0, j * TOP_K + k], 1), :],
                                     gbuf.at[b, k, pl.ds(j, 1), :], sems.at[b])

    def wait_tile(b):
        for k in range(TOP_K):
            pltpu.make_async_copy(y_hbm.at[pl.ds(0, TC_TOK), :], gbuf.at[b, k], sems.at[b]).wait()

    @pl.when(i == 0)
    def _():
        def body(j, carry):
            for k in range(TOP_K):
                row_copy(slot_ref, 0, j, k).start(priority=k % 2)
            return carry
        lax.fori_loop(0, TC_TOK, body, 0)

    b = i % 2
    nb = 1 - b

    def issue_next(j0, j1):
        for j in range(j0, j1):
            for k in range(TOP_K):
                row_copy(slot_next_ref, nb, j, k).start(priority=k % 2)

    issue_next(0, TC_TOK // 2)
    h = h2_ref[...].astype(BF16)
    su = _dot(h, sgu_ref[...])
    sg = su[:, :EXPERT_DIM]
    act = (sg * _sigmoid(sg) * su[:, EXPERT_DIM:]).astype(BF16)
    ffn = _dot(act, sdn_ref[...])
    wait_tile(b)
    issue_next(TC_TOK // 2, TC_TOK)
    for k in range(TOP_K):
        ffn = ffn + gbuf[b, k] * wt_ref[:, k:k + 1]
    out = _layer_norm(alpha * x1_ref[...] + g2_ref[0] * ffn, lg_ref[...], lb_ref[...])

    @pl.when(i < ctx_tiles)
    def _():
        oc_ref[...] = out

    @pl.when(i >= ctx_tiles)
    def _():
        ol_ref[...] = out

    @pl.when(i == n_tiles - 1)
    def _():
        wait_tile(nb)


def _combine(slot3, wts_t, x1, h2, g2, sh_gu, sh_dn, ln_g, ln_b, y_sorted, row_of_tile, alpha, tok_ctx):
    t_all = x1.shape[0]
    n_tiles = t_all // TC_TOK
    ctx_tiles = tok_ctx // TC_TOK
    tile = lambda w: pl.BlockSpec((TC_TOK, w), lambda i: (i, 0))
    full = lambda a: pl.BlockSpec(a.shape, lambda i, _n=a.ndim: (0,) * _n)
    return pl.pallas_call(
        functools.partial(_combine_kernel, n_tiles=n_tiles, ctx_tiles=ctx_tiles, alpha=alpha),
        grid=(n_tiles,),
        in_specs=[pl.BlockSpec((1, 1, TC_TOK * TOP_K), lambda i: (i, 0, 0), memory_space=pltpu.SMEM),
                  pl.BlockSpec((1, 1, TC_TOK * TOP_K), lambda i: (jnp.minimum(i + 1, n_tiles - 1), 0, 0),
                               memory_space=pltpu.SMEM),
                  tile(TOP_K), tile(D_MODEL), tile(D_MODEL),
                  pl.BlockSpec((1, 1, D_MODEL), lambda i: (row_of_tile(i), 0, 0)),
                  full(sh_gu), full(sh_dn), full(ln_g), full(ln_b),
                  pl.BlockSpec(memory_space=pl.ANY)],
        out_specs=_two_source_specs(TC_TOK, D_MODEL, ctx_tiles),
        out_shape=[jax.ShapeDtypeStruct((tok_ctx, D_MODEL), F32),
                   jax.ShapeDtypeStruct((t_all - tok_ctx, D_MODEL), F32)],
        scratch_shapes=[pltpu.VMEM((2, TOP_K, TC_TOK, D_MODEL), F32), pltpu.SemaphoreType.DMA((2,))],
        compiler_params=_cparams(("arbitrary",)),
        name="combine",
    )(slot3, slot3, wts_t, x1, h2, g2, sh_gu, sh_dn, ln_g, ln_b, y_sorted)


def kernel(x_prompt, x_sample, state_rwkv, c, c_ctx, w_ada, b_ada, w_in, mu_shift, w_decay0, w_decay2, w_a0, w_a2, w_g2, k_k, k_a, r_k, lnx_g, lnx_b, conv_w, conv_b, conv_ln_g, conv_ln_b, w_out, ln1_g, ln1_b, router_w, router_bias, expert_w_gu, expert_w_down, shared_w_gu, shared_w_down, ln2_g, ln2_b):
    depth = w_ada.shape[0]
    assert depth == 1
    alpha = (2.0 * depth) ** 0.25
    n_ctx, t_ctx, _ = x_prompt.shape
    n_lat, t_lat, _ = x_sample.shape
    tok_ctx = n_ctx * t_ctx
    tok_lat = n_lat * t_lat
    t_all = tok_ctx + tok_lat
    tm = 1024
    assert tok_ctx % tm == 0 and t_lat % tm == 0 and t_lat // GRID_W * GRID_W == t_lat
    l = 0

    x_ctx = x_prompt.reshape(tok_ctx, D_MODEL)
    x_lat = x_sample.reshape(tok_lat, D_MODEL)

    cond8 = jnp.zeros((8, D_MODEL), F32).at[0].set(c_ctx).at[1:1 + n_lat].set(c)
    mod = _ada(cond8, w_ada[l], b_ada[l])
    sh1, sc1, g1, sh2, sc2, g2 = [m.reshape(8, 1, D_MODEL) for m in jnp.split(mod, 6, axis=-1)]

    def row_of_tile_for(tile_rows):
        ctx_tiles = tok_ctx // tile_rows
        per_seq = t_lat // tile_rows
        return lambda i: jnp.where(i < ctx_tiles, 0, 1 + (i - ctx_tiles) // per_seq)

    w_in_a = w_in[l][:, :RWKV_IN].astype(BF16)
    w_in_b = w_in[l][:, RWKV_IN:].astype(BF16)
    proj_a, proj_b = _in_proj(x_ctx, x_lat, sc1, sh1, w_in_a, w_in_b, tm, row_of_tile_for(tm))

    zeros64 = jnp.zeros((64, 2 * RW), F32)
    wd2 = jnp.concatenate([w_decay2[l, 0], w_decay2[l, 1]], axis=1)
    wa2 = jnp.concatenate([w_a2[l, 0], w_a2[l, 1]], axis=1)
    w_lora = jnp.concatenate([jnp.concatenate([wd2, zeros64], axis=1),
                              jnp.concatenate([zeros64, wa2], axis=1)], axis=0).astype(BF16)
    row = lambda v: v.reshape(1, -1)
    rwkv_w = [row(mu_shift[l]), w_lora, w_g2[l].astype(BF16), row(w_decay0[l]), row(w_a0[l]),
              row(k_k[l]), row(k_a[l]), row(r_k[l]), row(lnx_g[l]), row(lnx_b[l])]
    consts = _rwkv_consts()

    def to_cat(s):
        b = s.shape[0]
        s = s.reshape(b, 2, 2, 4, HD, HD)
        return jnp.transpose(s, (0, 1, 2, 5, 3, 4)).reshape(b, 4, HD, GW)

    s0_ctx = jnp.zeros((n_ctx, 4, HD, GW), F32)
    s0_lat = to_cat(state_rwkv[:, l].astype(F32))
    ya_c, sfin_ctx = _rwkv(proj_a, s0_ctx, rwkv_w, consts, t_ctx, n_ctx, 0)
    ya_l, _ = _rwkv(proj_a, s0_lat, rwkv_w, consts, t_lat, n_lat, tok_ctx // t_lat)
    sfin = jnp.transpose(sfin_ctx.reshape(n_ctx, 2, 2, HD, 4, HD), (0, 1, 2, 4, 5, 3))
    new_state = sfin.reshape(n_ctx, 1, 2, 2 * 4, HD, HD).astype(x_prompt.dtype)

    conv_wts = [conv_w[l], row(conv_b[l]), row(conv_ln_g[l]), row(conv_ln_b[l])]
    yb_c = _conv(proj_b, conv_wts, t_ctx, n_ctx, 0, False)
    yb_l = _conv(proj_b, conv_wts, t_lat, n_lat, tok_ctx // t_lat, True)

    w_oa = w_out[l][:RW].astype(BF16)
    w_ob = w_out[l][RW:].astype(BF16)
    x1, h2 = _out_proj(ya_c, ya_l, yb_c, yb_l, x_ctx, x_lat, g1, sc2, sh2, w_oa, w_ob, row(ln1_g[l]), row(ln1_b[l]), tm,
                       row_of_tile_for(tm), alpha)

    rwt = router_w[l].T
    rwt_hi = rwt.astype(BF16)
    rwt_lo = (rwt - rwt_hi.astype(F32)).astype(BF16)
    eidx, wts, pos, counts = _route(h2, rwt_hi, rwt_lo, router_bias[l].reshape(N_EXPERTS, 1), 512)

    n_assign = t_all * TOP_K
    n_blocks = (n_assign + N_EXPERTS * (MOE_BLOCK - 1) + MOE_BLOCK - 1) // MOE_BLOCK
    cnt = counts.reshape(N_EXPERTS).astype(jnp.int32)
    padded = (cnt + MOE_BLOCK - 1) // MOE_BLOCK * MOE_BLOCK
    pend = jnp.cumsum(padded)
    pstart = pend - padded
    padded_x = (cnt + XS_ALIGN - 1) // XS_ALIGN * XS_ALIGN
    pend_x = jnp.cumsum(padded_x)
    pstart_x = pend_x - padded_x
    xs_rows = n_assign + N_EXPERTS * (XS_ALIGN - 1) // XS_ALIGN * XS_ALIGN + MOE_BLOCK
    slot_x, slot_y = _slots(eidx, pos, jnp.stack([pstart_x, pstart], axis=1).astype(F32), 512)
    per_tile = lambda s, tok: s.T.reshape(t_all // tok, 1, tok * TOP_K)
    xs = _dispatch(per_tile(slot_x, TD_TOK), h2, (pstart_x + cnt).astype(jnp.int32), (padded_x - cnt).astype(jnp.int32),
                   pend_x[-1].astype(jnp.int32).reshape(1), xs_rows)
    bstart = (pstart // MOE_BLOCK).astype(jnp.int32)
    nblk = (padded // MOE_BLOCK).astype(jnp.int32)
    nused = (pend[-1] // MOE_BLOCK).astype(jnp.int32).reshape(1)
    blk_ids = jnp.arange(n_blocks, dtype=jnp.int32)
    blk_e = jnp.minimum(jnp.sum((pend[None, :] <= (blk_ids * MOE_BLOCK)[:, None]).astype(jnp.int32), axis=1),
                        N_EXPERTS - 1)
    shift = jnp.sum(jnp.where(blk_e[:, None] == jnp.arange(N_EXPERTS, dtype=jnp.int32)[None, :],
                              (pstart_x - pstart)[None, :], 0), axis=1)
    xrow = jnp.clip(blk_ids * MOE_BLOCK + shift, 0, xs_rows - MOE_BLOCK).astype(jnp.int32)
    y_sorted = _experts(xs, bstart, nblk, nused, xrow, expert_w_gu[l], expert_w_down[l])
    y_ctx, y_lat = _combine(per_tile(slot_y, TC_TOK), wts.T, x1, h2, g2, shared_w_gu[l].astype(BF16),
                            shared_w_down[l].astype(BF16), row(ln2_g[l]), row(ln2_b[l]), y_sorted,
                            row_of_tile_for(TC_TOK), alpha, tok_ctx)
    return (y_ctx.reshape(n_ctx, t_ctx, D_MODEL), y_lat.reshape(n_lat, t_lat, D_MODEL), new_state)
```

```python
import functools
import math

import numpy as np
import jax
import jax.numpy as jnp
from jax import lax
from jax.experimental import pallas as pl
from jax.experimental.pallas import tpu as pltpu

F32 = jnp.float32
BF16 = jnp.bfloat16

D_MODEL = 1024
RW = 512
HD = 64
CW = 512
CONV_K = 31
RWKV_IN = 3 * RW + 64 + 64 + 128
N_EXPERTS = 256
TOP_K = 8
N_GROUPS = 8
TOPK_GROUPS = 4
EXPERT_DIM = 256
ROUTED_SCALE = 2.5
MOE_BLOCK = 128
GRID_W = 64
LN_EPS = 1e-5
GN_EPS = 64e-5
CHUNK = 64
GW = 256
SCAN_UNROLL = 2
DECAY_SCALE = math.exp(-0.5)
VMEM_LIMIT = 56 * 1024 * 1024


def _cparams(sem):
    return pltpu.CompilerParams(dimension_semantics=sem, vmem_limit_bytes=VMEM_LIMIT)


def _split_bf16(x):
    hi = x.astype(BF16)
    lo = (x - hi.astype(F32)).astype(BF16)
    return hi, lo


def _dot(a, b):
    return jnp.dot(a, b, preferred_element_type=F32)


def _dot_nt(a, b):
    return lax.dot_general(a, b, (((1,), (1,)), ((), ())), preferred_element_type=F32)


def _dot_tn(a, b):
    return lax.dot_general(a, b, (((0,), (0,)), ((), ())), preferred_element_type=F32)


def _sigmoid(x):
    return 1.0 / (1.0 + jnp.exp(-x))


def _layer_norm(x, g, b):
    mu = jnp.mean(x, axis=-1, keepdims=True)
    d = x - mu
    var = jnp.mean(d * d, axis=-1, keepdims=True)
    return d * lax.rsqrt(var + LN_EPS) * g + b


def _ada_kernel(c_ref, w_ref, b_ref, o_ref):
    c = c_ref[...]
    s = c * _sigmoid(c)
    s_hi, s_lo = _split_bf16(s)
    w_hi, w_lo = _split_bf16(w_ref[...])
    o_ref[...] = _dot(s_hi, w_hi) + _dot(s_lo, w_hi) + _dot(s_hi, w_lo) + b_ref[...]


def _ada(cond8, w_ada, b_ada):
    n = w_ada.shape[1]
    tn = 1536
    return pl.pallas_call(
        _ada_kernel,
        grid=(n // tn,),
        in_specs=[pl.BlockSpec((8, D_MODEL), lambda j: (0, 0)),
                  pl.BlockSpec((D_MODEL, tn), lambda j: (0, j)),
                  pl.BlockSpec((1, tn), lambda j: (0, j))],
        out_specs=pl.BlockSpec((8, tn), lambda j: (0, j)),
        out_shape=jax.ShapeDtypeStruct((8, n), F32),
        compiler_params=_cparams(("arbitrary",)),
        name="ada",
    )(cond8, w_ada, b_ada.reshape(1, n))


def _two_source_specs(tm, width, ctx_tiles):
    return [pl.BlockSpec((tm, width), lambda i: (jnp.minimum(i, ctx_tiles - 1), 0)),
            pl.BlockSpec((tm, width), lambda i: (jnp.maximum(i - ctx_tiles, 0), 0))]


def _inproj_kernel(xc_ref, xl_ref, sc_ref, sh_ref, wa_ref, wb_ref, oa_ref, ob_ref, *, ctx_tiles):
    x = jnp.where(pl.program_id(0) < ctx_tiles, xc_ref[...], xl_ref[...])
    h = (x * (1.0 + sc_ref[0]) + sh_ref[0]).astype(BF16)
    oa_ref[...] = _dot(h, wa_ref[...])
    ob_ref[...] = _dot(h, wb_ref[...])


def _in_proj(x_ctx, x_lat, sc1, sh1, w_a, w_b, tm, row_of_tile):
    t_all = x_ctx.shape[0] + x_lat.shape[0]
    ctx_tiles = x_ctx.shape[0] // tm
    return pl.pallas_call(
        functools.partial(_inproj_kernel, ctx_tiles=ctx_tiles),
        grid=(t_all // tm,),
        in_specs=_two_source_specs(tm, D_MODEL, ctx_tiles) + [
                  pl.BlockSpec((1, 1, D_MODEL), lambda i: (row_of_tile(i), 0, 0)),
                  pl.BlockSpec((1, 1, D_MODEL), lambda i: (row_of_tile(i), 0, 0)),
                  pl.BlockSpec(w_a.shape, lambda i: (0, 0)),
                  pl.BlockSpec(w_b.shape, lambda i: (0, 0))],
        out_specs=[pl.BlockSpec((tm, RWKV_IN), lambda i: (i, 0)),
                   pl.BlockSpec((tm, 2 * CW), lambda i: (i, 0))],
        out_shape=[jax.ShapeDtypeStruct((t_all, RWKV_IN), F32),
                   jax.ShapeDtypeStruct((t_all, 2 * CW), F32)],
        compiler_params=_cparams(("arbitrary",)),
        name="in_proj",
    )(x_ctx, x_lat, sc1, sh1, w_a, w_b)


def _rwkv_consts():
    c = CHUNK
    t = np.arange(c)[:, None]
    s = np.arange(GW)[None, :] % c
    sl = (s < t).astype(np.float32)
    il = (s <= t).astype(np.float32)
    su = (s > t).astype(np.float32)
    iu = (s >= t).astype(np.float32)
    eye_cat = (s == t).astype(np.float32)
    tri = np.stack([sl, il, su, iu, eye_cat])
    r = np.arange(GW)
    mask_bd = (r[:, None] // HD == r[None, :] // HD).astype(np.float32)
    eye = np.eye(GW, dtype=np.float32)
    cum = np.stack([np.tril(np.ones((c, c), np.float32)), np.triu(np.ones((c, c), np.float32))])
    return (jnp.asarray(tri), jnp.asarray(mask_bd, dtype=BF16), jnp.asarray(mask_bd), jnp.asarray(eye),
            jnp.asarray(cum, dtype=BF16), jnp.asarray(mask_bd, dtype=BF16))


def _rwkv_kernel(p_ref, s0_ref, mu_ref, wlora_ref, wg2_ref, wd0_ref, wa0_ref, kk_ref, ka_ref, rk_ref,
                 lng_ref, lnb_ref, tri_ref, mbd16_ref, mbd_ref, eye_ref, cum_ref, seg_ref,
                 ya_ref, sfin_ref,
                 r_s, v_s, kk_s, g_s, bon_s, a_s, lw_s, kd_s, y_s, st_s, *, seq_len):
    nc = seq_len // CHUNK
    scan_unroll = SCAN_UNROLL
    seg = seg_ref[...]

    def seg_sum(x):
        hi, lo = _split_bf16(x)
        halves = [_dot(hi[:, s:s + GW], seg) + _dot(lo[:, s:s + GW], seg) for s in range(0, RW, GW)]
        return jnp.concatenate(halves, axis=1)

    def phase_a(c, carry):
        t0 = pl.multiple_of(c * CHUNK, CHUNK)
        cur = p_ref[pl.ds(t0, CHUNK), :]
        prev_row = p_ref[pl.ds(jnp.maximum(t0 - 1, 0), 1), :] * (t0 > 0).astype(F32)
        next_row = p_ref[pl.ds(jnp.minimum(t0 + CHUNK, seq_len - 1), 1), :] * (t0 + CHUNK < seq_len).astype(F32)
        row = lax.broadcasted_iota(jnp.int32, cur.shape, 0)
        prev = jnp.where(row == 0, prev_row, pltpu.roll(cur, 1, 0))
        nxt = jnp.where(row == CHUNK - 1, next_row, pltpu.roll(cur, CHUNK - 1, 0))
        p = cur + mu_ref[...] * (0.5 * (prev + nxt) - cur)
        r = p[:, 0:RW]
        k = p[:, RW:2 * RW]
        v = p[:, 2 * RW:3 * RW]
        z = p[:, 3 * RW:3 * RW + 128]
        gd = p[:, 3 * RW + 128:3 * RW + 256]
        lane = lax.broadcasted_iota(jnp.int32, z.shape, 1)
        z = jnp.where(lane < 64, jnp.tanh(z), z)
        lora = _dot(z.astype(BF16), wlora_ref[...])
        g = _dot(_sigmoid(gd).astype(BF16), wg2_ref[...])
        kks = k * kk_ref[...]
        rows = pl.ds(t0, CHUNK)
        r_s[rows, :] = r
        v_s[rows, :] = v.astype(BF16)
        g_s[rows, :] = g
        kd_sum = jnp.zeros_like(k)
        for d in range(2):
            w_logit = wd0_ref[:, d * RW:(d + 1) * RW] + lora[:, d * RW:(d + 1) * RW]
            a = _sigmoid(wa0_ref[:, d * RW:(d + 1) * RW] + lora[:, (2 + d) * RW:(3 + d) * RW])
            kd = k * (1.0 + (a - 1.0) * ka_ref[...])
            lw_s[d, rows, :] = -DECAY_SCALE * _sigmoid(w_logit)
            a_s[d, rows, :] = a
            kd_s[d, rows, :] = kd
            kd_sum = kd_sum + kd
        sums = seg_sum(jnp.concatenate([kks * kks, r * kd_sum * rk_ref[...]], axis=0))
        kk_s[rows, :] = kks * lax.rsqrt(sums[:CHUNK] + 1e-12)
        bon_s[rows, :] = sums[CHUNK:] * v
        return carry

    lax.fori_loop(0, nc, phase_a, 0, unroll=2)

    mbd16 = mbd16_ref[...]
    mbd = mbd_ref[...]
    eye = eye_ref[...]

    def bd(x16):
        return jnp.concatenate([x16, x16, x16, x16], axis=0) * mbd16

    for u in range(4):
        s0 = s0_ref[0, u]
        st_s[u] = jnp.concatenate([s0, s0, s0, s0], axis=0) * mbd

    def prep(d, grp, t0):
        rows = pl.ds(t0, CHUNK)
        lanes = slice(grp * GW, (grp + 1) * GW)
        lw = lw_s[d, rows, lanes]
        hi, lo = _split_bf16(lw)
        cl = _dot(cum_ref[d], hi) + _dot(cum_ref[d], lo)
        e_cl = jnp.exp(cl)
        e_ce = jnp.exp(cl - lw)
        e_ncl = jnp.exp(-cl)
        tot = cl[CHUNK - 1:CHUNK, :] if d == 0 else cl[0:1, :]
        wc = jnp.exp(tot)
        kk = kk_s[rows, lanes]
        rt = r_s[rows, lanes] * e_cl
        bt = a_s[d, rows, lanes] * kk * e_ncl
        kt = kd_s[d, rows, lanes] * e_ncl
        kq16 = (kk * e_ce).astype(BF16)
        return dict(d=d, u=d * 2 + grp, rows=rows, lanes=lanes, wc=wc, rt=rt, kq16=kq16,
                    v16=v_s[rows, lanes], bt16=bt.astype(BF16), kt16=kt.astype(BF16),
                    btw16=(bt * wc).astype(BF16), ktw16=(kt * wc).astype(BF16),
                    g16=jnp.concatenate([kq16, rt.astype(BF16)], axis=0))

    def phase_b(i2, carry):
        us = []
        for j in range(scan_unroll):
            i = i2 * scan_unroll + j
            tf = pl.multiple_of(i * CHUNK, CHUNK)
            tb = pl.multiple_of((nc - 1 - i) * CHUNK, CHUNK)
            us += [prep(0, 0, tf), prep(1, 0, tb), prep(0, 1, tf), prep(1, 1, tb)]
        each = lambda fn: [fn(q) for q in us]
        strict = lambda q: tri_ref[2 * q['d']]
        incl = lambda q: tri_ref[2 * q['d'] + 1]
        out_b = each(lambda q: _dot_nt(q['g16'], bd(q['bt16'])))
        out_k = each(lambda q: _dot_nt(q['g16'], bd(q['kt16'])))
        a_bk = [o[:CHUNK] * strict(q) for o, q in zip(out_b, us)]
        p_rb = [(o[CHUNK:] * incl(q)).astype(BF16) for o, q in zip(out_b, us)]
        a_kk = [(o[:CHUNK] * strict(q)).astype(BF16) for o, q in zip(out_k, us)]
        p_rk = [(o[CHUNK:] * incl(q)).astype(BF16) for o, q in zip(out_k, us)]
        tinv = [tri_ref[4] - a for a in a_bk]
        pw = [a.astype(BF16) for a in a_bk]
        for _ in range(5):
            pw = [_dot(p, bd(p)).astype(BF16) for p in pw]
            tinv = [t + _dot(p, bd(t.astype(BF16))) for t, p in zip(tinv, pw)]
        bdv = each(lambda q: bd(q['v16']))
        akkv = [_dot(a, b) for a, b in zip(a_kk, bdv)]
        x = [_dot(t.astype(BF16), jnp.concatenate([bd(q['kq16']), bd(k.astype(BF16))], axis=1))
             for t, q, k in zip(tinv, us, akkv)]
        qu16 = [z.astype(BF16) for z in x]
        tn1 = [_dot_tn(q['btw16'], z) for q, z in zip(us, qu16)]
        tn2 = each(lambda q: _dot_tn(q['ktw16'], q['v16']))
        r1 = [_dot(p, jnp.concatenate([bd(z[:, :GW]), bd(z[:, GW:])], axis=1)) for p, z in zip(p_rb, qu16)]
        y0 = [_dot(p, b) - r[:, GW:] for p, b, r in zip(p_rk, bdv, r1)]
        for q, t1, t2, r, y in zip(us, tn1, tn2, r1, y0):
            mc = (eye * q['wc'] - t1[:, :GW]) * mbd
            ncm = (t2 - t1[:, GW:]) * mbd
            rh = q['rt'] - r[:, :GW]
            st16 = st_s[q['u']].astype(BF16)
            big = _dot(jnp.concatenate([rh.astype(BF16), mc.astype(BF16)], axis=0), st16)
            y_s[q['d'], q['rows'], q['lanes']] = big[:CHUNK] + y
            st_s[q['u']] = big[CHUNK:] + ncm
        return carry

    lax.fori_loop(0, nc // scan_unroll, phase_b, 0)

    for u in range(4):
        st = st_s[u]
        sfin_ref[0, u] = st[0:HD] + st[HD:2 * HD] + st[2 * HD:3 * HD] + st[3 * HD:4 * HD]

    def phase_c(c, carry):
        rows = pl.ds(pl.multiple_of(c * CHUNK, CHUNK), CHUNK)
        y = y_s[0, rows, :] + y_s[1, rows, :]
        mean = seg_sum(y) * (1.0 / HD)
        dlt = y - mean
        var = seg_sum(dlt * dlt) * (1.0 / HD)
        yn = dlt * lax.rsqrt(var + GN_EPS)
        ya_ref[rows, :] = (yn * lng_ref[...] + lnb_ref[...] + bon_s[rows, :]) * g_s[rows, :]
        return carry

    lax.fori_loop(0, nc, phase_c, 0, unroll=2)


def _rwkv(proj_a, s0_cat, wts, consts, seq_len, n_seq, first_block):
    full = lambda a: pl.BlockSpec(a.shape, lambda s, _n=a.ndim: (0,) * _n)
    p_mode = dict(pipeline_mode=pl.Buffered(1)) if n_seq <= 2 else {}
    in_specs = [pl.BlockSpec((seq_len, RWKV_IN), lambda s: (first_block + s, 0), **p_mode),
                pl.BlockSpec((1, 4, HD, GW), lambda s: (s, 0, 0, 0))]
    in_specs += [full(a) for a in wts] + [full(a) for a in consts]
    args = [proj_a, s0_cat] + list(wts) + list(consts)
    seq = (seq_len, RW)
    return pl.pallas_call(
        functools.partial(_rwkv_kernel, seq_len=seq_len),
        grid=(n_seq,),
        in_specs=in_specs,
        out_specs=[pl.BlockSpec((seq_len, RW), lambda s: (s, 0)),
                   pl.BlockSpec((1, 4, HD, GW), lambda s: (s, 0, 0, 0))],
        out_shape=[jax.ShapeDtypeStruct((n_seq * seq_len, RW), F32),
                   jax.ShapeDtypeStruct((n_seq, 4, HD, GW), F32)],
        scratch_shapes=[pltpu.VMEM(seq, F32), pltpu.VMEM(seq, BF16), pltpu.VMEM(seq, F32), pltpu.VMEM(seq, F32),
                        pltpu.VMEM(seq, F32),
                        pltpu.VMEM((2,) + seq, F32), pltpu.VMEM((2,) + seq, F32), pltpu.VMEM((2,) + seq, F32),
                        pltpu.VMEM((2,) + seq, F32), pltpu.VMEM((4, GW, GW), F32)],
        compiler_params=_cparams(("arbitrary",)),
        name="rwkv_%d" % seq_len,
    )(*args)


CONV_PAD = 16
VPAD = (CONV_K // 2) * GRID_W


def _conv_kernel(u_ref, w_ref, b_ref, g_ref, beta_ref, o_ref, pad_s, vpad_s, y_s, *, seq_len, grid):
    rb = 256
    glu = u_ref[:, :CW] * _sigmoid(u_ref[:, CW:])
    zeros = jnp.zeros((CONV_PAD, CW), F32)
    pad_s[0:CONV_PAD, :] = zeros
    pad_s[CONV_PAD + seq_len:2 * CONV_PAD + seq_len, :] = zeros
    pad_s[CONV_PAD:CONV_PAD + seq_len, :] = glu
    half = CW // 2
    if grid:
        vz = jnp.zeros((VPAD, half), F32)
        vpad_s[0:VPAD, :] = vz
        vpad_s[VPAD + seq_len:2 * VPAD + seq_len, :] = vz
        vpad_s[VPAD:VPAD + seq_len, :] = glu[:, half:]
    n_h_lanes = (half if grid else CW) // 128
    for r0 in range(0, seq_len, rb):
        for lb in range(n_h_lanes):
            lanes = slice(lb * 128, (lb + 1) * 128)
            acc = jnp.zeros((rb, 128), F32)
            if grid:
                col = lax.broadcasted_iota(jnp.int32, (rb, 128), 0) % GRID_W
            span = rb + 2 * CONV_PAD
            xpad = pad_s[r0:r0 + span, lanes]
            for res in range(8):
                xrot = xpad if res == 0 else pltpu.roll(xpad, span - res, 0)
                for k in range(CONV_K):
                    if (k + 1) % 8 != res:
                        continue
                    off = (k + 1) - res
                    term = xrot[off:off + rb] * w_ref[k:k + 1, lanes]
                    if grid:
                        src = col + (k - CONV_K // 2)
                        term = jnp.where((src >= 0) & (src < GRID_W), term, 0.0)
                    acc = acc + term
            y_s[r0:r0 + rb, lanes] = acc
        if grid:
            for lb in range(half // 128):
                lanes = slice(lb * 128, (lb + 1) * 128)
                wl = slice(half + lb * 128, half + (lb + 1) * 128)
                acc = jnp.zeros((rb, 128), F32)
                for k in range(CONV_K):
                    start = VPAD + r0 + (k - CONV_K // 2) * GRID_W
                    acc = acc + vpad_s[start:start + rb, lanes] * w_ref[k:k + 1, wl]
                y_s[r0:r0 + rb, wl] = acc
    for r0 in range(0, seq_len, rb):
        y = y_s[r0:r0 + rb, :] + b_ref[...]
        yl = _layer_norm(y, g_ref[...], beta_ref[...])
        o_ref[r0:r0 + rb, :] = yl * _sigmoid(yl)


def _conv(proj_b, wts, seq_len, n_seq, first_block, grid):
    full = lambda a: pl.BlockSpec(a.shape, lambda s, _n=a.ndim: (0,) * _n)
    in_specs = [pl.BlockSpec((seq_len, 2 * CW), lambda s: (first_block + s, 0))] + [full(a) for a in wts]
    args = [proj_b] + list(wts)
    return pl.pallas_call(
        functools.partial(_conv_kernel, seq_len=seq_len, grid=grid),
        grid=(n_seq,),
        in_specs=in_specs,
        out_specs=pl.BlockSpec((seq_len, CW), lambda s: (s, 0)),
        out_shape=jax.ShapeDtypeStruct((n_seq * seq_len, CW), F32),
        scratch_shapes=[pltpu.VMEM((seq_len + 2 * CONV_PAD, CW), F32),
                        pltpu.VMEM((seq_len + 2 * VPAD if grid else 8, CW // 2), F32),
                        pltpu.VMEM((seq_len, CW), F32)],
        compiler_params=_cparams(("arbitrary",)),
        name="conv_%d" % seq_len,
    )(*args)


def _outproj_kernel(yac_ref, yal_ref, ybc_ref, ybl_ref, xc_ref, xl_ref, g1_ref, sc2_ref, sh2_ref, wa_ref, wb_ref,
                    lg_ref, lb_ref, x1_ref, h2_ref, *, alpha, ctx_tiles):
    is_ctx = pl.program_id(0) < ctx_tiles
    ya = jnp.where(is_ctx, yac_ref[...], yal_ref[...])
    yb = jnp.where(is_ctx, ybc_ref[...], ybl_ref[...])
    x = jnp.where(is_ctx, xc_ref[...], xl_ref[...])
    mix = _dot(ya.astype(BF16), wa_ref[...]) + _dot(yb.astype(BF16), wb_ref[...])
    x1 = _layer_norm(alpha * x + g1_ref[0] * mix, lg_ref[...], lb_ref[...])
    x1_ref[...] = x1
    h2_ref[...] = x1 * (1.0 + sc2_ref[0]) + sh2_ref[0]


def _out_proj(ya_c, ya_l, yb_c, yb_l, x_ctx, x_lat, g1, sc2, sh2, w_oa, w_ob, ln_g, ln_b, tm, row_of_tile, alpha):
    t_all = x_ctx.shape[0] + x_lat.shape[0]
    ctx_tiles = x_ctx.shape[0] // tm
    tile = lambda w: pl.BlockSpec((tm, w), lambda i: (i, 0))
    modrow = pl.BlockSpec((1, 1, D_MODEL), lambda i: (row_of_tile(i), 0, 0))
    full = lambda a: pl.BlockSpec(a.shape, lambda i, _n=a.ndim: (0,) * _n)
    return pl.pallas_call(
        functools.partial(_outproj_kernel, alpha=alpha, ctx_tiles=ctx_tiles),
        grid=(t_all // tm,),
        in_specs=_two_source_specs(tm, RW, ctx_tiles) + _two_source_specs(tm, CW, ctx_tiles)
        + _two_source_specs(tm, D_MODEL, ctx_tiles)
        + [modrow, modrow, modrow, full(w_oa), full(w_ob), full(ln_g), full(ln_b)],
        out_specs=[tile(D_MODEL), tile(D_MODEL)],
        out_shape=[jax.ShapeDtypeStruct((t_all, D_MODEL), F32), jax.ShapeDtypeStruct((t_all, D_MODEL), F32)],
        compiler_params=_cparams(("arbitrary",)),
        name="out_proj",
    )(ya_c, ya_l, yb_c, yb_l, x_ctx, x_lat, g1, sc2, sh2, w_oa, w_ob, ln_g, ln_b)


def _route_kernel(h_ref, wt_hi_ref, wt_lo_ref, bias_ref, ustrict_ref,
                  eidx_ref, wts_ref, pos_ref, cnt_ref, carry_s, *, tm):
    i = pl.program_id(0)

    @pl.when(i == 0)
    def _():
        carry_s[...] = jnp.zeros_like(carry_s)

    h_hi, h_lo = _split_bf16(h_ref[...])
    logits = _dot_nt(wt_hi_ref[...], h_hi) + _dot_nt(wt_hi_ref[...], h_lo) + _dot_nt(wt_lo_ref[...], h_hi)
    scores = _sigmoid(logits)
    sel = scores + bias_ref[...]
    neg = jnp.float32(-jnp.inf)
    gsz = N_EXPERTS // N_GROUPS
    gs = []
    for g in range(N_GROUPS):
        blk = sel[g * gsz:(g + 1) * gsz]
        m1 = jnp.max(blk, axis=0, keepdims=True)
        eq = blk == m1
        cnt = jnp.sum(eq.astype(F32), axis=0, keepdims=True)
        m2 = jnp.max(jnp.where(eq, neg, blk), axis=0, keepdims=True)
        gs.append(m1 + jnp.where(cnt >= 2.0, m1, m2))
    masked = []
    for g in range(N_GROUPS):
        rank = jnp.zeros_like(gs[g])
        for o in range(N_GROUPS):
            if o == g:
                continue
            beats = (gs[o] > gs[g]) | ((gs[o] == gs[g]) & (o < g))
            rank = rank + beats.astype(F32)
        keep = rank < float(TOPK_GROUPS)
        masked.append(jnp.where(keep, sel[g * gsz:(g + 1) * gsz], neg))
    cur = jnp.concatenate(masked, axis=0)
    iota_e = lax.broadcasted_iota(jnp.int32, cur.shape, 0).astype(F32)
    idxs, ws = [], []
    selmask = jnp.zeros(cur.shape, F32)
    for _ in range(TOP_K):
        m = jnp.max(cur, axis=0, keepdims=True)
        idx = jnp.min(jnp.where(cur == m, iota_e, float(N_EXPERTS)), axis=0, keepdims=True)
        onehot = iota_e == idx
        ws.append(jnp.sum(jnp.where(onehot, scores, 0.0), axis=0, keepdims=True))
        idxs.append(idx)
        cur = jnp.where(onehot, neg, cur)
        selmask = jnp.where(onehot, 1.0, selmask)
    pos = _dot(selmask.astype(BF16), ustrict_ref[...]) + carry_s[...]
    carry_s[...] = carry_s[...] + jnp.sum(selmask, axis=1, keepdims=True)
    cnt_ref[...] = carry_s[...]
    wsum = ws[0]
    for k in range(1, TOP_K):
        wsum = wsum + ws[k]
    pks = [jnp.sum(jnp.where(iota_e == idxs[k], pos, 0.0), axis=0, keepdims=True) for k in range(TOP_K)]
    eidx_ref[...] = jnp.concatenate(idxs, axis=0).astype(jnp.int32)
    wts_ref[...] = jnp.concatenate([w / wsum * ROUTED_SCALE for w in ws], axis=0)
    pos_ref[...] = jnp.concatenate(pks, axis=0).astype(jnp.int32)


def _route(h2, wt_hi, wt_lo, bias_col, tm):
    t_all = h2.shape[0]
    ustrict = jnp.asarray(np.triu(np.ones((tm, tm), np.float32), 1), dtype=BF16)
    full = lambda a: pl.BlockSpec(a.shape, lambda i, _n=a.ndim: (0,) * _n)
    tok = pl.BlockSpec((TOP_K, tm), lambda i: (0, i))
    return pl.pallas_call(
        functools.partial(_route_kernel, tm=tm),
        grid=(t_all // tm,),
        in_specs=[pl.BlockSpec((tm, D_MODEL), lambda i: (i, 0)), full(wt_hi), full(wt_lo), full(bias_col),
                  full(ustrict)],
        out_specs=[tok, tok, tok, pl.BlockSpec((N_EXPERTS, 1), lambda i: (0, 0))],
        out_shape=[jax.ShapeDtypeStruct((TOP_K, t_all), jnp.int32), jax.ShapeDtypeStruct((TOP_K, t_all), F32),
                   jax.ShapeDtypeStruct((TOP_K, t_all), jnp.int32), jax.ShapeDtypeStruct((N_EXPERTS, 1), F32)],
        scratch_shapes=[pltpu.VMEM((N_EXPERTS, 1), F32)],
        compiler_params=_cparams(("arbitrary",)),
        name="route",
    )(h2, wt_hi, wt_lo, bias_col, ustrict)


def _slots_kernel(eidx_ref, pos_ref, pstart_ref, slotx_ref, sloty_ref):
    tm = eidx_ref.shape[1]
    iota_e = lax.broadcasted_iota(jnp.int32, (N_EXPERTS, tm), 0)
    for c, out in enumerate((slotx_ref, sloty_ref)):
        rows = []
        for k in range(TOP_K):
            onehot = iota_e == eidx_ref[k:k + 1, :]
            rows.append(jnp.sum(jnp.where(onehot, pstart_ref[:, c:c + 1], 0.0), axis=0, keepdims=True))
        out[...] = jnp.concatenate(rows, axis=0).astype(jnp.int32) + pos_ref[...]


def _slots(eidx, pos, pstart_cols, tm):
    t_all = eidx.shape[1]
    tok = pl.BlockSpec((TOP_K, tm), lambda i: (0, i))
    return pl.pallas_call(
        _slots_kernel,
        grid=(t_all // tm,),
        in_specs=[tok, tok, pl.BlockSpec((N_EXPERTS, 2), lambda i: (0, 0))],
        out_specs=[tok, tok],
        out_shape=[jax.ShapeDtypeStruct((TOP_K, t_all), jnp.int32), jax.ShapeDtypeStruct((TOP_K, t_all), jnp.int32)],
        compiler_params=_cparams(("arbitrary",)),
        name="slots",
    )(eidx, pos, pstart_cols)


TD_TOK = 512
XS_ALIGN = 8


def _dispatch_kernel(fstart_ref, flen_ref, total_ref, slot_ref, h_ref, xs_out, zero_s, sem, zsem, *, xs_rows):
    i = pl.program_id(0)

    def body(j, carry):
        for k in range(TOP_K):
            pltpu.make_async_copy(h_ref.at[pl.ds(j, 1), :], xs_out.at[pl.ds(slot_ref[0, 0, j * TOP_K + k], 1), :],
                                  sem).start(priority=k % 2)
        return carry

    lax.fori_loop(0, TD_TOK, body, 0)
    for k in range(TOP_K):
        pltpu.make_async_copy(h_ref, xs_out.at[pl.ds(0, TD_TOK), :], sem).wait()

    @pl.when(i == pl.num_programs(0) - 1)
    def _():
        zero_s[...] = jnp.zeros_like(zero_s)
        total = total_ref[0]
        n_tail = (xs_rows - total) // XS_ALIGN

        def gap_copy(e, r):
            return pltpu.make_async_copy(zero_s.at[pl.ds(0, 1), :], xs_out.at[pl.ds(fstart_ref[e] + r, 1), :], zsem)

        def tail_copy(q):
            start = pl.multiple_of(total + q * XS_ALIGN, XS_ALIGN)
            return pltpu.make_async_copy(zero_s, xs_out.at[pl.ds(start, XS_ALIGN), :], zsem)

        def for_gaps(fn):
            def per_expert(e, carry):
                for r in range(XS_ALIGN - 1):
                    @pl.when(r < flen_ref[e])
                    def _():
                        fn(gap_copy(e, r))
                return carry
            lax.fori_loop(0, N_EXPERTS, per_expert, 0)

        def for_tail(fn):
            def per_q(q, carry):
                fn(tail_copy(q))
                return carry
            lax.fori_loop(0, n_tail, per_q, 0)

        for_gaps(lambda cp: cp.start())
        for_tail(lambda cp: cp.start())
        for_gaps(lambda cp: cp.wait())
        for_tail(lambda cp: cp.wait())


def _dispatch(slot3, h2, fstart, flen, total, xs_rows):
    t_all = h2.shape[0]
    grid_spec = pltpu.PrefetchScalarGridSpec(
        num_scalar_prefetch=3,
        grid=(t_all // TD_TOK,),
        in_specs=[pl.BlockSpec((1, 1, TD_TOK * TOP_K), lambda i, a, b, c: (i, 0, 0), memory_space=pltpu.SMEM),
                  pl.BlockSpec((TD_TOK, D_MODEL), lambda i, a, b, c: (i, 0))],
        out_specs=pl.BlockSpec(memory_space=pl.ANY),
        scratch_shapes=[pltpu.VMEM((XS_ALIGN, D_MODEL), F32), pltpu.SemaphoreType.DMA(()),
                        pltpu.SemaphoreType.DMA(())],
    )
    return pl.pallas_call(
        functools.partial(_dispatch_kernel, xs_rows=xs_rows),
        grid_spec=grid_spec,
        out_shape=jax.ShapeDtypeStruct((xs_rows, D_MODEL), F32),
        compiler_params=_cparams(("arbitrary",)),
        name="dispatch",
    )(fstart, flen, total, slot3, h2)


X_RING = 6
Y_RING = 3
W_BUFFERS = 3


COPY_PARTS = 2


class _SplitCopy:
    def __init__(self, parts):
        self.parts = parts

    def start(self):
        for i, cp in enumerate(self.parts):
            cp.start(priority=i % 2)

    def wait(self):
        for cp in self.parts:
            cp.wait()


def _expert_kernel(bstart_ref, nblk_ref, nused_ref, xrow_ref, xs_hbm, wgu_hbm, wdn_hbm, ys_hbm,
                   xbuf, ybuf, wgu_f32, wdn_f32, wgu16, wdn16, sem_in, sem_out, sem_w, *, n_blocks):
    e = pl.program_id(0)
    nused = nused_ref[0]
    n = nblk_ref[e]
    g0 = bstart_ref[e]

    def split_copy(src, dst, sem, rows):
        h = rows // COPY_PARTS
        return _SplitCopy([pltpu.make_async_copy(src.at[pl.ds(r0, h), :], dst.at[pl.ds(r0, h), :], sem)
                           for r0 in range(0, rows, h)])

    def w_copies(ex):
        s = ex % W_BUFFERS
        return (split_copy(wgu_hbm.at[ex], wgu_f32.at[s], sem_w.at[0, s], D_MODEL),
                split_copy(wdn_hbm.at[ex], wdn_f32.at[s], sem_w.at[1, s], EXPERT_DIM))

    @pl.when(e == 0)
    def _():
        for q in range(W_BUFFERS - 1):
            for cp in w_copies(q):
                cp.start()

    @pl.when(e + (W_BUFFERS - 1) < N_EXPERTS)
    def _():
        for cp in w_copies(e + (W_BUFFERS - 1)):
            cp.start()

    def in_copy(g, s):
        start = pl.multiple_of(xrow_ref[g], XS_ALIGN)
        return split_copy(xs_hbm.at[pl.ds(start, MOE_BLOCK), :], xbuf.at[s], sem_in.at[s], MOE_BLOCK)

    def out_copy(g, s):
        return split_copy(ybuf.at[s], ys_hbm.at[pl.ds(g * MOE_BLOCK, MOE_BLOCK), :], sem_out.at[s], MOE_BLOCK)

    @pl.when(e == 0)
    def _():
        for q in range(X_RING - 1):
            @pl.when(q < nused)
            def _():
                in_copy(q, q).start()

    for cp in w_copies(e):
        cp.wait()

    @pl.when(n > 0)
    def _():
        wgu16[...] = wgu_f32[e % W_BUFFERS].astype(BF16)
        wdn16[...] = wdn_f32[e % W_BUFFERS].astype(BF16)

    def block(b, carry):
        g = g0 + b
        s = g % Y_RING
        ahead = g + (X_RING - 1)

        @pl.when(ahead < nused)
        def _():
            in_copy(ahead, ahead % X_RING).start()

        in_copy(g, g % X_RING).wait()
        gu = _dot(xbuf[g % X_RING].astype(BF16), wgu16[...])
        gate = gu[:, :EXPERT_DIM]
        act = (gate * _sigmoid(gate) * gu[:, EXPERT_DIM:]).astype(BF16)
        y = _dot(act, wdn16[...])

        @pl.when(g >= Y_RING)
        def _():
            out_copy(g - Y_RING, s).wait()

        ybuf[s] = y
        out_copy(g, s).start()
        return carry

    lax.fori_loop(0, n, block, 0)

    @pl.when(e == pl.num_programs(0) - 1)
    def _():
        for d in range(1, Y_RING + 1):
            @pl.when(nused >= d)
            def _():
                out_copy(nused - d, (nused - d) % Y_RING).wait()

        ybuf[0] = jnp.zeros((MOE_BLOCK, D_MODEL), F32)

        def tail(g, carry):
            out_copy(g, 0).start()
            out_copy(g, 0).wait()
            return carry

        lax.fori_loop(nused, n_blocks, tail, 0)


def _experts(xs, bstart, nblk, nused, xrow, w_gu, w_down):
    n_blocks = xrow.shape[0]
    grid_spec = pltpu.PrefetchScalarGridSpec(
        num_scalar_prefetch=4,
        grid=(N_EXPERTS,),
        in_specs=[pl.BlockSpec(memory_space=pl.ANY), pl.BlockSpec(memory_space=pl.ANY),
                  pl.BlockSpec(memory_space=pl.ANY)],
        out_specs=pl.BlockSpec(memory_space=pl.ANY),
        scratch_shapes=[pltpu.VMEM((X_RING, MOE_BLOCK, D_MODEL), F32),
                        pltpu.VMEM((Y_RING, MOE_BLOCK, D_MODEL), F32),
                        pltpu.VMEM((W_BUFFERS, D_MODEL, 2 * EXPERT_DIM), F32),
                        pltpu.VMEM((W_BUFFERS, EXPERT_DIM, D_MODEL), F32),
                        pltpu.VMEM((D_MODEL, 2 * EXPERT_DIM), BF16),
                        pltpu.VMEM((EXPERT_DIM, D_MODEL), BF16),
                        pltpu.SemaphoreType.DMA((X_RING,)), pltpu.SemaphoreType.DMA((Y_RING,)),
                        pltpu.SemaphoreType.DMA((2, W_BUFFERS))],
    )
    return pl.pallas_call(
        functools.partial(_expert_kernel, n_blocks=n_blocks),
        grid_spec=grid_spec,
        out_shape=jax.ShapeDtypeStruct((n_blocks * MOE_BLOCK, D_MODEL), F32),
        compiler_params=_cparams(("arbitrary",)),
        name="experts",
    )(bstart, nblk, nused, xrow, xs, w_gu, w_down)


TC_TOK = 128


def _combine_kernel(slot_ref, slot_next_ref, wt_ref, x1_ref, h2_ref, g2_ref, sgu_ref, sdn_ref, lg_ref, lb_ref,
                    y_hbm, oc_ref, ol_ref, gbuf, sems, *, n_tiles, ctx_tiles, alpha):
    i = pl.program_id(0)

    def row_copy(sref, b, j, k):
        return pltpu.make_async_copy(y_hbm.at[pl.ds(sref[0, 0, j * TOP_K + k], 1), :],
                                     gbuf.at[b, k, pl.ds(j, 1), :], sems.at[b])

    def wait_tile(b):
        for k in range(TOP_K):
            pltpu.make_async_copy(y_hbm.at[pl.ds(0, TC_TOK), :], gbuf.at[b, k], sems.at[b]).wait()

    @pl.when(i == 0)
    def _():
        def body(j, carry):
            for k in range(TOP_K):
                row_copy(slot_ref, 0, j, k).start(priority=k % 2)
            return carry
        lax.fori_loop(0, TC_TOK, body, 0)

    b = i % 2
    nb = 1 - b

    def issue_next(j0, j1):
        for j in range(j0, j1):
            for k in range(TOP_K):
                row_copy(slot_next_ref, nb, j, k).start(priority=k % 2)

    issue_next(0, TC_TOK // 2)
    h = h2_ref[...].astype(BF16)
    su = _dot(h, sgu_ref[...])
    sg = su[:, :EXPERT_DIM]
    act = (sg * _sigmoid(sg) * su[:, EXPERT_DIM:]).astype(BF16)
    ffn = _dot(act, sdn_ref[...])
    wait_tile(b)
    issue_next(TC_TOK // 2, TC_TOK)
    for k in range(TOP_K):
        ffn = ffn + gbuf[b, k] * wt_ref[:, k:k + 1]
    out = _layer_norm(alpha * x1_ref[...] + g2_ref[0] * ffn, lg_ref[...], lb_ref[...])

    @pl.when(i < ctx_tiles)
    def _():
        oc_ref[...] = out

    @pl.when(i >= ctx_tiles)
    def _():
        ol_ref[...] = out

    @pl.when(i == n_tiles - 1)
    def _():
        wait_tile(nb)


def _combine(slot3, wts_t, x1, h2, g2, sh_gu, sh_dn, ln_g, ln_b, y_sorted, row_of_tile, alpha, tok_ctx):
    t_all = x1.shape[0]
    n_tiles = t_all // TC_TOK
    ctx_tiles = tok_ctx // TC_TOK
    tile = lambda w: pl.BlockSpec((TC_TOK, w), lambda i: (i, 0))
    full = lambda a: pl.BlockSpec(a.shape, lambda i, _n=a.ndim: (0,) * _n)
    return pl.pallas_call(
        functools.partial(_combine_kernel, n_tiles=n_tiles, ctx_tiles=ctx_tiles, alpha=alpha),
        grid=(n_tiles,),
        in_specs=[pl.BlockSpec((1, 1, TC_TOK * TOP_K), lambda i: (i, 0, 0), memory_space=pltpu.SMEM),
                  pl.BlockSpec((1, 1, TC_TOK * TOP_K), lambda i: (jnp.minimum(i + 1, n_tiles - 1), 0, 0),
                               memory_space=pltpu.SMEM),
                  tile(TOP_K), tile(D_MODEL), tile(D_MODEL),
                  pl.BlockSpec((1, 1, D_MODEL), lambda i: (row_of_tile(i), 0, 0)),
                  full(sh_gu), full(sh_dn), full(ln_g), full(ln_b),
                  pl.BlockSpec(memory_space=pl.ANY)],
        out_specs=_two_source_specs(TC_TOK, D_MODEL, ctx_tiles),
        out_shape=[jax.ShapeDtypeStruct((tok_ctx, D_MODEL), F32),
                   jax.ShapeDtypeStruct((t_all - tok_ctx, D_MODEL), F32)],
        scratch_shapes=[pltpu.VMEM((2, TOP_K, TC_TOK, D_MODEL), F32), pltpu.SemaphoreType.DMA((2,))],
        compiler_params=_cparams(("arbitrary",)),
        name="combine",
    )(slot3, slot3, wts_t, x1, h2, g2, sh_gu, sh_dn, ln_g, ln_b, y_sorted)


def kernel(x_prompt, x_sample, state_rwkv, c, c_ctx, w_ada, b_ada, w_in, mu_shift, w_decay0, w_decay2, w_a0, w_a2, w_g2, k_k, k_a, r_k, lnx_g, lnx_b, conv_w, conv_b, conv_ln_g, conv_ln_b, w_out, ln1_g, ln1_b, router_w, router_bias, expert_w_gu, expert_w_down, shared_w_gu, shared_w_down, ln2_g, ln2_b):
    depth = w_ada.shape[0]
    assert depth == 1
    alpha = (2.0 * depth) ** 0.25
    n_ctx, t_ctx, _ = x_prompt.shape
    n_lat, t_lat, _ = x_sample.shape
    tok_ctx = n_ctx * t_ctx
    tok_lat = n_lat * t_lat
    t_all = tok_ctx + tok_lat
    tm = 1024
    assert tok_ctx % tm == 0 and t_lat % tm == 0 and t_lat // GRID_W * GRID_W == t_lat
    l = 0

    x_ctx = x_prompt.reshape(tok_ctx, D_MODEL)
    x_lat = x_sample.reshape(tok_lat, D_MODEL)

    cond8 = jnp.zeros((8, D_MODEL), F32).at[0].set(c_ctx).at[1:1 + n_lat].set(c)
    mod = _ada(cond8, w_ada[l], b_ada[l])
    sh1, sc1, g1, sh2, sc2, g2 = [m.reshape(8, 1, D_MODEL) for m in jnp.split(mod, 6, axis=-1)]

    def row_of_tile_for(tile_rows):
        ctx_tiles = tok_ctx // tile_rows
        per_seq = t_lat // tile_rows
        return lambda i: jnp.where(i < ctx_tiles, 0, 1 + (i - ctx_tiles) // per_seq)

    w_in_a = w_in[l][:, :RWKV_IN].astype(BF16)
    w_in_b = w_in[l][:, RWKV_IN:].astype(BF16)
    proj_a, proj_b = _in_proj(x_ctx, x_lat, sc1, sh1, w_in_a, w_in_b, tm, row_of_tile_for(tm))

    zeros64 = jnp.zeros((64, 2 * RW), F32)
    wd2 = jnp.concatenate([w_decay2[l, 0], w_decay2[l, 1]], axis=1)
    wa2 = jnp.concatenate([w_a2[l, 0], w_a2[l, 1]], axis=1)
    w_lora = jnp.concatenate([jnp.concatenate([wd2, zeros64], axis=1),
                              jnp.concatenate([zeros64, wa2], axis=1)], axis=0).astype(BF16)
    row = lambda v: v.reshape(1, -1)
    rwkv_w = [row(mu_shift[l]), w_lora, w_g2[l].astype(BF16), row(w_decay0[l]), row(w_a0[l]),
              row(k_k[l]), row(k_a[l]), row(r_k[l]), row(lnx_g[l]), row(lnx_b[l])]
    consts = _rwkv_consts()

    def to_cat(s):
        b = s.shape[0]
        s = s.reshape(b, 2, 2, 4, HD, HD)
        return jnp.transpose(s, (0, 1, 2, 5, 3, 4)).reshape(b, 4, HD, GW)

    s0_ctx = jnp.zeros((n_ctx, 4, HD, GW), F32)
    s0_lat = to_cat(state_rwkv[:, l].astype(F32))
    ya_c, sfin_ctx = _rwkv(proj_a, s0_ctx, rwkv_w, consts, t_ctx, n_ctx, 0)
    ya_l, _ = _rwkv(proj_a, s0_lat, rwkv_w, consts, t_lat, n_lat, tok_ctx // t_lat)
    sfin = jnp.transpose(sfin_ctx.reshape(n_ctx, 2, 2, HD, 4, HD), (0, 1, 2, 4, 5, 3))
    new_state = sfin.reshape(n_ctx, 1, 2, 2 * 4, HD, HD).astype(x_prompt.dtype)

    conv_wts = [conv_w[l], row(conv_b[l]), row(conv_ln_g[l]), row(conv_ln_b[l])]
    yb_c = _conv(proj_b, conv_wts, t_ctx, n_ctx, 0, False)
    yb_l = _conv(proj_b, conv_wts, t_lat, n_lat, tok_ctx // t_lat, True)

    w_oa = w_out[l][:RW].astype(BF16)
    w_ob = w_out[l][RW:].astype(BF16)
    x1, h2 = _out_proj(ya_c, ya_l, yb_c, yb_l, x_ctx, x_lat, g1, sc2, sh2, w_oa, w_ob, row(ln1_g[l]), row(ln1_b[l]), tm,
                       row_of_tile_for(tm), alpha)

    rwt = router_w[l].T
    rwt_hi = rwt.astype(BF16)
    rwt_lo = (rwt - rwt_hi.astype(F32)).astype(BF16)
    eidx, wts, pos, counts = _route(h2, rwt_hi, rwt_lo, router_bias[l].reshape(N_EXPERTS, 1), 512)

    n_assign = t_all * TOP_K
    n_blocks = (n_assign + N_EXPERTS * (MOE_BLOCK - 1) + MOE_BLOCK - 1) // MOE_BLOCK
    cnt = counts.reshape(N_EXPERTS).astype(jnp.int32)
    padded = (cnt + MOE_BLOCK - 1) // MOE_BLOCK * MOE_BLOCK
    pend = jnp.cumsum(padded)
    pstart = pend - padded
    padded_x = (cnt + XS_ALIGN - 1) // XS_ALIGN * XS_ALIGN
    pend_x = jnp.cumsum(padded_x)
    pstart_x = pend_x - padded_x
    xs_rows = n_assign + N_EXPERTS * (XS_ALIGN - 1) // XS_ALIGN * XS_ALIGN + MOE_BLOCK
    slot_x, slot_y = _slots(eidx, pos, jnp.stack([pstart_x, pstart], axis=1).astype(F32), 512)
    per_tile = lambda s, tok: s.T.reshape(t_all // tok, 1, tok * TOP_K)
    xs = _dispatch(per_tile(slot_x, TD_TOK), h2, (pstart_x + cnt).astype(jnp.int32), (padded_x - cnt).astype(jnp.int32),
                   pend_x[-1].astype(jnp.int32).reshape(1), xs_rows)
    bstart = (pstart // MOE_BLOCK).astype(jnp.int32)
    nblk = (padded // MOE_BLOCK).astype(jnp.int32)
    nused = (pend[-1] // MOE_BLOCK).astype(jnp.int32).reshape(1)
    blk_ids = jnp.arange(n_blocks, dtype=jnp.int32)
    blk_e = jnp.minimum(jnp.sum((pend[None, :] <= (blk_ids * MOE_BLOCK)[:, None]).astype(jnp.int32), axis=1),
                        N_EXPERTS - 1)
    shift = jnp.sum(jnp.where(blk_e[:, None] == jnp.arange(N_EXPERTS, dtype=jnp.int32)[None, :],
                              (pstart_x - pstart)[None, :], 0), axis=1)
    xrow = jnp.clip(blk_ids * MOE_BLOCK + shift, 0, xs_rows - MOE_BLOCK).astype(jnp.int32)
    y_sorted = _experts(xs, bstart, nblk, nused, xrow, expert_w_gu[l], expert_w_down[l])
    y_ctx, y_lat = _combine(per_tile(slot_y, TC_TOK), wts.T, x1, h2, g2, shared_w_gu[l].astype(BF16),
                            shared_w_down[l].astype(BF16), row(ln2_g[l]), row(ln2_b[l]), y_sorted,
                            row_of_tile_for(TC_TOK), alpha, tok_ctx)
    return (y_ctx.reshape(n_ctx, t_ctx, D_MODEL), y_lat.reshape(n_lat, t_lat, D_MODEL), new_state)
```

```python
import functools
import math

import numpy as np
import jax
import jax.numpy as jnp
from jax import lax
from jax.experimental import pallas as pl
from jax.experimental.pallas import tpu as pltpu

F32 = jnp.float32
BF16 = jnp.bfloat16

D_MODEL = 1024
RW = 512
HD = 64
CW = 512
CONV_K = 31
RWKV_IN = 3 * RW + 64 + 64 + 128
N_EXPERTS = 256
TOP_K = 8
N_GROUPS = 8
TOPK_GROUPS = 4
EXPERT_DIM = 256
ROUTED_SCALE = 2.5
MOE_BLOCK = 128
GRID_W = 64
LN_EPS = 1e-5
GN_EPS = 64e-5
CHUNK = 64
GW = 256
SCAN_UNROLL = 2
DECAY_SCALE = math.exp(-0.5)
VMEM_LIMIT = 56 * 1024 * 1024


def _cparams(sem):
    return pltpu.CompilerParams(dimension_semantics=sem, vmem_limit_bytes=VMEM_LIMIT)


def _split_bf16(x):
    hi = x.astype(BF16)
    lo = (x - hi.astype(F32)).astype(BF16)
    return hi, lo


def _dot(a, b):
    return jnp.dot(a, b, preferred_element_type=F32)


def _dot_nt(a, b):
    return lax.dot_general(a, b, (((1,), (1,)), ((), ())), preferred_element_type=F32)


def _dot_tn(a, b):
    return lax.dot_general(a, b, (((0,), (0,)), ((), ())), preferred_element_type=F32)


def _sigmoid(x):
    return 1.0 / (1.0 + jnp.exp(-x))


def _layer_norm(x, g, b):
    mu = jnp.mean(x, axis=-1, keepdims=True)
    d = x - mu
    var = jnp.mean(d * d, axis=-1, keepdims=True)
    return d * lax.rsqrt(var + LN_EPS) * g + b


def _ada_kernel(c_ref, w_ref, b_ref, o_ref):
    c = c_ref[...]
    s = c * _sigmoid(c)
    s_hi, s_lo = _split_bf16(s)
    w_hi, w_lo = _split_bf16(w_ref[...])
    o_ref[...] = _dot(s_hi, w_hi) + _dot(s_lo, w_hi) + _dot(s_hi, w_lo) + b_ref[...]


def _ada(cond8, w_ada, b_ada):
    n = w_ada.shape[1]
    tn = 1536
    return pl.pallas_call(
        _ada_kernel,
        grid=(n // tn,),
        in_specs=[pl.BlockSpec((8, D_MODEL), lambda j: (0, 0)),
                  pl.BlockSpec((D_MODEL, tn), lambda j: (0, j)),
                  pl.BlockSpec((1, tn), lambda j: (0, j))],
        out_specs=pl.BlockSpec((8, tn), lambda j: (0, j)),
        out_shape=jax.ShapeDtypeStruct((8, n), F32),
        compiler_params=_cparams(("arbitrary",)),
        name="ada",
    )(cond8, w_ada, b_ada.reshape(1, n))


def _two_source_specs(tm, width, ctx_tiles):
    return [pl.BlockSpec((tm, width), lambda i: (jnp.minimum(i, ctx_tiles - 1), 0)),
            pl.BlockSpec((tm, width), lambda i: (jnp.maximum(i - ctx_tiles, 0), 0))]


def _inproj_kernel(xc_ref, xl_ref, sc_ref, sh_ref, wa_ref, wb_ref, oa_ref, ob_ref, *, ctx_tiles):
    x = jnp.where(pl.program_id(0) < ctx_tiles, xc_ref[...], xl_ref[...])
    h = (x * (1.0 + sc_ref[0]) + sh_ref[0]).astype(BF16)
    oa_ref[...] = _dot(h, wa_ref[...])
    ob_ref[...] = _dot(h, wb_ref[...])


def _in_proj(x_ctx, x_lat, sc1, sh1, w_a, w_b, tm, row_of_tile):
    t_all = x_ctx.shape[0] + x_lat.shape[0]
    ctx_tiles = x_ctx.shape[0] // tm
    return pl.pallas_call(
        functools.partial(_inproj_kernel, ctx_tiles=ctx_tiles),
        grid=(t_all // tm,),
        in_specs=_two_source_specs(tm, D_MODEL, ctx_tiles) + [
                  pl.BlockSpec((1, 1, D_MODEL), lambda i: (row_of_tile(i), 0, 0)),
                  pl.BlockSpec((1, 1, D_MODEL), lambda i: (row_of_tile(i), 0, 0)),
                  pl.BlockSpec(w_a.shape, lambda i: (0, 0)),
                  pl.BlockSpec(w_b.shape, lambda i: (0, 0))],
        out_specs=[pl.BlockSpec((tm, RWKV_IN), lambda i: (i, 0)),
                   pl.BlockSpec((tm, 2 * CW), lambda i: (i, 0))],
        out_shape=[jax.ShapeDtypeStruct((t_all, RWKV_IN), F32),
                   jax.ShapeDtypeStruct((t_all, 2 * CW), F32)],
        compiler_params=_cparams(("arbitrary",)),
        name="in_proj",
    )(x_ctx, x_lat, sc1, sh1, w_a, w_b)


def _rwkv_consts():
    c = CHUNK
    t = np.arange(c)[:, None]
    s = np.arange(GW)[None, :] % c
    sl = (s < t).astype(np.float32)
    il = (s <= t).astype(np.float32)
    su = (s > t).astype(np.float32)
    iu = (s >= t).astype(np.float32)
    eye_cat = (s == t).astype(np.float32)
    tri = np.stack([sl, il, su, iu, eye_cat])
    r = np.arange(GW)
    mask_bd = (r[:, None] // HD == r[None, :] // HD).astype(np.float32)
    eye = np.eye(GW, dtype=np.float32)
    cum = np.stack([np.tril(np.ones((c, c), np.float32)), np.triu(np.ones((c, c), np.float32))])
    return (jnp.asarray(tri), jnp.asarray(mask_bd, dtype=BF16), jnp.asarray(mask_bd), jnp.asarray(eye),
            jnp.asarray(cum, dtype=BF16), jnp.asarray(mask_bd, dtype=BF16))


def _rwkv_kernel(p_ref, s0_ref, mu_ref, wlora_ref, wg2_ref, wd0_ref, wa0_ref, kk_ref, ka_ref, rk_ref,
                 lng_ref, lnb_ref, tri_ref, mbd16_ref, mbd_ref, eye_ref, cum_ref, seg_ref,
                 ya_ref, sfin_ref,
                 r_s, v_s, kk_s, g_s, bon_s, a_s, lw_s, kd_s, y_s, st_s, *, seq_len):
    nc = seq_len // CHUNK
    scan_unroll = SCAN_UNROLL
    seg = seg_ref[...]

    def seg_sum(x):
        hi, lo = _split_bf16(x)
        halves = [_dot(hi[:, s:s + GW], seg) + _dot(lo[:, s:s + GW], seg) for s in range(0, RW, GW)]
        return jnp.concatenate(halves, axis=1)

    def phase_a(c, carry):
        t0 = pl.multiple_of(c * CHUNK, CHUNK)
        cur = p_ref[pl.ds(t0, CHUNK), :]
        prev_row = p_ref[pl.ds(jnp.maximum(t0 - 1, 0), 1), :] * (t0 > 0).astype(F32)
        next_row = p_ref[pl.ds(jnp.minimum(t0 + CHUNK, seq_len - 1), 1), :] * (t0 + CHUNK < seq_len).astype(F32)
        row = lax.broadcasted_iota(jnp.int32, cur.shape, 0)
        prev = jnp.where(row == 0, prev_row, pltpu.roll(cur, 1, 0))
        nxt = jnp.where(row == CHUNK - 1, next_row, pltpu.roll(cur, CHUNK - 1, 0))
        p = cur + mu_ref[...] * (0.5 * (prev + nxt) - cur)
        r = p[:, 0:RW]
        k = p[:, RW:2 * RW]
        v = p[:, 2 * RW:3 * RW]
        z = p[:, 3 * RW:3 * RW + 128]
        gd = p[:, 3 * RW + 128:3 * RW + 256]
        lane = lax.broadcasted_iota(jnp.int32, z.shape, 1)
        z = jnp.where(lane < 64, jnp.tanh(z), z)
        lora = _dot(z.astype(BF16), wlora_ref[...])
        g = _dot(_sigmoid(gd).astype(BF16), wg2_ref[...])
        kks = k * kk_ref[...]
        rows = pl.ds(t0, CHUNK)
        r_s[rows, :] = r
        v_s[rows, :] = v.astype(BF16)
        g_s[rows, :] = g
        kd_sum = jnp.zeros_like(k)
        for d in range(2):
            w_logit = wd0_ref[:, d * RW:(d + 1) * RW] + lora[:, d * RW:(d + 1) * RW]
            a = _sigmoid(wa0_ref[:, d * RW:(d + 1) * RW] + lora[:, (2 + d) * RW:(3 + d) * RW])
            kd = k * (1.0 + (a - 1.0) * ka_ref[...])
            lw_s[d, rows, :] = -DECAY_SCALE * _sigmoid(w_logit)
            a_s[d, rows, :] = a
            kd_s[d, rows, :] = kd
            kd_sum = kd_sum + kd
        sums = seg_sum(jnp.concatenate([kks * kks, r * kd_sum * rk_ref[...]], axis=0))
        kk_s[rows, :] = kks * lax.rsqrt(sums[:CHUNK] + 1e-12)
        bon_s[rows, :] = sums[CHUNK:] * v
        return carry

    lax.fori_loop(0, nc, phase_a, 0, unroll=2)

    mbd16 = mbd16_ref[...]
    mbd = mbd_ref[...]
    eye = eye_ref[...]

    def bd(x16):
        return jnp.concatenate([x16, x16, x16, x16], axis=0) * mbd16

    for u in range(4):
        s0 = s0_ref[0, u]
        st_s[u] = jnp.concatenate([s0, s0, s0, s0], axis=0) * mbd

    def prep(d, grp, t0):
        rows = pl.ds(t0, CHUNK)
        lanes = slice(grp * GW, (grp + 1) * GW)
        lw = lw_s[d, rows, lanes]
        hi, lo = _split_bf16(lw)
        cl = _dot(cum_ref[d], hi) + _dot(cum_ref[d], lo)
        e_cl = jnp.exp(cl)
        e_ce = jnp.exp(cl - lw)
        e_ncl = jnp.exp(-cl)
        tot = cl[CHUNK - 1:CHUNK, :] if d == 0 else cl[0:1, :]
        wc = jnp.exp(tot)
        kk = kk_s[rows, lanes]
        rt = r_s[rows, lanes] * e_cl
        bt = a_s[d, rows, lanes] * kk * e_ncl
        kt = kd_s[d, rows, lanes] * e_ncl
        kq16 = (kk * e_ce).astype(BF16)
        return dict(d=d, u=d * 2 + grp, rows=rows, lanes=lanes, wc=wc, rt=rt, kq16=kq16,
                    v16=v_s[rows, lanes], bt16=bt.astype(BF16), kt16=kt.astype(BF16),
                    btw16=(bt * wc).astype(BF16), ktw16=(kt * wc).astype(BF16),
                    g16=jnp.concatenate([kq16, rt.astype(BF16)], axis=0))

    def phase_b(i2, carry):
        us = []
        for j in range(scan_unroll):
            i = i2 * scan_unroll + j
            tf = pl.multiple_of(i * CHUNK, CHUNK)
            tb = pl.multiple_of((nc - 1 - i) * CHUNK, CHUNK)
            us += [prep(0, 0, tf), prep(1, 0, tb), prep(0, 1, tf), prep(1, 1, tb)]
        each = lambda fn: [fn(q) for q in us]
        strict = lambda q: tri_ref[2 * q['d']]
        incl = lambda q: tri_ref[2 * q['d'] + 1]
        out_b = each(lambda q: _dot_nt(q['g16'], bd(q['bt16'])))
        out_k = each(lambda q: _dot_nt(q['g16'], bd(q['kt16'])))
        a_bk = [o[:CHUNK] * strict(q) for o, q in zip(out_b, us)]
        p_rb = [(o[CHUNK:] * incl(q)).astype(BF16) for o, q in zip(out_b, us)]
        a_kk = [(o[:CHUNK] * strict(q)).astype(BF16) for o, q in zip(out_k, us)]
        p_rk = [(o[CHUNK:] * incl(q)).astype(BF16) for o, q in zip(out_k, us)]
        tinv = [tri_ref[4] - a for a in a_bk]
        pw = [a.astype(BF16) for a in a_bk]
        for _ in range(5):
            pw = [_dot(p, bd(p)).astype(BF16) for p in pw]
            tinv = [t + _dot(p, bd(t.astype(BF16))) for t, p in zip(tinv, pw)]
        bdv = each(lambda q: bd(q['v16']))
        akkv = [_dot(a, b) for a, b in zip(a_kk, bdv)]
        x = [_dot(t.astype(BF16), jnp.concatenate([bd(q['kq16']), bd(k.astype(BF16))], axis=1))
             for t, q, k in zip(tinv, us, akkv)]
        qu16 = [z.astype(BF16) for z in x]
        tn1 = [_dot_tn(q['btw16'], z) for q, z in zip(us, qu16)]
        tn2 = each(lambda q: _dot_tn(q['ktw16'], q['v16']))
        r1 = [_dot(p, jnp.concatenate([bd(z[:, :GW]), bd(z[:, GW:])], axis=1)) for p, z in zip(p_rb, qu16)]
        y0 = [_dot(p, b) - r[:, GW:] for p, b, r in zip(p_rk, bdv, r1)]
        for q, t1, t2, r, y in zip(us, tn1, tn2, r1, y0):
            mc = (eye * q['wc'] - t1[:, :GW]) * mbd
            ncm = (t2 - t1[:, GW:]) * mbd
            rh = q['rt'] - r[:, :GW]
            st16 = st_s[q['u']].astype(BF16)
            big = _dot(jnp.concatenate([rh.astype(BF16), mc.astype(BF16)], axis=0), st16)
            y_s[q['d'], q['rows'], q['lanes']] = big[:CHUNK] + y
            st_s[q['u']] = big[CHUNK:] + ncm
        return carry

    lax.fori_loop(0, nc // scan_unroll, phase_b, 0)

    for u in range(4):
        st = st_s[u]
        sfin_ref[0, u] = st[0:HD] + st[HD:2 * HD] + st[2 * HD:3 * HD] + st[3 * HD:4 * HD]

    def phase_c(c, carry):
        rows = pl.ds(pl.multiple_of(c * CHUNK, CHUNK), CHUNK)
        y = y_s[0, rows, :] + y_s[1, rows, :]
        mean = seg_sum(y) * (1.0 / HD)
        dlt = y - mean
        var = seg_sum(dlt * dlt) * (1.0 / HD)
        yn = dlt * lax.rsqrt(var + GN_EPS)
        ya_ref[rows, :] = (yn * lng_ref[...] + lnb_ref[...] + bon_s[rows, :]) * g_s[rows, :]
        return carry

    lax.fori_loop(0, nc, phase_c, 0, unroll=2)


def _rwkv(proj_a, s0_cat, wts, consts, seq_len, n_seq, first_block):
    full = lambda a: pl.BlockSpec(a.shape, lambda s, _n=a.ndim: (0,) * _n)
    p_mode = dict(pipeline_mode=pl.Buffered(1)) if n_seq <= 2 else {}
    in_specs = [pl.BlockSpec((seq_len, RWKV_IN), lambda s: (first_block + s, 0), **p_mode),
                pl.BlockSpec((1, 4, HD, GW), lambda s: (s, 0, 0, 0))]
    in_specs += [full(a) for a in wts] + [full(a) for a in consts]
    args = [proj_a, s0_cat] + list(wts) + list(consts)
    seq = (seq_len, RW)
    return pl.pallas_call(
        functools.partial(_rwkv_kernel, seq_len=seq_len),
        grid=(n_seq,),
        in_specs=in_specs,
        out_specs=[pl.BlockSpec((seq_len, RW), lambda s: (s, 0)),
                   pl.BlockSpec((1, 4, HD, GW), lambda s: (s, 0, 0, 0))],
        out_shape=[jax.ShapeDtypeStruct((n_seq * seq_len, RW), F32),
                   jax.ShapeDtypeStruct((n_seq, 4, HD, GW), F32)],
        scratch_shapes=[pltpu.VMEM(seq, F32), pltpu.VMEM(seq, BF16), pltpu.VMEM(seq, F32), pltpu.VMEM(seq, F32),
                        pltpu.VMEM(seq, F32),
                        pltpu.VMEM((2,) + seq, F32), pltpu.VMEM((2,) + seq, F32), pltpu.VMEM((2,) + seq, F32),
                        pltpu.VMEM((2,) + seq, F32), pltpu.VMEM((4, GW, GW), F32)],
        compiler_params=_cparams(("arbitrary",)),
        name="rwkv_%d" % seq_len,
    )(*args)


CONV_PAD = 16
VPAD = (CONV_K // 2) * GRID_W


def _conv_kernel(u_ref, w_ref, b_ref, g_ref, beta_ref, o_ref, pad_s, vpad_s, y_s, *, seq_len, grid):
    rb = 256
    glu = u_ref[:, :CW] * _sigmoid(u_ref[:, CW:])
    zeros = jnp.zeros((CONV_PAD, CW), F32)
    pad_s[0:CONV_PAD, :] = zeros
    pad_s[CONV_PAD + seq_len:2 * CONV_PAD + seq_len, :] = zeros
    pad_s[CONV_PAD:CONV_PAD + seq_len, :] = glu
    half = CW // 2
    if grid:
        vz = jnp.zeros((VPAD, half), F32)
        vpad_s[0:VPAD, :] = vz
        vpad_s[VPAD + seq_len:2 * VPAD + seq_len, :] = vz
        vpad_s[VPAD:VPAD + seq_len, :] = glu[:, half:]
    n_h_lanes = (half if grid else CW) // 128
    for r0 in range(0, seq_len, rb):
        for lb in range(n_h_lanes):
            lanes = slice(lb * 128, (lb + 1) * 128)
            acc = jnp.zeros((rb, 128), F32)
            if grid:
                col = lax.broadcasted_iota(jnp.int32, (rb, 128), 0) % GRID_W
            span = rb + 2 * CONV_PAD
            xpad = pad_s[r0:r0 + span, lanes]
            for res in range(8):
                xrot = xpad if res == 0 else pltpu.roll(xpad, span - res, 0)
                for k in range(CONV_K):
                    if (k + 1) % 8 != res:
                        continue
                    off = (k + 1) - res
                    term = xrot[off:off + rb] * w_ref[k:k + 1, lanes]
                    if grid:
                        src = col + (k - CONV_K // 2)
                        term = jnp.where((src >= 0) & (src < GRID_W), term, 0.0)
                    acc = acc + term
            y_s[r0:r0 + rb, lanes] = acc
        if grid:
            for lb in range(half // 128):
                lanes = slice(lb * 128, (lb + 1) * 128)
                wl = slice(half + lb * 128, half + (lb + 1) * 128)
                acc = jnp.zeros((rb, 128), F32)
                for k in range(CONV_K):
                    start = VPAD + r0 + (k - CONV_K // 2) * GRID_W
                    acc = acc + vpad_s[start:start + rb, lanes] * w_ref[k:k + 1, wl]
                y_s[r0:r0 + rb, wl] = acc
    for r0 in range(0, seq_len, rb):
        y = y_s[r0:r0 + rb, :] + b_ref[...]
        yl = _layer_norm(y, g_ref[...], beta_ref[...])
        o_ref[r0:r0 + rb, :] = yl * _sigmoid(yl)


def _conv(proj_b, wts, seq_len, n_seq, first_block, grid):
    full = lambda a: pl.BlockSpec(a.shape, lambda s, _n=a.ndim: (0,) * _n)
    in_specs = [pl.BlockSpec((seq_len, 2 * CW), lambda s: (first_block + s, 0))] + [full(a) for a in wts]
    args = [proj_b] + list(wts)
    return pl.pallas_call(
        functools.partial(_conv_kernel, seq_len=seq_len, grid=grid),
        grid=(n_seq,),
        in_specs=in_specs,
        out_specs=pl.BlockSpec((seq_len, CW), lambda s: (s, 0)),
        out_shape=jax.ShapeDtypeStruct((n_seq * seq_len, CW), F32),
        scratch_shapes=[pltpu.VMEM((seq_len + 2 * CONV_PAD, CW), F32),
                        pltpu.VMEM((seq_len + 2 * VPAD if grid else 8, CW // 2), F32),
                        pltpu.VMEM((seq_len, CW), F32)],
        compiler_params=_cparams(("arbitrary",)),
        name="conv_%d" % seq_len,
    )(*args)


def _outproj_kernel(yac_ref, yal_ref, ybc_ref, ybl_ref, xc_ref, xl_ref, g1_ref, sc2_ref, sh2_ref, wa_ref, wb_ref,
                    lg_ref, lb_ref, x1_ref, h2_ref, *, alpha, ctx_tiles):
    is_ctx = pl.program_id(0) < ctx_tiles
    ya = jnp.where(is_ctx, yac_ref[...], yal_ref[...])
    yb = jnp.where(is_ctx, ybc_ref[...], ybl_ref[...])
    x = jnp.where(is_ctx, xc_ref[...], xl_ref[...])
    mix = _dot(ya.astype(BF16), wa_ref[...]) + _dot(yb.astype(BF16), wb_ref[...])
    x1 = _layer_norm(alpha * x + g1_ref[0] * mix, lg_ref[...], lb_ref[...])
    x1_ref[...] = x1
    h2_ref[...] = x1 * (1.0 + sc2_ref[0]) + sh2_ref[0]


def _out_proj(ya_c, ya_l, yb_c, yb_l, x_ctx, x_lat, g1, sc2, sh2, w_oa, w_ob, ln_g, ln_b, tm, row_of_tile, alpha):
    t_all = x_ctx.shape[0] + x_lat.shape[0]
    ctx_tiles = x_ctx.shape[0] // tm
    tile = lambda w: pl.BlockSpec((tm, w), lambda i: (i, 0))
    modrow = pl.BlockSpec((1, 1, D_MODEL), lambda i: (row_of_tile(i), 0, 0))
    full = lambda a: pl.BlockSpec(a.shape, lambda i, _n=a.ndim: (0,) * _n)
    return pl.pallas_call(
        functools.partial(_outproj_kernel, alpha=alpha, ctx_tiles=ctx_tiles),
        grid=(t_all // tm,),
        in_specs=_two_source_specs(tm, RW, ctx_tiles) + _two_source_specs(tm, CW, ctx_tiles)
        + _two_source_specs(tm, D_MODEL, ctx_tiles)
        + [modrow, modrow, modrow, full(w_oa), full(w_ob), full(ln_g), full(ln_b)],
        out_specs=[tile(D_MODEL), tile(D_MODEL)],
        out_shape=[jax.ShapeDtypeStruct((t_all, D_MODEL), F32), jax.ShapeDtypeStruct((t_all, D_MODEL), F32)],
        compiler_params=_cparams(("arbitrary",)),
        name="out_proj",
    )(ya_c, ya_l, yb_c, yb_l, x_ctx, x_lat, g1, sc2, sh2, w_oa, w_ob, ln_g, ln_b)


def _route_kernel(h_ref, wt_hi_ref, wt_lo_ref, bias_ref, ustrict_ref,
                  eidx_ref, wts_ref, pos_ref, cnt_ref, carry_s, *, tm):
    i = pl.program_id(0)

    @pl.when(i == 0)
    def _():
        carry_s[...] = jnp.zeros_like(carry_s)

    h_hi, h_lo = _split_bf16(h_ref[...])
    logits = _dot_nt(wt_hi_ref[...], h_hi) + _dot_nt(wt_hi_ref[...], h_lo) + _dot_nt(wt_lo_ref[...], h_hi)
    scores = _sigmoid(logits)
    sel = scores + bias_ref[...]
    neg = jnp.float32(-jnp.inf)
    gsz = N_EXPERTS // N_GROUPS
    gs = []
    for g in range(N_GROUPS):
        blk = sel[g * gsz:(g + 1) * gsz]
        m1 = jnp.max(blk, axis=0, keepdims=True)
        eq = blk == m1
        cnt = jnp.sum(eq.astype(F32), axis=0, keepdims=True)
        m2 = jnp.max(jnp.where(eq, neg, blk), axis=0, keepdims=True)
        gs.append(m1 + jnp.where(cnt >= 2.0, m1, m2))
    masked = []
    for g in range(N_GROUPS):
        rank = jnp.zeros_like(gs[g])
        for o in range(N_GROUPS):
            if o == g:
                continue
            beats = (gs[o] > gs[g]) | ((gs[o] == gs[g]) & (o < g))
            rank = rank + beats.astype(F32)
        keep = rank < float(TOPK_GROUPS)
        masked.append(jnp.where(keep, sel[g * gsz:(g + 1) * gsz], neg))
    cur = jnp.concatenate(masked, axis=0)
    iota_e = lax.broadcasted_iota(jnp.int32, cur.shape, 0).astype(F32)
    idxs, ws = [], []
    selmask = jnp.zeros(cur.shape, F32)
    for _ in range(TOP_K):
        m = jnp.max(cur, axis=0, keepdims=True)
        idx = jnp.min(jnp.where(cur == m, iota_e, float(N_EXPERTS)), axis=0, keepdims=True)
        onehot = iota_e == idx
        ws.append(jnp.sum(jnp.where(onehot, scores, 0.0), axis=0, keepdims=True))
        idxs.append(idx)
        cur = jnp.where(onehot, neg, cur)
        selmask = jnp.where(onehot, 1.0, selmask)
    pos = _dot(selmask.astype(BF16), ustrict_ref[...]) + carry_s[...]
    carry_s[...] = carry_s[...] + jnp.sum(selmask, axis=1, keepdims=True)
    cnt_ref[...] = carry_s[...]
    wsum = ws[0]
    for k in range(1, TOP_K):
        wsum = wsum + ws[k]
    pks = [jnp.sum(jnp.where(iota_e == idxs[k], pos, 0.0), axis=0, keepdims=True) for k in range(TOP_K)]
    eidx_ref[...] = jnp.concatenate(idxs, axis=0).astype(jnp.int32)
    wts_ref[...] = jnp.concatenate([w / wsum * ROUTED_SCALE for w in ws], axis=0)
    pos_ref[...] = jnp.concatenate(pks, axis=0).astype(jnp.int32)


def _route(h2, wt_hi, wt_lo, bias_col, tm):
    t_all = h2.shape[0]
    ustrict = jnp.asarray(np.triu(np.ones((tm, tm), np.float32), 1), dtype=BF16)
    full = lambda a: pl.BlockSpec(a.shape, lambda i, _n=a.ndim: (0,) * _n)
    tok = pl.BlockSpec((TOP_K, tm), lambda i: (0, i))
    return pl.pallas_call(
        functools.partial(_route_kernel, tm=tm),
        grid=(t_all // tm,),
        in_specs=[pl.BlockSpec((tm, D_MODEL), lambda i: (i, 0)), full(wt_hi), full(wt_lo), full(bias_col),
                  full(ustrict)],
        out_specs=[tok, tok, tok, pl.BlockSpec((N_EXPERTS, 1), lambda i: (0, 0))],
        out_shape=[jax.ShapeDtypeStruct((TOP_K, t_all), jnp.int32), jax.ShapeDtypeStruct((TOP_K, t_all), F32),
                   jax.ShapeDtypeStruct((TOP_K, t_all), jnp.int32), jax.ShapeDtypeStruct((N_EXPERTS, 1), F32)],
        scratch_shapes=[pltpu.VMEM((N_EXPERTS, 1), F32)],
        compiler_params=_cparams(("arbitrary",)),
        name="route",
    )(h2, wt_hi, wt_lo, bias_col, ustrict)


def _slots_kernel(eidx_ref, pos_ref, pstart_ref, slotx_ref, sloty_ref):
    tm = eidx_ref.shape[1]
    iota_e = lax.broadcasted_iota(jnp.int32, (N_EXPERTS, tm), 0)
    for c, out in enumerate((slotx_ref, sloty_ref)):
        rows = []
        for k in range(TOP_K):
            onehot = iota_e == eidx_ref[k:k + 1, :]
            rows.append(jnp.sum(jnp.where(onehot, pstart_ref[:, c:c + 1], 0.0), axis=0, keepdims=True))
        out[...] = jnp.concatenate(rows, axis=0).astype(jnp.int32) + pos_ref[...]


def _slots(eidx, pos, pstart_cols, tm):
    t_all = eidx.shape[1]
    tok = pl.BlockSpec((TOP_K, tm), lambda i: (0, i))
    return pl.pallas_call(
        _slots_kernel,
        grid=(t_all // tm,),
        in_specs=[tok, tok, pl.BlockSpec((N_EXPERTS, 2), lambda i: (0, 0))],
        out_specs=[tok, tok],
        out_shape=[jax.ShapeDtypeStruct((TOP_K, t_all), jnp.int32), jax.ShapeDtypeStruct((TOP_K, t_all), jnp.int32)],
        compiler_params=_cparams(("arbitrary",)),
        name="slots",
    )(eidx, pos, pstart_cols)


TD_TOK = 512
XS_ALIGN = 8


def _dispatch_kernel(fstart_ref, flen_ref, total_ref, slot_ref, h_ref, xs_out, zero_s, sem, zsem, *, xs_rows):
    i = pl.program_id(0)

    def body(j, carry):
        for k in range(TOP_K):
            pltpu.make_async_copy(h_ref.at[pl.ds(j, 1), :], xs_out.at[pl.ds(slot_ref[0, 0, j * TOP_K + k], 1), :],
                                  sem).start(priority=k % 2)
        return carry

    lax.fori_loop(0, TD_TOK, body, 0)
    for k in range(TOP_K):
        pltpu.make_async_copy(h_ref, xs_out.at[pl.ds(0, TD_TOK), :], sem).wait()

    @pl.when(i == pl.num_programs(0) - 1)
    def _():
        zero_s[...] = jnp.zeros_like(zero_s)
        total = total_ref[0]
        n_tail = (xs_rows - total) // XS_ALIGN

        def gap_copy(e, r):
            return pltpu.make_async_copy(zero_s.at[pl.ds(0, 1), :], xs_out.at[pl.ds(fstart_ref[e] + r, 1), :], zsem)

        def tail_copy(q):
            start = pl.multiple_of(total + q * XS_ALIGN, XS_ALIGN)
            return pltpu.make_async_copy(zero_s, xs_out.at[pl.ds(start, XS_ALIGN), :], zsem)

        def for_gaps(fn):
            def per_expert(e, carry):
                for r in range(XS_ALIGN - 1):
                    @pl.when(r < flen_ref[e])
                    def _():
                        fn(gap_copy(e, r))
                return carry
            lax.fori_loop(0, N_EXPERTS, per_expert, 0)

        def for_tail(fn):
            def per_q(q, carry):
                fn(tail_copy(q))
                return carry
            lax.fori_loop(0, n_tail, per_q, 0)

        for_gaps(lambda cp: cp.start())
        for_tail(lambda cp: cp.start())
        for_gaps(lambda cp: cp.wait())
        for_tail(lambda cp: cp.wait())


def _dispatch(slot3, h2, fstart, flen, total, xs_rows):
    t_all = h2.shape[0]
    grid_spec = pltpu.PrefetchScalarGridSpec(
        num_scalar_prefetch=3,
        grid=(t_all // TD_TOK,),
        in_specs=[pl.BlockSpec((1, 1, TD_TOK * TOP_K), lambda i, a, b, c: (i, 0, 0), memory_space=pltpu.SMEM),
                  pl.BlockSpec((TD_TOK, D_MODEL), lambda i, a, b, c: (i, 0))],
        out_specs=pl.BlockSpec(memory_space=pl.ANY),
        scratch_shapes=[pltpu.VMEM((XS_ALIGN, D_MODEL), F32), pltpu.SemaphoreType.DMA(()),
                        pltpu.SemaphoreType.DMA(())],
    )
    return pl.pallas_call(
        functools.partial(_dispatch_kernel, xs_rows=xs_rows),
        grid_spec=grid_spec,
        out_shape=jax.ShapeDtypeStruct((xs_rows, D_MODEL), F32),
        compiler_params=_cparams(("arbitrary",)),
        name="dispatch",
    )(fstart, flen, total, slot3, h2)


X_RING = 8
Y_RING = 4
W_BUFFERS = 4


COPY_PARTS = 2


class _SplitCopy:
    def __init__(self, parts):
        self.parts = parts

    def start(self):
        for i, cp in enumerate(self.parts):
            cp.start(priority=i % 2)

    def wait(self):
        for cp in self.parts:
            cp.wait()


def _expert_kernel(bstart_ref, nblk_ref, nused_ref, xrow_ref, xs_hbm, wgu_hbm, wdn_hbm, ys_hbm,
                   xbuf, ybuf, wgu_f32, wdn_f32, wgu16, wdn16, sem_in, sem_out, sem_w, *, n_blocks):
    e = pl.program_id(0)
    nused = nused_ref[0]
    n = nblk_ref[e]
    g0 = bstart_ref[e]

    def split_copy(src, dst, sem, rows):
        h = rows // COPY_PARTS
        return _SplitCopy([pltpu.make_async_copy(src.at[pl.ds(r0, h), :], dst.at[pl.ds(r0, h), :], sem)
                           for r0 in range(0, rows, h)])

    def w_copies(ex):
        s = ex % W_BUFFERS
        return (split_copy(wgu_hbm.at[ex], wgu_f32.at[s], sem_w.at[0, s], D_MODEL),
                split_copy(wdn_hbm.at[ex], wdn_f32.at[s], sem_w.at[1, s], EXPERT_DIM))

    @pl.when(e == 0)
    def _():
        for q in range(W_BUFFERS - 1):
            for cp in w_copies(q):
                cp.start()

    @pl.when(e + (W_BUFFERS - 1) < N_EXPERTS)
    def _():
        for cp in w_copies(e + (W_BUFFERS - 1)):
            cp.start()

    def in_copy(g, s):
        start = pl.multiple_of(xrow_ref[g], XS_ALIGN)
        return split_copy(xs_hbm.at[pl.ds(start, MOE_BLOCK), :], xbuf.at[s], sem_in.at[s], MOE_BLOCK)

    def out_copy(g, s):
        return split_copy(ybuf.at[s], ys_hbm.at[pl.ds(g * MOE_BLOCK, MOE_BLOCK), :], sem_out.at[s], MOE_BLOCK)

    @pl.when(e == 0)
    def _():
        for q in range(X_RING - 1):
            @pl.when(q < nused)
            def _():
                in_copy(q, q).start()

    for cp in w_copies(e):
        cp.wait()

    @pl.when(n > 0)
    def _():
        wgu16[...] = wgu_f32[e % W_BUFFERS].astype(BF16)
        wdn16[...] = wdn_f32[e % W_BUFFERS].astype(BF16)

    def block(b, carry):
        g = g0 + b
        s = g % Y_RING
        ahead = g + (X_RING - 1)

        @pl.when(ahead < nused)
        def _():
            in_copy(ahead, ahead % X_RING).start()

        in_copy(g, g % X_RING).wait()
        gu = _dot(xbuf[g % X_RING].astype(BF16), wgu16[...])
        gate = gu[:, :EXPERT_DIM]
        act = (gate * _sigmoid(gate) * gu[:, EXPERT_DIM:]).astype(BF16)
        y = _dot(act, wdn16[...])

        @pl.when(g >= Y_RING)
        def _():
            out_copy(g - Y_RING, s).wait()

        ybuf[s] = y
        out_copy(g, s).start()
        return carry

    lax.fori_loop(0, n, block, 0)

    @pl.when(e == pl.num_programs(0) - 1)
    def _():
        for d in range(1, Y_RING + 1):
            @pl.when(nused >= d)
            def _():
                out_copy(nused - d, (nused - d) % Y_RING).wait()

        ybuf[0] = jnp.zeros((MOE_BLOCK, D_MODEL), F32)

        def tail(g, carry):
            out_copy(g, 0).start()
            out_copy(g, 0).wait()
            return carry

        lax.fori_loop(nused, n_blocks, tail, 0)


def _experts(xs, bstart, nblk, nused, xrow, w_gu, w_down):
    n_blocks = xrow.shape[0]
    grid_spec = pltpu.PrefetchScalarGridSpec(
        num_scalar_prefetch=4,
        grid=(N_EXPERTS,),
        in_specs=[pl.BlockSpec(memory_space=pl.ANY), pl.BlockSpec(memory_space=pl.ANY),
                  pl.BlockSpec(memory_space=pl.ANY)],
        out_specs=pl.BlockSpec(memory_space=pl.ANY),
        scratch_shapes=[pltpu.VMEM((X_RING, MOE_BLOCK, D_MODEL), F32),
                        pltpu.VMEM((Y_RING, MOE_BLOCK, D_MODEL), F32),
                        pltpu.VMEM((W_BUFFERS, D_MODEL, 2 * EXPERT_DIM), F32),
                        pltpu.VMEM((W_BUFFERS, EXPERT_DIM, D_MODEL), F32),
                        pltpu.VMEM((D_MODEL, 2 * EXPERT_DIM), BF16),
                        pltpu.VMEM((EXPERT_DIM, D_MODEL), BF16),
                        pltpu.SemaphoreType.DMA((X_RING,)), pltpu.SemaphoreType.DMA((Y_RING,)),
                        pltpu.SemaphoreType.DMA((2, W_BUFFERS))],
    )
    return pl.pallas_call(
        functools.partial(_expert_kernel, n_blocks=n_blocks),
        grid_spec=grid_spec,
        out_shape=jax.ShapeDtypeStruct((n_blocks * MOE_BLOCK, D_MODEL), F32),
        compiler_params=_cparams(("arbitrary",)),
        name="experts",
    )(bstart, nblk, nused, xrow, xs, w_gu, w_down)


TC_TOK = 128


def _combine_kernel(slot_ref, slot_next_ref, wt_ref, x1_ref, h2_ref, g2_ref, sgu_ref, sdn_ref, lg_ref, lb_ref,
                    y_hbm, oc_ref, ol_ref, gbuf, sems, *, n_tiles, ctx_tiles, alpha):
    i = pl.program_id(0)

    def row_copy(sref, b, j, k):
        return pltpu.make_async_copy(y_hbm.at[pl.ds(sref[0, 0, j * TOP_K + k], 1), :],
                                     gbuf.at[b, k, pl.ds(j, 1), :], sems.at[b])

    def wait_tile(b):
        for k in range(TOP_K):
            pltpu.make_async_copy(y_hbm.at[pl.ds(0, TC_TOK), :], gbuf.at[b, k], sems.at[b]).wait()

    @pl.when(i == 0)
    def _():
        def body(j, carry):
            for k in range(TOP_K):
                row_copy(slot_ref, 0, j, k).start(priority=k % 2)
            return carry
        lax.fori_loop(0, TC_TOK, body, 0)

    b = i % 2
    nb = 1 - b

    def issue_next(j0, j1):
        for j in range(j0, j1):
            for k in range(TOP_K):
                row_copy(slot_next_ref, nb, j, k).start(priority=k % 2)

    issue_next(0, TC_TOK // 2)
    h = h2_ref[...].astype(BF16)
    su = _dot(h, sgu_ref[...])
    sg = su[:, :EXPERT_DIM]
    act = (sg * _sigmoid(sg) * su[:, EXPERT_DIM:]).astype(BF16)
    ffn = _dot(act, sdn_ref[...])
    wait_tile(b)
    issue_next(TC_TOK // 2, TC_TOK)
    for k in range(TOP_K):
        ffn = ffn + gbuf[b, k] * wt_ref[:, k:k + 1]
    out = _layer_norm(alpha * x1_ref[...] + g2_ref[0] * ffn, lg_ref[...], lb_ref[...])

    @pl.when(i < ctx_tiles)
    def _():
        oc_ref[...] = out

    @pl.when(i >= ctx_tiles)
    def _():
        ol_ref[...] = out

    @pl.when(i == n_tiles - 1)
    def _():
        wait_tile(nb)


def _combine(slot3, wts_t, x1, h2, g2, sh_gu, sh_dn, ln_g, ln_b, y_sorted, row_of_tile, alpha, tok_ctx):
    t_all = x1.shape[0]
    n_tiles = t_all // TC_TOK
    ctx_tiles = tok_ctx // TC_TOK
    tile = lambda w: pl.BlockSpec((TC_TOK, w), lambda i: (i, 0))
    full = lambda a: pl.BlockSpec(a.shape, lambda i, _n=a.ndim: (0,) * _n)
    return pl.pallas_call(
        functools.partial(_combine_kernel, n_tiles=n_tiles, ctx_tiles=ctx_tiles, alpha=alpha),
        grid=(n_tiles,),
        in_specs=[pl.BlockSpec((1, 1, TC_TOK * TOP_K), lambda i: (i, 0, 0), memory_space=pltpu.SMEM),
                  pl.BlockSpec((1, 1, TC_TOK * TOP_K), lambda i: (jnp.minimum(i + 1, n_tiles - 1), 0, 0),
                               memory_space=pltpu.SMEM),
                  tile(TOP_K), tile(D_MODEL), tile(D_MODEL),
                  pl.BlockSpec((1, 1, D_MODEL), lambda i: (row_of_tile(i), 0, 0)),
                  full(sh_gu), full(sh_dn), full(ln_g), full(ln_b),
                  pl.BlockSpec(memory_space=pl.ANY)],
        out_specs=_two_source_specs(TC_TOK, D_MODEL, ctx_tiles),
        out_shape=[jax.ShapeDtypeStruct((tok_ctx, D_MODEL), F32),
                   jax.ShapeDtypeStruct((t_all - tok_ctx, D_MODEL), F32)],
        scratch_shapes=[pltpu.VMEM((2, TOP_K, TC_TOK, D_MODEL), F32), pltpu.SemaphoreType.DMA((2,))],
        compiler_params=_cparams(("arbitrary",)),
        name="combine",
    )(slot3, slot3, wts_t, x1, h2, g2, sh_gu, sh_dn, ln_g, ln_b, y_sorted)


def kernel(x_prompt, x_sample, state_rwkv, c, c_ctx, w_ada, b_ada, w_in, mu_shift, w_decay0, w_decay2, w_a0, w_a2, w_g2, k_k, k_a, r_k, lnx_g, lnx_b, conv_w, conv_b, conv_ln_g, conv_ln_b, w_out, ln1_g, ln1_b, router_w, router_bias, expert_w_gu, expert_w_down, shared_w_gu, shared_w_down, ln2_g, ln2_b):
    depth = w_ada.shape[0]
    assert depth == 1
    alpha = (2.0 * depth) ** 0.25
    n_ctx, t_ctx, _ = x_prompt.shape
    n_lat, t_lat, _ = x_sample.shape
    tok_ctx = n_ctx * t_ctx
    tok_lat = n_lat * t_lat
    t_all = tok_ctx + tok_lat
    tm = 1024
    assert tok_ctx % tm == 0 and t_lat % tm == 0 and t_lat // GRID_W * GRID_W == t_lat
    l = 0

    x_ctx = x_prompt.reshape(tok_ctx, D_MODEL)
    x_lat = x_sample.reshape(tok_lat, D_MODEL)

    cond8 = jnp.zeros((8, D_MODEL), F32).at[0].set(c_ctx).at[1:1 + n_lat].set(c)
    mod = _ada(cond8, w_ada[l], b_ada[l])
    sh1, sc1, g1, sh2, sc2, g2 = [m.reshape(8, 1, D_MODEL) for m in jnp.split(mod, 6, axis=-1)]

    def row_of_tile_for(tile_rows):
        ctx_tiles = tok_ctx // tile_rows
        per_seq = t_lat // tile_rows
        return lambda i: jnp.where(i < ctx_tiles, 0, 1 + (i - ctx_tiles) // per_seq)

    w_in_a = w_in[l][:, :RWKV_IN].astype(BF16)
    w_in_b = w_in[l][:, RWKV_IN:].astype(BF16)
    proj_a, proj_b = _in_proj(x_ctx, x_lat, sc1, sh1, w_in_a, w_in_b, tm, row_of_tile_for(tm))

    zeros64 = jnp.zeros((64, 2 * RW), F32)
    wd2 = jnp.concatenate([w_decay2[l, 0], w_decay2[l, 1]], axis=1)
    wa2 = jnp.concatenate([w_a2[l, 0], w_a2[l, 1]], axis=1)
    w_lora = jnp.concatenate([jnp.concatenate([wd2, zeros64], axis=1),
                              jnp.concatenate([zeros64, wa2], axis=1)], axis=0).astype(BF16)
    row = lambda v: v.reshape(1, -1)
    rwkv_w = [row(mu_shift[l]), w_lora, w_g2[l].astype(BF16), row(w_decay0[l]), row(w_a0[l]),
              row(k_k[l]), row(k_a[l]), row(r_k[l]), row(lnx_g[l]), row(lnx_b[l])]
    consts = _rwkv_consts()

    def to_cat(s):
        b = s.shape[0]
        s = s.reshape(b, 2, 2, 4, HD, HD)
        return jnp.transpose(s, (0, 1, 2, 5, 3, 4)).reshape(b, 4, HD, GW)

    s0_ctx = jnp.zeros((n_ctx, 4, HD, GW), F32)
    s0_lat = to_cat(state_rwkv[:, l].astype(F32))
    ya_c, sfin_ctx = _rwkv(proj_a, s0_ctx, rwkv_w, consts, t_ctx, n_ctx, 0)
    ya_l, _ = _rwkv(proj_a, s0_lat, rwkv_w, consts, t_lat, n_lat, tok_ctx // t_lat)
    sfin = jnp.transpose(sfin_ctx.reshape(n_ctx, 2, 2, HD, 4, HD), (0, 1, 2, 4, 5, 3))
    new_state = sfin.reshape(n_ctx, 1, 2, 2 * 4, HD, HD).astype(x_prompt.dtype)

    conv_wts = [conv_w[l], row(conv_b[l]), row(conv_ln_g[l]), row(conv_ln_b[l])]
    yb_c = _conv(proj_b, conv_wts, t_ctx, n_ctx, 0, False)
    yb_l = _conv(proj_b, conv_wts, t_lat, n_lat, tok_ctx // t_lat, True)

    w_oa = w_out[l][:RW].astype(BF16)
    w_ob = w_out[l][RW:].astype(BF16)
    x1, h2 = _out_proj(ya_c, ya_l, yb_c, yb_l, x_ctx, x_lat, g1, sc2, sh2, w_oa, w_ob, row(ln1_g[l]), row(ln1_b[l]), tm,
                       row_of_tile_for(tm), alpha)

    rwt = router_w[l].T
    rwt_hi = rwt.astype(BF16)
    rwt_lo = (rwt - rwt_hi.astype(F32)).astype(BF16)
    eidx, wts, pos, counts = _route(h2, rwt_hi, rwt_lo, router_bias[l].reshape(N_EXPERTS, 1), 512)

    n_assign = t_all * TOP_K
    n_blocks = (n_assign + N_EXPERTS * (MOE_BLOCK - 1) + MOE_BLOCK - 1) // MOE_BLOCK
    cnt = counts.reshape(N_EXPERTS).astype(jnp.int32)
    padded = (cnt + MOE_BLOCK - 1) // MOE_BLOCK * MOE_BLOCK
    pend = jnp.cumsum(padded)
    pstart = pend - padded
    padded_x = (cnt + XS_ALIGN - 1) // XS_ALIGN * XS_ALIGN
    pend_x = jnp.cumsum(padded_x)
    pstart_x = pend_x - padded_x
    xs_rows = n_assign + N_EXPERTS * (XS_ALIGN - 1) // XS_ALIGN * XS_ALIGN + MOE_BLOCK
    slot_x, slot_y = _slots(eidx, pos, jnp.stack([pstart_x, pstart], axis=1).astype(F32), 512)
    per_tile = lambda s, tok: s.T.reshape(t_all // tok, 1, tok * TOP_K)
    xs = _dispatch(per_tile(slot_x, TD_TOK), h2, (pstart_x + cnt).astype(jnp.int32), (padded_x - cnt).astype(jnp.int32),
                   pend_x[-1].astype(jnp.int32).reshape(1), xs_rows)
    bstart = (pstart // MOE_BLOCK).astype(jnp.int32)
    nblk = (padded // MOE_BLOCK).astype(jnp.int32)
    nused = (pend[-1] // MOE_BLOCK).astype(jnp.int32).reshape(1)
    blk_ids = jnp.arange(n_blocks, dtype=jnp.int32)
    blk_e = jnp.minimum(jnp.sum((pend[None, :] <= (blk_ids * MOE_BLOCK)[:, None]).astype(jnp.int32), axis=1),
                        N_EXPERTS - 1)
    shift = jnp.sum(jnp.where(blk_e[:, None] == jnp.arange(N_EXPERTS, dtype=jnp.int32)[None, :],
                              (pstart_x - pstart)[None, :], 0), axis=1)
    xrow = jnp.clip(blk_ids * MOE_BLOCK + shift, 0, xs_rows - MOE_BLOCK).astype(jnp.int32)
    y_sorted = _experts(xs, bstart, nblk, nused, xrow, expert_w_gu[l], expert_w_down[l])
    y_ctx, y_lat = _combine(per_tile(slot_y, TC_TOK), wts.T, x1, h2, g2, shared_w_gu[l].astype(BF16),
                            shared_w_down[l].astype(BF16), row(ln2_g[l]), row(ln2_b[l]), y_sorted,
                            row_of_tile_for(TC_TOK), alpha, tok_ctx)
    return (y_ctx.reshape(n_ctx, t_ctx, D_MODEL), y_lat.reshape(n_lat, t_lat, D_MODEL), new_state)
```

```python
import functools
import math

import numpy as np
import jax
import jax.numpy as jnp
from jax import lax
from jax.experimental import pallas as pl
from jax.experimental.pallas import tpu as pltpu

F32 = jnp.float32
BF16 = jnp.bfloat16

D_MODEL = 1024
RW = 512
HD = 64
CW = 512
CONV_K = 31
RWKV_IN = 3 * RW + 64 + 64 + 128
N_EXPERTS = 256
TOP_K = 8
N_GROUPS = 8
TOPK_GROUPS = 4
EXPERT_DIM = 256
ROUTED_SCALE = 2.5
MOE_BLOCK = 128
GRID_W = 64
LN_EPS = 1e-5
GN_EPS = 64e-5
CHUNK = 64
GW = 256
SCAN_UNROLL = 2
DECAY_SCALE = math.exp(-0.5)
VMEM_LIMIT = 56 * 1024 * 1024


def _cparams(sem):
    return pltpu.CompilerParams(dimension_semantics=sem, vmem_limit_bytes=VMEM_LIMIT)


def _split_bf16(x):
    hi = x.astype(BF16)
    lo = (x - hi.astype(F32)).astype(BF16)
    return hi, lo


def _dot(a, b):
    return jnp.dot(a, b, preferred_element_type=F32)


def _dot_nt(a, b):
    return lax.dot_general(a, b, (((1,), (1,)), ((), ())), preferred_element_type=F32)


def _dot_tn(a, b):
    return lax.dot_general(a, b, (((0,), (0,)), ((), ())), preferred_element_type=F32)


def _sigmoid(x):
    return 1.0 / (1.0 + jnp.exp(-x))


def _layer_norm(x, g, b):
    mu = jnp.mean(x, axis=-1, keepdims=True)
    d = x - mu
    var = jnp.mean(d * d, axis=-1, keepdims=True)
    return d * lax.rsqrt(var + LN_EPS) * g + b


def _ada_kernel(c_ref, w_ref, b_ref, o_ref):
    c = c_ref[...]
    s = c * _sigmoid(c)
    s_hi, s_lo = _split_bf16(s)
    w_hi, w_lo = _split_bf16(w_ref[...])
    o_ref[...] = _dot(s_hi, w_hi) + _dot(s_lo, w_hi) + _dot(s_hi, w_lo) + b_ref[...]


def _ada(cond8, w_ada, b_ada):
    n = w_ada.shape[1]
    tn = 1536
    return pl.pallas_call(
        _ada_kernel,
        grid=(n // tn,),
        in_specs=[pl.BlockSpec((8, D_MODEL), lambda j: (0, 0)),
                  pl.BlockSpec((D_MODEL, tn), lambda j: (0, j)),
                  pl.BlockSpec((1, tn), lambda j: (0, j))],
        out_specs=pl.BlockSpec((8, tn), lambda j: (0, j)),
        out_shape=jax.ShapeDtypeStruct((8, n), F32),
        compiler_params=_cparams(("arbitrary",)),
        name="ada",
    )(cond8, w_ada, b_ada.reshape(1, n))


def _two_source_specs(tm, width, ctx_tiles):
    return [pl.BlockSpec((tm, width), lambda i: (jnp.minimum(i, ctx_tiles - 1), 0)),
            pl.BlockSpec((tm, width), lambda i: (jnp.maximum(i - ctx_tiles, 0), 0))]


def _inproj_kernel(xc_ref, xl_ref, sc_ref, sh_ref, wa_ref, wb_ref, oa_ref, ob_ref, *, ctx_tiles):
    x = jnp.where(pl.program_id(0) < ctx_tiles, xc_ref[...], xl_ref[...])
    h = (x * (1.0 + sc_ref[0]) + sh_ref[0]).astype(BF16)
    oa_ref[...] = _dot(h, wa_ref[...])
    ob_ref[...] = _dot(h, wb_ref[...])


def _in_proj(x_ctx, x_lat, sc1, sh1, w_a, w_b, tm, row_of_tile):
    t_all = x_ctx.shape[0] + x_lat.shape[0]
    ctx_tiles = x_ctx.shape[0] // tm
    return pl.pallas_call(
        functools.partial(_inproj_kernel, ctx_tiles=ctx_tiles),
        grid=(t_all // tm,),
        in_specs=_two_source_specs(tm, D_MODEL, ctx_tiles) + [
                  pl.BlockSpec((1, 1, D_MODEL), lambda i: (row_of_tile(i), 0, 0)),
                  pl.BlockSpec((1, 1, D_MODEL), lambda i: (row_of_tile(i), 0, 0)),
                  pl.BlockSpec(w_a.shape, lambda i: (0, 0)),
                  pl.BlockSpec(w_b.shape, lambda i: (0, 0))],
        out_specs=[pl.BlockSpec((tm, RWKV_IN), lambda i: (i, 0)),
                   pl.BlockSpec((tm, 2 * CW), lambda i: (i, 0))],
        out_shape=[jax.ShapeDtypeStruct((t_all, RWKV_IN), F32),
                   jax.ShapeDtypeStruct((t_all, 2 * CW), F32)],
        compiler_params=_cparams(("arbitrary",)),
        name="in_proj",
    )(x_ctx, x_lat, sc1, sh1, w_a, w_b)


def _rwkv_consts():
    c = CHUNK
    t = np.arange(c)[:, None]
    s = np.arange(GW)[None, :] % c
    sl = (s < t).astype(np.float32)
    il = (s <= t).astype(np.float32)
    su = (s > t).astype(np.float32)
    iu = (s >= t).astype(np.float32)
    eye_cat = (s == t).astype(np.float32)
    tri = np.stack([sl, il, su, iu, eye_cat])
    r = np.arange(GW)
    mask_bd = (r[:, None] // HD == r[None, :] // HD).astype(np.float32)
    eye = np.eye(GW, dtype=np.float32)
    cum = np.stack([np.tril(np.ones((c, c), np.float32)), np.triu(np.ones((c, c), np.float32))])
    return (jnp.asarray(tri), jnp.asarray(mask_bd, dtype=BF16), jnp.asarray(mask_bd), jnp.asarray(eye),
            jnp.asarray(cum, dtype=BF16), jnp.asarray(mask_bd, dtype=BF16))


def _rwkv_kernel(p_ref, s0_ref, mu_ref, wlora_ref, wg2_ref, wd0_ref, wa0_ref, kk_ref, ka_ref, rk_ref,
                 lng_ref, lnb_ref, tri_ref, mbd16_ref, mbd_ref, eye_ref, cum_ref, seg_ref,
                 ya_ref, sfin_ref,
                 r_s, v_s, kk_s, g_s, bon_s, a_s, lw_s, kd_s, y_s, st_s, *, seq_len):
    nc = seq_len // CHUNK
    scan_unroll = SCAN_UNROLL
    seg = seg_ref[...]

    def seg_sum(x):
        hi, lo = _split_bf16(x)
        halves = [_dot(hi[:, s:s + GW], seg) + _dot(lo[:, s:s + GW], seg) for s in range(0, RW, GW)]
        return jnp.concatenate(halves, axis=1)

    def phase_a(c, carry):
        t0 = pl.multiple_of(c * CHUNK, CHUNK)
        cur = p_ref[pl.ds(t0, CHUNK), :]
        prev_row = p_ref[pl.ds(jnp.maximum(t0 - 1, 0), 1), :] * (t0 > 0).astype(F32)
        next_row = p_ref[pl.ds(jnp.minimum(t0 + CHUNK, seq_len - 1), 1), :] * (t0 + CHUNK < seq_len).astype(F32)
        row = lax.broadcasted_iota(jnp.int32, cur.shape, 0)
        prev = jnp.where(row == 0, prev_row, pltpu.roll(cur, 1, 0))
        nxt = jnp.where(row == CHUNK - 1, next_row, pltpu.roll(cur, CHUNK - 1, 0))
        p = cur + mu_ref[...] * (0.5 * (prev + nxt) - cur)
        r = p[:, 0:RW]
        k = p[:, RW:2 * RW]
        v = p[:, 2 * RW:3 * RW]
        z = p[:, 3 * RW:3 * RW + 128]
        gd = p[:, 3 * RW + 128:3 * RW + 256]
        lane = lax.broadcasted_iota(jnp.int32, z.shape, 1)
        z = jnp.where(lane < 64, jnp.tanh(z), z)
        lora = _dot(z.astype(BF16), wlora_ref[...])
        g = _dot(_sigmoid(gd).astype(BF16), wg2_ref[...])
        kks = k * kk_ref[...]
        rows = pl.ds(t0, CHUNK)
        r_s[rows, :] = r
        v_s[rows, :] = v.astype(BF16)
        g_s[rows, :] = g
        kd_sum = jnp.zeros_like(k)
        for d in range(2):
            w_logit = wd0_ref[:, d * RW:(d + 1) * RW] + lora[:, d * RW:(d + 1) * RW]
            a = _sigmoid(wa0_ref[:, d * RW:(d + 1) * RW] + lora[:, (2 + d) * RW:(3 + d) * RW])
            kd = k * (1.0 + (a - 1.0) * ka_ref[...])
            lw_s[d, rows, :] = -DECAY_SCALE * _sigmoid(w_logit)
            a_s[d, rows, :] = a
            kd_s[d, rows, :] = kd
            kd_sum = kd_sum + kd
        sums = seg_sum(jnp.concatenate([kks * kks, r * kd_sum * rk_ref[...]], axis=0))
        kk_s[rows, :] = kks * lax.rsqrt(sums[:CHUNK] + 1e-12)
        bon_s[rows, :] = sums[CHUNK:] * v
        return carry

    lax.fori_loop(0, nc, phase_a, 0, unroll=2)

    mbd16 = mbd16_ref[...]
    mbd = mbd_ref[...]
    eye = eye_ref[...]

    def bd(x16):
        return jnp.concatenate([x16, x16, x16, x16], axis=0) * mbd16

    for u in range(4):
        s0 = s0_ref[0, u]
        st_s[u] = jnp.concatenate([s0, s0, s0, s0], axis=0) * mbd

    def prep(d, grp, t0):
        rows = pl.ds(t0, CHUNK)
        lanes = slice(grp * GW, (grp + 1) * GW)
        lw = lw_s[d, rows, lanes]
        hi, lo = _split_bf16(lw)
        cl = _dot(cum_ref[d], hi) + _dot(cum_ref[d], lo)
        e_cl = jnp.exp(cl)
        e_ce = jnp.exp(cl - lw)
        e_ncl = jnp.exp(-cl)
        tot = cl[CHUNK - 1:CHUNK, :] if d == 0 else cl[0:1, :]
        wc = jnp.exp(tot)
        kk = kk_s[rows, lanes]
        rt = r_s[rows, lanes] * e_cl
        bt = a_s[d, rows, lanes] * kk * e_ncl
        kt = kd_s[d, rows, lanes] * e_ncl
        kq16 = (kk * e_ce).astype(BF16)
        return dict(d=d, u=d * 2 + grp, rows=rows, lanes=lanes, wc=wc, rt=rt, kq16=kq16,
                    v16=v_s[rows, lanes], bt16=bt.astype(BF16), kt16=kt.astype(BF16),
                    btw16=(bt * wc).astype(BF16), ktw16=(kt * wc).astype(BF16),
                    g16=jnp.concatenate([kq16, rt.astype(BF16)], axis=0))

    def phase_b(i2, carry):
        us = []
        for j in range(scan_unroll):
            i = i2 * scan_unroll + j
            tf = pl.multiple_of(i * CHUNK, CHUNK)
            tb = pl.multiple_of((nc - 1 - i) * CHUNK, CHUNK)
            us += [prep(0, 0, tf), prep(1, 0, tb), prep(0, 1, tf), prep(1, 1, tb)]
        each = lambda fn: [fn(q) for q in us]
        strict = lambda q: tri_ref[2 * q['d']]
        incl = lambda q: tri_ref[2 * q['d'] + 1]
        out_b = each(lambda q: _dot_nt(q['g16'], bd(q['bt16'])))
        out_k = each(lambda q: _dot_nt(q['g16'], bd(q['kt16'])))
        a_bk = [o[:CHUNK] * strict(q) for o, q in zip(out_b, us)]
        p_rb = [(o[CHUNK:] * incl(q)).astype(BF16) for o, q in zip(out_b, us)]
        a_kk = [(o[:CHUNK] * strict(q)).astype(BF16) for o, q in zip(out_k, us)]
        p_rk = [(o[CHUNK:] * incl(q)).astype(BF16) for o, q in zip(out_k, us)]
        tinv = [tri_ref[4] - a for a in a_bk]
        pw = [a.astype(BF16) for a in a_bk]
        for _ in range(5):
            pw = [_dot(p, bd(p)).astype(BF16) for p in pw]
            tinv = [t + _dot(p, bd(t.astype(BF16))) for t, p in zip(tinv, pw)]
        bdv = each(lambda q: bd(q['v16']))
        akkv = [_dot(a, b) for a, b in zip(a_kk, bdv)]
        x = [_dot(t.astype(BF16), jnp.concatenate([bd(q['kq16']), bd(k.astype(BF16))], axis=1))
             for t, q, k in zip(tinv, us, akkv)]
        qu16 = [z.astype(BF16) for z in x]
        tn1 = [_dot_tn(q['btw16'], z) for q, z in zip(us, qu16)]
        tn2 = each(lambda q: _dot_tn(q['ktw16'], q['v16']))
        r1 = [_dot(p, jnp.concatenate([bd(z[:, :GW]), bd(z[:, GW:])], axis=1)) for p, z in zip(p_rb, qu16)]
        y0 = [_dot(p, b) - r[:, GW:] for p, b, r in zip(p_rk, bdv, r1)]
        for q, t1, t2, r, y in zip(us, tn1, tn2, r1, y0):
            mc = (eye * q['wc'] - t1[:, :GW]) * mbd
            ncm = (t2 - t1[:, GW:]) * mbd
            rh = q['rt'] - r[:, :GW]
            st16 = st_s[q['u']].astype(BF16)
            big = _dot(jnp.concatenate([rh.astype(BF16), mc.astype(BF16)], axis=0), st16)
            y_s[q['d'], q['rows'], q['lanes']] = big[:CHUNK] + y
            st_s[q['u']] = big[CHUNK:] + ncm
        return carry

    lax.fori_loop(0, nc // scan_unroll, phase_b, 0)

    for u in range(4):
        st = st_s[u]
        sfin_ref[0, u] = st[0:HD] + st[HD:2 * HD] + st[2 * HD:3 * HD] + st[3 * HD:4 * HD]

    def phase_c(c, carry):
        rows = pl.ds(pl.multiple_of(c * CHUNK, CHUNK), CHUNK)
        y = y_s[0, rows, :] + y_s[1, rows, :]
        mean = seg_sum(y) * (1.0 / HD)
        dlt = y - mean
        var = seg_sum(dlt * dlt) * (1.0 / HD)
        yn = dlt * lax.rsqrt(var + GN_EPS)
        ya_ref[rows, :] = (yn * lng_ref[...] + lnb_ref[...] + bon_s[rows, :]) * g_s[rows, :]
        return carry

    lax.fori_loop(0, nc, phase_c, 0, unroll=2)


def _rwkv(proj_a, s0_cat, wts, consts, seq_len, n_seq, first_block):
    full = lambda a: pl.BlockSpec(a.shape, lambda s, _n=a.ndim: (0,) * _n)
    p_mode = dict(pipeline_mode=pl.Buffered(1)) if n_seq <= 2 else {}
    in_specs = [pl.BlockSpec((seq_len, RWKV_IN), lambda s: (first_block + s, 0), **p_mode),
                pl.BlockSpec((1, 4, HD, GW), lambda s: (s, 0, 0, 0))]
    in_specs += [full(a) for a in wts] + [full(a) for a in consts]
    args = [proj_a, s0_cat] + list(wts) + list(consts)
    seq = (seq_len, RW)
    return pl.pallas_call(
        functools.partial(_rwkv_kernel, seq_len=seq_len),
        grid=(n_seq,),
        in_specs=in_specs,
        out_specs=[pl.BlockSpec((seq_len, RW), lambda s: (s, 0)),
                   pl.BlockSpec((1, 4, HD, GW), lambda s: (s, 0, 0, 0))],
        out_shape=[jax.ShapeDtypeStruct((n_seq * seq_len, RW), F32),
                   jax.ShapeDtypeStruct((n_seq, 4, HD, GW), F32)],
        scratch_shapes=[pltpu.VMEM(seq, F32), pltpu.VMEM(seq, BF16), pltpu.VMEM(seq, F32), pltpu.VMEM(seq, F32),
                        pltpu.VMEM(seq, F32),
                        pltpu.VMEM((2,) + seq, F32), pltpu.VMEM((2,) + seq, F32), pltpu.VMEM((2,) + seq, F32),
                        pltpu.VMEM((2,) + seq, F32), pltpu.VMEM((4, GW, GW), F32)],
        compiler_params=_cparams(("arbitrary",)),
        name="rwkv_%d" % seq_len,
    )(*args)


CONV_PAD = 16
VPAD = (CONV_K // 2) * GRID_W


def _conv_kernel(u_ref, w_ref, b_ref, g_ref, beta_ref, o_ref, pad_s, vpad_s, y_s, *, seq_len, grid):
    rb = 256
    glu = u_ref[:, :CW] * _sigmoid(u_ref[:, CW:])
    zeros = jnp.zeros((CONV_PAD, CW), F32)
    pad_s[0:CONV_PAD, :] = zeros
    pad_s[CONV_PAD + seq_len:2 * CONV_PAD + seq_len, :] = zeros
    pad_s[CONV_PAD:CONV_PAD + seq_len, :] = glu
    half = CW // 2
    if grid:
        vz = jnp.zeros((VPAD, half), F32)
        vpad_s[0:VPAD, :] = vz
        vpad_s[VPAD + seq_len:2 * VPAD + seq_len, :] = vz
        vpad_s[VPAD:VPAD + seq_len, :] = glu[:, half:]
    n_h_lanes = (half if grid else CW) // 128
    for r0 in range(0, seq_len, rb):
        for lb in range(n_h_lanes):
            lanes = slice(lb * 128, (lb + 1) * 128)
            acc = jnp.zeros((rb, 128), F32)
            if grid:
                col = lax.broadcasted_iota(jnp.int32, (rb, 128), 0) % GRID_W
            span = rb + 2 * CONV_PAD
            xpad = pad_s[r0:r0 + span, lanes]
            for res in range(8):
                xrot = xpad if res == 0 else pltpu.roll(xpad, span - res, 0)
                for k in range(CONV_K):
                    if (k + 1) % 8 != res:
                        continue
                    off = (k + 1) - res
                    term = xrot[off:off + rb] * w_ref[k:k + 1, lanes]
                    if grid:
                        src = col + (k - CONV_K // 2)
                        term = jnp.where((src >= 0) & (src < GRID_W), term, 0.0)
                    acc = acc + term
            y_s[r0:r0 + rb, lanes] = acc
        if grid:
            for lb in range(half // 128):
                lanes = slice(lb * 128, (lb + 1) * 128)
                wl = slice(half + lb * 128, half + (lb + 1) * 128)
                acc = jnp.zeros((rb, 128), F32)
                for k in range(CONV_K):
                    start = VPAD + r0 + (k - CONV_K // 2) * GRID_W
                    acc = acc + vpad_s[start:start + rb, lanes] * w_ref[k:k + 1, wl]
                y_s[r0:r0 + rb, wl] = acc
    for r0 in range(0, seq_len, rb):
        y = y_s[r0:r0 + rb, :] + b_ref[...]
        yl = _layer_norm(y, g_ref[...], beta_ref[...])
        o_ref[r0:r0 + rb, :] = yl * _sigmoid(yl)


def _conv(proj_b, wts, seq_len, n_seq, first_block, grid):
    full = lambda a: pl.BlockSpec(a.shape, lambda s, _n=a.ndim: (0,) * _n)
    in_specs = [pl.BlockSpec((seq_len, 2 * CW), lambda s: (first_block + s, 0))] + [full(a) for a in wts]
    args = [proj_b] + list(wts)
    return pl.pallas_call(
        functools.partial(_conv_kernel, seq_len=seq_len, grid=grid),
        grid=(n_seq,),
        in_specs=in_specs,
        out_specs=pl.BlockSpec((seq_len, CW), lambda s: (s, 0)),
        out_shape=jax.ShapeDtypeStruct((n_seq * seq_len, CW), F32),
        scratch_shapes=[pltpu.VMEM((seq_len + 2 * CONV_PAD, CW), F32),
                        pltpu.VMEM((seq_len + 2 * VPAD if grid else 8, CW // 2), F32),
                        pltpu.VMEM((seq_len, CW), F32)],
        compiler_params=_cparams(("arbitrary",)),
        name="conv_%d" % seq_len,
    )(*args)


def _outproj_kernel(yac_ref, yal_ref, ybc_ref, ybl_ref, xc_ref, xl_ref, g1_ref, sc2_ref, sh2_ref, wa_ref, wb_ref,
                    lg_ref, lb_ref, x1_ref, h2_ref, *, alpha, ctx_tiles):
    is_ctx = pl.program_id(0) < ctx_tiles
    ya = jnp.where(is_ctx, yac_ref[...], yal_ref[...])
    yb = jnp.where(is_ctx, ybc_ref[...], ybl_ref[...])
    x = jnp.where(is_ctx, xc_ref[...], xl_ref[...])
    mix = _dot(ya.astype(BF16), wa_ref[...]) + _dot(yb.astype(BF16), wb_ref[...])
    x1 = _layer_norm(alpha * x + g1_ref[0] * mix, lg_ref[...], lb_ref[...])
    x1_ref[...] = x1
    h2_ref[...] = x1 * (1.0 + sc2_ref[0]) + sh2_ref[0]


def _out_proj(ya_c, ya_l, yb_c, yb_l, x_ctx, x_lat, g1, sc2, sh2, w_oa, w_ob, ln_g, ln_b, tm, row_of_tile, alpha):
    t_all = x_ctx.shape[0] + x_lat.shape[0]
    ctx_tiles = x_ctx.shape[0] // tm
    tile = lambda w: pl.BlockSpec((tm, w), lambda i: (i, 0))
    modrow = pl.BlockSpec((1, 1, D_MODEL), lambda i: (row_of_tile(i), 0, 0))
    full = lambda a: pl.BlockSpec(a.shape, lambda i, _n=a.ndim: (0,) * _n)
    return pl.pallas_call(
        functools.partial(_outproj_kernel, alpha=alpha, ctx_tiles=ctx_tiles),
        grid=(t_all // tm,),
        in_specs=_two_source_specs(tm, RW, ctx_tiles) + _two_source_specs(tm, CW, ctx_tiles)
        + _two_source_specs(tm, D_MODEL, ctx_tiles)
        + [modrow, modrow, modrow, full(w_oa), full(w_ob), full(ln_g), full(ln_b)],
        out_specs=[tile(D_MODEL), tile(D_MODEL)],
        out_shape=[jax.ShapeDtypeStruct((t_all, D_MODEL), F32), jax.ShapeDtypeStruct((t_all, D_MODEL), F32)],
        compiler_params=_cparams(("arbitrary",)),
        name="out_proj",
    )(ya_c, ya_l, yb_c, yb_l, x_ctx, x_lat, g1, sc2, sh2, w_oa, w_ob, ln_g, ln_b)


def _route_kernel(h_ref, wt_hi_ref, wt_lo_ref, bias_ref, ustrict_ref,
                  eidx_ref, wts_ref, pos_ref, cnt_ref, carry_s, *, tm):
    i = pl.program_id(0)

    @pl.when(i == 0)
    def _():
        carry_s[...] = jnp.zeros_like(carry_s)

    h_hi, h_lo = _split_bf16(h_ref[...])
    logits = _dot_nt(wt_hi_ref[...], h_hi) + _dot_nt(wt_hi_ref[...], h_lo) + _dot_nt(wt_lo_ref[...], h_hi)
    scores = _sigmoid(logits)
    sel = scores + bias_ref[...]
    neg = jnp.float32(-jnp.inf)
    gsz = N_EXPERTS // N_GROUPS
    gs = []
    for g in range(N_GROUPS):
        blk = sel[g * gsz:(g + 1) * gsz]
        m1 = jnp.max(blk, axis=0, keepdims=True)
        eq = blk == m1
        cnt = jnp.sum(eq.astype(F32), axis=0, keepdims=True)
        m2 = jnp.max(jnp.where(eq, neg, blk), axis=0, keepdims=True)
        gs.append(m1 + jnp.where(cnt >= 2.0, m1, m2))
    masked = []
    for g in range(N_GROUPS):
        rank = jnp.zeros_like(gs[g])
        for o in range(N_GROUPS):
            if o == g:
                continue
            beats = (gs[o] > gs[g]) | ((gs[o] == gs[g]) & (o < g))
            rank = rank + beats.astype(F32)
        keep = rank < float(TOPK_GROUPS)
        masked.append(jnp.where(keep, sel[g * gsz:(g + 1) * gsz], neg))
    cur = jnp.concatenate(masked, axis=0)
    iota_e = lax.broadcasted_iota(jnp.int32, cur.shape, 0).astype(F32)
    idxs, ws = [], []
    selmask = jnp.zeros(cur.shape, F32)
    for _ in range(TOP_K):
        m = jnp.max(cur, axis=0, keepdims=True)
        idx = jnp.min(jnp.where(cur == m, iota_e, float(N_EXPERTS)), axis=0, keepdims=True)
        onehot = iota_e == idx
        ws.append(jnp.sum(jnp.where(onehot, scores, 0.0), axis=0, keepdims=True))
        idxs.append(idx)
        cur = jnp.where(onehot, neg, cur)
        selmask = jnp.where(onehot, 1.0, selmask)
    pos = _dot(selmask.astype(BF16), ustrict_ref[...]) + carry_s[...]
    carry_s[...] = carry_s[...] + jnp.sum(selmask, axis=1, keepdims=True)
    cnt_ref[...] = carry_s[...]
    wsum = ws[0]
    for k in range(1, TOP_K):
        wsum = wsum + ws[k]
    pks = [jnp.sum(jnp.where(iota_e == idxs[k], pos, 0.0), axis=0, keepdims=True) for k in range(TOP_K)]
    eidx_ref[...] = jnp.concatenate(idxs, axis=0).astype(jnp.int32)
    wts_ref[...] = jnp.concatenate([w / wsum * ROUTED_SCALE for w in ws], axis=0)
    pos_ref[...] = jnp.concatenate(pks, axis=0).astype(jnp.int32)


def _route(h2, wt_hi, wt_lo, bias_col, tm):
    t_all = h2.shape[0]
    ustrict = jnp.asarray(np.triu(np.ones((tm, tm), np.float32), 1), dtype=BF16)
    full = lambda a: pl.BlockSpec(a.shape, lambda i, _n=a.ndim: (0,) * _n)
    tok = pl.BlockSpec((TOP_K, tm), lambda i: (0, i))
    return pl.pallas_call(
        functools.partial(_route_kernel, tm=tm),
        grid=(t_all // tm,),
        in_specs=[pl.BlockSpec((tm, D_MODEL), lambda i: (i, 0)), full(wt_hi), full(wt_lo), full(bias_col),
                  full(ustrict)],
        out_specs=[tok, tok, tok, pl.BlockSpec((N_EXPERTS, 1), lambda i: (0, 0))],
        out_shape=[jax.ShapeDtypeStruct((TOP_K, t_all), jnp.int32), jax.ShapeDtypeStruct((TOP_K, t_all), F32),
                   jax.ShapeDtypeStruct((TOP_K, t_all), jnp.int32), jax.ShapeDtypeStruct((N_EXPERTS, 1), F32)],
        scratch_shapes=[pltpu.VMEM((N_EXPERTS, 1), F32)],
        compiler_params=_cparams(("arbitrary",)),
        name="route",
    )(h2, wt_hi, wt_lo, bias_col, ustrict)


def _slots_kernel(eidx_ref, pos_ref, pstart_ref, slotx_ref, sloty_ref):
    tm = eidx_ref.shape[1]
    iota_e = lax.broadcasted_iota(jnp.int32, (N_EXPERTS, tm), 0)
    for c, out in enumerate((slotx_ref, sloty_ref)):
        rows = []
        for k in range(TOP_K):
            onehot = iota_e == eidx_ref[k:k + 1, :]
            rows.append(jnp.sum(jnp.where(onehot, pstart_ref[:, c:c + 1], 0.0), axis=0, keepdims=True))
        out[...] = jnp.concatenate(rows, axis=0).astype(jnp.int32) + pos_ref[...]


def _slots(eidx, pos, pstart_cols, tm):
    t_all = eidx.shape[1]
    tok = pl.BlockSpec((TOP_K, tm), lambda i: (0, i))
    return pl.pallas_call(
        _slots_kernel,
        grid=(t_all // tm,),
        in_specs=[tok, tok, pl.BlockSpec((N_EXPERTS, 2), lambda i: (0, 0))],
        out_specs=[tok, tok],
        out_shape=[jax.ShapeDtypeStruct((TOP_K, t_all), jnp.int32), jax.ShapeDtypeStruct((TOP_K, t_all), jnp.int32)],
        compiler_params=_cparams(("arbitrary",)),
        name="slots",
    )(eidx, pos, pstart_cols)


TD_TOK = 1024
XS_ALIGN = 8


def _dispatch_kernel(fstart_ref, flen_ref, total_ref, slot_ref, h_ref, xs_out, zero_s, sem, zsem, *, xs_rows):
    i = pl.program_id(0)

    def body(j, carry):
        for k in range(TOP_K):
            pltpu.make_async_copy(h_ref.at[pl.ds(j, 1), :], xs_out.at[pl.ds(slot_ref[0, 0, j * TOP_K + k], 1), :],
                                  sem).start(priority=k % 2)
        return carry

    lax.fori_loop(0, TD_TOK, body, 0)
    for k in range(TOP_K):
        pltpu.make_async_copy(h_ref, xs_out.at[pl.ds(0, TD_TOK), :], sem).wait()

    @pl.when(i == pl.num_programs(0) - 1)
    def _():
        zero_s[...] = jnp.zeros_like(zero_s)
        total = total_ref[0]
        n_tail = (xs_rows - total) // XS_ALIGN

        def gap_copy(e, r):
            return pltpu.make_async_copy(zero_s.at[pl.ds(0, 1), :], xs_out.at[pl.ds(fstart_ref[e] + r, 1), :], zsem)

        def tail_copy(q):
            start = pl.multiple_of(total + q * XS_ALIGN, XS_ALIGN)
            return pltpu.make_async_copy(zero_s, xs_out.at[pl.ds(start, XS_ALIGN), :], zsem)

        def for_gaps(fn):
            def per_expert(e, carry):
                for r in range(XS_ALIGN - 1):
                    @pl.when(r < flen_ref[e])
                    def _():
                        fn(gap_copy(e, r))
                return carry
            lax.fori_loop(0, N_EXPERTS, per_expert, 0)

        def for_tail(fn):
            def per_q(q, carry):
                fn(tail_copy(q))
                return carry
            lax.fori_loop(0, n_tail, per_q, 0)

        for_gaps(lambda cp: cp.start())
        for_tail(lambda cp: cp.start())
        for_gaps(lambda cp: cp.wait())
        for_tail(lambda cp: cp.wait())


def _dispatch(slot3, h2, fstart, flen, total, xs_rows):
    t_all = h2.shape[0]
    grid_spec = pltpu.PrefetchScalarGridSpec(
        num_scalar_prefetch=3,
        grid=(t_all // TD_TOK,),
        in_specs=[pl.BlockSpec((1, 1, TD_TOK * TOP_K), lambda i, a, b, c: (i, 0, 0), memory_space=pltpu.SMEM),
                  pl.BlockSpec((TD_TOK, D_MODEL), lambda i, a, b, c: (i, 0))],
        out_specs=pl.BlockSpec(memory_space=pl.ANY),
        scratch_shapes=[pltpu.VMEM((XS_ALIGN, D_MODEL), F32), pltpu.SemaphoreType.DMA(()),
                        pltpu.SemaphoreType.DMA(())],
    )
    return pl.pallas_call(
        functools.partial(_dispatch_kernel, xs_rows=xs_rows),
        grid_spec=grid_spec,
        out_shape=jax.ShapeDtypeStruct((xs_rows, D_MODEL), F32),
        compiler_params=_cparams(("arbitrary",)),
        name="dispatch",
    )(fstart, flen, total, slot3, h2)


X_RING = 6
Y_RING = 3
W_BUFFERS = 3


COPY_PARTS = 2


class _SplitCopy:
    def __init__(self, parts):
        self.parts = parts

    def start(self):
        for i, cp in enumerate(self.parts):
            cp.start(priority=i % 2)

    def wait(self):
        for cp in self.parts:
            cp.wait()


def _expert_kernel(bstart_ref, nblk_ref, nused_ref, xrow_ref, xs_hbm, wgu_hbm, wdn_hbm, ys_hbm,
                   xbuf, ybuf, wgu_f32, wdn_f32, wgu16, wdn16, sem_in, sem_out, sem_w, *, n_blocks):
    e = pl.program_id(0)
    nused = nused_ref[0]
    n = nblk_ref[e]
    g0 = bstart_ref[e]

    def split_copy(src, dst, sem, rows):
        h = rows // COPY_PARTS
        return _SplitCopy([pltpu.make_async_copy(src.at[pl.ds(r0, h), :], dst.at[pl.ds(r0, h), :], sem)
                           for r0 in range(0, rows, h)])

    def w_copies(ex):
        s = ex % W_BUFFERS
        return (split_copy(wgu_hbm.at[ex], wgu_f32.at[s], sem_w.at[0, s], D_MODEL),
                split_copy(wdn_hbm.at[ex], wdn_f32.at[s], sem_w.at[1, s], EXPERT_DIM))

    @pl.when(e == 0)
    def _():
        for q in range(W_BUFFERS - 1):
            for cp in w_copies(q):
                cp.start()

    @pl.when(e + (W_BUFFERS - 1) < N_EXPERTS)
    def _():
        for cp in w_copies(e + (W_BUFFERS - 1)):
            cp.start()

    def in_copy(g, s):
        start = pl.multiple_of(xrow_ref[g], XS_ALIGN)
        return split_copy(xs_hbm.at[pl.ds(start, MOE_BLOCK), :], xbuf.at[s], sem_in.at[s], MOE_BLOCK)

    def out_copy(g, s):
        return split_copy(ybuf.at[s], ys_hbm.at[pl.ds(g * MOE_BLOCK, MOE_BLOCK), :], sem_out.at[s], MOE_BLOCK)

    @pl.when(e == 0)
    def _():
        for q in range(X_RING - 1):
            @pl.when(q < nused)
            def _():
                in_copy(q, q).start()

    for cp in w_copies(e):
        cp.wait()

    @pl.when(n > 0)
    def _():
        wgu16[...] = wgu_f32[e % W_BUFFERS].astype(BF16)
        wdn16[...] = wdn_f32[e % W_BUFFERS].astype(BF16)

    def block(b, carry):
        g = g0 + b
        s = g % Y_RING
        ahead = g + (X_RING - 1)

        @pl.when(ahead < nused)
        def _():
            in_copy(ahead, ahead % X_RING).start()

        in_copy(g, g % X_RING).wait()
        gu = _dot(xbuf[g % X_RING].astype(BF16), wgu16[...])
        gate = gu[:, :EXPERT_DIM]
        act = (gate * _sigmoid(gate) * gu[:, EXPERT_DIM:]).astype(BF16)
        y = _dot(act, wdn16[...])

        @pl.when(g >= Y_RING)
        def _():
            out_copy(g - Y_RING, s).wait()

        ybuf[s] = y
        out_copy(g, s).start()
        return carry

    lax.fori_loop(0, n, block, 0)

    @pl.when(e == pl.num_programs(0) - 1)
    def _():
        for d in range(1, Y_RING + 1):
            @pl.when(nused >= d)
            def _():
                out_copy(nused - d, (nused - d) % Y_RING).wait()

        ybuf[0] = jnp.zeros((MOE_BLOCK, D_MODEL), F32)

        def tail(g, carry):
            out_copy(g, 0).start()
            out_copy(g, 0).wait()
            return carry

        lax.fori_loop(nused, n_blocks, tail, 0)


def _experts(xs, bstart, nblk, nused, xrow, w_gu, w_down):
    n_blocks = xrow.shape[0]
    grid_spec = pltpu.PrefetchScalarGridSpec(
        num_scalar_prefetch=4,
        grid=(N_EXPERTS,),
        in_specs=[pl.BlockSpec(memory_space=pl.ANY), pl.BlockSpec(memory_space=pl.ANY),
                  pl.BlockSpec(memory_space=pl.ANY)],
        out_specs=pl.BlockSpec(memory_space=pl.ANY),
        scratch_shapes=[pltpu.VMEM((X_RING, MOE_BLOCK, D_MODEL), F32),
                        pltpu.VMEM((Y_RING, MOE_BLOCK, D_MODEL), F32),
                        pltpu.VMEM((W_BUFFERS, D_MODEL, 2 * EXPERT_DIM), F32),
                        pltpu.VMEM((W_BUFFERS, EXPERT_DIM, D_MODEL), F32),
                        pltpu.VMEM((D_MODEL, 2 * EXPERT_DIM), BF16),
                        pltpu.VMEM((EXPERT_DIM, D_MODEL), BF16),
                        pltpu.SemaphoreType.DMA((X_RING,)), pltpu.SemaphoreType.DMA((Y_RING,)),
                        pltpu.SemaphoreType.DMA((2, W_BUFFERS))],
    )
    return pl.pallas_call(
        functools.partial(_expert_kernel, n_blocks=n_blocks),
        grid_spec=grid_spec,
        out_shape=jax.ShapeDtypeStruct((n_blocks * MOE_BLOCK, D_MODEL), F32),
        compiler_params=_cparams(("arbitrary",)),
        name="experts",
    )(bstart, nblk, nused, xrow, xs, w_gu, w_down)


TC_TOK = 128


def _combine_kernel(slot_ref, slot_next_ref, wt_ref, x1_ref, h2_ref, g2_ref, sgu_ref, sdn_ref, lg_ref, lb_ref,
                    y_hbm, oc_ref, ol_ref, gbuf, sems, *, n_tiles, ctx_tiles, alpha):
    i = pl.program_id(0)

    def row_copy(sref, b, j, k):
        return pltpu.make_async_copy(y_hbm.at[pl.ds(sref[0, 0, j * TOP_K + k], 1), :],
                                     gbuf.at[b, k, pl.ds(j, 1), :], sems.at[b])

    def wait_tile(b):
        for k in range(TOP_K):
            pltpu.make_async_copy(y_hbm.at[pl.ds(0, TC_TOK), :], gbuf.at[b, k], sems.at[b]).wait()

    @pl.when(i == 0)
    def _():
        def body(j, carry):
            for k in range(TOP_K):
                row_copy(slot_ref, 0, j, k).start(priority=k % 2)
            return carry
        lax.fori_loop(0, TC_TOK, body, 0)

    b = i % 2
    nb = 1 - b

    def issue_next(j0, j1):
        for j in range(j0, j1):
            for k in range(TOP_K):
                row_copy(slot_next_ref, nb, j, k).start(priority=k % 2)

    issue_next(0, TC_TOK // 2)
    h = h2_ref[...].astype(BF16)
    su = _dot(h, sgu_ref[...])
    sg = su[:, :EXPERT_DIM]
    act = (sg * _sigmoid(sg) * su[:, EXPERT_DIM:]).astype(BF16)
    ffn = _dot(act, sdn_ref[...])
    wait_tile(b)
    issue_next(TC_TOK // 2, TC_TOK)
    for k in range(TOP_K):
        ffn = ffn + gbuf[b, k] * wt_ref[:, k:k + 1]
    out = _layer_norm(alpha * x1_ref[...] + g2_ref[0] * ffn, lg_ref[...], lb_ref[...])

    @pl.when(i < ctx_tiles)
    def _():
        oc_ref[...] = out

    @pl.when(i >= ctx_tiles)
    def _():
        ol_ref[...] = out

    @pl.when(i == n_tiles - 1)
    def _():
        wait_tile(nb)


def _combine(slot3, wts_t, x1, h2, g2, sh_gu, sh_dn, ln_g, ln_b, y_sorted, row_of_tile, alpha, tok_ctx):
    t_all = x1.shape[0]
    n_tiles = t_all // TC_TOK
    ctx_tiles = tok_ctx // TC_TOK
    tile = lambda w: pl.BlockSpec((TC_TOK, w), lambda i: (i, 0))
    full = lambda a: pl.BlockSpec(a.shape, lambda i, _n=a.ndim: (0,) * _n)
    return pl.pallas_call(
        functools.partial(_combine_kernel, n_tiles=n_tiles, ctx_tiles=ctx_tiles, alpha=alpha),
        grid=(n_tiles,),
        in_specs=[pl.BlockSpec((1, 1, TC_TOK * TOP_K), lambda i: (i, 0, 0), memory_space=pltpu.SMEM),
                  pl.BlockSpec((1, 1, TC_TOK * TOP_K), lambda i: (jnp.minimum(i + 1, n_tiles - 1), 0, 0),
                               memory_space=pltpu.SMEM),
                  tile(TOP_K), tile(D_MODEL), tile(D_MODEL),
                  pl.BlockSpec((1, 1, D_MODEL), lambda i: (row_of_tile(i), 0, 0)),
                  full(sh_gu), full(sh_dn), full(ln_g), full(ln_b),
                  pl.BlockSpec(memory_space=pl.ANY)],
        out_specs=_two_source_specs(TC_TOK, D_MODEL, ctx_tiles),
        out_shape=[jax.ShapeDtypeStruct((tok_ctx, D_MODEL), F32),
                   jax.ShapeDtypeStruct((t_all - tok_ctx, D_MODEL), F32)],
        scratch_shapes=[pltpu.VMEM((2, TOP_K, TC_TOK, D_MODEL), F32), pltpu.SemaphoreType.DMA((2,))],
        compiler_params=_cparams(("arbitrary",)),
        name="combine",
    )(slot3, slot3, wts_t, x1, h2, g2, sh_gu, sh_dn, ln_g, ln_b, y_sorted)


def kernel(x_prompt, x_sample, state_rwkv, c, c_ctx, w_ada, b_ada, w_in, mu_shift, w_decay0, w_decay2, w_a0, w_a2, w_g2, k_k, k_a, r_k, lnx_g, lnx_b, conv_w, conv_b, conv_ln_g, conv_ln_b, w_out, ln1_g, ln1_b, router_w, router_bias, expert_w_gu, expert_w_down, shared_w_gu, shared_w_down, ln2_g, ln2_b):
    depth = w_ada.shape[0]
    assert depth == 1
    alpha = (2.0 * depth) ** 0.25
    n_ctx, t_ctx, _ = x_prompt.shape
    n_lat, t_lat, _ = x_sample.shape
    tok_ctx = n_ctx * t_ctx
    tok_lat = n_lat * t_lat
    t_all = tok_ctx + tok_lat
    tm = 1024
    assert tok_ctx % tm == 0 and t_lat % tm == 0 and t_lat // GRID_W * GRID_W == t_lat
    l = 0

    x_ctx = x_prompt.reshape(tok_ctx, D_MODEL)
    x_lat = x_sample.reshape(tok_lat, D_MODEL)

    cond8 = jnp.zeros((8, D_MODEL), F32).at[0].set(c_ctx).at[1:1 + n_lat].set(c)
    mod = _ada(cond8, w_ada[l], b_ada[l])
    sh1, sc1, g1, sh2, sc2, g2 = [m.reshape(8, 1, D_MODEL) for m in jnp.split(mod, 6, axis=-1)]

    def row_of_tile_for(tile_rows):
        ctx_tiles = tok_ctx // tile_rows
        per_seq = t_lat // tile_rows
        return lambda i: jnp.where(i < ctx_tiles, 0, 1 + (i - ctx_tiles) // per_seq)

    w_in_a = w_in[l][:, :RWKV_IN].astype(BF16)
    w_in_b = w_in[l][:, RWKV_IN:].astype(BF16)
    proj_a, proj_b = _in_proj(x_ctx, x_lat, sc1, sh1, w_in_a, w_in_b, tm, row_of_tile_for(tm))

    zeros64 = jnp.zeros((64, 2 * RW), F32)
    wd2 = jnp.concatenate([w_decay2[l, 0], w_decay2[l, 1]], axis=1)
    wa2 = jnp.concatenate([w_a2[l, 0], w_a2[l, 1]], axis=1)
    w_lora = jnp.concatenate([jnp.concatenate([wd2, zeros64], axis=1),
                              jnp.concatenate([zeros64, wa2], axis=1)], axis=0).astype(BF16)
    row = lambda v: v.reshape(1, -1)
    rwkv_w = [row(mu_shift[l]), w_lora, w_g2[l].astype(BF16), row(w_decay0[l]), row(w_a0[l]),
              row(k_k[l]), row(k_a[l]), row(r_k[l]), row(lnx_g[l]), row(lnx_b[l])]
    consts = _rwkv_consts()

    def to_cat(s):
        b = s.shape[0]
        s = s.reshape(b, 2, 2, 4, HD, HD)
        return jnp.transpose(s, (0, 1, 2, 5, 3, 4)).reshape(b, 4, HD, GW)

    s0_ctx = jnp.zeros((n_ctx, 4, HD, GW), F32)
    s0_lat = to_cat(state_rwkv[:, l].astype(F32))
    ya_c, sfin_ctx = _rwkv(proj_a, s0_ctx, rwkv_w, consts, t_ctx, n_ctx, 0)
    ya_l, _ = _rwkv(proj_a, s0_lat, rwkv_w, consts, t_lat, n_lat, tok_ctx // t_lat)
    sfin = jnp.transpose(sfin_ctx.reshape(n_ctx, 2, 2, HD, 4, HD), (0, 1, 2, 4, 5, 3))
    new_state = sfin.reshape(n_ctx, 1, 2, 2 * 4, HD, HD).astype(x_prompt.dtype)

    conv_wts = [conv_w[l], row(conv_b[l]), row(conv_ln_g[l]), row(conv_ln_b[l])]
    yb_c = _conv(proj_b, conv_wts, t_ctx, n_ctx, 0, False)
    yb_l = _conv(proj_b, conv_wts, t_lat, n_lat, tok_ctx // t_lat, True)

    w_oa = w_out[l][:RW].astype(BF16)
    w_ob = w_out[l][RW:].astype(BF16)
    x1, h2 = _out_proj(ya_c, ya_l, yb_c, yb_l, x_ctx, x_lat, g1, sc2, sh2, w_oa, w_ob, row(ln1_g[l]), row(ln1_b[l]), tm,
                       row_of_tile_for(tm), alpha)

    rwt = router_w[l].T
    rwt_hi = rwt.astype(BF16)
    rwt_lo = (rwt - rwt_hi.astype(F32)).astype(BF16)
    eidx, wts, pos, counts = _route(h2, rwt_hi, rwt_lo, router_bias[l].reshape(N_EXPERTS, 1), 512)

    n_assign = t_all * TOP_K
    n_blocks = (n_assign + N_EXPERTS * (MOE_BLOCK - 1) + MOE_BLOCK - 1) // MOE_BLOCK
    cnt = counts.reshape(N_EXPERTS).astype(jnp.int32)
    padded = (cnt + MOE_BLOCK - 1) // MOE_BLOCK * MOE_BLOCK
    pend = jnp.cumsum(padded)
    pstart = pend - padded
    padded_x = (cnt + XS_ALIGN - 1) // XS_ALIGN * XS_ALIGN
    pend_x = jnp.cumsum(padded_x)
    pstart_x = pend_x - padded_x
    xs_rows = n_assign + N_EXPERTS * (XS_ALIGN - 1) // XS_ALIGN * XS_ALIGN + MOE_BLOCK
    slot_x, slot_y = _slots(eidx, pos, jnp.stack([pstart_x, pstart], axis=1).astype(F32), 512)
    per_tile = lambda s, tok: s.T.reshape(t_all // tok, 1, tok * TOP_K)
    xs = _dispatch(per_tile(slot_x, TD_TOK), h2, (pstart_x + cnt).astype(jnp.int32), (padded_x - cnt).astype(jnp.int32),
                   pend_x[-1].astype(jnp.int32).reshape(1), xs_rows)
    bstart = (pstart // MOE_BLOCK).astype(jnp.int32)
    nblk = (padded // MOE_BLOCK).astype(jnp.int32)
    nused = (pend[-1] // MOE_BLOCK).astype(jnp.int32).reshape(1)
    blk_ids = jnp.arange(n_blocks, dtype=jnp.int32)
    blk_e = jnp.minimum(jnp.sum((pend[None, :] <= (blk_ids * MOE_BLOCK)[:, None]).astype(jnp.int32), axis=1),
                        N_EXPERTS - 1)
    shift = jnp.sum(jnp.where(blk_e[:, None] == jnp.arange(N_EXPERTS, dtype=jnp.int32)[None, :],
                              (pstart_x - pstart)[None, :], 0), axis=1)
    xrow = jnp.clip(blk_ids * MOE_BLOCK + shift, 0, xs_rows - MOE_BLOCK).astype(jnp.int32)
    y_sorted = _experts(xs, bstart, nblk, nused, xrow, expert_w_gu[l], expert_w_down[l])
    y_ctx, y_lat = _combine(per_tile(slot_y, TC_TOK), wts.T, x1, h2, g2, shared_w_gu[l].astype(BF16),
                            shared_w_down[l].astype(BF16), row(ln2_g[l]), row(ln2_b[l]), y_sorted,
                            row_of_tile_for(TC_TOK), alpha, tok_ctx)
    return (y_ctx.reshape(n_ctx, t_ctx, D_MODEL), y_lat.reshape(n_lat, t_lat, D_MODEL), new_state)
```
